```python
import math
import jax
import jax.numpy as jnp
from jax import lax
import numpy as np

D_MODEL = 2048
BATCH = 16
SEQ = 2048
DEPTH = 2

HEAD_DIM = 64
SB_HEADS = 8
DIL_PATTERNS = ((128, 1), (512, 4), (2048, 16))
DIL_HEADS_PER_GROUP = 4
DIL_HEADS = DIL_HEADS_PER_GROUP * len(DIL_PATTERNS)
SSM_WIDTH = 512
SSM_GROUP = 16
SSM_GROUPS = SSM_WIDTH // SSM_GROUP
SSM_STATE = 64
DSA_HEADS = 8
IDX_HEADS = 8
IDX_DIM = 64
DSA_TOPK = 256
N_BRANCH = 4
D_FF = 5632
REL_BUCKETS = 32
REL_MAX_DIST = 128
REL_HEADS = DIL_HEADS + DSA_HEADS
Q_BLOCK = 128
DN_ALPHA = (2.0 * DEPTH) ** 0.25
DN_BETA = (8.0 * DEPTH) ** -0.25
LN_EPS = 1e-5
NEG_INF = -1e30
MACARON = 0.5

IN_SPLITS = (3 * SB_HEADS * HEAD_DIM, 3 * DIL_HEADS * HEAD_DIM, SSM_WIDTH, 3 * DSA_HEADS * HEAD_DIM,
             IDX_HEADS * IDX_DIM, IDX_DIM, IDX_HEADS, N_BRANCH * D_MODEL)
IN_COLS = sum(IN_SPLITS)
IN_OFFSETS = tuple(int(o) for o in np.cumsum(IN_SPLITS)[:-1])

kernel_name = "hybrid_gated_mixer_block"


def layer_norm(x, g, b):
    xf = x.astype(jnp.float32)
    mu = jnp.mean(xf, axis=-1, keepdims=True)
    var = jnp.mean(jnp.square(xf - mu), axis=-1, keepdims=True)
    return ((xf - mu) * lax.rsqrt(var + LN_EPS) * g.astype(jnp.float32) + b.astype(jnp.float32)).astype(x.dtype)


def swiglu_ffn(x, w_up, w_down):
    a, b = jnp.split(x @ w_up, 2, axis=-1)
    return (jax.nn.silu(a) * b) @ w_down


def t5_bucket(dist):
    max_exact = REL_BUCKETS // 2
    d = jnp.maximum(dist, 1).astype(jnp.float32)
    large = max_exact + (jnp.log(d / max_exact) / math.log(REL_MAX_DIST / max_exact)
                         * (REL_BUCKETS - max_exact)).astype(jnp.int32)
    large = jnp.minimum(large, REL_BUCKETS - 1)
    return jnp.where(dist < max_exact, dist, large)


def stick_breaking_attention(q, k, v):
    bsz, L, nh, dh = q.shape
    scale = dh ** -0.5
    key_pos = jnp.arange(L)

    def block(i):
        q0 = i * Q_BLOCK
        q_pos = q0 + jnp.arange(Q_BLOCK)
        qb = lax.dynamic_slice_in_dim(q, q0, Q_BLOCK, axis=1)
        z = jnp.einsum('bqhd,bkhd->bhqk', qb, k).astype(jnp.float32) * scale
        strict = key_pos[None, :] < q_pos[:, None]
        log_1mb = jnp.where(strict, jax.nn.log_sigmoid(-z), 0.0)
        suffix = lax.cumsum(log_1mb, axis=3, reverse=True) - log_1mb
        att = jnp.where(strict, jnp.exp(jax.nn.log_sigmoid(z) + suffix), 0.0)
        return jnp.einsum('bhqk,bkhd->bqhd', att, v.astype(jnp.float32)).astype(q.dtype)

    out = lax.map(block, jnp.arange(L // Q_BLOCK))
    return jnp.moveaxis(out, 0, 1).reshape(bsz, L, nh * dh)


def dilated_attention(q, k, v, rel_bias):
    bsz, L, _, dh = q.shape
    hg = DIL_HEADS_PER_GROUP
    scale = dh ** -0.5
    groups = [(q[:, :, g * hg:(g + 1) * hg], k[:, :, g * hg:(g + 1) * hg], v[:, :, g * hg:(g + 1) * hg],
               rel_bias[:, g * hg:(g + 1) * hg], window, dil)
              for g, (window, dil) in enumerate(DIL_PATTERNS)]

    def block(i):
        q0 = i * Q_BLOCK
        q_pos = q0 + jnp.arange(Q_BLOCK)
        outs, lses = [], []
        for qg, kg, vg, bg, window, dil in groups:
            steps = jnp.arange(window // dil + 1)
            idx = q_pos[:, None] - dil * steps[None, :]
            valid = idx >= 0
            idx = jnp.maximum(idx, 0)
            qb = lax.dynamic_slice_in_dim(qg, q0, Q_BLOCK, axis=1)
            kb = jnp.take(kg, idx, axis=1)
            vb = jnp.take(vg, idx, axis=1)
            s = jnp.einsum('bqhd,bqkhd->bhqk', qb, kb).astype(jnp.float32) * scale
            bias = bg[t5_bucket(dil * steps)].astype(jnp.float32)
            s = jnp.where(valid[None, None], s + bias.T[None, :, None, :], NEG_INF)
            lse = jax.nn.logsumexp(s, axis=-1)
            p = jnp.exp(s - lse[..., None])
            outs.append(jnp.einsum('bhqk,bqkhd->bqhd', p, vb.astype(jnp.float32)))
            lses.append(lse)
        wts = jax.nn.softmax(jnp.stack(lses), axis=0)
        o = jnp.einsum('gbhq,gbqhd->bqhd', wts, jnp.stack(outs))
        return o.astype(q.dtype)

    out = lax.map(block, jnp.arange(L // Q_BLOCK))
    return jnp.moveaxis(out, 0, 1).reshape(bsz, L, hg * dh)


def _complex_linear_combine(e1, e2):
    a1r, a1i, b1r, b1i = e1
    a2r, a2i, b2r, b2i = e2
    return (a2r * a1r - a2i * a1i, a2r * a1i + a2i * a1r,
            a2r * b1r - a2i * b1i + b2r, a2r * b1i + a2i * b1r + b2i)


def s5_glu(u, lam_re, lam_im, log_dt, b_re, b_im, c_re, c_im, d_skip, w_glu):
    bsz, L, _ = u.shape
    f32 = jnp.float32
    uf = u.astype(f32).reshape(bsz, L, SSM_GROUPS, SSM_GROUP)
    lr, li = lam_re.astype(f32), lam_im.astype(f32)
    dt = jnp.exp(log_dt.astype(f32))[:, None]
    mag = jnp.exp(lr * dt)
    a_re, a_im = mag * jnp.cos(li * dt), mag * jnp.sin(li * dt)
    den = lr * lr + li * li
    f_re = ((a_re - 1.0) * lr + a_im * li) / den
    f_im = (a_im * lr - (a_re - 1.0) * li) / den
    br, bi = b_re.astype(f32), b_im.astype(f32)
    bb_re = f_re[..., None] * br - f_im[..., None] * bi
    bb_im = f_re[..., None] * bi + f_im[..., None] * br
    bu_re = jnp.einsum('blgc,gpc->blgp', uf, bb_re)
    bu_im = jnp.einsum('blgc,gpc->blgp', uf, bb_im)
    a_re_l = jnp.broadcast_to(a_re, (1, L) + a_re.shape)
    a_im_l = jnp.broadcast_to(a_im, (1, L) + a_im.shape)
    _, _, x_re, x_im = lax.associative_scan(_complex_linear_combine, (a_re_l, a_im_l, bu_re, bu_im), axis=1)
    y = (jnp.einsum('gcp,blgp->blgc', c_re.astype(f32), x_re)
         - jnp.einsum('gcp,blgp->blgc', c_im.astype(f32), x_im)
         + d_skip.astype(f32).reshape(SSM_GROUPS, SSM_GROUP) * uf)
    y = jax.nn.gelu(y.reshape(bsz, L, SSM_WIDTH)).astype(u.dtype)
    ga, gb = jnp.split(y @ w_glu, 2, axis=-1)
    return ga * jax.nn.sigmoid(gb)


def dsa_attention(q, k, v, q_idx, k_idx, w_idx, rel_bias):
    bsz, L, nh, dh = q.shape
    top_k = min(DSA_TOPK, L // 4)
    scale = dh ** -0.5
    key_pos = jnp.arange(L)

    def block(i):
        q0 = i * Q_BLOCK
        q_pos = q0 + jnp.arange(Q_BLOCK)
        qi = lax.dynamic_slice_in_dim(q_idx, q0, Q_BLOCK, axis=1)
        wi = lax.dynamic_slice_in_dim(w_idx, q0, Q_BLOCK, axis=1)
        dots = jnp.einsum('bqhd,bkd->bqhk', qi, k_idx).astype(jnp.float32)
        index_score = jnp.einsum('bqh,bqhk->bqk', wi.astype(jnp.float32), jax.nn.relu(dots))
        index_score = jnp.where(key_pos[None, None, :] <= q_pos[None, :, None], index_score, NEG_INF)
        _, sel = lax.top_k(index_score, top_k)
        dist = q_pos[None, :, None] - sel
        valid = dist >= 0
        kb = jax.vmap(lambda kk, ii: kk[ii])(k, sel)
        vb = jax.vmap(lambda vv, ii: vv[ii])(v, sel)
        qb = lax.dynamic_slice_in_dim(q, q0, Q_BLOCK, axis=1)
        s = jnp.einsum('bqhd,bqkhd->bhqk', qb, kb).astype(jnp.float32) * scale
        bias = rel_bias[t5_bucket(jnp.maximum(dist, 0))].astype(jnp.float32)
        s = jnp.where(valid[:, None], s + jnp.transpose(bias, (0, 3, 1, 2)), NEG_INF)
        p = jax.nn.softmax(s, axis=-1)
        return jnp.einsum('bhqk,bqkhd->bqhd', p, vb.astype(jnp.float32)).astype(q.dtype)

    out = lax.map(block, jnp.arange(L // Q_BLOCK))
    return jnp.moveaxis(out, 0, 1).reshape(bsz, L, nh * dh)


def hybrid_mixer(x, w_in, rel_bias, lam_re, lam_im, log_dt, b_re, b_im, c_re, c_im, d_skip, w_glu,
                 w_br_sb, w_br_dil, w_br_ssm, w_br_dsa, w_out):
    bsz, L, _ = x.shape
    h = x @ w_in
    sb_qkv, dil_qkv, ssm_u, dsa_qkv, idx_q, idx_k, idx_w, gate_logits = jnp.split(h, IN_OFFSETS, axis=-1)
    sb_qkv = sb_qkv.reshape(bsz, L, 3, SB_HEADS, HEAD_DIM)
    y_sb = stick_breaking_attention(sb_qkv[:, :, 0], sb_qkv[:, :, 1], sb_qkv[:, :, 2])
    dil_qkv = dil_qkv.reshape(bsz, L, 3, DIL_HEADS, HEAD_DIM)
    y_dil = dilated_attention(dil_qkv[:, :, 0], dil_qkv[:, :, 1], dil_qkv[:, :, 2], rel_bias[:, :DIL_HEADS])
    y_ssm = s5_glu(ssm_u, lam_re, lam_im, log_dt, b_re, b_im, c_re, c_im, d_skip, w_glu)
    dsa_qkv = dsa_qkv.reshape(bsz, L, 3, DSA_HEADS, HEAD_DIM)
    y_dsa = dsa_attention(dsa_qkv[:, :, 0], dsa_qkv[:, :, 1], dsa_qkv[:, :, 2],
                          idx_q.reshape(bsz, L, IDX_HEADS, IDX_DIM), idx_k, idx_w, rel_bias[:, DIL_HEADS:])
    gates = jax.nn.sigmoid(gate_logits.reshape(bsz, L, N_BRANCH, D_MODEL))
    merged = (gates[:, :, 0] * (y_sb @ w_br_sb) + gates[:, :, 1] * (y_dil @ w_br_dil)
              + gates[:, :, 2] * (y_ssm @ w_br_ssm) + gates[:, :, 3] * (y_dsa @ w_br_dsa))
    return merged @ w_out


def setup_inputs(seed: int = 0) -> dict:
    key = jax.random.key(seed)
    ks = jax.random.split(key, 24)
    f32 = jnp.float32

    def nrm(k, shape, scale):
        return jax.random.normal(k, shape, f32) * scale

    G, P, C = SSM_GROUPS, SSM_STATE, SSM_GROUP
    return {
        "x": nrm(ks[0], (BATCH, SEQ, D_MODEL), 1.0),
        "ln_g": 1.0 + nrm(ks[1], (DEPTH, 3, D_MODEL), 0.02),
        "ln_b": nrm(ks[2], (DEPTH, 3, D_MODEL), 0.02),
        "ffn1_w_up": nrm(ks[3], (DEPTH, D_MODEL, 2 * D_FF), D_MODEL ** -0.5),
        "ffn1_w_down": nrm(ks[4], (DEPTH, D_FF, D_MODEL), DN_BETA * D_FF ** -0.5),
        "w_in": nrm(ks[5], (DEPTH, D_MODEL, IN_COLS), D_MODEL ** -0.5),
        "rel_bias": nrm(ks[6], (REL_BUCKETS, REL_HEADS), 0.2),
        "ssm_lam_re": -0.5 + nrm(ks[7], (DEPTH, G, P), 0.01),
        "ssm_lam_im": math.pi * jnp.arange(P, dtype=f32) + nrm(ks[8], (DEPTH, G, P), 0.01),
        "ssm_log_dt": jax.random.uniform(ks[9], (DEPTH, G), f32, math.log(1e-3), math.log(1e-1)),
        "ssm_b_re": nrm(ks[10], (DEPTH, G, P, C), C ** -0.5),
        "ssm_b_im": nrm(ks[11], (DEPTH, G, P, C), C ** -0.5),
        "ssm_c_re": nrm(ks[12], (DEPTH, G, C, P), P ** -0.5),
        "ssm_c_im": nrm(ks[13], (DEPTH, G, C, P), P ** -0.5),
        "ssm_d": nrm(ks[14], (DEPTH, SSM_WIDTH), 1.0),
        "ssm_w_glu": nrm(ks[15], (DEPTH, SSM_WIDTH, 2 * SSM_WIDTH), SSM_WIDTH ** -0.5),
        "w_br_sb": nrm(ks[16], (DEPTH, SB_HEADS * HEAD_DIM, D_MODEL), (SB_HEADS * HEAD_DIM) ** -0.5),
        "w_br_dil": nrm(ks[17], (DEPTH, DIL_HEADS_PER_GROUP * HEAD_DIM, D_MODEL), (DIL_HEADS_PER_GROUP * HEAD_DIM) ** -0.5),
        "w_br_ssm": nrm(ks[18], (DEPTH, SSM_WIDTH, D_MODEL), SSM_WIDTH ** -0.5),
        "w_br_dsa": nrm(ks[19], (DEPTH, DSA_HEADS * HEAD_DIM, D_MODEL), (DSA_HEADS * HEAD_DIM) ** -0.5),
        "w_out": nrm(ks[20], (DEPTH, D_MODEL, D_MODEL), DN_BETA * D_MODEL ** -0.5),
        "ffn2_w_up": nrm(ks[21], (DEPTH, D_MODEL, 2 * D_FF), D_MODEL ** -0.5),
        "ffn2_w_down": nrm(ks[22], (DEPTH, D_FF, D_MODEL), DN_BETA * D_FF ** -0.5),
    }


def reference(x, ln_g, ln_b, ffn1_w_up, ffn1_w_down, w_in, rel_bias, ssm_lam_re, ssm_lam_im, ssm_log_dt,
              ssm_b_re, ssm_b_im, ssm_c_re, ssm_c_im, ssm_d, ssm_w_glu, w_br_sb, w_br_dil, w_br_ssm, w_br_dsa,
              w_out, ffn2_w_up, ffn2_w_down):
    for l in range(DEPTH):
        x = layer_norm(DN_ALPHA * x + MACARON * swiglu_ffn(x, ffn1_w_up[l], ffn1_w_down[l]), ln_g[l, 0], ln_b[l, 0])
        mix = hybrid_mixer(x, w_in[l], rel_bias, ssm_lam_re[l], ssm_lam_im[l], ssm_log_dt[l],
                           ssm_b_re[l], ssm_b_im[l], ssm_c_re[l], ssm_c_im[l], ssm_d[l], ssm_w_glu[l],
                           w_br_sb[l], w_br_dil[l], w_br_ssm[l], w_br_dsa[l], w_out[l])
        x = layer_norm(DN_ALPHA * x + mix, ln_g[l, 1], ln_b[l, 1])
        x = layer_norm(DN_ALPHA * x + MACARON * swiglu_ffn(x, ffn2_w_up[l], ffn2_w_down[l]), ln_g[l, 2], ln_b[l, 2])
    return x
```

```python
import functools
import math

import jax
import jax.numpy as jnp
from jax import lax
from jax.experimental import pallas as pl
from jax.experimental.pallas import tpu as pltpu

F32 = jnp.float32
BF16 = jnp.bfloat16

D_MODEL = 2048
DEPTH = 2
HEAD_DIM = 64
SB_HEADS = 8
DIL_PATTERNS = ((128, 1), (512, 4), (2048, 16))
DIL_HEADS_PER_GROUP = 4
DIL_HEADS = DIL_HEADS_PER_GROUP * len(DIL_PATTERNS)
SSM_WIDTH = 512
SSM_GROUP = 16
SSM_GROUPS = SSM_WIDTH // SSM_GROUP
SSM_STATE = 64
DSA_HEADS = 8
IDX_HEADS = 8
IDX_DIM = 64
DSA_TOPK = 256
N_BRANCH = 4
D_FF = 5632
REL_BUCKETS = 32
REL_MAX_DIST = 128
DN_ALPHA = (2.0 * DEPTH) ** 0.25
LN_EPS = 1e-5
NEG_INF = -1e30
MACARON = 0.5

IN_SPLITS = (3 * SB_HEADS * HEAD_DIM, 3 * DIL_HEADS * HEAD_DIM, SSM_WIDTH, 3 * DSA_HEADS * HEAD_DIM,
             IDX_HEADS * IDX_DIM, IDX_DIM, IDX_HEADS, N_BRANCH * D_MODEL)

LANES = 128
ATT_BLOCK = 128
SB_BLOCK = 256
SSM_CHUNK = 256
VMEM_LIMIT = 56 * 1024 * 1024
INT_MIN = -2 ** 31

_NT = (((1,), (1,)), ((), ()))


def _dot(a, b):
    return jnp.dot(a, b, preferred_element_type=F32)


def _dot_nt(a, b):
    return lax.dot_general(a, b, _NT, preferred_element_type=F32)


def _params(*sem):
    return pltpu.CompilerParams(dimension_semantics=sem, vmem_limit_bytes=VMEM_LIMIT)


def _sigmoid(x):
    return 1.0 / (1.0 + jnp.exp(-x))


def _mm_kernel(x_ref, w_ref, o_ref):
    o_ref[...] = _dot(x_ref[...], w_ref[...]).astype(o_ref.dtype)


def matmul(x, w, out_dtype, tm=1024, tn=512):
    m, k = x.shape
    n = w.shape[1]
    tn = min(tn, n)
    assert m % tm == 0 and n % tn == 0
    return pl.pallas_call(
        _mm_kernel,
        grid=(m // tm, n // tn),
        in_specs=[pl.BlockSpec((tm, k), lambda i, j: (i, 0)),
                  pl.BlockSpec((k, tn), lambda i, j: (0, j))],
        out_specs=pl.BlockSpec((tm, tn), lambda i, j: (i, j)),
        out_shape=jax.ShapeDtypeStruct((m, n), out_dtype),
        compiler_params=_params("parallel", "arbitrary"),
        name="matmul",
    )(x, w)


def _ffn_up_kernel(x_ref, wa_ref, wb_ref, o_ref):
    x = x_ref[...]
    a = _dot(x, wa_ref[...])
    b = _dot(x, wb_ref[...])
    o_ref[...] = (a * _sigmoid(a) * b).astype(o_ref.dtype)


def ffn_up(x, w_up, tm=1024, tn=512):
    m, k = x.shape
    f = w_up.shape[1] // 2
    assert m % tm == 0 and f % tn == 0
    nb = f // tn
    return pl.pallas_call(
        _ffn_up_kernel,
        grid=(m // tm, nb),
        in_specs=[pl.BlockSpec((tm, k), lambda i, j: (i, 0)),
                  pl.BlockSpec((k, tn), lambda i, j: (0, j)),
                  pl.BlockSpec((k, tn), lambda i, j: (0, j + nb))],
        out_specs=pl.BlockSpec((tm, tn), lambda i, j: (i, j)),
        out_shape=jax.ShapeDtypeStruct((m, f), BF16),
        compiler_params=_params("parallel", "arbitrary"),
        name="ffn_up",
    )(x, w_up, w_up)


def _mm_res_ln_kernel(h_ref, w_ref, x_ref, g_ref, b_ref, o_ref, obf_ref, acc_ref, *, scale, nk):
    kk = pl.program_id(1)

    @pl.when(kk == 0)
    def _():
        acc_ref[...] = jnp.zeros_like(acc_ref)

    acc_ref[...] += _dot(h_ref[...], w_ref[...])

    @pl.when(kk == nk - 1)
    def _():
        y = DN_ALPHA * x_ref[...] + scale * acc_ref[...]
        mu = jnp.mean(y, axis=-1, keepdims=True)
        yc = y - mu
        var = jnp.mean(yc * yc, axis=-1, keepdims=True)
        out = yc * lax.rsqrt(var + LN_EPS) * g_ref[...] + b_ref[...]
        o_ref[...] = out
        obf_ref[...] = out.astype(BF16)


def matmul_residual_layernorm(h, w, x, g, b, scale, tm=512, tk=None):
    m, k = h.shape
    n = w.shape[1]
    if tk is None:
        tk = k
    assert m % tm == 0 and k % tk == 0
    nk = k // tk
    return pl.pallas_call(
        functools.partial(_mm_res_ln_kernel, scale=scale, nk=nk),
        grid=(m // tm, nk),
        in_specs=[pl.BlockSpec((tm, tk), lambda i, kk: (i, kk)),
                  pl.BlockSpec((tk, n), lambda i, kk: (kk, 0)),
                  pl.BlockSpec((tm, n), lambda i, kk: (i, 0)),
                  pl.BlockSpec((1, n), lambda i, kk: (0, 0)),
                  pl.BlockSpec((1, n), lambda i, kk: (0, 0))],
        out_specs=[pl.BlockSpec((tm, n), lambda i, kk: (i, 0)),
                   pl.BlockSpec((tm, n), lambda i, kk: (i, 0))],
        out_shape=[jax.ShapeDtypeStruct((m, n), F32), jax.ShapeDtypeStruct((m, n), BF16)],
        scratch_shapes=[pltpu.VMEM((tm, n), F32)],
        compiler_params=_params("parallel", "arbitrary"),
        name="matmul_residual_layernorm",
    )(h, w, x, g.reshape(1, n), b.reshape(1, n))


def _branch_merge_kernel(ysb_ref, ydil_ref, yssm_ref, ydsa_ref, wsb_ref, wdil_ref, wssm_ref, wdsa_ref,
                         g0_ref, g1_ref, g2_ref, g3_ref, o_ref):
    acc = _sigmoid(g0_ref[...]) * _dot(ysb_ref[...], wsb_ref[...])
    acc += _sigmoid(g1_ref[...]) * _dot(ydil_ref[...], wdil_ref[...])
    acc += _sigmoid(g2_ref[...]) * _dot(yssm_ref[...], wssm_ref[...])
    acc += _sigmoid(g3_ref[...]) * _dot(ydsa_ref[...], wdsa_ref[...])
    o_ref[...] = acc.astype(o_ref.dtype)


def branch_merge(ys, ws, gate_logits, tm=512, tn=512):
    m = ys[0].shape[0]
    n = ws[0].shape[1]
    nb = n // tn
    y_specs = [pl.BlockSpec((tm, y.shape[1]), lambda i, j: (i, 0)) for y in ys]
    w_specs = [pl.BlockSpec((w.shape[0], tn), lambda i, j: (0, j)) for w in ws]
    g_specs = [pl.BlockSpec((tm, tn), functools.partial(lambda i, j, br: (i, j + br * nb), br=br))
               for br in range(N_BRANCH)]
    return pl.pallas_call(
        _branch_merge_kernel,
        grid=(m // tm, nb),
        in_specs=y_specs + w_specs + g_specs,
        out_specs=pl.BlockSpec((tm, tn), lambda i, j: (i, j)),
        out_shape=jax.ShapeDtypeStruct((m, n), BF16),
        compiler_params=_params("parallel", "arbitrary"),
        name="branch_merge",
    )(*ys, *ws, gate_logits, gate_logits, gate_logits, gate_logits)


def _sb_kernel(q_ref, k_ref, v_ref, o_ref, acc_ref, *, blk, scale):
    i = pl.program_id(2)
    lane = lax.broadcasted_iota(jnp.int32, (1, LANES), 1)
    q2 = q_ref[0]
    zero = jnp.zeros_like(q2)
    q_heads = (jnp.where(lane < HEAD_DIM, q2, zero), jnp.where(lane >= HEAD_DIM, q2, zero))
    r_io = lax.broadcasted_iota(jnp.int32, (blk, blk), 0)
    c_io = lax.broadcasted_iota(jnp.int32, (blk, blk), 1)
    later = jnp.where(r_io > c_io, 1.0, 0.0).astype(BF16)
    acc_ref[...] = jnp.zeros_like(acc_ref)

    def body(jj, carry):
        j = i - jj
        start = pl.multiple_of(j * blk, blk)
        k2 = k_ref[0, pl.ds(start, blk), :]
        v2 = v_ref[0, pl.ds(start, blk), :]
        strict = (c_io + j * blk) < (r_io + i * blk)
        new_carry = []
        for h in range(2):
            z = _dot_nt(q_heads[h], k2) * scale
            sp = jnp.maximum(z, 0.0) + jnp.log1p(jnp.exp(-jnp.abs(z)))
            log_1mb = jnp.where(strict, -sp, 0.0)
            hi = log_1mb.astype(BF16)
            lo = (log_1mb - hi.astype(F32)).astype(BF16)
            suffix = _dot(hi, later) + _dot(lo, later) + carry[h]
            att = jnp.where(strict, jnp.exp(z - sp + suffix), 0.0)
            acc_ref[h] += _dot(att.astype(BF16), v2)
            new_carry.append(carry[h] + jnp.sum(log_1mb, axis=-1, keepdims=True))
        return tuple(new_carry)

    zcol = jnp.zeros((blk, 1), F32)
    lax.fori_loop(0, i + 1, body, (zcol, zcol))
    o_ref[0] = jnp.where(lane < HEAD_DIM, acc_ref[0], acc_ref[1]).astype(o_ref.dtype)


def stick_breaking_attention(qkv, blk=SB_BLOCK):
    bsz, seq, _ = qkv.shape
    width = SB_HEADS * HEAD_DIM
    npair = width // LANES
    assert seq % blk == 0
    return pl.pallas_call(
        functools.partial(_sb_kernel, blk=blk, scale=HEAD_DIM ** -0.5),
        grid=(bsz, npair, seq // blk),
        in_specs=[pl.BlockSpec((1, blk, LANES), lambda b, p, i: (b, i, p)),
                  pl.BlockSpec((1, seq, LANES), lambda b, p, i: (b, 0, npair + p)),
                  pl.BlockSpec((1, seq, LANES), lambda b, p, i: (b, 0, 2 * npair + p))],
        out_specs=pl.BlockSpec((1, blk, LANES), lambda b, p, i: (b, i, p)),
        out_shape=jax.ShapeDtypeStruct((bsz, seq, width), BF16),
        scratch_shapes=[pltpu.VMEM((2, blk, LANES), F32)],
        compiler_params=_params("parallel", "parallel", "arbitrary"),
        name="stick_breaking_attention",
    )(qkv, qkv, qkv)


def _t5_bucket(dist):
    max_exact = REL_BUCKETS // 2
    d = jnp.maximum(dist, 1).astype(F32)
    large = max_exact + (jnp.log(d / max_exact) / math.log(REL_MAX_DIST / max_exact)
                         * (REL_BUCKETS - max_exact)).astype(jnp.int32)
    large = jnp.minimum(large, REL_BUCKETS - 1)
    return jnp.where(dist < max_exact, dist, large)


def _dilated_bias_tiles(rel_bias_group, dil, blk):
    a = jnp.arange(blk)[:, None]
    cc = jnp.arange(2 * blk)[None, :]
    step = blk + a - cc
    valid = (step >= 0) & (step <= blk)
    bias = rel_bias_group[_t5_bucket(dil * jnp.clip(step, 0, blk))]
    tile = jnp.where(valid[..., None], bias.astype(F32), NEG_INF)
    return jnp.transpose(tile, (2, 0, 1))


def _dsa_bias_tiles(rel_bias_dsa, blk):
    a = jnp.arange(blk)[:, None]
    c = jnp.arange(blk)[None, :]
    own = jnp.where((a - c >= 0)[..., None], rel_bias_dsa[_t5_bucket(jnp.maximum(a - c, 0))].astype(F32), NEG_INF)
    prev = rel_bias_dsa[_t5_bucket(blk + a - c)].astype(F32)
    assert REL_BUCKETS // 2 + int(math.log((blk + 1) / (REL_BUCKETS // 2)) / math.log(REL_MAX_DIST / (REL_BUCKETS // 2))
                                  * (REL_BUCKETS - REL_BUCKETS // 2)) >= REL_BUCKETS - 1
    far = jnp.broadcast_to(rel_bias_dsa[REL_BUCKETS - 1].astype(F32), prev.shape)
    return jnp.transpose(jnp.stack([own, prev, far]), (3, 0, 1, 2))


def _dil_kernel(q_ref, k_ref, v_ref, bias_ref, o_ref, lse_ref, *, blk, scale):
    i = pl.program_id(2)
    width = DIL_HEADS_PER_GROUP * HEAD_DIM
    lane = lax.broadcasted_iota(jnp.int32, (1, width), 1)
    q = q_ref[0]
    zero = jnp.zeros_like(q)
    cur0 = pl.multiple_of(i * blk, blk)
    prev0 = pl.multiple_of(jnp.maximum(i - 1, 0) * blk, blk)
    kc = k_ref[0, pl.ds(cur0, blk), :]
    kp = k_ref[0, pl.ds(prev0, blk), :]
    vc = v_ref[0, pl.ds(cur0, blk), :]
    vp = v_ref[0, pl.ds(prev0, blk), :]
    has_prev = i > 0
    out = jnp.zeros((blk, width), F32)
    lse_b = jnp.zeros((blk, width), F32)
    for h in range(DIL_HEADS_PER_GROUP):
        head = (lane >= h * HEAD_DIM) & (lane < (h + 1) * HEAD_DIM)
        qm = jnp.where(head, q, zero)
        s_prev = _dot_nt(qm, kp) * scale + bias_ref[h, :, :blk]
        s_prev = jnp.where(has_prev, s_prev, NEG_INF)
        s_cur = _dot_nt(qm, kc) * scale + bias_ref[h, :, blk:]
        m = jnp.maximum(jnp.max(s_prev, axis=-1, keepdims=True), jnp.max(s_cur, axis=-1, keepdims=True))
        p_prev = jnp.exp(s_prev - m)
        p_cur = jnp.exp(s_cur - m)
        denom = jnp.sum(p_prev, axis=-1, keepdims=True) + jnp.sum(p_cur, axis=-1, keepdims=True)
        o = (_dot(p_prev.astype(BF16), vp) + _dot(p_cur.astype(BF16), vc)) / denom
        out = jnp.where(head, o, out)
        lse_b = jnp.where(head, m + jnp.log(denom), lse_b)
    o_ref[0] = out
    lse_ref[0] = lse_b


def dilated_group_attention(qkv, bias_tiles, group, dil, blk=ATT_BLOCK):
    bsz, seq, cols = qkv.shape
    width = DIL_HEADS_PER_GROUP * HEAD_DIM
    per_row = cols // width
    ngroups = len(DIL_PATTERNS)
    n = seq // dil
    assert n % blk == 0
    x = qkv.reshape(bsz, n, dil * cols)
    q_map = lambda b, c, i: (b, i, c * per_row + group)
    k_map = lambda b, c, i: (b, 0, c * per_row + ngroups + group)
    v_map = lambda b, c, i: (b, 0, c * per_row + 2 * ngroups + group)
    out, lse = pl.pallas_call(
        functools.partial(_dil_kernel, blk=blk, scale=HEAD_DIM ** -0.5),
        grid=(bsz, dil, n // blk),
        in_specs=[pl.BlockSpec((1, blk, width), q_map),
                  pl.BlockSpec((1, n, width), k_map),
                  pl.BlockSpec((1, n, width), v_map),
                  pl.BlockSpec((DIL_HEADS_PER_GROUP, blk, 2 * blk), lambda b, c, i: (0, 0, 0))],
        out_specs=[pl.BlockSpec((1, blk, width), lambda b, c, i: (b, i, c)),
                   pl.BlockSpec((1, blk, width), lambda b, c, i: (b, i, c))],
        out_shape=[jax.ShapeDtypeStruct((bsz, n, dil * width), F32),
                   jax.ShapeDtypeStruct((bsz, n, dil * width), F32)],
        compiler_params=_params("parallel", "parallel", "arbitrary"),
        name=f"dilated_attention_g{group}",
    )(x, x, x, bias_tiles)
    return out.reshape(bsz * seq, width), lse.reshape(bsz * seq, width)


def _dil_merge_kernel(o0_ref, o1_ref, o2_ref, l0_ref, l1_ref, l2_ref, y_ref):
    l0, l1, l2 = l0_ref[...], l1_ref[...], l2_ref[...]
    m = jnp.maximum(jnp.maximum(l0, l1), l2)
    w0, w1, w2 = jnp.exp(l0 - m), jnp.exp(l1 - m), jnp.exp(l2 - m)
    y = (w0 * o0_ref[...] + w1 * o1_ref[...] + w2 * o2_ref[...]) / (w0 + w1 + w2)
    y_ref[...] = y.astype(y_ref.dtype)


def dilated_merge(outs, lses, tm=1024):
    m, width = outs[0].shape
    spec = pl.BlockSpec((tm, width), lambda i: (i, 0))
    return pl.pallas_call(
        _dil_merge_kernel,
        grid=(m // tm,),
        in_specs=[spec] * 6,
        out_specs=spec,
        out_shape=jax.ShapeDtypeStruct((m, width), BF16),
        compiler_params=_params("parallel"),
        name="dilated_merge",
    )(*outs, *lses)


def _ssm_kernel(u_ref, bb_ref, apow_ref, cc_ref, d_ref, wg_ref, o_ref, bu_ref, carry_ref, *, chunk, nstrips):
    @pl.when(pl.program_id(1) == 0)
    def _():
        carry_ref[...] = jnp.zeros_like(carry_ref)

    u = u_ref[0]
    bu = _dot(u.astype(BF16), bb_ref[...])
    for s in range(2 * nstrips):
        bu_ref[s] = bu[:, s * LANES:(s + 1) * LANES]
    row = lax.broadcasted_iota(jnp.int32, (chunk, LANES), 0)
    nsteps = chunk.bit_length() - 1

    def strip(s, y):
        xr = bu_ref[s]
        xi = bu_ref[nstrips + s]
        pr = apow_ref[0, s]
        pi = apow_ref[1, s]
        for kstep in range(nsteps):
            sh = 1 << kstep
            ar = pr[sh - 1:sh, :]
            ai = pi[sh - 1:sh, :]
            sxr = jnp.where(row >= sh, pltpu.roll(xr, sh, 0), 0.0)
            sxi = jnp.where(row >= sh, pltpu.roll(xi, sh, 0), 0.0)
            xr, xi = xr + ar * sxr - ai * sxi, xi + ar * sxi + ai * sxr
        cr = carry_ref[0, s][0:1, :]
        ci = carry_ref[1, s][0:1, :]
        xr, xi = xr + pr * cr - pi * ci, xi + pr * ci + pi * cr
        carry_ref[0, s] = jnp.broadcast_to(xr[chunk - 1:chunk, :], (8, LANES))
        carry_ref[1, s] = jnp.broadcast_to(xi[chunk - 1:chunk, :], (8, LANES))
        return y + _dot(xr.astype(BF16), cc_ref[s]) + _dot(xi.astype(BF16), cc_ref[nstrips + s])

    y = lax.fori_loop(0, nstrips, strip, jnp.zeros(u.shape, F32))
    y = y + d_ref[...] * u
    y = 0.5 * y * (1.0 + jnp.tanh(math.sqrt(2.0 / math.pi) * (y + 0.044715 * (y * y * y))))
    z = _dot(y.astype(BF16), wg_ref[...])
    width = u.shape[1]
    o_ref[0] = (z[:, :width] * _sigmoid(z[:, width:])).astype(o_ref.dtype)


def _block_diag(blocks):
    g, r, c = blocks.shape
    eye = jnp.eye(g, dtype=blocks.dtype)
    return (eye[:, None, :, None] * blocks[:, :, None, :]).reshape(g * r, g * c)


def _ssm_tables(lam_re, lam_im, log_dt, b_re, b_im, c_re, c_im, chunk):
    lr, li = lam_re.astype(F32), lam_im.astype(F32)
    dt = jnp.exp(log_dt.astype(F32))[:, None]
    mag = jnp.exp(lr * dt)
    a_re, a_im = mag * jnp.cos(li * dt), mag * jnp.sin(li * dt)
    den = lr * lr + li * li
    f_re = ((a_re - 1.0) * lr + a_im * li) / den
    f_im = (a_im * lr - (a_re - 1.0) * li) / den
    br, bi = b_re.astype(F32), b_im.astype(F32)
    bb_re = f_re[..., None] * br - f_im[..., None] * bi
    bb_im = f_re[..., None] * bi + f_im[..., None] * br
    bb = jnp.concatenate([_block_diag(jnp.transpose(bb_re, (0, 2, 1))),
                          _block_diag(jnp.transpose(bb_im, (0, 2, 1)))], axis=1)
    cc = jnp.concatenate([_block_diag(jnp.transpose(c_re.astype(F32), (0, 2, 1))),
                          -_block_diag(jnp.transpose(c_im.astype(F32), (0, 2, 1)))], axis=0)
    nstates = SSM_GROUPS * SSM_STATE

    def cmul(x, y):
        return x[0] * y[0] - x[1] * y[1], x[0] * y[1] + x[1] * y[0]

    ar = jnp.broadcast_to(a_re.reshape(1, nstates), (chunk, nstates))
    ai = jnp.broadcast_to(a_im.reshape(1, nstates), (chunk, nstates))
    pw_re, pw_im = lax.associative_scan(cmul, (ar, ai), axis=0)
    nstrips = nstates // LANES
    apow = jnp.stack([pw_re, pw_im]).reshape(2, chunk, nstrips, LANES).transpose(0, 2, 1, 3)
    return bb.astype(BF16), apow, cc.astype(BF16).reshape(2 * nstrips, LANES, SSM_WIDTH)


def s5_glu(u, bb, apow, cc, d_skip, w_glu, chunk=SSM_CHUNK):
    bsz, seq, width = u.shape
    nstrips = apow.shape[1]
    assert seq % chunk == 0 and chunk & (chunk - 1) == 0
    const = lambda *shape: pl.BlockSpec(shape, lambda b, c: (0,) * len(shape))
    return pl.pallas_call(
        functools.partial(_ssm_kernel, chunk=chunk, nstrips=nstrips),
        grid=(bsz, seq // chunk),
        in_specs=[pl.BlockSpec((1, chunk, width), lambda b, c: (b, c, 0)),
                  const(*bb.shape), const(*apow.shape), const(*cc.shape),
                  const(1, width), const(*w_glu.shape)],
        out_specs=pl.BlockSpec((1, chunk, width), lambda b, c: (b, c, 0)),
        out_shape=jax.ShapeDtypeStruct((bsz, seq, width), BF16),
        scratch_shapes=[pltpu.VMEM((2 * nstrips, chunk, LANES), F32),
                        pltpu.VMEM((2, nstrips, 8, LANES), F32)],
        compiler_params=_params("parallel", "arbitrary"),
        name="s5_glu",
    )(u, bb, apow, cc, d_skip.reshape(1, width).astype(F32), w_glu)


def _dsa_kernel(q_ref, k_ref, v_ref, iq_ref, ik_ref, iw_ref, bias_ref, o_ref,
                key_ref, madd_ref, m_ref, l_ref, acc_ref, *, blk, topk, scale):
    i = pl.program_id(1)
    nblk = i + 1
    lane = lax.broadcasted_iota(jnp.int32, (1, LANES), 1)
    low, high = lane < HEAD_DIM, lane >= HEAD_DIM
    r_io = lax.broadcasted_iota(jnp.int32, (blk, blk), 0)
    c_io = lax.broadcasted_iota(jnp.int32, (blk, blk), 1)

    def head_slices(x):
        parts = []
        for h in range(DSA_HEADS):
            pair = x[:, (h // 2) * LANES:(h // 2 + 1) * LANES]
            parts.append(jnp.where(low if h % 2 == 0 else high, pair, jnp.zeros_like(pair)))
        return parts

    iq_heads = head_slices(iq_ref[0])
    iw = iw_ref[0]
    w_cols = [iw[:, h:h + 1] for h in range(IDX_HEADS)]

    def score_block(j, _):
        start = pl.multiple_of(j * blk, blk)
        kk = ik_ref[0, pl.ds(start, blk), :].astype(BF16)
        sc = jnp.zeros((blk, blk), F32)
        for h in range(IDX_HEADS):
            sc = sc + w_cols[h] * jnp.maximum(_dot_nt(iq_heads[h], kk), 0.0)
        sc = jnp.where((c_io + j * blk) <= (r_io + i * blk), sc, NEG_INF)
        bits = pltpu.bitcast(sc, jnp.int32)
        key_ref[j] = bits ^ ((bits >> 31) & 0x7FFFFFFF)
        return 0

    lax.fori_loop(0, nblk, score_block, 0)

    def count(pred):
        def body(j, acc):
            return acc + jnp.where(pred(key_ref[j]), 1.0, 0.0)
        return jnp.sum(lax.fori_loop(0, nblk, body, jnp.zeros((blk, blk), F32)), axis=-1, keepdims=True)

    def bit_step(b, thr):
        cand = thr + lax.shift_left(jnp.int32(1), 31 - b)
        return jnp.where(count(lambda key: key >= cand) >= topk, cand, thr)

    thr = lax.fori_loop(0, 32, bit_step, jnp.full((blk, 1), INT_MIN, jnp.int32))

    need = topk - count(lambda key: key > thr)
    upto = jnp.where(r_io <= c_io, 1.0, 0.0).astype(BF16)

    def select_block(j, run):
        key = key_ref[j]
        tie = jnp.where(key == thr, 1.0, 0.0)
        rank = _dot(tie.astype(BF16), upto) + run
        keep_tie = jnp.where(rank <= need, 0.0, NEG_INF)
        madd_ref[j] = jnp.where(key > thr, 0.0, jnp.where(key == thr, keep_tie, NEG_INF))
        return run + jnp.sum(tie, axis=-1, keepdims=True)

    lax.fori_loop(0, nblk, select_block, jnp.zeros((blk, 1), F32))

    q_heads = head_slices(q_ref[0])
    m_ref[...] = jnp.full(m_ref.shape, NEG_INF, F32)
    l_ref[...] = jnp.zeros_like(l_ref)
    acc_ref[...] = jnp.zeros_like(acc_ref)

    def attend_block(j, _):
        start = pl.multiple_of(j * blk, blk)
        kblk = k_ref[0, pl.ds(start, blk), :]
        vblk = v_ref[0, pl.ds(start, blk), :]
        madd = madd_ref[j]
        which = jnp.minimum(i - j, 2)
        for h in range(DSA_HEADS):
            g = h // 2
            s = _dot_nt(q_heads[h], kblk[:, g * LANES:(g + 1) * LANES]) * scale + bias_ref[h, which] + madd
            m_old = m_ref[h]
            m_new = jnp.maximum(m_old, jnp.max(s, axis=-1, keepdims=True))
            p = jnp.exp(s - m_new)
            alpha = jnp.exp(m_old - m_new)
            l_ref[h] = alpha * l_ref[h] + jnp.sum(p, axis=-1, keepdims=True)
            acc_ref[h] = alpha * acc_ref[h] + _dot(p.astype(BF16), vblk[:, g * LANES:(g + 1) * LANES])
            m_ref[h] = m_new
        return 0

    lax.fori_loop(0, nblk, attend_block, 0)
    for g in range(DSA_HEADS // 2):
        even = acc_ref[2 * g] / l_ref[2 * g]
        odd = acc_ref[2 * g + 1] / l_ref[2 * g + 1]
        o_ref[0, :, g * LANES:(g + 1) * LANES] = jnp.where(low, even, odd).astype(o_ref.dtype)


def dsa_attention(qkv, idx_q, idx_kw, bias_tiles, blk=ATT_BLOCK):
    bsz, seq, _ = qkv.shape
    width = DSA_HEADS * HEAD_DIM
    nblk = seq // blk
    topk = min(DSA_TOPK, seq // 4)
    return pl.pallas_call(
        functools.partial(_dsa_kernel, blk=blk, topk=topk, scale=HEAD_DIM ** -0.5),
        grid=(bsz, nblk),
        in_specs=[pl.BlockSpec((1, blk, width), lambda b, i: (b, i, 0)),
                  pl.BlockSpec((1, seq, width), lambda b, i: (b, 0, 1)),
                  pl.BlockSpec((1, seq, width), lambda b, i: (b, 0, 2)),
                  pl.BlockSpec((1, blk, IDX_HEADS * IDX_DIM), lambda b, i: (b, i, 0)),
                  pl.BlockSpec((1, seq, LANES), lambda b, i: (b, 0, 0)),
                  pl.BlockSpec((1, blk, LANES), lambda b, i: (b, i, 1)),
                  pl.BlockSpec((DSA_HEADS, 3, blk, blk), lambda b, i: (0, 0, 0, 0))],
        out_specs=pl.BlockSpec((1, blk, width), lambda b, i: (b, i, 0)),
        out_shape=jax.ShapeDtypeStruct((bsz, seq, width), BF16),
        scratch_shapes=[pltpu.VMEM((nblk, blk, blk), jnp.int32),
                        pltpu.VMEM((nblk, blk, blk), F32),
                        pltpu.VMEM((DSA_HEADS, blk, 1), F32),
                        pltpu.VMEM((DSA_HEADS, blk, 1), F32),
                        pltpu.VMEM((DSA_HEADS, blk, LANES), F32)],
        compiler_params=_params("parallel", "arbitrary"),
        name="dsa_attention",
    )(qkv, qkv, qkv, idx_q, idx_kw, idx_kw, bias_tiles)


def hybrid_mixer(xbf, bsz, seq, w_in, rel_bias, ssm_params, d_skip, w_glu, w_branches):
    offs = [0]
    for width in IN_SPLITS:
        offs.append(offs[-1] + width)
    col = lambda a, b: w_in[:, offs[a]:offs[b]].astype(BF16)
    w_idx_k = w_in[:, offs[5]:offs[6]]
    w_idx_w = w_in[:, offs[6]:offs[7]]
    w_small = jnp.concatenate([w_idx_k, w_idx_k, w_idx_w,
                               jnp.zeros((w_in.shape[0], LANES - IDX_HEADS), w_in.dtype)], axis=1).astype(BF16)

    sb_qkv = matmul(xbf, col(0, 1), BF16).reshape(bsz, seq, -1)
    dil_qkv = matmul(xbf, col(1, 2), BF16, tn=768).reshape(bsz, seq, -1)
    ssm_u = matmul(xbf, col(2, 3), F32).reshape(bsz, seq, -1)
    dsa_qkv = matmul(xbf, col(3, 4), BF16).reshape(bsz, seq, -1)
    idx_q = matmul(xbf, col(4, 5), BF16).reshape(bsz, seq, -1)
    idx_kw = matmul(xbf, w_small, F32, tn=2 * LANES).reshape(bsz, seq, -1)
    gate_logits = matmul(xbf, col(7, 8), F32)

    y_sb = stick_breaking_attention(sb_qkv).reshape(bsz * seq, -1)

    outs, lses = [], []
    for g, (window, dil) in enumerate(DIL_PATTERNS):
        assert window // dil == ATT_BLOCK
        tiles = _dilated_bias_tiles(rel_bias[:, g * DIL_HEADS_PER_GROUP:(g + 1) * DIL_HEADS_PER_GROUP], dil, ATT_BLOCK)
        o, lse = dilated_group_attention(dil_qkv, tiles, g, dil)
        outs.append(o)
        lses.append(lse)
    y_dil = dilated_merge(outs, lses)

    bb, apow, cc = _ssm_tables(*ssm_params, SSM_CHUNK)
    y_ssm = s5_glu(ssm_u, bb, apow, cc, d_skip, w_glu.astype(BF16)).reshape(bsz * seq, -1)

    y_dsa = dsa_attention(dsa_qkv, idx_q, idx_kw, _dsa_bias_tiles(rel_bias[:, DIL_HEADS:], ATT_BLOCK))
    y_dsa = y_dsa.reshape(bsz * seq, -1)

    return branch_merge([y_sb, y_dil, y_ssm, y_dsa], [w.astype(BF16) for w in w_branches], gate_logits)


def kernel(x, ln_g, ln_b, ffn1_w_up, ffn1_w_down, w_in, rel_bias, ssm_lam_re, ssm_lam_im, ssm_log_dt,
           ssm_b_re, ssm_b_im, ssm_c_re, ssm_c_im, ssm_d, ssm_w_glu, w_br_sb, w_br_dil, w_br_ssm, w_br_dsa,
           w_out, ffn2_w_up, ffn2_w_down):
    bsz, seq, d = x.shape
    xf = x.reshape(bsz * seq, d)
    xbf = xf.astype(BF16)
    ffn_tk = D_FF // 4
    for l in range(DEPTH):
        h = ffn_up(xbf, ffn1_w_up[l].astype(BF16))
        xf, xbf = matmul_residual_layernorm(h, ffn1_w_down[l].astype(BF16), xf, ln_g[l, 0], ln_b[l, 0],
                                            MACARON, tk=ffn_tk)
        merged = hybrid_mixer(xbf, bsz, seq, w_in[l], rel_bias,
                              (ssm_lam_re[l], ssm_lam_im[l], ssm_log_dt[l], ssm_b_re[l], ssm_b_im[l],
                               ssm_c_re[l], ssm_c_im[l]), ssm_d[l], ssm_w_glu[l],
                              (w_br_sb[l], w_br_dil[l], w_br_ssm[l], w_br_dsa[l]))
        xf, xbf = matmul_residual_layernorm(merged, w_out[l].astype(BF16), xf, ln_g[l, 1], ln_b[l, 1], 1.0)
        h = ffn_up(xbf, ffn2_w_up[l].astype(BF16))
        xf, xbf = matmul_residual_layernorm(h, ffn2_w_down[l].astype(BF16), xf, ln_g[l, 2], ln_b[l, 2],
                                            MACARON, tk=ffn_tk)
    return xf.reshape(bsz, seq, d)
```

```python
import functools
import math

import jax
import jax.numpy as jnp
from jax import lax
from jax.experimental import pallas as pl
from jax.experimental.pallas import tpu as pltpu

F32 = jnp.float32
BF16 = jnp.bfloat16

D_MODEL = 2048
DEPTH = 2
HEAD_DIM = 64
SB_HEADS = 8
DIL_PATTERNS = ((128, 1), (512, 4), (2048, 16))
DIL_HEADS_PER_GROUP = 4
DIL_HEADS = DIL_HEADS_PER_GROUP * len(DIL_PATTERNS)
SSM_WIDTH = 512
SSM_GROUP = 16
SSM_GROUPS = SSM_WIDTH // SSM_GROUP
SSM_STATE = 64
DSA_HEADS = 8
IDX_HEADS = 8
IDX_DIM = 64
DSA_TOPK = 256
N_BRANCH = 4
D_FF = 5632
REL_BUCKETS = 32
REL_MAX_DIST = 128
DN_ALPHA = (2.0 * DEPTH) ** 0.25
LN_EPS = 1e-5
NEG_INF = -1e30
MACARON = 0.5

IN_SPLITS = (3 * SB_HEADS * HEAD_DIM, 3 * DIL_HEADS * HEAD_DIM, SSM_WIDTH, 3 * DSA_HEADS * HEAD_DIM,
             IDX_HEADS * IDX_DIM, IDX_DIM, IDX_HEADS, N_BRANCH * D_MODEL)

LANES = 128
ATT_BLOCK = 128
SB_BLOCK = 256
SSM_CHUNK = 256
VMEM_LIMIT = 56 * 1024 * 1024
INT_MIN = -2 ** 31

_NT = (((1,), (1,)), ((), ()))


def _dot(a, b):
    return jnp.dot(a, b, preferred_element_type=F32)


def _dot_nt(a, b):
    return lax.dot_general(a, b, _NT, preferred_element_type=F32)


def _params(*sem):
    return pltpu.CompilerParams(dimension_semantics=sem, vmem_limit_bytes=VMEM_LIMIT)


def _sigmoid(x):
    return 1.0 / (1.0 + jnp.exp(-x))


def _mm_kernel(x_ref, w_ref, o_ref):
    o_ref[...] = _dot(x_ref[...], w_ref[...]).astype(o_ref.dtype)


def matmul(x, w, out_dtype, tm=1024, tn=512):
    m, k = x.shape
    n = w.shape[1]
    tn = min(tn, n)
    assert m % tm == 0 and n % tn == 0
    return pl.pallas_call(
        _mm_kernel,
        grid=(m // tm, n // tn),
        in_specs=[pl.BlockSpec((tm, k), lambda i, j: (i, 0)),
                  pl.BlockSpec((k, tn), lambda i, j: (0, j))],
        out_specs=pl.BlockSpec((tm, tn), lambda i, j: (i, j)),
        out_shape=jax.ShapeDtypeStruct((m, n), out_dtype),
        compiler_params=_params("parallel", "arbitrary"),
        name="matmul",
    )(x, w)


def _ffn_up_kernel(x_ref, wa_ref, wb_ref, o_ref):
    x = x_ref[...]
    a = _dot(x, wa_ref[...])
    b = _dot(x, wb_ref[...])
    o_ref[...] = (a * _sigmoid(a) * b).astype(o_ref.dtype)


def ffn_up(x, w_up, tm=1024, tn=512):
    m, k = x.shape
    f = w_up.shape[1] // 2
    assert m % tm == 0 and f % tn == 0
    nb = f // tn
    return pl.pallas_call(
        _ffn_up_kernel,
        grid=(m // tm, nb),
        in_specs=[pl.BlockSpec((tm, k), lambda i, j: (i, 0)),
                  pl.BlockSpec((k, tn), lambda i, j: (0, j)),
                  pl.BlockSpec((k, tn), lambda i, j: (0, j + nb))],
        out_specs=pl.BlockSpec((tm, tn), lambda i, j: (i, j)),
        out_shape=jax.ShapeDtypeStruct((m, f), BF16),
        compiler_params=_params("parallel", "arbitrary"),
        name="ffn_up",
    )(x, w_up, w_up)


def _mm_res_ln_kernel(h_ref, w_ref, x_ref, g_ref, b_ref, o_ref, obf_ref, acc_ref, *, scale, nk):
    kk = pl.program_id(1)

    @pl.when(kk == 0)
    def _():
        acc_ref[...] = jnp.zeros_like(acc_ref)

    acc_ref[...] += _dot(h_ref[...], w_ref[...])

    @pl.when(kk == nk - 1)
    def _():
        y = DN_ALPHA * x_ref[...] + scale * acc_ref[...]
        mu = jnp.mean(y, axis=-1, keepdims=True)
        yc = y - mu
        var = jnp.mean(yc * yc, axis=-1, keepdims=True)
        out = yc * lax.rsqrt(var + LN_EPS) * g_ref[...] + b_ref[...]
        o_ref[...] = out
        obf_ref[...] = out.astype(BF16)


def matmul_residual_layernorm(h, w, x, g, b, scale, tm=512, tk=None):
    m, k = h.shape
    n = w.shape[1]
    if tk is None:
        tk = k
    assert m % tm == 0 and k % tk == 0
    nk = k // tk
    return pl.pallas_call(
        functools.partial(_mm_res_ln_kernel, scale=scale, nk=nk),
        grid=(m // tm, nk),
        in_specs=[pl.BlockSpec((tm, tk), lambda i, kk: (i, kk)),
                  pl.BlockSpec((tk, n), lambda i, kk: (kk, 0)),
                  pl.BlockSpec((tm, n), lambda i, kk: (i, 0)),
                  pl.BlockSpec((1, n), lambda i, kk: (0, 0)),
                  pl.BlockSpec((1, n), lambda i, kk: (0, 0))],
        out_specs=[pl.BlockSpec((tm, n), lambda i, kk: (i, 0)),
                   pl.BlockSpec((tm, n), lambda i, kk: (i, 0))],
        out_shape=[jax.ShapeDtypeStruct((m, n), F32), jax.ShapeDtypeStruct((m, n), BF16)],
        scratch_shapes=[pltpu.VMEM((tm, n), F32)],
        compiler_params=_params("parallel", "arbitrary"),
        name="matmul_residual_layernorm",
    )(h, w, x, g.reshape(1, n), b.reshape(1, n))


def _branch_merge_kernel(ysb_ref, ydil_ref, yssm_ref, ydsa_ref, wsb_ref, wdil_ref, wssm_ref, wdsa_ref,
                         g0_ref, g1_ref, g2_ref, g3_ref, o_ref):
    acc = _sigmoid(g0_ref[...]) * _dot(ysb_ref[...], wsb_ref[...])
    acc += _sigmoid(g1_ref[...]) * _dot(ydil_ref[...], wdil_ref[...])
    acc += _sigmoid(g2_ref[...]) * _dot(yssm_ref[...], wssm_ref[...])
    acc += _sigmoid(g3_ref[...]) * _dot(ydsa_ref[...], wdsa_ref[...])
    o_ref[...] = acc.astype(o_ref.dtype)


def branch_merge(ys, ws, gate_logits, tm=512, tn=512):
    m = ys[0].shape[0]
    n = ws[0].shape[1]
    nb = n // tn
    y_specs = [pl.BlockSpec((tm, y.shape[1]), lambda i, j: (i, 0)) for y in ys]
    w_specs = [pl.BlockSpec((w.shape[0], tn), lambda i, j: (0, j)) for w in ws]
    g_specs = [pl.BlockSpec((tm, tn), functools.partial(lambda i, j, br: (i, j + br * nb), br=br))
               for br in range(N_BRANCH)]
    return pl.pallas_call(
        _branch_merge_kernel,
        grid=(m // tm, nb),
        in_specs=y_specs + w_specs + g_specs,
        out_specs=pl.BlockSpec((tm, tn), lambda i, j: (i, j)),
        out_shape=jax.ShapeDtypeStruct((m, n), BF16),
        compiler_params=_params("parallel", "arbitrary"),
        name="branch_merge",
    )(*ys, *ws, gate_logits, gate_logits, gate_logits, gate_logits)


def _blocked_transpose(x, blk):
    bsz, seq, c = x.shape
    return jnp.swapaxes(x.reshape(bsz, seq // blk, blk, c), 2, 3)


def _is_power_of_two(x):
    return math.frexp(x)[0] == 0.5


def _sb_kernel(qT_ref, k_ref, vT_ref, o_ref, acc_ref, *, blk, scale):
    i = pl.program_id(2)
    fold_scale = _is_power_of_two(scale)
    qT = qT_ref[0, 0]
    if fold_scale:
        qT = qT * jnp.asarray(scale, qT.dtype)
    row = lax.broadcasted_iota(jnp.int32, (LANES, 1), 0)
    zero = jnp.zeros_like(qT)
    q_rhs = jnp.concatenate([jnp.where(row < HEAD_DIM, qT, zero), jnp.where(row >= HEAD_DIM, qT, zero)], axis=1)
    key_io = lax.broadcasted_iota(jnp.int32, (blk, blk), 0)
    qry_io = lax.broadcasted_iota(jnp.int32, (blk, blk), 1)
    later = jnp.where(qry_io > key_io, 1.0, 0.0).astype(BF16)
    acc_ref[...] = jnp.zeros_like(acc_ref)

    def block(j, carry, diagonal):
        start = pl.multiple_of(j * blk, blk)
        z2 = _dot(k_ref[0, pl.ds(start, blk), :], q_rhs)
        vT = vT_ref[0, j]
        new_carry = []
        for h in range(2):
            z = z2[:, h * blk:(h + 1) * blk]
            if not fold_scale:
                z = z * scale
            sp = jnp.maximum(z, 0.0) + jnp.log(1.0 + jnp.exp(-jnp.abs(z)))
            strict = key_io < qry_io
            log_1mb = jnp.where(strict, -sp, 0.0) if diagonal else -sp
            hi = log_1mb.astype(BF16)
            lo = (log_1mb - hi.astype(F32)).astype(BF16)
            suffix = _dot(later, hi) + _dot(later, lo) + carry[h]
            att = jnp.exp(z - sp + suffix)
            if diagonal:
                att = jnp.where(strict, att, 0.0)
            rows = slice(h * HEAD_DIM, (h + 1) * HEAD_DIM)
            acc_ref[rows, :] += _dot(vT[rows, :], att.astype(BF16))
            new_carry.append(carry[h] + jnp.sum(log_1mb, axis=0, keepdims=True))
        return tuple(new_carry)

    zrow = jnp.zeros((1, blk), F32)
    carry = block(i, (zrow, zrow), True)
    lax.fori_loop(0, i, lambda jj, c: block(i - 1 - jj, c, False), carry)
    o_ref[0] = acc_ref[...].T.astype(o_ref.dtype)


def stick_breaking_attention(qkv, blk=SB_BLOCK):
    bsz, seq, _ = qkv.shape
    width = SB_HEADS * HEAD_DIM
    npair = width // LANES
    nblk = seq // blk
    assert seq % blk == 0
    qT = _blocked_transpose(qkv[..., :width], blk)
    vT = _blocked_transpose(qkv[..., 2 * width:], blk)
    return pl.pallas_call(
        functools.partial(_sb_kernel, blk=blk, scale=HEAD_DIM ** -0.5),
        grid=(bsz, npair, nblk),
        in_specs=[pl.BlockSpec((1, 1, LANES, blk), lambda b, p, i: (b, i, p, 0)),
                  pl.BlockSpec((1, seq, LANES), lambda b, p, i: (b, 0, npair + p)),
                  pl.BlockSpec((1, nblk, LANES, blk), lambda b, p, i: (b, 0, p, 0))],
        out_specs=pl.BlockSpec((1, blk, LANES), lambda b, p, i: (b, i, p)),
        out_shape=jax.ShapeDtypeStruct((bsz, seq, width), BF16),
        scratch_shapes=[pltpu.VMEM((LANES, blk), F32)],
        compiler_params=_params("parallel", "parallel", "arbitrary"),
        name="stick_breaking_attention",
    )(qT, qkv, vT)


def _t5_bucket(dist):
    max_exact = REL_BUCKETS // 2
    d = jnp.maximum(dist, 1).astype(F32)
    large = max_exact + (jnp.log(d / max_exact) / math.log(REL_MAX_DIST / max_exact)
                         * (REL_BUCKETS - max_exact)).astype(jnp.int32)
    large = jnp.minimum(large, REL_BUCKETS - 1)
    return jnp.where(dist < max_exact, dist, large)


def _dilated_bias_tiles(rel_bias_group, dil, blk):
    a = jnp.arange(blk)[:, None]
    cc = jnp.arange(2 * blk)[None, :]
    step = blk + a - cc
    valid = (step >= 0) & (step <= blk)
    bias = rel_bias_group[_t5_bucket(dil * jnp.clip(step, 0, blk))]
    tile = jnp.where(valid[..., None], bias.astype(F32), NEG_INF)
    return jnp.transpose(tile, (2, 0, 1))


def _dsa_bias_tiles(rel_bias_dsa, blk):
    c = jnp.arange(blk)[:, None]
    a = jnp.arange(blk)[None, :]
    own = jnp.where((a - c >= 0)[..., None], rel_bias_dsa[_t5_bucket(jnp.maximum(a - c, 0))].astype(F32), NEG_INF)
    prev = rel_bias_dsa[_t5_bucket(blk + a - c)].astype(F32)
    half = REL_BUCKETS // 2
    assert half + int(math.log((blk + 1) / half) / math.log(REL_MAX_DIST / half) * (REL_BUCKETS - half)) >= REL_BUCKETS - 1
    far = jnp.broadcast_to(rel_bias_dsa[REL_BUCKETS - 1].astype(F32), prev.shape)
    return jnp.transpose(jnp.stack([own, prev, far]), (3, 0, 1, 2))


def _dil_kernel(q_ref, k_ref, v_ref, bias_ref, o_ref, lse_ref, *, blk, scale):
    i = pl.program_id(2)
    width = DIL_HEADS_PER_GROUP * HEAD_DIM
    lane = lax.broadcasted_iota(jnp.int32, (1, width), 1)
    q = q_ref[0]
    zero = jnp.zeros_like(q)
    cur0 = pl.multiple_of(i * blk, blk)
    prev0 = pl.multiple_of(jnp.maximum(i - 1, 0) * blk, blk)
    kc = k_ref[0, pl.ds(cur0, blk), :]
    kp = k_ref[0, pl.ds(prev0, blk), :]
    vc = v_ref[0, pl.ds(cur0, blk), :]
    vp = v_ref[0, pl.ds(prev0, blk), :]
    has_prev = i > 0
    out = jnp.zeros((blk, width), F32)
    lse_b = jnp.zeros((blk, width), F32)
    for h in range(DIL_HEADS_PER_GROUP):
        head = (lane >= h * HEAD_DIM) & (lane < (h + 1) * HEAD_DIM)
        qm = jnp.where(head, q, zero)
        s_prev = _dot_nt(qm, kp) * scale + bias_ref[h, :, :blk]
        s_prev = jnp.where(has_prev, s_prev, NEG_INF)
        s_cur = _dot_nt(qm, kc) * scale + bias_ref[h, :, blk:]
        m = jnp.maximum(jnp.max(s_prev, axis=-1, keepdims=True), jnp.max(s_cur, axis=-1, keepdims=True))
        p_prev = jnp.exp(s_prev - m)
        p_cur = jnp.exp(s_cur - m)
        denom = jnp.sum(p_prev, axis=-1, keepdims=True) + jnp.sum(p_cur, axis=-1, keepdims=True)
        o = (_dot(p_prev.astype(BF16), vp) + _dot(p_cur.astype(BF16), vc)) / denom
        out = jnp.where(head, o, out)
        lse_b = jnp.where(head, m + jnp.log(denom), lse_b)
    o_ref[0] = out
    lse_ref[0] = lse_b


def dilated_group_attention(qkv, bias_tiles, group, dil, blk=ATT_BLOCK):
    bsz, seq, cols = qkv.shape
    width = DIL_HEADS_PER_GROUP * HEAD_DIM
    per_row = cols // width
    ngroups = len(DIL_PATTERNS)
    n = seq // dil
    assert n % blk == 0
    x = qkv.reshape(bsz, n, dil * cols)
    q_map = lambda b, c, i: (b, i, c * per_row + group)
    k_map = lambda b, c, i: (b, 0, c * per_row + ngroups + group)
    v_map = lambda b, c, i: (b, 0, c * per_row + 2 * ngroups + group)
    out, lse = pl.pallas_call(
        functools.partial(_dil_kernel, blk=blk, scale=HEAD_DIM ** -0.5),
        grid=(bsz, dil, n // blk),
        in_specs=[pl.BlockSpec((1, blk, width), q_map),
                  pl.BlockSpec((1, n, width), k_map),
                  pl.BlockSpec((1, n, width), v_map),
                  pl.BlockSpec((DIL_HEADS_PER_GROUP, blk, 2 * blk), lambda b, c, i: (0, 0, 0))],
        out_specs=[pl.BlockSpec((1, blk, width), lambda b, c, i: (b, i, c)),
                   pl.BlockSpec((1, blk, width), lambda b, c, i: (b, i, c))],
        out_shape=[jax.ShapeDtypeStruct((bsz, n, dil * width), F32),
                   jax.ShapeDtypeStruct((bsz, n, dil * width), F32)],
        compiler_params=_params("parallel", "parallel", "arbitrary"),
        name=f"dilated_attention_g{group}",
    )(x, x, x, bias_tiles)
    return out.reshape(bsz * seq, width), lse.reshape(bsz * seq, width)


def _dil_merge_kernel(o0_ref, o1_ref, o2_ref, l0_ref, l1_ref, l2_ref, y_ref):
    l0, l1, l2 = l0_ref[...], l1_ref[...], l2_ref[...]
    m = jnp.maximum(jnp.maximum(l0, l1), l2)
    w0, w1, w2 = jnp.exp(l0 - m), jnp.exp(l1 - m), jnp.exp(l2 - m)
    y = (w0 * o0_ref[...] + w1 * o1_ref[...] + w2 * o2_ref[...]) / (w0 + w1 + w2)
    y_ref[...] = y.astype(y_ref.dtype)


def dilated_merge(outs, lses, tm=1024):
    m, width = outs[0].shape
    spec = pl.BlockSpec((tm, width), lambda i: (i, 0))
    return pl.pallas_call(
        _dil_merge_kernel,
        grid=(m // tm,),
        in_specs=[spec] * 6,
        out_specs=spec,
        out_shape=jax.ShapeDtypeStruct((m, width), BF16),
        compiler_params=_params("parallel"),
        name="dilated_merge",
    )(*outs, *lses)


def _ssm_kernel(u_ref, bb_ref, apow_ref, cc_ref, d_ref, wg_ref, o_ref, bu_ref, carry_ref, *, chunk, nstrips):
    @pl.when(pl.program_id(1) == 0)
    def _():
        carry_ref[...] = jnp.zeros_like(carry_ref)

    u = u_ref[0]
    bu = _dot(u.astype(BF16), bb_ref[...])
    for s in range(2 * nstrips):
        bu_ref[s] = bu[:, s * LANES:(s + 1) * LANES]
    row = lax.broadcasted_iota(jnp.int32, (chunk, LANES), 0)
    nsteps = chunk.bit_length() - 1

    def strip(s, y):
        xr = bu_ref[s]
        xi = bu_ref[nstrips + s]
        pr = apow_ref[0, s]
        pi = apow_ref[1, s]
        for kstep in range(nsteps):
            sh = 1 << kstep
            ar = pr[sh - 1:sh, :]
            ai = pi[sh - 1:sh, :]
            sxr = jnp.where(row >= sh, pltpu.roll(xr, sh, 0), 0.0)
            sxi = jnp.where(row >= sh, pltpu.roll(xi, sh, 0), 0.0)
            xr, xi = xr + ar * sxr - ai * sxi, xi + ar * sxi + ai * sxr
        cr = carry_ref[0, s][0:1, :]
        ci = carry_ref[1, s][0:1, :]
        xr, xi = xr + pr * cr - pi * ci, xi + pr * ci + pi * cr
        carry_ref[0, s] = jnp.broadcast_to(xr[chunk - 1:chunk, :], (8, LANES))
        carry_ref[1, s] = jnp.broadcast_to(xi[chunk - 1:chunk, :], (8, LANES))
        return y + _dot(xr.astype(BF16), cc_ref[s]) + _dot(xi.astype(BF16), cc_ref[nstrips + s])

    y = lax.fori_loop(0, nstrips, strip, jnp.zeros(u.shape, F32))
    y = y + d_ref[...] * u
    y = 0.5 * y * (1.0 + jnp.tanh(math.sqrt(2.0 / math.pi) * (y + 0.044715 * (y * y * y))))
    z = _dot(y.astype(BF16), wg_ref[...])
    width = u.shape[1]
    o_ref[0] = (z[:, :width] * _sigmoid(z[:, width:])).astype(o_ref.dtype)


def _block_diag(blocks):
    g, r, c = blocks.shape
    eye = jnp.eye(g, dtype=blocks.dtype)
    return (eye[:, None, :, None] * blocks[:, :, None, :]).reshape(g * r, g * c)


def _ssm_tables(lam_re, lam_im, log_dt, b_re, b_im, c_re, c_im, chunk):
    lr, li = lam_re.astype(F32), lam_im.astype(F32)
    dt = jnp.exp(log_dt.astype(F32))[:, None]
    mag = jnp.exp(lr * dt)
    a_re, a_im = mag * jnp.cos(li * dt), mag * jnp.sin(li * dt)
    den = lr * lr + li * li
    f_re = ((a_re - 1.0) * lr + a_im * li) / den
    f_im = (a_im * lr - (a_re - 1.0) * li) / den
    br, bi = b_re.astype(F32), b_im.astype(F32)
    bb_re = f_re[..., None] * br - f_im[..., None] * bi
    bb_im = f_re[..., None] * bi + f_im[..., None] * br
    bb = jnp.concatenate([_block_diag(jnp.transpose(bb_re, (0, 2, 1))),
                          _block_diag(jnp.transpose(bb_im, (0, 2, 1)))], axis=1)
    cc = jnp.concatenate([_block_diag(jnp.transpose(c_re.astype(F32), (0, 2, 1))),
                          -_block_diag(jnp.transpose(c_im.astype(F32), (0, 2, 1)))], axis=0)
    nstates = SSM_GROUPS * SSM_STATE

    def cmul(x, y):
        return x[0] * y[0] - x[1] * y[1], x[0] * y[1] + x[1] * y[0]

    ar = jnp.broadcast_to(a_re.reshape(1, nstates), (chunk, nstates))
    ai = jnp.broadcast_to(a_im.reshape(1, nstates), (chunk, nstates))
    pw_re, pw_im = lax.associative_scan(cmul, (ar, ai), axis=0)
    nstrips = nstates // LANES
    apow = jnp.stack([pw_re, pw_im]).reshape(2, chunk, nstrips, LANES).transpose(0, 2, 1, 3)
    return bb.astype(BF16), apow, cc.astype(BF16).reshape(2 * nstrips, LANES, SSM_WIDTH)


def s5_glu(u, bb, apow, cc, d_skip, w_glu, chunk=SSM_CHUNK):
    bsz, seq, width = u.shape
    nstrips = apow.shape[1]
    assert seq % chunk == 0 and chunk & (chunk - 1) == 0
    const = lambda *shape: pl.BlockSpec(shape, lambda b, c: (0,) * len(shape))
    return pl.pallas_call(
        functools.partial(_ssm_kernel, chunk=chunk, nstrips=nstrips),
        grid=(bsz, seq // chunk),
        in_specs=[pl.BlockSpec((1, chunk, width), lambda b, c: (b, c, 0)),
                  const(*bb.shape), const(*apow.shape), const(*cc.shape),
                  const(1, width), const(*w_glu.shape)],
        out_specs=pl.BlockSpec((1, chunk, width), lambda b, c: (b, c, 0)),
        out_shape=jax.ShapeDtypeStruct((bsz, seq, width), BF16),
        scratch_shapes=[pltpu.VMEM((2 * nstrips, chunk, LANES), F32),
                        pltpu.VMEM((2, nstrips, 8, LANES), F32)],
        compiler_params=_params("parallel", "arbitrary"),
        name="s5_glu",
    )(u, bb, apow, cc, d_skip.reshape(1, width).astype(F32), w_glu)


def _dsa_kernel(qT_ref, k_ref, vT_ref, iqT_ref, ik_ref, iwT_ref, bias_ref, o_ref,
                key_ref, madd_ref, acc_ref, *, blk, topk, scale):
    i = pl.program_id(1)
    nblk = i + 1
    key_io = lax.broadcasted_iota(jnp.int32, (blk, blk), 0)
    qry_io = lax.broadcasted_iota(jnp.int32, (blk, blk), 1)

    iqT = iqT_ref[0, 0]
    idx_rhs = jnp.concatenate([iqT[h * IDX_DIM:(h + 1) * IDX_DIM, :] for h in range(IDX_HEADS)], axis=1)
    iw = iwT_ref[0]

    def score_block(j, _):
        start = pl.multiple_of(j * blk, blk)
        kk = ik_ref[0, pl.ds(start, blk), :][:, :IDX_DIM].astype(BF16)
        d = _dot(kk, idx_rhs)
        sc = jnp.zeros((blk, blk), F32)
        for h in range(IDX_HEADS):
            sc = sc + iw[h:h + 1, :] * jnp.maximum(d[:, h * blk:(h + 1) * blk], 0.0)
        sc = jnp.where((key_io + j * blk) <= (qry_io + i * blk), sc, NEG_INF)
        bits = pltpu.bitcast(sc, jnp.int32)
        key_ref[j] = bits ^ ((bits >> 31) & 0x7FFFFFFF)
        return 0

    lax.fori_loop(0, nblk, score_block, 0)

    def count(pred):
        def body(j, acc):
            return acc + jnp.where(pred(key_ref[j]), 1.0, 0.0)
        return jnp.sum(lax.fori_loop(0, nblk, body, jnp.zeros((blk, blk), F32)), axis=0, keepdims=True)

    def bit_step(b, thr):
        cand = thr + lax.shift_left(jnp.int32(1), 31 - b)
        return jnp.where(count(lambda key: key >= cand) >= topk, cand, thr)

    thr = lax.fori_loop(0, 32, bit_step, jnp.full((1, blk), INT_MIN, jnp.int32))

    need = topk - count(lambda key: key > thr)
    upto = jnp.where(qry_io <= key_io, 1.0, 0.0).astype(BF16)

    def select_block(j, run):
        key = key_ref[j]
        tie = jnp.where(key == thr, 1.0, 0.0)
        rank = _dot(upto, tie.astype(BF16)) + run
        keep_tie = jnp.where(rank <= need, 0.0, NEG_INF)
        madd_ref[j] = jnp.where(key > thr, 0.0, jnp.where(key == thr, keep_tie, NEG_INF))
        return run + jnp.sum(tie, axis=0, keepdims=True)

    lax.fori_loop(0, nblk, select_block, jnp.zeros((1, blk), F32))

    fold_scale = _is_power_of_two(scale)
    qT = qT_ref[0, 0]
    if fold_scale:
        qT = qT * jnp.asarray(scale, qT.dtype)
    row = lax.broadcasted_iota(jnp.int32, (LANES, 1), 0)
    q_rhs = []
    for g in range(DSA_HEADS // 2):
        pair = qT[g * LANES:(g + 1) * LANES, :]
        zero = jnp.zeros_like(pair)
        q_rhs.append(jnp.concatenate([jnp.where(row < HEAD_DIM, pair, zero),
                                      jnp.where(row >= HEAD_DIM, pair, zero)], axis=1))
    acc_ref[...] = jnp.zeros_like(acc_ref)

    def attend_block(j, carry):
        m_all, l_all = carry
        start = pl.multiple_of(j * blk, blk)
        kblk = k_ref[0, pl.ds(start, blk), :]
        vT = vT_ref[0, j]
        madd = madd_ref[j]
        which = jnp.minimum(i - j, 2)
        m_out, l_out = [], []
        for g in range(DSA_HEADS // 2):
            s2 = _dot(kblk[:, g * LANES:(g + 1) * LANES], q_rhs[g])
            for e in range(2):
                h = 2 * g + e
                s = s2[:, e * blk:(e + 1) * blk]
                if not fold_scale:
                    s = s * scale
                s = s + bias_ref[h, which] + madd
                m_new = jnp.maximum(m_all[h], jnp.max(s, axis=0, keepdims=True))
                p = jnp.exp(s - m_new)
                alpha = jnp.exp(m_all[h] - m_new)
                l_out.append(alpha * l_all[h] + jnp.sum(p, axis=0, keepdims=True))
                m_out.append(m_new)
                rows = slice(h * HEAD_DIM, (h + 1) * HEAD_DIM)
                acc_ref[rows, :] = alpha * acc_ref[rows, :] + _dot(vT[rows, :], p.astype(BF16))
        return tuple(m_out), tuple(l_out)

    init = (tuple(jnp.full((1, blk), NEG_INF, F32) for _ in range(DSA_HEADS)),
            tuple(jnp.zeros((1, blk), F32) for _ in range(DSA_HEADS)))
    _, l_all = lax.fori_loop(0, nblk, attend_block, init)
    for h in range(DSA_HEADS):
        rows = slice(h * HEAD_DIM, (h + 1) * HEAD_DIM)
        acc_ref[rows, :] = acc_ref[rows, :] / l_all[h]
    o_ref[0] = acc_ref[...].T.astype(o_ref.dtype)


def dsa_attention(qkv, idx_q, idx_kw, bias_tiles, blk=ATT_BLOCK):
    bsz, seq, _ = qkv.shape
    width = DSA_HEADS * HEAD_DIM
    nblk = seq // blk
    topk = min(DSA_TOPK, seq // 4)
    qT = _blocked_transpose(qkv[..., :width], blk)
    vT = _blocked_transpose(qkv[..., 2 * width:], blk)
    iqT = _blocked_transpose(idx_q, blk)
    iwT = jnp.swapaxes(idx_kw[..., IDX_DIM:IDX_DIM + IDX_HEADS], 1, 2)
    return pl.pallas_call(
        functools.partial(_dsa_kernel, blk=blk, topk=topk, scale=HEAD_DIM ** -0.5),
        grid=(bsz, nblk),
        in_specs=[pl.BlockSpec((1, 1, width, blk), lambda b, i: (b, i, 0, 0)),
                  pl.BlockSpec((1, seq, width), lambda b, i: (b, 0, 1)),
                  pl.BlockSpec((1, nblk, width, blk), lambda b, i: (b, 0, 0, 0)),
                  pl.BlockSpec((1, 1, IDX_HEADS * IDX_DIM, blk), lambda b, i: (b, i, 0, 0)),
                  pl.BlockSpec((1, seq, LANES), lambda b, i: (b, 0, 0)),
                  pl.BlockSpec((1, IDX_HEADS, blk), lambda b, i: (b, 0, i)),
                  pl.BlockSpec((DSA_HEADS, 3, blk, blk), lambda b, i: (0, 0, 0, 0))],
        out_specs=pl.BlockSpec((1, blk, width), lambda b, i: (b, i, 0)),
        out_shape=jax.ShapeDtypeStruct((bsz, seq, width), BF16),
        scratch_shapes=[pltpu.VMEM((nblk, blk, blk), jnp.int32),
                        pltpu.VMEM((nblk, blk, blk), F32),
                        pltpu.VMEM((width, blk), F32)],
        compiler_params=_params("parallel", "arbitrary"),
        name="dsa_attention",
    )(qT, qkv, vT, iqT, idx_kw, iwT, bias_tiles)


def hybrid_mixer(xbf, bsz, seq, w_in, rel_bias, ssm_params, d_skip, w_glu, w_branches):
    offs = [0]
    for width in IN_SPLITS:
        offs.append(offs[-1] + width)
    col = lambda a, b: w_in[:, offs[a]:offs[b]].astype(BF16)
    w_idx_k = w_in[:, offs[5]:offs[6]]
    w_idx_w = w_in[:, offs[6]:offs[7]]
    w_small = jnp.concatenate([w_idx_k, w_idx_w,
                               jnp.zeros((w_in.shape[0], LANES - IDX_DIM - IDX_HEADS), w_in.dtype)], axis=1).astype(BF16)

    sb_qkv = matmul(xbf, col(0, 1), BF16).reshape(bsz, seq, -1)
    dil_qkv = matmul(xbf, col(1, 2), BF16, tn=768).reshape(bsz, seq, -1)
    ssm_u = matmul(xbf, col(2, 3), F32).reshape(bsz, seq, -1)
    dsa_qkv = matmul(xbf, col(3, 4), BF16).reshape(bsz, seq, -1)
    idx_q = matmul(xbf, col(4, 5), BF16).reshape(bsz, seq, -1)
    idx_kw = matmul(xbf, w_small, F32, tn=LANES).reshape(bsz, seq, -1)
    gate_logits = matmul(xbf, col(7, 8), F32)

    y_sb = stick_breaking_attention(sb_qkv).reshape(bsz * seq, -1)

    outs, lses = [], []
    for g, (window, dil) in enumerate(DIL_PATTERNS):
        assert window // dil == ATT_BLOCK
        tiles = _dilated_bias_tiles(rel_bias[:, g * DIL_HEADS_PER_GROUP:(g + 1) * DIL_HEADS_PER_GROUP], dil, ATT_BLOCK)
        o, lse = dilated_group_attention(dil_qkv, tiles, g, dil)
        outs.append(o)
        lses.append(lse)
    y_dil = dilated_merge(outs, lses)

    bb, apow, cc = _ssm_tables(*ssm_params, SSM_CHUNK)
    y_ssm = s5_glu(ssm_u, bb, apow, cc, d_skip, w_glu.astype(BF16)).reshape(bsz * seq, -1)

    y_dsa = dsa_attention(dsa_qkv, idx_q, idx_kw, _dsa_bias_tiles(rel_bias[:, DIL_HEADS:], ATT_BLOCK))
    y_dsa = y_dsa.reshape(bsz * seq, -1)

    return branch_merge([y_sb, y_dil, y_ssm, y_dsa], [w.astype(BF16) for w in w_branches], gate_logits)


def kernel(x, ln_g, ln_b, ffn1_w_up, ffn1_w_down, w_in, rel_bias, ssm_lam_re, ssm_lam_im, ssm_log_dt,
           ssm_b_re, ssm_b_im, ssm_c_re, ssm_c_im, ssm_d, ssm_w_glu, w_br_sb, w_br_dil, w_br_ssm, w_br_dsa,
           w_out, ffn2_w_up, ffn2_w_down):
    bsz, seq, d = x.shape
    xf = x.reshape(bsz * seq, d)
    xbf = xf.astype(BF16)
    ffn_tk = D_FF // 4
    for l in range(DEPTH):
        h = ffn_up(xbf, ffn1_w_up[l].astype(BF16))
        xf, xbf = matmul_residual_layernorm(h, ffn1_w_down[l].astype(BF16), xf, ln_g[l, 0], ln_b[l, 0],
                                            MACARON, tk=ffn_tk)
        merged = hybrid_mixer(xbf, bsz, seq, w_in[l], rel_bias,
                              (ssm_lam_re[l], ssm_lam_im[l], ssm_log_dt[l], ssm_b_re[l], ssm_b_im[l],
                               ssm_c_re[l], ssm_c_im[l]), ssm_d[l], ssm_w_glu[l],
                              (w_br_sb[l], w_br_dil[l], w_br_ssm[l], w_br_dsa[l]))
        xf, xbf = matmul_residual_layernorm(merged, w_out[l].astype(BF16), xf, ln_g[l, 1], ln_b[l, 1], 1.0)
        h = ffn_up(xbf, ffn2_w_up[l].astype(BF16))
        xf, xbf = matmul_residual_layernorm(h, ffn2_w_down[l].astype(BF16), xf, ln_g[l, 2], ln_b[l, 2],
                                            MACARON, tk=ffn_tk)
    return xf.reshape(bsz, seq, d)
```

```python
import functools
import math

import jax
import jax.numpy as jnp
from jax import lax
from jax.experimental import pallas as pl
from jax.experimental.pallas import tpu as pltpu

F32 = jnp.float32
BF16 = jnp.bfloat16

D_MODEL = 2048
DEPTH = 2
HEAD_DIM = 64
SB_HEADS = 8
DIL_PATTERNS = ((128, 1), (512, 4), (2048, 16))
DIL_HEADS_PER_GROUP = 4
DIL_HEADS = DIL_HEADS_PER_GROUP * len(DIL_PATTERNS)
SSM_WIDTH = 512
SSM_GROUP = 16
SSM_GROUPS = SSM_WIDTH // SSM_GROUP
SSM_STATE = 64
DSA_HEADS = 8
IDX_HEADS = 8
IDX_DIM = 64
DSA_TOPK = 256
N_BRANCH = 4
D_FF = 5632
REL_BUCKETS = 32
REL_MAX_DIST = 128
DN_ALPHA = (2.0 * DEPTH) ** 0.25
LN_EPS = 1e-5
NEG_INF = -1e30
MACARON = 0.5

IN_SPLITS = (3 * SB_HEADS * HEAD_DIM, 3 * DIL_HEADS * HEAD_DIM, SSM_WIDTH, 3 * DSA_HEADS * HEAD_DIM,
             IDX_HEADS * IDX_DIM, IDX_DIM, IDX_HEADS, N_BRANCH * D_MODEL)

LANES = 128
ATT_BLOCK = 128
DSA_BLOCK = 256
SB_QUERY_BLOCK = 1024
SB_KEY_BLOCK = 256
SSM_CHUNK = 256
VMEM_LIMIT = 56 * 1024 * 1024
INT_MIN = -2 ** 31

_NT = (((1,), (1,)), ((), ()))


def _dot(a, b):
    return jnp.dot(a, b, preferred_element_type=F32)


def _dot_nt(a, b):
    return lax.dot_general(a, b, _NT, preferred_element_type=F32)


def _params(*sem):
    return pltpu.CompilerParams(dimension_semantics=sem, vmem_limit_bytes=VMEM_LIMIT)


def _sigmoid(x):
    return 1.0 / (1.0 + jnp.exp(-x))


def _mm_kernel(x_ref, w_ref, o_ref):
    o_ref[...] = _dot(x_ref[...], w_ref[...]).astype(o_ref.dtype)


def matmul(x, w, out_dtype, tm=1024, tn=512):
    m, k = x.shape
    n = w.shape[1]
    tn = min(tn, n)
    assert m % tm == 0 and n % tn == 0
    return pl.pallas_call(
        _mm_kernel,
        grid=(m // tm, n // tn),
        in_specs=[pl.BlockSpec((tm, k), lambda i, j: (i, 0)),
                  pl.BlockSpec((k, tn), lambda i, j: (0, j))],
        out_specs=pl.BlockSpec((tm, tn), lambda i, j: (i, j)),
        out_shape=jax.ShapeDtypeStruct((m, n), out_dtype),
        compiler_params=_params("parallel", "arbitrary"),
        name="matmul",
    )(x, w)


def _ffn_up_kernel(x_ref, wa_ref, wb_ref, o_ref):
    x = x_ref[...]
    a = _dot(x, wa_ref[...])
    b = _dot(x, wb_ref[...])
    o_ref[...] = (a * _sigmoid(a) * b).astype(o_ref.dtype)


def ffn_up(x, w_up, tm=1024, tn=512):
    m, k = x.shape
    f = w_up.shape[1] // 2
    assert m % tm == 0 and f % tn == 0
    nb = f // tn
    return pl.pallas_call(
        _ffn_up_kernel,
        grid=(m // tm, nb),
        in_specs=[pl.BlockSpec((tm, k), lambda i, j: (i, 0)),
                  pl.BlockSpec((k, tn), lambda i, j: (0, j)),
                  pl.BlockSpec((k, tn), lambda i, j: (0, j + nb))],
        out_specs=pl.BlockSpec((tm, tn), lambda i, j: (i, j)),
        out_shape=jax.ShapeDtypeStruct((m, f), BF16),
        compiler_params=_params("parallel", "arbitrary"),
        name="ffn_up",
    )(x, w_up, w_up)


def _mm_res_ln_kernel(h_ref, w_ref, x_ref, g_ref, b_ref, o_ref, obf_ref, acc_ref, *, scale, nk):
    kk = pl.program_id(1)

    @pl.when(kk == 0)
    def _():
        acc_ref[...] = jnp.zeros_like(acc_ref)

    acc_ref[...] += _dot(h_ref[...], w_ref[...])

    @pl.when(kk == nk - 1)
    def _():
        y = DN_ALPHA * x_ref[...] + scale * acc_ref[...]
        mu = jnp.mean(y, axis=-1, keepdims=True)
        yc = y - mu
        var = jnp.mean(yc * yc, axis=-1, keepdims=True)
        out = yc * lax.rsqrt(var + LN_EPS) * g_ref[...] + b_ref[...]
        o_ref[...] = out
        obf_ref[...] = out.astype(BF16)


def matmul_residual_layernorm(h, w, x, g, b, scale, tm=512, tk=None):
    m, k = h.shape
    n = w.shape[1]
    if tk is None:
        tk = k
    assert m % tm == 0 and k % tk == 0
    nk = k // tk
    return pl.pallas_call(
        functools.partial(_mm_res_ln_kernel, scale=scale, nk=nk),
        grid=(m // tm, nk),
        in_specs=[pl.BlockSpec((tm, tk), lambda i, kk: (i, kk)),
                  pl.BlockSpec((tk, n), lambda i, kk: (kk, 0)),
                  pl.BlockSpec((tm, n), lambda i, kk: (i, 0)),
                  pl.BlockSpec((1, n), lambda i, kk: (0, 0)),
                  pl.BlockSpec((1, n), lambda i, kk: (0, 0))],
        out_specs=[pl.BlockSpec((tm, n), lambda i, kk: (i, 0)),
                   pl.BlockSpec((tm, n), lambda i, kk: (i, 0))],
        out_shape=[jax.ShapeDtypeStruct((m, n), F32), jax.ShapeDtypeStruct((m, n), BF16)],
        scratch_shapes=[pltpu.VMEM((tm, n), F32)],
        compiler_params=_params("parallel", "arbitrary"),
        name="matmul_residual_layernorm",
    )(h, w, x, g.reshape(1, n), b.reshape(1, n))


def _branch_merge_kernel(ysb_ref, ydil_ref, yssm_ref, ydsa_ref, wsb_ref, wdil_ref, wssm_ref, wdsa_ref,
                         g0_ref, g1_ref, g2_ref, g3_ref, o_ref):
    acc = _sigmoid(g0_ref[...]) * _dot(ysb_ref[...], wsb_ref[...])
    acc += _sigmoid(g1_ref[...]) * _dot(ydil_ref[...], wdil_ref[...])
    acc += _sigmoid(g2_ref[...]) * _dot(yssm_ref[...], wssm_ref[...])
    acc += _sigmoid(g3_ref[...]) * _dot(ydsa_ref[...], wdsa_ref[...])
    o_ref[...] = acc.astype(o_ref.dtype)


def branch_merge(ys, ws, gate_logits, tm=512, tn=512):
    m = ys[0].shape[0]
    n = ws[0].shape[1]
    nb = n // tn
    y_specs = [pl.BlockSpec((tm, y.shape[1]), lambda i, j: (i, 0)) for y in ys]
    w_specs = [pl.BlockSpec((w.shape[0], tn), lambda i, j: (0, j)) for w in ws]
    g_specs = [pl.BlockSpec((tm, tn), functools.partial(lambda i, j, br: (i, j + br * nb), br=br))
               for br in range(N_BRANCH)]
    return pl.pallas_call(
        _branch_merge_kernel,
        grid=(m // tm, nb),
        in_specs=y_specs + w_specs + g_specs,
        out_specs=pl.BlockSpec((tm, tn), lambda i, j: (i, j)),
        out_shape=jax.ShapeDtypeStruct((m, n), BF16),
        compiler_params=_params("parallel", "arbitrary"),
        name="branch_merge",
    )(*ys, *ws, gate_logits, gate_logits, gate_logits, gate_logits)


def _blocked_transpose(x, blk):
    bsz, seq, c = x.shape
    return jnp.swapaxes(x.reshape(bsz, seq // blk, blk, c), 2, 3)


def _is_power_of_two(x):
    return math.frexp(x)[0] == 0.5


def _sb_kernel(qT_ref, k_ref, vT_ref, o_ref, acc_ref, *, qblk, kblk, scale):
    i = pl.program_id(2)
    ratio = qblk // kblk
    fold_scale = _is_power_of_two(scale)
    qT = qT_ref[0, 0]
    if fold_scale:
        qT = qT * jnp.asarray(scale, qT.dtype)
    row = lax.broadcasted_iota(jnp.int32, (LANES, 1), 0)
    zero = jnp.zeros_like(qT)
    q_rhs = jnp.concatenate([jnp.where(row < HEAD_DIM, qT, zero), jnp.where(row >= HEAD_DIM, qT, zero)], axis=1)
    key_io = lax.broadcasted_iota(jnp.int32, (kblk, kblk), 0)
    qry_io = lax.broadcasted_iota(jnp.int32, (kblk, kblk), 1)
    later = jnp.where(qry_io > key_io, 1.0, 0.0).astype(BF16)
    acc_ref[...] = jnp.zeros_like(acc_ref)
    nsub = 2 * ratio

    def block(j, carry, diag_sub):
        start = pl.multiple_of(j * kblk, kblk)
        z_all = _dot(k_ref[0, pl.ds(start, kblk), :], q_rhs)
        vT = vT_ref[0, j]
        strict = key_io < qry_io
        stage = []
        for c in range(nsub):
            h, sub = divmod(c, ratio)
            if diag_sub is not None and sub < diag_sub:
                stage.append(None)
                continue
            masked = diag_sub is not None and sub == diag_sub
            z = z_all[:, c * kblk:(c + 1) * kblk]
            if not fold_scale:
                z = z * scale
            sp = jnp.maximum(z, 0.0) + jnp.log(1.0 + jnp.exp(-jnp.abs(z)))
            log_1mb = jnp.where(strict, -sp, 0.0) if masked else -sp
            suf = _dot(later, log_1mb.astype(BF16))
            stage.append((z - sp, suf, jnp.sum(log_1mb, axis=0, keepdims=True), masked))
        new_carry = []
        for c in range(nsub):
            h, sub = divmod(c, ratio)
            if stage[c] is None:
                new_carry.append(carry[c])
                continue
            log_beta, suf, colsum, masked = stage[c]
            att = jnp.exp(log_beta + suf + carry[c])
            if masked:
                att = jnp.where(strict, att, 0.0)
            rows = slice(h * HEAD_DIM, (h + 1) * HEAD_DIM)
            cols = slice(sub * kblk, (sub + 1) * kblk)
            acc_ref[rows, cols] += _dot(vT[rows, :], att.astype(BF16))
            new_carry.append(carry[c] + colsum)
        return tuple(new_carry)

    carry = tuple(jnp.zeros((1, kblk), F32) for _ in range(nsub))
    for sub in reversed(range(ratio)):
        carry = block(i * ratio + sub, carry, sub)
    lax.fori_loop(0, i * ratio, lambda jj, c: block(i * ratio - 1 - jj, c, None), carry)
    o_ref[0] = acc_ref[...].T.astype(o_ref.dtype)


def stick_breaking_attention(qkv, qblk=SB_QUERY_BLOCK, kblk=SB_KEY_BLOCK):
    bsz, seq, _ = qkv.shape
    width = SB_HEADS * HEAD_DIM
    npair = width // LANES
    qblk = min(qblk, seq)
    assert seq % qblk == 0 and qblk % kblk == 0
    qT = _blocked_transpose(qkv[..., :width], qblk)
    vT = _blocked_transpose(qkv[..., 2 * width:], kblk)
    return pl.pallas_call(
        functools.partial(_sb_kernel, qblk=qblk, kblk=kblk, scale=HEAD_DIM ** -0.5),
        grid=(bsz, npair, seq // qblk),
        in_specs=[pl.BlockSpec((1, 1, LANES, qblk), lambda b, p, i: (b, i, p, 0)),
                  pl.BlockSpec((1, seq, LANES), lambda b, p, i: (b, 0, npair + p)),
                  pl.BlockSpec((1, seq // kblk, LANES, kblk), lambda b, p, i: (b, 0, p, 0))],
        out_specs=pl.BlockSpec((1, qblk, LANES), lambda b, p, i: (b, i, p)),
        out_shape=jax.ShapeDtypeStruct((bsz, seq, width), BF16),
        scratch_shapes=[pltpu.VMEM((LANES, qblk), F32)],
        compiler_params=_params("parallel", "parallel", "arbitrary"),
        name="stick_breaking_attention",
    )(qT, qkv, vT)


def _t5_bucket(dist):
    max_exact = REL_BUCKETS // 2
    d = jnp.maximum(dist, 1).astype(F32)
    large = max_exact + (jnp.log(d / max_exact) / math.log(REL_MAX_DIST / max_exact)
                         * (REL_BUCKETS - max_exact)).astype(jnp.int32)
    large = jnp.minimum(large, REL_BUCKETS - 1)
    return jnp.where(dist < max_exact, dist, large)


def _dilated_bias_tiles(rel_bias_group, dil, blk):
    a = jnp.arange(blk)[:, None]
    cc = jnp.arange(2 * blk)[None, :]
    step = blk + a - cc
    valid = (step >= 0) & (step <= blk)
    bias = rel_bias_group[_t5_bucket(dil * jnp.clip(step, 0, blk))]
    tile = jnp.where(valid[..., None], bias.astype(F32), NEG_INF)
    return jnp.transpose(tile, (2, 0, 1))


def _dsa_bias_tiles(rel_bias_dsa, blk):
    c = jnp.arange(blk)[:, None]
    a = jnp.arange(blk)[None, :]
    own = jnp.where((a - c >= 0)[..., None], rel_bias_dsa[_t5_bucket(jnp.maximum(a - c, 0))].astype(F32), NEG_INF)
    prev = rel_bias_dsa[_t5_bucket(blk + a - c)].astype(F32)
    half = REL_BUCKETS // 2
    assert half + int(math.log((blk + 1) / half) / math.log(REL_MAX_DIST / half) * (REL_BUCKETS - half)) >= REL_BUCKETS - 1
    far = jnp.broadcast_to(rel_bias_dsa[REL_BUCKETS - 1].astype(F32), prev.shape)
    return jnp.transpose(jnp.stack([own, prev, far]), (3, 0, 1, 2))


def _dil_kernel(q_ref, k_ref, v_ref, bias_ref, o_ref, lse_ref, *, blk, scale):
    i = pl.program_id(2)
    width = DIL_HEADS_PER_GROUP * HEAD_DIM
    lane = lax.broadcasted_iota(jnp.int32, (1, width), 1)
    q = q_ref[0]
    zero = jnp.zeros_like(q)
    cur0 = pl.multiple_of(i * blk, blk)
    prev0 = pl.multiple_of(jnp.maximum(i - 1, 0) * blk, blk)
    kc = k_ref[0, pl.ds(cur0, blk), :]
    kp = k_ref[0, pl.ds(prev0, blk), :]
    vc = v_ref[0, pl.ds(cur0, blk), :]
    vp = v_ref[0, pl.ds(prev0, blk), :]
    has_prev = i > 0
    out = jnp.zeros((blk, width), F32)
    lse_b = jnp.zeros((blk, width), F32)
    for h in range(DIL_HEADS_PER_GROUP):
        head = (lane >= h * HEAD_DIM) & (lane < (h + 1) * HEAD_DIM)
        qm = jnp.where(head, q, zero)
        s_prev = _dot_nt(qm, kp) * scale + bias_ref[h, :, :blk]
        s_prev = jnp.where(has_prev, s_prev, NEG_INF)
        s_cur = _dot_nt(qm, kc) * scale + bias_ref[h, :, blk:]
        m = jnp.maximum(jnp.max(s_prev, axis=-1, keepdims=True), jnp.max(s_cur, axis=-1, keepdims=True))
        p_prev = jnp.exp(s_prev - m)
        p_cur = jnp.exp(s_cur - m)
        denom = jnp.sum(p_prev, axis=-1, keepdims=True) + jnp.sum(p_cur, axis=-1, keepdims=True)
        o = (_dot(p_prev.astype(BF16), vp) + _dot(p_cur.astype(BF16), vc)) / denom
        out = jnp.where(head, o, out)
        lse_b = jnp.where(head, m + jnp.log(denom), lse_b)
    o_ref[0] = out
    lse_ref[0] = lse_b


def dilated_group_attention(qkv, bias_tiles, group, dil, blk=ATT_BLOCK):
    bsz, seq, cols = qkv.shape
    width = DIL_HEADS_PER_GROUP * HEAD_DIM
    per_row = cols // width
    ngroups = len(DIL_PATTERNS)
    n = seq // dil
    assert n % blk == 0
    x = qkv.reshape(bsz, n, dil * cols)
    q_map = lambda b, c, i: (b, i, c * per_row + group)
    k_map = lambda b, c, i: (b, 0, c * per_row + ngroups + group)
    v_map = lambda b, c, i: (b, 0, c * per_row + 2 * ngroups + group)
    out, lse = pl.pallas_call(
        functools.partial(_dil_kernel, blk=blk, scale=HEAD_DIM ** -0.5),
        grid=(bsz, dil, n // blk),
        in_specs=[pl.BlockSpec((1, blk, width), q_map),
                  pl.BlockSpec((1, n, width), k_map),
                  pl.BlockSpec((1, n, width), v_map),
                  pl.BlockSpec((DIL_HEADS_PER_GROUP, blk, 2 * blk), lambda b, c, i: (0, 0, 0))],
        out_specs=[pl.BlockSpec((1, blk, width), lambda b, c, i: (b, i, c)),
                   pl.BlockSpec((1, blk, width), lambda b, c, i: (b, i, c))],
        out_shape=[jax.ShapeDtypeStruct((bsz, n, dil * width), F32),
                   jax.ShapeDtypeStruct((bsz, n, dil * width), F32)],
        compiler_params=_params("parallel", "parallel", "arbitrary"),
        name=f"dilated_attention_g{group}",
    )(x, x, x, bias_tiles)
    return out.reshape(bsz * seq, width), lse.reshape(bsz * seq, width)


def _dil_merge_kernel(o0_ref, o1_ref, o2_ref, l0_ref, l1_ref, l2_ref, y_ref):
    l0, l1, l2 = l0_ref[...], l1_ref[...], l2_ref[...]
    m = jnp.maximum(jnp.maximum(l0, l1), l2)
    w0, w1, w2 = jnp.exp(l0 - m), jnp.exp(l1 - m), jnp.exp(l2 - m)
    y = (w0 * o0_ref[...] + w1 * o1_ref[...] + w2 * o2_ref[...]) / (w0 + w1 + w2)
    y_ref[...] = y.astype(y_ref.dtype)


def dilated_merge(outs, lses, tm=1024):
    m, width = outs[0].shape
    spec = pl.BlockSpec((tm, width), lambda i: (i, 0))
    return pl.pallas_call(
        _dil_merge_kernel,
        grid=(m // tm,),
        in_specs=[spec] * 6,
        out_specs=spec,
        out_shape=jax.ShapeDtypeStruct((m, width), BF16),
        compiler_params=_params("parallel"),
        name="dilated_merge",
    )(*outs, *lses)


def _ssm_kernel(u_ref, bb_ref, apow_ref, cc_ref, d_ref, wg_ref, o_ref, bu_ref, carry_ref, *, chunk, nstrips):
    @pl.when(pl.program_id(1) == 0)
    def _():
        carry_ref[...] = jnp.zeros_like(carry_ref)

    u = u_ref[0]
    bu = _dot(u.astype(BF16), bb_ref[...])
    for s in range(2 * nstrips):
        bu_ref[s] = bu[:, s * LANES:(s + 1) * LANES]
    row = lax.broadcasted_iota(jnp.int32, (chunk, LANES), 0)
    nsteps = chunk.bit_length() - 1

    def strip(s, y):
        xr = bu_ref[s]
        xi = bu_ref[nstrips + s]
        pr = apow_ref[0, s]
        pi = apow_ref[1, s]
        for kstep in range(nsteps):
            sh = 1 << kstep
            ar = pr[sh - 1:sh, :]
            ai = pi[sh - 1:sh, :]
            sxr = jnp.where(row >= sh, pltpu.roll(xr, sh, 0), 0.0)
            sxi = jnp.where(row >= sh, pltpu.roll(xi, sh, 0), 0.0)
            xr, xi = xr + ar * sxr - ai * sxi, xi + ar * sxi + ai * sxr
        cr = carry_ref[0, s][0:1, :]
        ci = carry_ref[1, s][0:1, :]
        xr, xi = xr + pr * cr - pi * ci, xi + pr * ci + pi * cr
        carry_ref[0, s] = jnp.broadcast_to(xr[chunk - 1:chunk, :], (8, LANES))
        carry_ref[1, s] = jnp.broadcast_to(xi[chunk - 1:chunk, :], (8, LANES))
        return y + _dot(xr.astype(BF16), cc_ref[s]) + _dot(xi.astype(BF16), cc_ref[nstrips + s])

    y = lax.fori_loop(0, nstrips, strip, jnp.zeros(u.shape, F32))
    y = y + d_ref[...] * u
    y = 0.5 * y * (1.0 + jnp.tanh(math.sqrt(2.0 / math.pi) * (y + 0.044715 * (y * y * y))))
    z = _dot(y.astype(BF16), wg_ref[...])
    width = u.shape[1]
    o_ref[0] = (z[:, :width] * _sigmoid(z[:, width:])).astype(o_ref.dtype)


def _block_diag(blocks):
    g, r, c = blocks.shape
    eye = jnp.eye(g, dtype=blocks.dtype)
    return (eye[:, None, :, None] * blocks[:, :, None, :]).reshape(g * r, g * c)


def _ssm_tables(lam_re, lam_im, log_dt, b_re, b_im, c_re, c_im, chunk):
    lr, li = lam_re.astype(F32), lam_im.astype(F32)
    dt = jnp.exp(log_dt.astype(F32))[:, None]
    mag = jnp.exp(lr * dt)
    a_re, a_im = mag * jnp.cos(li * dt), mag * jnp.sin(li * dt)
    den = lr * lr + li * li
    f_re = ((a_re - 1.0) * lr + a_im * li) / den
    f_im = (a_im * lr - (a_re - 1.0) * li) / den
    br, bi = b_re.astype(F32), b_im.astype(F32)
    bb_re = f_re[..., None] * br - f_im[..., None] * bi
    bb_im = f_re[..., None] * bi + f_im[..., None] * br
    bb = jnp.concatenate([_block_diag(jnp.transpose(bb_re, (0, 2, 1))),
                          _block_diag(jnp.transpose(bb_im, (0, 2, 1)))], axis=1)
    cc = jnp.concatenate([_block_diag(jnp.transpose(c_re.astype(F32), (0, 2, 1))),
                          -_block_diag(jnp.transpose(c_im.astype(F32), (0, 2, 1)))], axis=0)
    nstates = SSM_GROUPS * SSM_STATE

    def cmul(x, y):
        return x[0] * y[0] - x[1] * y[1], x[0] * y[1] + x[1] * y[0]

    ar = jnp.broadcast_to(a_re.reshape(1, nstates), (chunk, nstates))
    ai = jnp.broadcast_to(a_im.reshape(1, nstates), (chunk, nstates))
    pw_re, pw_im = lax.associative_scan(cmul, (ar, ai), axis=0)
    nstrips = nstates // LANES
    apow = jnp.stack([pw_re, pw_im]).reshape(2, chunk, nstrips, LANES).transpose(0, 2, 1, 3)
    return bb.astype(BF16), apow, cc.astype(BF16).reshape(2 * nstrips, LANES, SSM_WIDTH)


def s5_glu(u, bb, apow, cc, d_skip, w_glu, chunk=SSM_CHUNK):
    bsz, seq, width = u.shape
    nstrips = apow.shape[1]
    assert seq % chunk == 0 and chunk & (chunk - 1) == 0
    const = lambda *shape: pl.BlockSpec(shape, lambda b, c: (0,) * len(shape))
    return pl.pallas_call(
        functools.partial(_ssm_kernel, chunk=chunk, nstrips=nstrips),
        grid=(bsz, seq // chunk),
        in_specs=[pl.BlockSpec((1, chunk, width), lambda b, c: (b, c, 0)),
                  const(*bb.shape), const(*apow.shape), const(*cc.shape),
                  const(1, width), const(*w_glu.shape)],
        out_specs=pl.BlockSpec((1, chunk, width), lambda b, c: (b, c, 0)),
        out_shape=jax.ShapeDtypeStruct((bsz, seq, width), BF16),
        scratch_shapes=[pltpu.VMEM((2 * nstrips, chunk, LANES), F32),
                        pltpu.VMEM((2, nstrips, 8, LANES), F32)],
        compiler_params=_params("parallel", "arbitrary"),
        name="s5_glu",
    )(u, bb, apow, cc, d_skip.reshape(1, width).astype(F32), w_glu)


def _dsa_kernel(qT_ref, k_ref, vT_ref, iqT_ref, ik_ref, iwT_ref, bias_ref, o_ref,
                key_ref, madd_ref, acc_ref, *, blk, topk, scale):
    i = pl.program_id(1)
    nblk = i + 1
    key_io = lax.broadcasted_iota(jnp.int32, (blk, blk), 0)
    qry_io = lax.broadcasted_iota(jnp.int32, (blk, blk), 1)

    iqT = iqT_ref[0, 0]
    idx_rhs = jnp.concatenate([iqT[h * IDX_DIM:(h + 1) * IDX_DIM, :] for h in range(IDX_HEADS)], axis=1)
    iw = iwT_ref[0]

    def score_block(j, _):
        start = pl.multiple_of(j * blk, blk)
        kk = ik_ref[0, pl.ds(start, blk), :][:, :IDX_DIM].astype(BF16)
        d = _dot(kk, idx_rhs)
        sc = jnp.zeros((blk, blk), F32)
        for h in range(IDX_HEADS):
            sc = sc + iw[h:h + 1, :] * jnp.maximum(d[:, h * blk:(h + 1) * blk], 0.0)
        sc = jnp.where((key_io + j * blk) <= (qry_io + i * blk), sc, NEG_INF)
        bits = pltpu.bitcast(sc, jnp.int32)
        key_ref[j] = bits ^ ((bits >> 31) & 0x7FFFFFFF)
        return 0

    lax.fori_loop(0, nblk, score_block, 0)

    def count(pred):
        def body(j, acc):
            hit = jnp.where(pred(key_ref[j]), 1.0, 0.0)
            return acc + jnp.sum(hit.reshape(blk // 8, 8, blk), axis=0)
        return jnp.sum(lax.fori_loop(0, nblk, body, jnp.zeros((8, blk), F32)), axis=0, keepdims=True)

    def bit_step(b, thr):
        cand = thr + lax.shift_left(jnp.int32(1), 31 - b)
        return jnp.where(count(lambda key: key >= cand) >= topk, cand, thr)

    thr = lax.fori_loop(0, 32, bit_step, jnp.full((1, blk), INT_MIN, jnp.int32))

    need = topk - count(lambda key: key > thr)
    upto = jnp.where(qry_io <= key_io, 1.0, 0.0).astype(BF16)

    def select_block(j, run):
        key = key_ref[j]
        tie = jnp.where(key == thr, 1.0, 0.0)
        rank = _dot(upto, tie.astype(BF16)) + run
        keep_tie = jnp.where(rank <= need, 0.0, NEG_INF)
        madd_ref[j] = jnp.where(key > thr, 0.0, jnp.where(key == thr, keep_tie, NEG_INF))
        return run + jnp.sum(tie, axis=0, keepdims=True)

    lax.fori_loop(0, nblk, select_block, jnp.zeros((1, blk), F32))

    fold_scale = _is_power_of_two(scale)
    qT = qT_ref[0, 0]
    if fold_scale:
        qT = qT * jnp.asarray(scale, qT.dtype)
    row = lax.broadcasted_iota(jnp.int32, (LANES, 1), 0)
    q_rhs = []
    for g in range(DSA_HEADS // 2):
        pair = qT[g * LANES:(g + 1) * LANES, :]
        zero = jnp.zeros_like(pair)
        q_rhs.append(jnp.concatenate([jnp.where(row < HEAD_DIM, pair, zero),
                                      jnp.where(row >= HEAD_DIM, pair, zero)], axis=1))
    acc_ref[...] = jnp.zeros_like(acc_ref)

    def attend_block(j, carry):
        m_all, l_all = carry
        start = pl.multiple_of(j * blk, blk)
        kblk = k_ref[0, pl.ds(start, blk), :]
        vT = vT_ref[0, j]
        madd = madd_ref[j]
        which = jnp.minimum(i - j, 2)
        s2 = [_dot(kblk[:, g * LANES:(g + 1) * LANES], q_rhs[g]) for g in range(DSA_HEADS // 2)]
        m_out, l_out, probs, alphas = [], [], [], []
        for h in range(DSA_HEADS):
            s = s2[h // 2][:, (h % 2) * blk:(h % 2 + 1) * blk]
            if not fold_scale:
                s = s * scale
            s = s + bias_ref[h, which] + madd
            m_new = jnp.maximum(m_all[h], jnp.max(s, axis=0, keepdims=True))
            p = jnp.exp(s - m_new)
            alpha = jnp.exp(m_all[h] - m_new)
            l_out.append(alpha * l_all[h] + jnp.sum(p, axis=0, keepdims=True))
            m_out.append(m_new)
            probs.append(p.astype(BF16))
            alphas.append(alpha)
        for h in range(DSA_HEADS):
            rows = slice(h * HEAD_DIM, (h + 1) * HEAD_DIM)
            acc_ref[rows, :] = alphas[h] * acc_ref[rows, :] + _dot(vT[rows, :], probs[h])
        return tuple(m_out), tuple(l_out)

    init = (tuple(jnp.full((1, blk), NEG_INF, F32) for _ in range(DSA_HEADS)),
            tuple(jnp.zeros((1, blk), F32) for _ in range(DSA_HEADS)))
    _, l_all = lax.fori_loop(0, nblk, attend_block, init)
    for h in range(DSA_HEADS):
        rows = slice(h * HEAD_DIM, (h + 1) * HEAD_DIM)
        acc_ref[rows, :] = acc_ref[rows, :] / l_all[h]
    o_ref[0] = acc_ref[...].T.astype(o_ref.dtype)


def dsa_attention(qkv, idx_q, idx_kw, bias_tiles, blk=DSA_BLOCK):
    bsz, seq, _ = qkv.shape
    width = DSA_HEADS * HEAD_DIM
    nblk = seq // blk
    topk = min(DSA_TOPK, seq // 4)
    qT = _blocked_transpose(qkv[..., :width], blk)
    vT = _blocked_transpose(qkv[..., 2 * width:], blk)
    iqT = _blocked_transpose(idx_q, blk)
    iwT = jnp.swapaxes(idx_kw[..., IDX_DIM:IDX_DIM + IDX_HEADS], 1, 2)
    return pl.pallas_call(
        functools.partial(_dsa_kernel, blk=blk, topk=topk, scale=HEAD_DIM ** -0.5),
        grid=(bsz, nblk),
        in_specs=[pl.BlockSpec((1, 1, width, blk), lambda b, i: (b, i, 0, 0)),
                  pl.BlockSpec((1, seq, width), lambda b, i: (b, 0, 1)),
                  pl.BlockSpec((1, nblk, width, blk), lambda b, i: (b, 0, 0, 0)),
                  pl.BlockSpec((1, 1, IDX_HEADS * IDX_DIM, blk), lambda b, i: (b, i, 0, 0)),
                  pl.BlockSpec((1, seq, LANES), lambda b, i: (b, 0, 0)),
                  pl.BlockSpec((1, IDX_HEADS, blk), lambda b, i: (b, 0, i)),
                  pl.BlockSpec((DSA_HEADS, 3, blk, blk), lambda b, i: (0, 0, 0, 0))],
        out_specs=pl.BlockSpec((1, blk, width), lambda b, i: (b, i, 0)),
        out_shape=jax.ShapeDtypeStruct((bsz, seq, width), BF16),
        scratch_shapes=[pltpu.VMEM((nblk, blk, blk), jnp.int32),
                        pltpu.VMEM((nblk, blk, blk), F32),
                        pltpu.VMEM((width, blk), F32)],
        compiler_params=_params("parallel", "arbitrary"),
        name="dsa_attention",
    )(qT, qkv, vT, iqT, idx_kw, iwT, bias_tiles)


def hybrid_mixer(xbf, bsz, seq, w_in, rel_bias, ssm_params, d_skip, w_glu, w_branches):
    offs = [0]
    for width in IN_SPLITS:
        offs.append(offs[-1] + width)
    col = lambda a, b: w_in[:, offs[a]:offs[b]].astype(BF16)
    w_idx_k = w_in[:, offs[5]:offs[6]]
    w_idx_w = w_in[:, offs[6]:offs[7]]
    w_small = jnp.concatenate([w_idx_k, w_idx_w,
                               jnp.zeros((w_in.shape[0], LANES - IDX_DIM - IDX_HEADS), w_in.dtype)], axis=1).astype(BF16)

    sb_qkv = matmul(xbf, col(0, 1), BF16).reshape(bsz, seq, -1)
    dil_qkv = matmul(xbf, col(1, 2), BF16, tn=768).reshape(bsz, seq, -1)
    ssm_u = matmul(xbf, col(2, 3), F32).reshape(bsz, seq, -1)
    dsa_qkv = matmul(xbf, col(3, 4), BF16).reshape(bsz, seq, -1)
    idx_q = matmul(xbf, col(4, 5), BF16).reshape(bsz, seq, -1)
    idx_kw = matmul(xbf, w_small, F32, tn=LANES).reshape(bsz, seq, -1)
    gate_logits = matmul(xbf, col(7, 8), F32)

    y_sb = stick_breaking_attention(sb_qkv).reshape(bsz * seq, -1)

    outs, lses = [], []
    for g, (window, dil) in enumerate(DIL_PATTERNS):
        assert window // dil == ATT_BLOCK
        tiles = _dilated_bias_tiles(rel_bias[:, g * DIL_HEADS_PER_GROUP:(g + 1) * DIL_HEADS_PER_GROUP], dil, ATT_BLOCK)
        o, lse = dilated_group_attention(dil_qkv, tiles, g, dil)
        outs.append(o)
        lses.append(lse)
    y_dil = dilated_merge(outs, lses)

    bb, apow, cc = _ssm_tables(*ssm_params, SSM_CHUNK)
    y_ssm = s5_glu(ssm_u, bb, apow, cc, d_skip, w_glu.astype(BF16)).reshape(bsz * seq, -1)

    y_dsa = dsa_attention(dsa_qkv, idx_q, idx_kw, _dsa_bias_tiles(rel_bias[:, DIL_HEADS:], DSA_BLOCK))
    y_dsa = y_dsa.reshape(bsz * seq, -1)

    return branch_merge([y_sb, y_dil, y_ssm, y_dsa], [w.astype(BF16) for w in w_branches], gate_logits)


def kernel(x, ln_g, ln_b, ffn1_w_up, ffn1_w_down, w_in, rel_bias, ssm_lam_re, ssm_lam_im, ssm_log_dt,
           ssm_b_re, ssm_b_im, ssm_c_re, ssm_c_im, ssm_d, ssm_w_glu, w_br_sb, w_br_dil, w_br_ssm, w_br_dsa,
           w_out, ffn2_w_up, ffn2_w_down):
    bsz, seq, d = x.shape
    xf = x.reshape(bsz * seq, d)
    xbf = xf.astype(BF16)
    ffn_tk = D_FF // 4
    for l in range(DEPTH):
        h = ffn_up(xbf, ffn1_w_up[l].astype(BF16))
        xf, xbf = matmul_residual_layernorm(h, ffn1_w_down[l].astype(BF16), xf, ln_g[l, 0], ln_b[l, 0],
                                            MACARON, tk=ffn_tk)
        merged = hybrid_mixer(xbf, bsz, seq, w_in[l], rel_bias,
                              (ssm_lam_re[l], ssm_lam_im[l], ssm_log_dt[l], ssm_b_re[l], ssm_b_im[l],
                               ssm_c_re[l], ssm_c_im[l]), ssm_d[l], ssm_w_glu[l],
                              (w_br_sb[l], w_br_dil[l], w_br_ssm[l], w_br_dsa[l]))
        xf, xbf = matmul_residual_layernorm(merged, w_out[l].astype(BF16), xf, ln_g[l, 1], ln_b[l, 1], 1.0)
        h = ffn_up(xbf, ffn2_w_up[l].astype(BF16))
        xf, xbf = matmul_residual_layernorm(h, ffn2_w_down[l].astype(BF16), xf, ln_g[l, 2], ln_b[l, 2],
                                            MACARON, tk=ffn_tk)
    return xf.reshape(bsz, seq, d)
```

```python
import functools
import math

import jax
import jax.numpy as jnp
from jax import lax
from jax.experimental import pallas as pl
from jax.experimental.pallas import tpu as pltpu

F32 = jnp.float32
BF16 = jnp.bfloat16

D_MODEL = 2048
DEPTH = 2
HEAD_DIM = 64
SB_HEADS = 8
DIL_PATTERNS = ((128, 1), (512, 4), (2048, 16))
DIL_HEADS_PER_GROUP = 4
DIL_HEADS = DIL_HEADS_PER_GROUP * len(DIL_PATTERNS)
SSM_WIDTH = 512
SSM_GROUP = 16
SSM_GROUPS = SSM_WIDTH // SSM_GROUP
SSM_STATE = 64
DSA_HEADS = 8
IDX_HEADS = 8
IDX_DIM = 64
DSA_TOPK = 256
N_BRANCH = 4
D_FF = 5632
REL_BUCKETS = 32
REL_MAX_DIST = 128
DN_ALPHA = (2.0 * DEPTH) ** 0.25
LN_EPS = 1e-5
NEG_INF = -1e30
MACARON = 0.5

IN_SPLITS = (3 * SB_HEADS * HEAD_DIM, 3 * DIL_HEADS * HEAD_DIM, SSM_WIDTH, 3 * DSA_HEADS * HEAD_DIM,
             IDX_HEADS * IDX_DIM, IDX_DIM, IDX_HEADS, N_BRANCH * D_MODEL)

LANES = 128
ATT_BLOCK = 128
DSA_BLOCK = 256
SB_QUERY_BLOCK = 1024
SB_KEY_BLOCK = 256
SSM_CHUNK = 256
VMEM_LIMIT = 56 * 1024 * 1024
INT_MIN = -2 ** 31

_NT = (((1,), (1,)), ((), ()))


def _dot(a, b):
    return jnp.dot(a, b, preferred_element_type=F32)


def _dot_nt(a, b):
    return lax.dot_general(a, b, _NT, preferred_element_type=F32)


def _params(*sem):
    return pltpu.CompilerParams(dimension_semantics=sem, vmem_limit_bytes=VMEM_LIMIT)


def _sigmoid(x):
    return 1.0 / (1.0 + jnp.exp(-x))


def _mm_kernel(x_ref, w_ref, o_ref):
    o_ref[...] = _dot(x_ref[...], w_ref[...]).astype(o_ref.dtype)


def matmul(x, w, out_dtype, tm=1024, tn=512):
    m, k = x.shape
    n = w.shape[1]
    tn = min(tn, n)
    assert m % tm == 0 and n % tn == 0
    return pl.pallas_call(
        _mm_kernel,
        grid=(m // tm, n // tn),
        in_specs=[pl.BlockSpec((tm, k), lambda i, j: (i, 0)),
                  pl.BlockSpec((k, tn), lambda i, j: (0, j))],
        out_specs=pl.BlockSpec((tm, tn), lambda i, j: (i, j)),
        out_shape=jax.ShapeDtypeStruct((m, n), out_dtype),
        compiler_params=_params("parallel", "arbitrary"),
        name="matmul",
    )(x, w)


def _ffn_up_kernel(x_ref, wa_ref, wb_ref, o_ref):
    x = x_ref[...]
    a = _dot(x, wa_ref[...])
    b = _dot(x, wb_ref[...])
    o_ref[...] = (a * _sigmoid(a) * b).astype(o_ref.dtype)


def ffn_up(x, w_up, tm=1024, tn=512):
    m, k = x.shape
    f = w_up.shape[1] // 2
    assert m % tm == 0 and f % tn == 0
    nb = f // tn
    return pl.pallas_call(
        _ffn_up_kernel,
        grid=(m // tm, nb),
        in_specs=[pl.BlockSpec((tm, k), lambda i, j: (i, 0)),
                  pl.BlockSpec((k, tn), lambda i, j: (0, j)),
                  pl.BlockSpec((k, tn), lambda i, j: (0, j + nb))],
        out_specs=pl.BlockSpec((tm, tn), lambda i, j: (i, j)),
        out_shape=jax.ShapeDtypeStruct((m, f), BF16),
        compiler_params=_params("parallel", "arbitrary"),
        name="ffn_up",
    )(x, w_up, w_up)


def _mm_res_ln_kernel(h_ref, w_ref, x_ref, g_ref, b_ref, o_ref, obf_ref, *, scale):
    y = DN_ALPHA * x_ref[...] + scale * _dot(h_ref[...], w_ref[...])
    mu = jnp.mean(y, axis=-1, keepdims=True)
    yc = y - mu
    var = jnp.mean(yc * yc, axis=-1, keepdims=True)
    out = yc * lax.rsqrt(var + LN_EPS) * g_ref[...] + b_ref[...]
    o_ref[...] = out
    obf_ref[...] = out.astype(BF16)


def matmul_residual_layernorm(h, w, x, g, b, scale, tm=256):
    m, k = h.shape
    n = w.shape[1]
    assert m % tm == 0
    return pl.pallas_call(
        functools.partial(_mm_res_ln_kernel, scale=scale),
        grid=(m // tm,),
        in_specs=[pl.BlockSpec((tm, k), lambda i: (i, 0)),
                  pl.BlockSpec((k, n), lambda i: (0, 0), pipeline_mode=pl.Buffered(1)),
                  pl.BlockSpec((tm, n), lambda i: (i, 0)),
                  pl.BlockSpec((1, n), lambda i: (0, 0)),
                  pl.BlockSpec((1, n), lambda i: (0, 0))],
        out_specs=[pl.BlockSpec((tm, n), lambda i: (i, 0)),
                   pl.BlockSpec((tm, n), lambda i: (i, 0))],
        out_shape=[jax.ShapeDtypeStruct((m, n), F32), jax.ShapeDtypeStruct((m, n), BF16)],
        compiler_params=_params("parallel"),
        name="matmul_residual_layernorm",
    )(h, w, x, g.reshape(1, n), b.reshape(1, n))


def _gated_merge_kernel(x_ref, ysb_ref, ydil_ref, yssm_ref, ydsa_ref, wsb_ref, wdil_ref, wssm_ref, wdsa_ref,
                        g0_ref, g1_ref, g2_ref, g3_ref, o_ref):
    x = x_ref[...]
    acc = _sigmoid(_dot(x, g0_ref[...])) * _dot(ysb_ref[...], wsb_ref[...])
    acc += _sigmoid(_dot(x, g1_ref[...])) * _dot(ydil_ref[...], wdil_ref[...])
    acc += _sigmoid(_dot(x, g2_ref[...])) * _dot(yssm_ref[...], wssm_ref[...])
    acc += _sigmoid(_dot(x, g3_ref[...])) * _dot(ydsa_ref[...], wdsa_ref[...])
    o_ref[...] = acc.astype(o_ref.dtype)


def gated_branch_merge(x, ys, ws, w_gate, tm=1024, tn=512):
    m, k = x.shape
    n = ws[0].shape[1]
    assert m % tm == 0 and n % tn == 0
    nb = n // tn
    y_specs = [pl.BlockSpec((tm, y.shape[1]), lambda i, j: (i, 0)) for y in ys]
    w_specs = [pl.BlockSpec((w.shape[0], tn), lambda i, j: (0, j)) for w in ws]
    g_specs = [pl.BlockSpec((k, tn), functools.partial(lambda i, j, br: (0, j + br * nb), br=br))
               for br in range(N_BRANCH)]
    return pl.pallas_call(
        _gated_merge_kernel,
        grid=(m // tm, nb),
        in_specs=[pl.BlockSpec((tm, k), lambda i, j: (i, 0))] + y_specs + w_specs + g_specs,
        out_specs=pl.BlockSpec((tm, tn), lambda i, j: (i, j)),
        out_shape=jax.ShapeDtypeStruct((m, n), BF16),
        compiler_params=_params("parallel", "arbitrary"),
        name="gated_branch_merge",
    )(x, *ys, *ws, w_gate, w_gate, w_gate, w_gate)


def _blocked_transpose(x, blk):
    bsz, seq, c = x.shape
    return jnp.swapaxes(x.reshape(bsz, seq // blk, blk, c), 2, 3)


def _is_power_of_two(x):
    return math.frexp(x)[0] == 0.5


def _sb_kernel(qT_ref, k_ref, vT_ref, o_ref, acc_ref, *, qblk, kblk, scale):
    i = pl.program_id(2)
    ratio = qblk // kblk
    fold_scale = _is_power_of_two(scale)
    qT = qT_ref[0, 0]
    if fold_scale:
        qT = qT * jnp.asarray(scale, qT.dtype)
    row = lax.broadcasted_iota(jnp.int32, (LANES, 1), 0)
    zero = jnp.zeros_like(qT)
    q_rhs = jnp.concatenate([jnp.where(row < HEAD_DIM, qT, zero), jnp.where(row >= HEAD_DIM, qT, zero)], axis=1)
    key_io = lax.broadcasted_iota(jnp.int32, (kblk, kblk), 0)
    qry_io = lax.broadcasted_iota(jnp.int32, (kblk, kblk), 1)
    later = jnp.where(qry_io > key_io, 1.0, 0.0).astype(BF16)
    acc_ref[...] = jnp.zeros_like(acc_ref)
    nsub = 2 * ratio

    def block(j, carry, diag_sub):
        start = pl.multiple_of(j * kblk, kblk)
        z_all = _dot(k_ref[0, pl.ds(start, kblk), :], q_rhs)
        vT = vT_ref[0, j]
        strict = key_io < qry_io
        stage = []
        for c in range(nsub):
            h, sub = divmod(c, ratio)
            if diag_sub is not None and sub < diag_sub:
                stage.append(None)
                continue
            masked = diag_sub is not None and sub == diag_sub
            z = z_all[:, c * kblk:(c + 1) * kblk]
            if not fold_scale:
                z = z * scale
            sp = jnp.maximum(z, 0.0) + jnp.log(1.0 + jnp.exp(-jnp.abs(z)))
            log_1mb = jnp.where(strict, -sp, 0.0) if masked else -sp
            suf = _dot(later, log_1mb.astype(BF16))
            stage.append((z - sp, suf, jnp.sum(log_1mb, axis=0, keepdims=True), masked))
        new_carry = []
        for c in range(nsub):
            h, sub = divmod(c, ratio)
            if stage[c] is None:
                new_carry.append(carry[c])
                continue
            log_beta, suf, colsum, masked = stage[c]
            att = jnp.exp(log_beta + suf + carry[c])
            if masked:
                att = jnp.where(strict, att, 0.0)
            rows = slice(h * HEAD_DIM, (h + 1) * HEAD_DIM)
            cols = slice(sub * kblk, (sub + 1) * kblk)
            acc_ref[rows, cols] += _dot(vT[rows, :], att.astype(BF16))
            new_carry.append(carry[c] + colsum)
        return tuple(new_carry)

    carry = tuple(jnp.zeros((1, kblk), F32) for _ in range(nsub))
    for sub in reversed(range(ratio)):
        carry = block(i * ratio + sub, carry, sub)
    lax.fori_loop(0, i * ratio, lambda jj, c: block(i * ratio - 1 - jj, c, None), carry)
    o_ref[0] = acc_ref[...].T.astype(o_ref.dtype)


def stick_breaking_attention(qkv, qblk=SB_QUERY_BLOCK, kblk=SB_KEY_BLOCK):
    bsz, seq, _ = qkv.shape
    width = SB_HEADS * HEAD_DIM
    npair = width // LANES
    qblk = min(qblk, seq)
    assert seq % qblk == 0 and qblk % kblk == 0
    qT = _blocked_transpose(qkv[..., :width], qblk)
    vT = _blocked_transpose(qkv[..., 2 * width:], kblk)
    return pl.pallas_call(
        functools.partial(_sb_kernel, qblk=qblk, kblk=kblk, scale=HEAD_DIM ** -0.5),
        grid=(bsz, npair, seq // qblk),
        in_specs=[pl.BlockSpec((1, 1, LANES, qblk), lambda b, p, i: (b, i, p, 0)),
                  pl.BlockSpec((1, seq, LANES), lambda b, p, i: (b, 0, npair + p)),
                  pl.BlockSpec((1, seq // kblk, LANES, kblk), lambda b, p, i: (b, 0, p, 0))],
        out_specs=pl.BlockSpec((1, qblk, LANES), lambda b, p, i: (b, i, p)),
        out_shape=jax.ShapeDtypeStruct((bsz, seq, width), BF16),
        scratch_shapes=[pltpu.VMEM((LANES, qblk), F32)],
        compiler_params=_params("parallel", "parallel", "arbitrary"),
        name="stick_breaking_attention",
    )(qT, qkv, vT)


def _t5_bucket(dist):
    max_exact = REL_BUCKETS // 2
    d = jnp.maximum(dist, 1).astype(F32)
    large = max_exact + (jnp.log(d / max_exact) / math.log(REL_MAX_DIST / max_exact)
                         * (REL_BUCKETS - max_exact)).astype(jnp.int32)
    large = jnp.minimum(large, REL_BUCKETS - 1)
    return jnp.where(dist < max_exact, dist, large)


def _bias_of_distance(rel_bias, dist):
    one_hot = jax.nn.one_hot(_t5_bucket(dist), REL_BUCKETS, dtype=F32)
    return jnp.einsum("...b,bh->...h", one_hot, rel_bias.astype(F32), precision=lax.Precision.HIGHEST)


def _dilated_bias_tiles(rel_bias_group, dil, blk):
    a = jnp.arange(blk)[:, None]
    cc = jnp.arange(2 * blk)[None, :]
    step = blk + a - cc
    valid = (step >= 0) & (step <= blk)
    bias = _bias_of_distance(rel_bias_group, dil * jnp.clip(step, 0, blk))
    tile = jnp.where(valid[..., None], bias, NEG_INF)
    return jnp.transpose(tile, (2, 0, 1))


def _dsa_bias_tiles(rel_bias_dsa, blk):
    c = jnp.arange(blk)[:, None]
    a = jnp.arange(blk)[None, :]
    own = jnp.where((a - c >= 0)[..., None], _bias_of_distance(rel_bias_dsa, jnp.maximum(a - c, 0)), NEG_INF)
    prev = _bias_of_distance(rel_bias_dsa, blk + a - c)
    half = REL_BUCKETS // 2
    assert half + int(math.log((blk + 1) / half) / math.log(REL_MAX_DIST / half) * (REL_BUCKETS - half)) >= REL_BUCKETS - 1
    far = jnp.broadcast_to(rel_bias_dsa[REL_BUCKETS - 1].astype(F32), prev.shape)
    return jnp.transpose(jnp.stack([own, prev, far]), (3, 0, 1, 2))


def _dil_kernel(q_ref, k_ref, v_ref, bias_ref, o_ref, lse_ref, *, blk, scale):
    i = pl.program_id(2)
    width = DIL_HEADS_PER_GROUP * HEAD_DIM
    lane = lax.broadcasted_iota(jnp.int32, (1, width), 1)
    q = q_ref[0]
    zero = jnp.zeros_like(q)
    cur0 = pl.multiple_of(i * blk, blk)
    prev0 = pl.multiple_of(jnp.maximum(i - 1, 0) * blk, blk)
    kc = k_ref[0, pl.ds(cur0, blk), :]
    kp = k_ref[0, pl.ds(prev0, blk), :]
    vc = v_ref[0, pl.ds(cur0, blk), :]
    vp = v_ref[0, pl.ds(prev0, blk), :]
    has_prev = i > 0
    out = jnp.zeros((blk, width), F32)
    lse_b = jnp.zeros((blk, width), F32)
    for h in range(DIL_HEADS_PER_GROUP):
        head = (lane >= h * HEAD_DIM) & (lane < (h + 1) * HEAD_DIM)
        qm = jnp.where(head, q, zero)
        s_prev = _dot_nt(qm, kp) * scale + bias_ref[h, :, :blk]
        s_prev = jnp.where(has_prev, s_prev, NEG_INF)
        s_cur = _dot_nt(qm, kc) * scale + bias_ref[h, :, blk:]
        m = jnp.maximum(jnp.max(s_prev, axis=-1, keepdims=True), jnp.max(s_cur, axis=-1, keepdims=True))
        p_prev = jnp.exp(s_prev - m)
        p_cur = jnp.exp(s_cur - m)
        denom = jnp.sum(p_prev, axis=-1, keepdims=True) + jnp.sum(p_cur, axis=-1, keepdims=True)
        o = (_dot(p_prev.astype(BF16), vp) + _dot(p_cur.astype(BF16), vc)) / denom
        out = jnp.where(head, o, out)
        lse_b = jnp.where(head, m + jnp.log(denom), lse_b)
    o_ref[0] = out
    lse_ref[0] = lse_b


def dilated_group_attention(qkv, bias_tiles, group, dil, blk=ATT_BLOCK):
    bsz, seq, cols = qkv.shape
    width = DIL_HEADS_PER_GROUP * HEAD_DIM
    per_row = cols // width
    ngroups = len(DIL_PATTERNS)
    n = seq // dil
    assert n % blk == 0
    x = qkv.reshape(bsz, n, dil * cols)
    q_map = lambda b, c, i: (b, i, c * per_row + group)
    k_map = lambda b, c, i: (b, 0, c * per_row + ngroups + group)
    v_map = lambda b, c, i: (b, 0, c * per_row + 2 * ngroups + group)
    out, lse = pl.pallas_call(
        functools.partial(_dil_kernel, blk=blk, scale=HEAD_DIM ** -0.5),
        grid=(bsz, dil, n // blk),
        in_specs=[pl.BlockSpec((1, blk, width), q_map),
                  pl.BlockSpec((1, n, width), k_map),
                  pl.BlockSpec((1, n, width), v_map),
                  pl.BlockSpec((DIL_HEADS_PER_GROUP, blk, 2 * blk), lambda b, c, i: (0, 0, 0))],
        out_specs=[pl.BlockSpec((1, blk, width), lambda b, c, i: (b, i, c)),
                   pl.BlockSpec((1, blk, width), lambda b, c, i: (b, i, c))],
        out_shape=[jax.ShapeDtypeStruct((bsz, n, dil * width), F32),
                   jax.ShapeDtypeStruct((bsz, n, dil * width), F32)],
        compiler_params=_params("parallel", "parallel", "arbitrary"),
        name=f"dilated_attention_g{group}",
    )(x, x, x, bias_tiles)
    return out.reshape(bsz * seq, width), lse.reshape(bsz * seq, width)


def _dil_merge_kernel(o0_ref, o1_ref, o2_ref, l0_ref, l1_ref, l2_ref, y_ref):
    l0, l1, l2 = l0_ref[...], l1_ref[...], l2_ref[...]
    m = jnp.maximum(jnp.maximum(l0, l1), l2)
    w0, w1, w2 = jnp.exp(l0 - m), jnp.exp(l1 - m), jnp.exp(l2 - m)
    y = (w0 * o0_ref[...] + w1 * o1_ref[...] + w2 * o2_ref[...]) / (w0 + w1 + w2)
    y_ref[...] = y.astype(y_ref.dtype)


def dilated_merge(outs, lses, tm=1024):
    m, width = outs[0].shape
    spec = pl.BlockSpec((tm, width), lambda i: (i, 0))
    return pl.pallas_call(
        _dil_merge_kernel,
        grid=(m // tm,),
        in_specs=[spec] * 6,
        out_specs=spec,
        out_shape=jax.ShapeDtypeStruct((m, width), BF16),
        compiler_params=_params("parallel"),
        name="dilated_merge",
    )(*outs, *lses)


def _ssm_kernel(u_ref, bb_ref, apow_ref, cc_ref, d_ref, wg_ref, o_ref, bu_ref, carry_ref, *, chunk, nstrips):
    @pl.when(pl.program_id(1) == 0)
    def _():
        carry_ref[...] = jnp.zeros_like(carry_ref)

    u = u_ref[0]
    bu = _dot(u.astype(BF16), bb_ref[...])
    for s in range(2 * nstrips):
        bu_ref[s] = bu[:, s * LANES:(s + 1) * LANES]
    row = lax.broadcasted_iota(jnp.int32, (chunk, LANES), 0)
    nsteps = chunk.bit_length() - 1

    def strip(s, y):
        xr = bu_ref[s]
        xi = bu_ref[nstrips + s]
        pr = apow_ref[0, s]
        pi = apow_ref[1, s]
        for kstep in range(nsteps):
            sh = 1 << kstep
            ar = pr[sh - 1:sh, :]
            ai = pi[sh - 1:sh, :]
            sxr = jnp.where(row >= sh, pltpu.roll(xr, sh, 0), 0.0)
            sxi = jnp.where(row >= sh, pltpu.roll(xi, sh, 0), 0.0)
            xr, xi = xr + ar * sxr - ai * sxi, xi + ar * sxi + ai * sxr
        cr = carry_ref[0, s][0:1, :]
        ci = carry_ref[1, s][0:1, :]
        xr, xi = xr + pr * cr - pi * ci, xi + pr * ci + pi * cr
        carry_ref[0, s] = jnp.broadcast_to(xr[chunk - 1:chunk, :], (8, LANES))
        carry_ref[1, s] = jnp.broadcast_to(xi[chunk - 1:chunk, :], (8, LANES))
        return y + _dot(xr.astype(BF16), cc_ref[s]) + _dot(xi.astype(BF16), cc_ref[nstrips + s])

    y = lax.fori_loop(0, nstrips, strip, jnp.zeros(u.shape, F32))
    y = y + d_ref[...] * u
    y = 0.5 * y * (1.0 + jnp.tanh(math.sqrt(2.0 / math.pi) * (y + 0.044715 * (y * y * y))))
    z = _dot(y.astype(BF16), wg_ref[...])
    width = u.shape[1]
    o_ref[0] = (z[:, :width] * _sigmoid(z[:, width:])).astype(o_ref.dtype)


def _block_diag(blocks):
    g, r, c = blocks.shape
    eye = jnp.eye(g, dtype=blocks.dtype)
    return (eye[:, None, :, None] * blocks[:, :, None, :]).reshape(g * r, g * c)


def _ssm_tables(lam_re, lam_im, log_dt, b_re, b_im, c_re, c_im, chunk):
    lr, li = lam_re.astype(F32), lam_im.astype(F32)
    dt = jnp.exp(log_dt.astype(F32))[:, None]
    mag = jnp.exp(lr * dt)
    a_re, a_im = mag * jnp.cos(li * dt), mag * jnp.sin(li * dt)
    den = lr * lr + li * li
    f_re = ((a_re - 1.0) * lr + a_im * li) / den
    f_im = (a_im * lr - (a_re - 1.0) * li) / den
    br, bi = b_re.astype(F32), b_im.astype(F32)
    bb_re = f_re[..., None] * br - f_im[..., None] * bi
    bb_im = f_re[..., None] * bi + f_im[..., None] * br
    bb = jnp.concatenate([_block_diag(jnp.transpose(bb_re, (0, 2, 1))),
                          _block_diag(jnp.transpose(bb_im, (0, 2, 1)))], axis=1)
    cc = jnp.concatenate([_block_diag(jnp.transpose(c_re.astype(F32), (0, 2, 1))),
                          -_block_diag(jnp.transpose(c_im.astype(F32), (0, 2, 1)))], axis=0)
    nstates = SSM_GROUPS * SSM_STATE

    def cmul(x, y):
        return x[0] * y[0] - x[1] * y[1], x[0] * y[1] + x[1] * y[0]

    ar = jnp.broadcast_to(a_re.reshape(1, nstates), (chunk, nstates))
    ai = jnp.broadcast_to(a_im.reshape(1, nstates), (chunk, nstates))
    pw_re, pw_im = lax.associative_scan(cmul, (ar, ai), axis=0)
    nstrips = nstates // LANES
    apow = jnp.stack([pw_re, pw_im]).reshape(2, chunk, nstrips, LANES).transpose(0, 2, 1, 3)
    return bb.astype(BF16), apow, cc.astype(BF16).reshape(2 * nstrips, LANES, SSM_WIDTH)


def s5_glu(u, bb, apow, cc, d_skip, w_glu, chunk=SSM_CHUNK):
    bsz, seq, width = u.shape
    nstrips = apow.shape[1]
    assert seq % chunk == 0 and chunk & (chunk - 1) == 0
    const = lambda *shape: pl.BlockSpec(shape, lambda b, c: (0,) * len(shape))
    return pl.pallas_call(
        functools.partial(_ssm_kernel, chunk=chunk, nstrips=nstrips),
        grid=(bsz, seq // chunk),
        in_specs=[pl.BlockSpec((1, chunk, width), lambda b, c: (b, c, 0)),
                  const(*bb.shape), const(*apow.shape), const(*cc.shape),
                  const(1, width), const(*w_glu.shape)],
        out_specs=pl.BlockSpec((1, chunk, width), lambda b, c: (b, c, 0)),
        out_shape=jax.ShapeDtypeStruct((bsz, seq, width), BF16),
        scratch_shapes=[pltpu.VMEM((2 * nstrips, chunk, LANES), F32),
                        pltpu.VMEM((2, nstrips, 8, LANES), F32)],
        compiler_params=_params("parallel", "arbitrary"),
        name="s5_glu",
    )(u, bb, apow, cc, d_skip.reshape(1, width).astype(F32), w_glu)


def _dsa_kernel(qT_ref, k_ref, vT_ref, iqT_ref, ik_ref, iwT_ref, bias_ref, o_ref,
                key_ref, madd_ref, acc_ref, *, blk, topk, scale):
    i = pl.program_id(1)
    nblk = i + 1
    key_io = lax.broadcasted_iota(jnp.int32, (blk, blk), 0)
    qry_io = lax.broadcasted_iota(jnp.int32, (blk, blk), 1)

    iqT = iqT_ref[0, 0]
    idx_rhs = jnp.concatenate([iqT[h * IDX_DIM:(h + 1) * IDX_DIM, :] for h in range(IDX_HEADS)], axis=1)
    iw = iwT_ref[0]

    def score_block(j, _):
        start = pl.multiple_of(j * blk, blk)
        kk = ik_ref[0, pl.ds(start, blk), :][:, :IDX_DIM].astype(BF16)
        d = _dot(kk, idx_rhs)
        sc = jnp.zeros((blk, blk), F32)
        for h in range(IDX_HEADS):
            sc = sc + iw[h:h + 1, :] * jnp.maximum(d[:, h * blk:(h + 1) * blk], 0.0)
        sc = jnp.where((key_io + j * blk) <= (qry_io + i * blk), sc, NEG_INF)
        bits = pltpu.bitcast(sc, jnp.int32)
        key_ref[j] = bits ^ ((bits >> 31) & 0x7FFFFFFF)
        return 0

    lax.fori_loop(0, nblk, score_block, 0)

    def count(pred):
        def body(j, acc):
            hit = jnp.where(pred(key_ref[j]), 1.0, 0.0)
            return acc + jnp.sum(hit.reshape(blk // 8, 8, blk), axis=0)
        return jnp.sum(lax.fori_loop(0, nblk, body, jnp.zeros((8, blk), F32)), axis=0, keepdims=True)

    def bit_step(b, thr):
        cand = thr + lax.shift_left(jnp.int32(1), 31 - b)
        return jnp.where(count(lambda key: key >= cand) >= topk, cand, thr)

    thr = lax.fori_loop(0, 32, bit_step, jnp.full((1, blk), INT_MIN, jnp.int32))

    need = topk - count(lambda key: key > thr)
    upto = jnp.where(qry_io <= key_io, 1.0, 0.0).astype(BF16)

    def select_block(j, run):
        key = key_ref[j]
        tie = jnp.where(key == thr, 1.0, 0.0)
        rank = _dot(upto, tie.astype(BF16)) + run
        keep_tie = jnp.where(rank <= need, 0.0, NEG_INF)
        madd_ref[j] = jnp.where(key > thr, 0.0, jnp.where(key == thr, keep_tie, NEG_INF))
        return run + jnp.sum(tie, axis=0, keepdims=True)

    lax.fori_loop(0, nblk, select_block, jnp.zeros((1, blk), F32))

    fold_scale = _is_power_of_two(scale)
    qT = qT_ref[0, 0]
    if fold_scale:
        qT = qT * jnp.asarray(scale, qT.dtype)
    row = lax.broadcasted_iota(jnp.int32, (LANES, 1), 0)
    q_rhs = []
    for g in range(DSA_HEADS // 2):
        pair = qT[g * LANES:(g + 1) * LANES, :]
        zero = jnp.zeros_like(pair)
        q_rhs.append(jnp.concatenate([jnp.where(row < HEAD_DIM, pair, zero),
                                      jnp.where(row >= HEAD_DIM, pair, zero)], axis=1))
    acc_ref[...] = jnp.zeros_like(acc_ref)

    def attend_block(j, carry):
        m_all, l_all = carry
        start = pl.multiple_of(j * blk, blk)
        kblk = k_ref[0, pl.ds(start, blk), :]
        vT = vT_ref[0, j]
        madd = madd_ref[j]
        which = jnp.minimum(i - j, 2)
        s2 = [_dot(kblk[:, g * LANES:(g + 1) * LANES], q_rhs[g]) for g in range(DSA_HEADS // 2)]
        m_out, l_out, probs, alphas = [], [], [], []
        for h in range(DSA_HEADS):
            s = s2[h // 2][:, (h % 2) * blk:(h % 2 + 1) * blk]
            if not fold_scale:
                s = s * scale
            s = s + bias_ref[h, which] + madd
            m_new = jnp.maximum(m_all[h], jnp.max(s, axis=0, keepdims=True))
            p = jnp.exp(s - m_new)
            alpha = jnp.exp(m_all[h] - m_new)
            l_out.append(alpha * l_all[h] + jnp.sum(p, axis=0, keepdims=True))
            m_out.append(m_new)
            probs.append(p.astype(BF16))
            alphas.append(alpha)
        for h in range(DSA_HEADS):
            rows = slice(h * HEAD_DIM, (h + 1) * HEAD_DIM)
            acc_ref[rows, :] = alphas[h] * acc_ref[rows, :] + _dot(vT[rows, :], probs[h])
        return tuple(m_out), tuple(l_out)

    init = (tuple(jnp.full((1, blk), NEG_INF, F32) for _ in range(DSA_HEADS)),
            tuple(jnp.zeros((1, blk), F32) for _ in range(DSA_HEADS)))
    _, l_all = lax.fori_loop(0, nblk, attend_block, init)
    for h in range(DSA_HEADS):
        rows = slice(h * HEAD_DIM, (h + 1) * HEAD_DIM)
        acc_ref[rows, :] = acc_ref[rows, :] / l_all[h]
    o_ref[0] = acc_ref[...].T.astype(o_ref.dtype)


def dsa_attention(qkv, idx_q, idx_kw, bias_tiles, blk=DSA_BLOCK):
    bsz, seq, _ = qkv.shape
    width = DSA_HEADS * HEAD_DIM
    nblk = seq // blk
    topk = min(DSA_TOPK, seq // 4)
    qT = _blocked_transpose(qkv[..., :width], blk)
    vT = _blocked_transpose(qkv[..., 2 * width:], blk)
    iqT = _blocked_transpose(idx_q, blk)
    iwT = jnp.swapaxes(idx_kw[..., IDX_DIM:IDX_DIM + IDX_HEADS], 1, 2)
    return pl.pallas_call(
        functools.partial(_dsa_kernel, blk=blk, topk=topk, scale=HEAD_DIM ** -0.5),
        grid=(bsz, nblk),
        in_specs=[pl.BlockSpec((1, 1, width, blk), lambda b, i: (b, i, 0, 0)),
                  pl.BlockSpec((1, seq, width), lambda b, i: (b, 0, 1)),
                  pl.BlockSpec((1, nblk, width, blk), lambda b, i: (b, 0, 0, 0)),
                  pl.BlockSpec((1, 1, IDX_HEADS * IDX_DIM, blk), lambda b, i: (b, i, 0, 0)),
                  pl.BlockSpec((1, seq, LANES), lambda b, i: (b, 0, 0)),
                  pl.BlockSpec((1, IDX_HEADS, blk), lambda b, i: (b, 0, i)),
                  pl.BlockSpec((DSA_HEADS, 3, blk, blk), lambda b, i: (0, 0, 0, 0))],
        out_specs=pl.BlockSpec((1, blk, width), lambda b, i: (b, i, 0)),
        out_shape=jax.ShapeDtypeStruct((bsz, seq, width), BF16),
        scratch_shapes=[pltpu.VMEM((nblk, blk, blk), jnp.int32),
                        pltpu.VMEM((nblk, blk, blk), F32),
                        pltpu.VMEM((width, blk), F32)],
        compiler_params=_params("parallel", "arbitrary"),
        name="dsa_attention",
    )(qT, qkv, vT, iqT, idx_kw, iwT, bias_tiles)


def hybrid_mixer(xbf, bsz, seq, w_in, rel_bias, ssm_params, d_skip, w_glu, w_branches):
    offs = [0]
    for width in IN_SPLITS:
        offs.append(offs[-1] + width)
    col = lambda a, b: w_in[:, offs[a]:offs[b]].astype(BF16)
    w_idx_k = w_in[:, offs[5]:offs[6]]
    w_idx_w = w_in[:, offs[6]:offs[7]]
    w_small = jnp.concatenate([w_idx_k, w_idx_w,
                               jnp.zeros((w_in.shape[0], LANES - IDX_DIM - IDX_HEADS), w_in.dtype)], axis=1).astype(BF16)

    sb_qkv = matmul(xbf, col(0, 1), BF16).reshape(bsz, seq, -1)
    dil_qkv = matmul(xbf, col(1, 2), BF16, tn=768).reshape(bsz, seq, -1)
    ssm_u = matmul(xbf, col(2, 3), F32).reshape(bsz, seq, -1)
    dsa_qkv = matmul(xbf, col(3, 4), BF16).reshape(bsz, seq, -1)
    idx_q = matmul(xbf, col(4, 5), BF16).reshape(bsz, seq, -1)
    idx_kw = matmul(xbf, w_small, F32, tn=LANES).reshape(bsz, seq, -1)

    y_sb = stick_breaking_attention(sb_qkv).reshape(bsz * seq, -1)

    outs, lses = [], []
    for g, (window, dil) in enumerate(DIL_PATTERNS):
        assert window // dil == ATT_BLOCK
        tiles = _dilated_bias_tiles(rel_bias[:, g * DIL_HEADS_PER_GROUP:(g + 1) * DIL_HEADS_PER_GROUP], dil, ATT_BLOCK)
        o, lse = dilated_group_attention(dil_qkv, tiles, g, dil)
        outs.append(o)
        lses.append(lse)
    y_dil = dilated_merge(outs, lses)

    bb, apow, cc = _ssm_tables(*ssm_params, SSM_CHUNK)
    y_ssm = s5_glu(ssm_u, bb, apow, cc, d_skip, w_glu.astype(BF16)).reshape(bsz * seq, -1)

    y_dsa = dsa_attention(dsa_qkv, idx_q, idx_kw, _dsa_bias_tiles(rel_bias[:, DIL_HEADS:], DSA_BLOCK))
    y_dsa = y_dsa.reshape(bsz * seq, -1)

    return gated_branch_merge(xbf, [y_sb, y_dil, y_ssm, y_dsa], [w.astype(BF16) for w in w_branches], col(7, 8))


def kernel(x, ln_g, ln_b, ffn1_w_up, ffn1_w_down, w_in, rel_bias, ssm_lam_re, ssm_lam_im, ssm_log_dt,
           ssm_b_re, ssm_b_im, ssm_c_re, ssm_c_im, ssm_d, ssm_w_glu, w_br_sb, w_br_dil, w_br_ssm, w_br_dsa,
           w_out, ffn2_w_up, ffn2_w_down):
    bsz, seq, d = x.shape
    xf = x.reshape(bsz * seq, d)
    xbf = xf.astype(BF16)
    for l in range(DEPTH):
        h = ffn_up(xbf, ffn1_w_up[l].astype(BF16))
        xf, xbf = matmul_residual_layernorm(h, ffn1_w_down[l].astype(BF16), xf, ln_g[l, 0], ln_b[l, 0], MACARON)
        merged = hybrid_mixer(xbf, bsz, seq, w_in[l], rel_bias,
                              (ssm_lam_re[l], ssm_lam_im[l], ssm_log_dt[l], ssm_b_re[l], ssm_b_im[l],
                               ssm_c_re[l], ssm_c_im[l]), ssm_d[l], ssm_w_glu[l],
                              (w_br_sb[l], w_br_dil[l], w_br_ssm[l], w_br_dsa[l]))
        xf, xbf = matmul_residual_layernorm(merged, w_out[l].astype(BF16), xf, ln_g[l, 1], ln_b[l, 1], 1.0)
        h = ffn_up(xbf, ffn2_w_up[l].astype(BF16))
        xf, xbf = matmul_residual_layernorm(h, ffn2_w_down[l].astype(BF16), xf, ln_g[l, 2], ln_b[l, 2], MACARON)
    return xf.reshape(bsz, seq, d)
```

```python
import functools
import math

import jax
import jax.numpy as jnp
from jax import lax
from jax.experimental import pallas as pl
from jax.experimental.pallas import tpu as pltpu

F32 = jnp.float32
BF16 = jnp.bfloat16

D_MODEL = 2048
DEPTH = 2
HEAD_DIM = 64
SB_HEADS = 8
DIL_PATTERNS = ((128, 1), (512, 4), (2048, 16))
DIL_HEADS_PER_GROUP = 4
DIL_HEADS = DIL_HEADS_PER_GROUP * len(DIL_PATTERNS)
SSM_WIDTH = 512
SSM_GROUP = 16
SSM_GROUPS = SSM_WIDTH // SSM_GROUP
SSM_STATE = 64
DSA_HEADS = 8
IDX_HEADS = 8
IDX_DIM = 64
DSA_TOPK = 256
N_BRANCH = 4
D_FF = 5632
REL_BUCKETS = 32
REL_MAX_DIST = 128
DN_ALPHA = (2.0 * DEPTH) ** 0.25
LN_EPS = 1e-5
NEG_INF = -1e30
MACARON = 0.5

IN_SPLITS = (3 * SB_HEADS * HEAD_DIM, 3 * DIL_HEADS * HEAD_DIM, SSM_WIDTH, 3 * DSA_HEADS * HEAD_DIM,
             IDX_HEADS * IDX_DIM, IDX_DIM, IDX_HEADS, N_BRANCH * D_MODEL)

LANES = 128
ATT_BLOCK = 128
DSA_BLOCK = 256
SB_QUERY_BLOCK = 1024
SB_KEY_BLOCK = 256
SUBLANES = 8
SSM_STEPS = 32
SSM_STRIP = 512
VMEM_LIMIT = 56 * 1024 * 1024
INT_MIN = -2 ** 31

_NT = (((1,), (1,)), ((), ()))


def _dot(a, b):
    return jnp.dot(a, b, preferred_element_type=F32)


def _dot_nt(a, b):
    return lax.dot_general(a, b, _NT, preferred_element_type=F32)


def _params(*sem):
    return pltpu.CompilerParams(dimension_semantics=sem, vmem_limit_bytes=VMEM_LIMIT)


def _sigmoid(x):
    return 1.0 / (1.0 + jnp.exp(-x))


def _mm_kernel(x_ref, w_ref, o_ref):
    o_ref[...] = _dot(x_ref[...], w_ref[...]).astype(o_ref.dtype)


def matmul(x, w, out_dtype, tm=1024, tn=512):
    m, k = x.shape
    n = w.shape[1]
    tn = min(tn, n)
    assert m % tm == 0 and n % tn == 0
    return pl.pallas_call(
        _mm_kernel,
        grid=(m // tm, n // tn),
        in_specs=[pl.BlockSpec((tm, k), lambda i, j: (i, 0)),
                  pl.BlockSpec((k, tn), lambda i, j: (0, j))],
        out_specs=pl.BlockSpec((tm, tn), lambda i, j: (i, j)),
        out_shape=jax.ShapeDtypeStruct((m, n), out_dtype),
        compiler_params=_params("parallel", "arbitrary"),
        name="matmul",
    )(x, w)


def _ffn_up_kernel(x_ref, wa_ref, wb_ref, o_ref):
    x = x_ref[...]
    a = _dot(x, wa_ref[...])
    b = _dot(x, wb_ref[...])
    o_ref[...] = (a * _sigmoid(a) * b).astype(o_ref.dtype)


def ffn_up(x, w_up, tm=1024, tn=512):
    m, k = x.shape
    f = w_up.shape[1] // 2
    assert m % tm == 0 and f % tn == 0
    nb = f // tn
    return pl.pallas_call(
        _ffn_up_kernel,
        grid=(m // tm, nb),
        in_specs=[pl.BlockSpec((tm, k), lambda i, j: (i, 0)),
                  pl.BlockSpec((k, tn), lambda i, j: (0, j)),
                  pl.BlockSpec((k, tn), lambda i, j: (0, j + nb))],
        out_specs=pl.BlockSpec((tm, tn), lambda i, j: (i, j)),
        out_shape=jax.ShapeDtypeStruct((m, f), BF16),
        compiler_params=_params("parallel", "arbitrary"),
        name="ffn_up",
    )(x, w_up, w_up)


def _mm_res_ln_kernel(h_ref, w_ref, x_ref, g_ref, b_ref, o_ref, obf_ref, *, scale):
    y = DN_ALPHA * x_ref[...] + scale * _dot(h_ref[...], w_ref[...])
    mu = jnp.mean(y, axis=-1, keepdims=True)
    yc = y - mu
    var = jnp.mean(yc * yc, axis=-1, keepdims=True)
    out = yc * lax.rsqrt(var + LN_EPS) * g_ref[...] + b_ref[...]
    o_ref[...] = out
    obf_ref[...] = out.astype(BF16)


def matmul_residual_layernorm(h, w, x, g, b, scale, tm=256):
    m, k = h.shape
    n = w.shape[1]
    assert m % tm == 0
    return pl.pallas_call(
        functools.partial(_mm_res_ln_kernel, scale=scale),
        grid=(m // tm,),
        in_specs=[pl.BlockSpec((tm, k), lambda i: (i, 0)),
                  pl.BlockSpec((k, n), lambda i: (0, 0), pipeline_mode=pl.Buffered(1)),
                  pl.BlockSpec((tm, n), lambda i: (i, 0)),
                  pl.BlockSpec((1, n), lambda i: (0, 0)),
                  pl.BlockSpec((1, n), lambda i: (0, 0))],
        out_specs=[pl.BlockSpec((tm, n), lambda i: (i, 0)),
                   pl.BlockSpec((tm, n), lambda i: (i, 0))],
        out_shape=[jax.ShapeDtypeStruct((m, n), F32), jax.ShapeDtypeStruct((m, n), BF16)],
        compiler_params=_params("parallel"),
        name="matmul_residual_layernorm",
    )(h, w, x, g.reshape(1, n), b.reshape(1, n))


def _gated_merge_kernel(x_ref, ysb_ref, ydil_ref, yssm_ref, ydsa_ref, wsb_ref, wdil_ref, wssm_ref, wdsa_ref,
                        g0_ref, g1_ref, g2_ref, g3_ref, o_ref):
    x = x_ref[...]
    acc = _sigmoid(_dot(x, g0_ref[...])) * _dot(ysb_ref[...], wsb_ref[...])
    acc += _sigmoid(_dot(x, g1_ref[...])) * _dot(ydil_ref[...], wdil_ref[...])
    acc += _sigmoid(_dot(x, g2_ref[...])) * _dot(yssm_ref[...], wssm_ref[...])
    acc += _sigmoid(_dot(x, g3_ref[...])) * _dot(ydsa_ref[...], wdsa_ref[...])
    o_ref[...] = acc.astype(o_ref.dtype)


def gated_branch_merge(x, ys, ws, w_gate, tm=1024, tn=512):
    m, k = x.shape
    n = ws[0].shape[1]
    assert m % tm == 0 and n % tn == 0
    nb = n // tn
    y_specs = [pl.BlockSpec((tm, y.shape[1]), lambda i, j: (i, 0)) for y in ys]
    w_specs = [pl.BlockSpec((w.shape[0], tn), lambda i, j: (0, j)) for w in ws]
    g_specs = [pl.BlockSpec((k, tn), functools.partial(lambda i, j, br: (0, j + br * nb), br=br))
               for br in range(N_BRANCH)]
    return pl.pallas_call(
        _gated_merge_kernel,
        grid=(m // tm, nb),
        in_specs=[pl.BlockSpec((tm, k), lambda i, j: (i, 0))] + y_specs + w_specs + g_specs,
        out_specs=pl.BlockSpec((tm, tn), lambda i, j: (i, j)),
        out_shape=jax.ShapeDtypeStruct((m, n), BF16),
        compiler_params=_params("parallel", "arbitrary"),
        name="gated_branch_merge",
    )(x, *ys, *ws, w_gate, w_gate, w_gate, w_gate)


def _blocked_transpose(x, blk):
    bsz, seq, c = x.shape
    return jnp.swapaxes(x.reshape(bsz, seq // blk, blk, c), 2, 3)


def _is_power_of_two(x):
    return math.frexp(x)[0] == 0.5


def _sb_kernel(qT_ref, k_ref, vT_ref, o_ref, acc_ref, *, qblk, kblk, scale):
    i = pl.program_id(2)
    ratio = qblk // kblk
    fold_scale = _is_power_of_two(scale)
    qT = qT_ref[0, 0]
    if fold_scale:
        qT = qT * jnp.asarray(scale, qT.dtype)
    row = lax.broadcasted_iota(jnp.int32, (LANES, 1), 0)
    zero = jnp.zeros_like(qT)
    q_rhs = jnp.concatenate([jnp.where(row < HEAD_DIM, qT, zero), jnp.where(row >= HEAD_DIM, qT, zero)], axis=1)
    key_io = lax.broadcasted_iota(jnp.int32, (kblk, kblk), 0)
    qry_io = lax.broadcasted_iota(jnp.int32, (kblk, kblk), 1)
    later = jnp.where(qry_io > key_io, 1.0, 0.0).astype(BF16)
    acc_ref[...] = jnp.zeros_like(acc_ref)
    nsub = 2 * ratio

    def block(j, carry, diag_sub):
        start = pl.multiple_of(j * kblk, kblk)
        z_all = _dot(k_ref[0, pl.ds(start, kblk), :], q_rhs)
        vT = vT_ref[0, j]
        strict = key_io < qry_io
        stage = []
        for c in range(nsub):
            h, sub = divmod(c, ratio)
            if diag_sub is not None and sub < diag_sub:
                stage.append(None)
                continue
            masked = diag_sub is not None and sub == diag_sub
            z = z_all[:, c * kblk:(c + 1) * kblk]
            if not fold_scale:
                z = z * scale
            sp = jnp.maximum(z, 0.0) + jnp.log(1.0 + jnp.exp(-jnp.abs(z)))
            log_1mb = jnp.where(strict, -sp, 0.0) if masked else -sp
            suf = _dot(later, log_1mb.astype(BF16))
            stage.append((z - sp, suf, jnp.sum(log_1mb, axis=0, keepdims=True), masked))
        new_carry = []
        for c in range(nsub):
            h, sub = divmod(c, ratio)
            if stage[c] is None:
                new_carry.append(carry[c])
                continue
            log_beta, suf, colsum, masked = stage[c]
            att = jnp.exp(log_beta + suf + carry[c])
            if masked:
                att = jnp.where(strict, att, 0.0)
            rows = slice(h * HEAD_DIM, (h + 1) * HEAD_DIM)
            cols = slice(sub * kblk, (sub + 1) * kblk)
            acc_ref[rows, cols] += _dot(vT[rows, :], att.astype(BF16))
            new_carry.append(carry[c] + colsum)
        return tuple(new_carry)

    carry = tuple(jnp.zeros((1, kblk), F32) for _ in range(nsub))
    for sub in reversed(range(ratio)):
        carry = block(i * ratio + sub, carry, sub)
    lax.fori_loop(0, i * ratio, lambda jj, c: block(i * ratio - 1 - jj, c, None), carry)
    o_ref[0] = acc_ref[...].T.astype(o_ref.dtype)


def stick_breaking_attention(qkv, qblk=SB_QUERY_BLOCK, kblk=SB_KEY_BLOCK):
    bsz, seq, _ = qkv.shape
    width = SB_HEADS * HEAD_DIM
    npair = width // LANES
    qblk = min(qblk, seq)
    assert seq % qblk == 0 and qblk % kblk == 0
    qT = _blocked_transpose(qkv[..., :width], qblk)
    vT = _blocked_transpose(qkv[..., 2 * width:], kblk)
    return pl.pallas_call(
        functools.partial(_sb_kernel, qblk=qblk, kblk=kblk, scale=HEAD_DIM ** -0.5),
        grid=(bsz, npair, seq // qblk),
        in_specs=[pl.BlockSpec((1, 1, LANES, qblk), lambda b, p, i: (b, i, p, 0)),
                  pl.BlockSpec((1, seq, LANES), lambda b, p, i: (b, 0, npair + p)),
                  pl.BlockSpec((1, seq // kblk, LANES, kblk), lambda b, p, i: (b, 0, p, 0))],
        out_specs=pl.BlockSpec((1, qblk, LANES), lambda b, p, i: (b, i, p)),
        out_shape=jax.ShapeDtypeStruct((bsz, seq, width), BF16),
        scratch_shapes=[pltpu.VMEM((LANES, qblk), F32)],
        compiler_params=_params("parallel", "parallel", "arbitrary"),
        name="stick_breaking_attention",
    )(qT, qkv, vT)


def _t5_bucket(dist):
    max_exact = REL_BUCKETS // 2
    d = jnp.maximum(dist, 1).astype(F32)
    large = max_exact + (jnp.log(d / max_exact) / math.log(REL_MAX_DIST / max_exact)
                         * (REL_BUCKETS - max_exact)).astype(jnp.int32)
    large = jnp.minimum(large, REL_BUCKETS - 1)
    return jnp.where(dist < max_exact, dist, large)


def _bias_of_distance(rel_bias, dist):
    one_hot = jax.nn.one_hot(_t5_bucket(dist), REL_BUCKETS, dtype=F32)
    return jnp.einsum("...b,bh->...h", one_hot, rel_bias.astype(F32), precision=lax.Precision.HIGHEST)


def _dilated_bias_tiles(rel_bias_group, dil, blk):
    a = jnp.arange(blk)[:, None]
    cc = jnp.arange(2 * blk)[None, :]
    step = blk + a - cc
    valid = (step >= 0) & (step <= blk)
    bias = _bias_of_distance(rel_bias_group, dil * jnp.clip(step, 0, blk))
    tile = jnp.where(valid[..., None], bias, NEG_INF)
    return jnp.transpose(tile, (2, 0, 1))


def _dsa_bias_tiles(rel_bias_dsa, blk):
    c = jnp.arange(blk)[:, None]
    a = jnp.arange(blk)[None, :]
    own = jnp.where((a - c >= 0)[..., None], _bias_of_distance(rel_bias_dsa, jnp.maximum(a - c, 0)), NEG_INF)
    prev = _bias_of_distance(rel_bias_dsa, blk + a - c)
    half = REL_BUCKETS // 2
    assert half + int(math.log((blk + 1) / half) / math.log(REL_MAX_DIST / half) * (REL_BUCKETS - half)) >= REL_BUCKETS - 1
    far = jnp.broadcast_to(rel_bias_dsa[REL_BUCKETS - 1].astype(F32), prev.shape)
    return jnp.transpose(jnp.stack([own, prev, far]), (3, 0, 1, 2))


def _dil_kernel(q_ref, k_ref, v_ref, bias_ref, o_ref, lse_ref, *, blk, scale):
    i = pl.program_id(2)
    width = DIL_HEADS_PER_GROUP * HEAD_DIM
    lane = lax.broadcasted_iota(jnp.int32, (1, width), 1)
    q = q_ref[0]
    zero = jnp.zeros_like(q)
    cur0 = pl.multiple_of(i * blk, blk)
    prev0 = pl.multiple_of(jnp.maximum(i - 1, 0) * blk, blk)
    kc = k_ref[0, pl.ds(cur0, blk), :]
    kp = k_ref[0, pl.ds(prev0, blk), :]
    vc = v_ref[0, pl.ds(cur0, blk), :]
    vp = v_ref[0, pl.ds(prev0, blk), :]
    has_prev = i > 0
    out = jnp.zeros((blk, width), F32)
    lse_b = jnp.zeros((blk, width), F32)
    for h in range(DIL_HEADS_PER_GROUP):
        head = (lane >= h * HEAD_DIM) & (lane < (h + 1) * HEAD_DIM)
        qm = jnp.where(head, q, zero)
        s_prev = _dot_nt(qm, kp) * scale + bias_ref[h, :, :blk]
        s_prev = jnp.where(has_prev, s_prev, NEG_INF)
        s_cur = _dot_nt(qm, kc) * scale + bias_ref[h, :, blk:]
        m = jnp.maximum(jnp.max(s_prev, axis=-1, keepdims=True), jnp.max(s_cur, axis=-1, keepdims=True))
        p_prev = jnp.exp(s_prev - m)
        p_cur = jnp.exp(s_cur - m)
        denom = jnp.sum(p_prev, axis=-1, keepdims=True) + jnp.sum(p_cur, axis=-1, keepdims=True)
        o = (_dot(p_prev.astype(BF16), vp) + _dot(p_cur.astype(BF16), vc)) / denom
        out = jnp.where(head, o, out)
        lse_b = jnp.where(head, m + jnp.log(denom), lse_b)
    o_ref[0] = out
    lse_ref[0] = lse_b


def dilated_group_attention(qkv, bias_tiles, group, dil, blk=ATT_BLOCK):
    bsz, seq, cols = qkv.shape
    width = DIL_HEADS_PER_GROUP * HEAD_DIM
    per_row = cols // width
    ngroups = len(DIL_PATTERNS)
    n = seq // dil
    assert n % blk == 0
    x = qkv.reshape(bsz, n, dil * cols)
    q_map = lambda b, c, i: (b, i, c * per_row + group)
    k_map = lambda b, c, i: (b, 0, c * per_row + ngroups + group)
    v_map = lambda b, c, i: (b, 0, c * per_row + 2 * ngroups + group)
    out, lse = pl.pallas_call(
        functools.partial(_dil_kernel, blk=blk, scale=HEAD_DIM ** -0.5),
        grid=(bsz, dil, n // blk),
        in_specs=[pl.BlockSpec((1, blk, width), q_map),
                  pl.BlockSpec((1, n, width), k_map),
                  pl.BlockSpec((1, n, width), v_map),
                  pl.BlockSpec((DIL_HEADS_PER_GROUP, blk, 2 * blk), lambda b, c, i: (0, 0, 0))],
        out_specs=[pl.BlockSpec((1, blk, width), lambda b, c, i: (b, i, c)),
                   pl.BlockSpec((1, blk, width), lambda b, c, i: (b, i, c))],
        out_shape=[jax.ShapeDtypeStruct((bsz, n, dil * width), F32),
                   jax.ShapeDtypeStruct((bsz, n, dil * width), F32)],
        compiler_params=_params("parallel", "parallel", "arbitrary"),
        name=f"dilated_attention_g{group}",
    )(x, x, x, bias_tiles)
    return out.reshape(bsz * seq, width), lse.reshape(bsz * seq, width)


def _dil_merge_kernel(o0_ref, o1_ref, o2_ref, l0_ref, l1_ref, l2_ref, y_ref):
    l0, l1, l2 = l0_ref[...], l1_ref[...], l2_ref[...]
    m = jnp.maximum(jnp.maximum(l0, l1), l2)
    w0, w1, w2 = jnp.exp(l0 - m), jnp.exp(l1 - m), jnp.exp(l2 - m)
    y = (w0 * o0_ref[...] + w1 * o1_ref[...] + w2 * o2_ref[...]) / (w0 + w1 + w2)
    y_ref[...] = y.astype(y_ref.dtype)


def dilated_merge(outs, lses, tm=1024):
    m, width = outs[0].shape
    spec = pl.BlockSpec((tm, width), lambda i: (i, 0))
    return pl.pallas_call(
        _dil_merge_kernel,
        grid=(m // tm,),
        in_specs=[spec] * 6,
        out_specs=spec,
        out_shape=jax.ShapeDtypeStruct((m, width), BF16),
        compiler_params=_params("parallel"),
        name="dilated_merge",
    )(*outs, *lses)


def _ssm_kernel(u_ref, bb_ref, a_ref, cc_ref, d_ref, wg_ref, o_ref, x_ref, carry_ref, *, steps, strip):
    @pl.when(pl.program_id(1) == 0)
    def _():
        carry_ref[...] = jnp.zeros_like(carry_ref)

    u = u_ref[0]
    ub = u.astype(BF16)
    width = u.shape[1]
    half = width // 2
    nstates = x_ref.shape[1] // 2
    hs = nstates // 2
    for part in range(2):
        bu = _dot(ub[:, part * half:(part + 1) * half], bb_ref[part])
        x_ref[:, part * hs:(part + 1) * hs] = bu[:, :hs]
        x_ref[:, nstates + part * hs:nstates + (part + 1) * hs] = bu[:, hs:]
    for s in range(nstates // strip):
        re = slice(s * strip, (s + 1) * strip)
        im = slice(nstates + s * strip, nstates + (s + 1) * strip)
        ar = jnp.broadcast_to(a_ref[0:1, re], (SUBLANES, strip))
        ai = jnp.broadcast_to(a_ref[1:2, re], (SUBLANES, strip))
        xr = carry_ref[:, re]
        xi = carry_ref[:, im]
        for t in range(steps):
            rows = slice(t * SUBLANES, (t + 1) * SUBLANES)
            xr, xi = ar * xr - ai * xi + x_ref[rows, re], ar * xi + ai * xr + x_ref[rows, im]
            x_ref[rows, re] = xr
            x_ref[rows, im] = xi
        carry_ref[:, re] = xr
        carry_ref[:, im] = xi
    y_parts = []
    for part in range(2):
        xr = x_ref[:, part * hs:(part + 1) * hs].astype(BF16)
        xi = x_ref[:, nstates + part * hs:nstates + (part + 1) * hs].astype(BF16)
        y_parts.append(_dot(xr, cc_ref[0, part]) + _dot(xi, cc_ref[1, part]))
    y = jnp.concatenate(y_parts, axis=1) + d_ref[...] * u
    y = 0.5 * y * (1.0 + jnp.tanh(math.sqrt(2.0 / math.pi) * (y + 0.044715 * (y * y * y))))
    z = _dot(y.astype(BF16), wg_ref[...])
    o_ref[0] = (z[:, :width] * _sigmoid(z[:, width:])).astype(o_ref.dtype)


def _block_diag(blocks):
    g, r, c = blocks.shape
    eye = jnp.eye(g, dtype=blocks.dtype)
    return (eye[:, None, :, None] * blocks[:, :, None, :]).reshape(g * r, g * c)


def _ssm_tables(lam_re, lam_im, log_dt, b_re, b_im, c_re, c_im):
    lr, li = lam_re.astype(F32), lam_im.astype(F32)
    dt = jnp.exp(log_dt.astype(F32))[:, None]
    mag = jnp.exp(lr * dt)
    a_re, a_im = mag * jnp.cos(li * dt), mag * jnp.sin(li * dt)
    den = lr * lr + li * li
    f_re = ((a_re - 1.0) * lr + a_im * li) / den
    f_im = (a_im * lr - (a_re - 1.0) * li) / den
    br, bi = b_re.astype(F32), b_im.astype(F32)
    bb_re = _block_diag(jnp.transpose(f_re[..., None] * br - f_im[..., None] * bi, (0, 2, 1)))
    bb_im = _block_diag(jnp.transpose(f_re[..., None] * bi + f_im[..., None] * br, (0, 2, 1)))
    cc_re = _block_diag(jnp.transpose(c_re.astype(F32), (0, 2, 1)))
    cc_im = -_block_diag(jnp.transpose(c_im.astype(F32), (0, 2, 1)))
    nstates = SSM_GROUPS * SSM_STATE
    hw, hs = SSM_WIDTH // 2, nstates // 2
    bb = jnp.stack([jnp.concatenate([bb_re[p * hw:(p + 1) * hw, p * hs:(p + 1) * hs],
                                     bb_im[p * hw:(p + 1) * hw, p * hs:(p + 1) * hs]], axis=1) for p in range(2)])
    cc = jnp.stack([jnp.stack([m[p * hs:(p + 1) * hs, p * hw:(p + 1) * hw] for p in range(2)])
                    for m in (cc_re, cc_im)])
    a = jnp.stack([a_re.reshape(nstates), a_im.reshape(nstates)])
    return bb.astype(BF16), a, cc.astype(BF16)


def s5_glu(u, bb, a, cc, d_skip, w_glu, steps=SSM_STEPS, strip=SSM_STRIP):
    bsz, seq, width = u.shape
    nstates = a.shape[1]
    assert bsz % SUBLANES == 0 and seq % steps == 0 and nstates % strip == 0
    ngrp = bsz // SUBLANES
    rows = steps * SUBLANES
    ut = u.reshape(ngrp, SUBLANES, seq, width).transpose(0, 2, 1, 3).reshape(ngrp, seq * SUBLANES, width)
    const = lambda *shape: pl.BlockSpec(shape, lambda g, c: (0,) * len(shape))
    out = pl.pallas_call(
        functools.partial(_ssm_kernel, steps=steps, strip=strip),
        grid=(ngrp, seq // steps),
        in_specs=[pl.BlockSpec((1, rows, width), lambda g, c: (g, c, 0)),
                  const(*bb.shape), const(*a.shape), const(*cc.shape),
                  const(1, width), const(*w_glu.shape)],
        out_specs=pl.BlockSpec((1, rows, width), lambda g, c: (g, c, 0)),
        out_shape=jax.ShapeDtypeStruct((ngrp, seq * SUBLANES, width), BF16),
        scratch_shapes=[pltpu.VMEM((rows, 2 * nstates), F32),
                        pltpu.VMEM((SUBLANES, 2 * nstates), F32)],
        compiler_params=_params("parallel", "arbitrary"),
        name="s5_glu",
    )(ut, bb, a, cc, d_skip.reshape(1, width).astype(F32), w_glu)
    return out.reshape(ngrp, seq, SUBLANES, width).transpose(0, 2, 1, 3).reshape(bsz * seq, width)


def _dsa_kernel(qT_ref, k_ref, vT_ref, iqT_ref, ik_ref, iwT_ref, bias_ref, o_ref,
                key_ref, madd_ref, acc_ref, *, blk, topk, scale):
    i = pl.program_id(1)
    nblk = i + 1
    key_io = lax.broadcasted_iota(jnp.int32, (blk, blk), 0)
    qry_io = lax.broadcasted_iota(jnp.int32, (blk, blk), 1)

    iqT = iqT_ref[0, 0]
    idx_rhs = jnp.concatenate([iqT[h * IDX_DIM:(h + 1) * IDX_DIM, :] for h in range(IDX_HEADS)], axis=1)
    iw = iwT_ref[0]

    def score_block(j, _):
        start = pl.multiple_of(j * blk, blk)
        kk = ik_ref[0, pl.ds(start, blk), :][:, :IDX_DIM].astype(BF16)
        d = _dot(kk, idx_rhs)
        sc = jnp.zeros((blk, blk), F32)
        for h in range(IDX_HEADS):
            sc = sc + iw[h:h + 1, :] * jnp.maximum(d[:, h * blk:(h + 1) * blk], 0.0)
        sc = jnp.where((key_io + j * blk) <= (qry_io + i * blk), sc, NEG_INF)
        bits = pltpu.bitcast(sc, jnp.int32)
        key_ref[j] = bits ^ ((bits >> 31) & 0x7FFFFFFF)
        return 0

    lax.fori_loop(0, nblk, score_block, 0)

    def count(pred):
        def body(j, acc):
            hit = jnp.where(pred(key_ref[j]), 1.0, 0.0)
            return acc + jnp.sum(hit.reshape(blk // 8, 8, blk), axis=0)
        return jnp.sum(lax.fori_loop(0, nblk, body, jnp.zeros((8, blk), F32)), axis=0, keepdims=True)

    def bit_step(b, thr):
        cand = thr + lax.shift_left(jnp.int32(1), 31 - b)
        return jnp.where(count(lambda key: key >= cand) >= topk, cand, thr)

    thr = lax.fori_loop(0, 32, bit_step, jnp.full((1, blk), INT_MIN, jnp.int32))

    need = topk - count(lambda key: key > thr)
    upto = jnp.where(qry_io <= key_io, 1.0, 0.0).astype(BF16)

    def select_block(j, run):
        key = key_ref[j]
        tie = jnp.where(key == thr, 1.0, 0.0)
        rank = _dot(upto, tie.astype(BF16)) + run
        keep_tie = jnp.where(rank <= need, 0.0, NEG_INF)
        madd_ref[j] = jnp.where(key > thr, 0.0, jnp.where(key == thr, keep_tie, NEG_INF))
        return run + jnp.sum(tie, axis=0, keepdims=True)

    lax.fori_loop(0, nblk, select_block, jnp.zeros((1, blk), F32))

    fold_scale = _is_power_of_two(scale)
    qT = qT_ref[0, 0]
    if fold_scale:
        qT = qT * jnp.asarray(scale, qT.dtype)
    row = lax.broadcasted_iota(jnp.int32, (LANES, 1), 0)
    q_rhs = []
    for g in range(DSA_HEADS // 2):
        pair = qT[g * LANES:(g + 1) * LANES, :]
        zero = jnp.zeros_like(pair)
        q_rhs.append(jnp.concatenate([jnp.where(row < HEAD_DIM, pair, zero),
                                      jnp.where(row >= HEAD_DIM, pair, zero)], axis=1))
    acc_ref[...] = jnp.zeros_like(acc_ref)

    def attend_block(j, carry):
        m_all, l_all = carry
        start = pl.multiple_of(j * blk, blk)
        kblk = k_ref[0, pl.ds(start, blk), :]
        vT = vT_ref[0, j]
        madd = madd_ref[j]
        which = jnp.minimum(i - j, 2)
        s2 = [_dot(kblk[:, g * LANES:(g + 1) * LANES], q_rhs[g]) for g in range(DSA_HEADS // 2)]
        m_out, l_out, probs, alphas = [], [], [], []
        for h in range(DSA_HEADS):
            s = s2[h // 2][:, (h % 2) * blk:(h % 2 + 1) * blk]
            if not fold_scale:
                s = s * scale
            s = s + bias_ref[h, which] + madd
            m_new = jnp.maximum(m_all[h], jnp.max(s, axis=0, keepdims=True))
            p = jnp.exp(s - m_new)
            alpha = jnp.exp(m_all[h] - m_new)
            l_out.append(alpha * l_all[h] + jnp.sum(p, axis=0, keepdims=True))
            m_out.append(m_new)
            probs.append(p.astype(BF16))
            alphas.append(alpha)
        for h in range(DSA_HEADS):
            rows = slice(h * HEAD_DIM, (h + 1) * HEAD_DIM)
            acc_ref[rows, :] = alphas[h] * acc_ref[rows, :] + _dot(vT[rows, :], probs[h])
        return tuple(m_out), tuple(l_out)

    init = (tuple(jnp.full((1, blk), NEG_INF, F32) for _ in range(DSA_HEADS)),
            tuple(jnp.zeros((1, blk), F32) for _ in range(DSA_HEADS)))
    _, l_all = lax.fori_loop(0, nblk, attend_block, init)
    for h in range(DSA_HEADS):
        rows = slice(h * HEAD_DIM, (h + 1) * HEAD_DIM)
        acc_ref[rows, :] = acc_ref[rows, :] / l_all[h]
    o_ref[0] = acc_ref[...].T.astype(o_ref.dtype)


def dsa_attention(qkv, idx_q, idx_kw, bias_tiles, blk=DSA_BLOCK):
    bsz, seq, _ = qkv.shape
    width = DSA_HEADS * HEAD_DIM
    nblk = seq // blk
    topk = min(DSA_TOPK, seq // 4)
    qT = _blocked_transpose(qkv[..., :width], blk)
    vT = _blocked_transpose(qkv[..., 2 * width:], blk)
    iqT = _blocked_transpose(idx_q, blk)
    iwT = jnp.swapaxes(idx_kw[..., IDX_DIM:IDX_DIM + IDX_HEADS], 1, 2)
    return pl.pallas_call(
        functools.partial(_dsa_kernel, blk=blk, topk=topk, scale=HEAD_DIM ** -0.5),
        grid=(bsz, nblk),
        in_specs=[pl.BlockSpec((1, 1, width, blk), lambda b, i: (b, i, 0, 0)),
                  pl.BlockSpec((1, seq, width), lambda b, i: (b, 0, 1)),
                  pl.BlockSpec((1, nblk, width, blk), lambda b, i: (b, 0, 0, 0)),
                  pl.BlockSpec((1, 1, IDX_HEADS * IDX_DIM, blk), lambda b, i: (b, i, 0, 0)),
                  pl.BlockSpec((1, seq, LANES), lambda b, i: (b, 0, 0)),
                  pl.BlockSpec((1, IDX_HEADS, blk), lambda b, i: (b, 0, i)),
                  pl.BlockSpec((DSA_HEADS, 3, blk, blk), lambda b, i: (0, 0, 0, 0))],
        out_specs=pl.BlockSpec((1, blk, width), lambda b, i: (b, i, 0)),
        out_shape=jax.ShapeDtypeStruct((bsz, seq, width), BF16),
        scratch_shapes=[pltpu.VMEM((nblk, blk, blk), jnp.int32),
                        pltpu.VMEM((nblk, blk, blk), F32),
                        pltpu.VMEM((width, blk), F32)],
        compiler_params=_params("parallel", "arbitrary"),
        name="dsa_attention",
    )(qT, qkv, vT, iqT, idx_kw, iwT, bias_tiles)


def hybrid_mixer(xbf, bsz, seq, w_in, rel_bias, ssm_params, d_skip, w_glu, w_branches):
    offs = [0]
    for width in IN_SPLITS:
        offs.append(offs[-1] + width)
    col = lambda a, b: w_in[:, offs[a]:offs[b]].astype(BF16)
    w_idx_k = w_in[:, offs[5]:offs[6]]
    w_idx_w = w_in[:, offs[6]:offs[7]]
    w_small = jnp.concatenate([w_idx_k, w_idx_w,
                               jnp.zeros((w_in.shape[0], LANES - IDX_DIM - IDX_HEADS), w_in.dtype)], axis=1).astype(BF16)

    sb_qkv = matmul(xbf, col(0, 1), BF16).reshape(bsz, seq, -1)
    dil_qkv = matmul(xbf, col(1, 2), BF16, tn=768).reshape(bsz, seq, -1)
    ssm_u = matmul(xbf, col(2, 3), F32).reshape(bsz, seq, -1)
    dsa_qkv = matmul(xbf, col(3, 4), BF16).reshape(bsz, seq, -1)
    idx_q = matmul(xbf, col(4, 5), BF16).reshape(bsz, seq, -1)
    idx_kw = matmul(xbf, w_small, F32, tn=LANES).reshape(bsz, seq, -1)

    y_sb = stick_breaking_attention(sb_qkv).reshape(bsz * seq, -1)

    outs, lses = [], []
    for g, (window, dil) in enumerate(DIL_PATTERNS):
        assert window // dil == ATT_BLOCK
        tiles = _dilated_bias_tiles(rel_bias[:, g * DIL_HEADS_PER_GROUP:(g + 1) * DIL_HEADS_PER_GROUP], dil, ATT_BLOCK)
        o, lse = dilated_group_attention(dil_qkv, tiles, g, dil)
        outs.append(o)
        lses.append(lse)
    y_dil = dilated_merge(outs, lses)

    bb, a_bar, cc = _ssm_tables(*ssm_params)
    y_ssm = s5_glu(ssm_u, bb, a_bar, cc, d_skip, w_glu.astype(BF16))

    y_dsa = dsa_attention(dsa_qkv, idx_q, idx_kw, _dsa_bias_tiles(rel_bias[:, DIL_HEADS:], DSA_BLOCK))
    y_dsa = y_dsa.reshape(bsz * seq, -1)

    return gated_branch_merge(xbf, [y_sb, y_dil, y_ssm, y_dsa], [w.astype(BF16) for w in w_branches], col(7, 8))


def kernel(x, ln_g, ln_b, ffn1_w_up, ffn1_w_down, w_in, rel_bias, ssm_lam_re, ssm_lam_im, ssm_log_dt,
           ssm_b_re, ssm_b_im, ssm_c_re, ssm_c_im, ssm_d, ssm_w_glu, w_br_sb, w_br_dil, w_br_ssm, w_br_dsa,
           w_out, ffn2_w_up, ffn2_w_down):
    bsz, seq, d = x.shape
    xf = x.reshape(bsz * seq, d)
    xbf = xf.astype(BF16)
    for l in range(DEPTH):
        h = ffn_up(xbf, ffn1_w_up[l].astype(BF16))
        xf, xbf = matmul_residual_layernorm(h, ffn1_w_down[l].astype(BF16), xf, ln_g[l, 0], ln_b[l, 0], MACARON)
        merged = hybrid_mixer(xbf, bsz, seq, w_in[l], rel_bias,
                              (ssm_lam_re[l], ssm_lam_im[l], ssm_log_dt[l], ssm_b_re[l], ssm_b_im[l],
                               ssm_c_re[l], ssm_c_im[l]), ssm_d[l], ssm_w_glu[l],
                              (w_br_sb[l], w_br_dil[l], w_br_ssm[l], w_br_dsa[l]))
        xf, xbf = matmul_residual_layernorm(merged, w_out[l].astype(BF16), xf, ln_g[l, 1], ln_b[l, 1], 1.0)
        h = ffn_up(xbf, ffn2_w_up[l].astype(BF16))
        xf, xbf = matmul_residual_layernorm(h, ffn2_w_down[l].astype(BF16), xf, ln_g[l, 2], ln_b[l, 2], MACARON)
    return xf.reshape(bsz, seq, d)
```

```python
import functools
import math

import jax
import jax.numpy as jnp
from jax import lax
from jax.experimental import pallas as pl
from jax.experimental.pallas import tpu as pltpu

F32 = jnp.float32
BF16 = jnp.bfloat16

D_MODEL = 2048
DEPTH = 2
HEAD_DIM = 64
SB_HEADS = 8
DIL_PATTERNS = ((128, 1), (512, 4), (2048, 16))
DIL_HEADS_PER_GROUP = 4
DIL_HEADS = DIL_HEADS_PER_GROUP * len(DIL_PATTERNS)
SSM_WIDTH = 512
SSM_GROUP = 16
SSM_GROUPS = SSM_WIDTH // SSM_GROUP
SSM_STATE = 64
DSA_HEADS = 8
IDX_HEADS = 8
IDX_DIM = 64
DSA_TOPK = 256
N_BRANCH = 4
D_FF = 5632
REL_BUCKETS = 32
REL_MAX_DIST = 128
DN_ALPHA = (2.0 * DEPTH) ** 0.25
LN_EPS = 1e-5
NEG_INF = -1e30
MACARON = 0.5

IN_SPLITS = (3 * SB_HEADS * HEAD_DIM, 3 * DIL_HEADS * HEAD_DIM, SSM_WIDTH, 3 * DSA_HEADS * HEAD_DIM,
             IDX_HEADS * IDX_DIM, IDX_DIM, IDX_HEADS, N_BRANCH * D_MODEL)

LANES = 128
ATT_BLOCK = 128
DIL_UNITS = 2
DSA_BLOCK = 256
SB_QUERY_BLOCK = 1024
SB_KEY_BLOCK = 256
SUBLANES = 8
SSM_STEPS = 32
SSM_STRIP = 512
VMEM_LIMIT = 56 * 1024 * 1024
INT_MIN = -2 ** 31

_NT = (((1,), (1,)), ((), ()))


def _dot(a, b):
    return jnp.dot(a, b, preferred_element_type=F32)


def _dot_nt(a, b):
    return lax.dot_general(a, b, _NT, preferred_element_type=F32)


def _params(*sem):
    return pltpu.CompilerParams(dimension_semantics=sem, vmem_limit_bytes=VMEM_LIMIT)


def _sigmoid(x):
    return 1.0 / (1.0 + jnp.exp(-x))


def _mm_kernel(x_ref, w_ref, o_ref):
    o_ref[...] = _dot(x_ref[...], w_ref[...]).astype(o_ref.dtype)


def matmul(x, w, out_dtype, tm=1024, tn=512):
    m, k = x.shape
    n = w.shape[1]
    tn = min(tn, n)
    assert m % tm == 0 and n % tn == 0
    return pl.pallas_call(
        _mm_kernel,
        grid=(m // tm, n // tn),
        in_specs=[pl.BlockSpec((tm, k), lambda i, j: (i, 0)),
                  pl.BlockSpec((k, tn), lambda i, j: (0, j))],
        out_specs=pl.BlockSpec((tm, tn), lambda i, j: (i, j)),
        out_shape=jax.ShapeDtypeStruct((m, n), out_dtype),
        compiler_params=_params("parallel", "arbitrary"),
        name="matmul",
    )(x, w)


def _ffn_up_kernel(x_ref, wa_ref, wb_ref, o_ref):
    x = x_ref[...]
    a = _dot(x, wa_ref[...])
    b = _dot(x, wb_ref[...])
    o_ref[...] = (a * _sigmoid(a) * b).astype(o_ref.dtype)


def ffn_up(x, w_up, tm=1024, tn=512):
    m, k = x.shape
    f = w_up.shape[1] // 2
    assert m % tm == 0 and f % tn == 0
    nb = f // tn
    return pl.pallas_call(
        _ffn_up_kernel,
        grid=(m // tm, nb),
        in_specs=[pl.BlockSpec((tm, k), lambda i, j: (i, 0)),
                  pl.BlockSpec((k, tn), lambda i, j: (0, j)),
                  pl.BlockSpec((k, tn), lambda i, j: (0, j + nb))],
        out_specs=pl.BlockSpec((tm, tn), lambda i, j: (i, j)),
        out_shape=jax.ShapeDtypeStruct((m, f), BF16),
        compiler_params=_params("parallel", "arbitrary"),
        name="ffn_up",
    )(x, w_up, w_up)


def _mm_res_ln_kernel(h_ref, w_ref, x_ref, g_ref, b_ref, o_ref, obf_ref, *, scale):
    y = DN_ALPHA * x_ref[...] + scale * _dot(h_ref[...], w_ref[...])
    mu = jnp.mean(y, axis=-1, keepdims=True)
    yc = y - mu
    var = jnp.mean(yc * yc, axis=-1, keepdims=True)
    out = yc * lax.rsqrt(var + LN_EPS) * g_ref[...] + b_ref[...]
    o_ref[...] = out
    obf_ref[...] = out.astype(BF16)


def matmul_residual_layernorm(h, w, x, g, b, scale, tm=256):
    m, k = h.shape
    n = w.shape[1]
    assert m % tm == 0
    return pl.pallas_call(
        functools.partial(_mm_res_ln_kernel, scale=scale),
        grid=(m // tm,),
        in_specs=[pl.BlockSpec((tm, k), lambda i: (i, 0)),
                  pl.BlockSpec((k, n), lambda i: (0, 0), pipeline_mode=pl.Buffered(1)),
                  pl.BlockSpec((tm, n), lambda i: (i, 0)),
                  pl.BlockSpec((1, n), lambda i: (0, 0)),
                  pl.BlockSpec((1, n), lambda i: (0, 0))],
        out_specs=[pl.BlockSpec((tm, n), lambda i: (i, 0)),
                   pl.BlockSpec((tm, n), lambda i: (i, 0))],
        out_shape=[jax.ShapeDtypeStruct((m, n), F32), jax.ShapeDtypeStruct((m, n), BF16)],
        compiler_params=_params("parallel"),
        name="matmul_residual_layernorm",
    )(h, w, x, g.reshape(1, n), b.reshape(1, n))


def _gated_merge_kernel(x_ref, ysb_ref, ydil_ref, yssm_ref, ydsa_ref, wsb_ref, wdil_ref, wssm_ref, wdsa_ref,
                        g0_ref, g1_ref, g2_ref, g3_ref, o_ref):
    x = x_ref[...]
    acc = _sigmoid(_dot(x, g0_ref[...])) * _dot(ysb_ref[...], wsb_ref[...])
    acc += _sigmoid(_dot(x, g1_ref[...])) * _dot(ydil_ref[...], wdil_ref[...])
    acc += _sigmoid(_dot(x, g2_ref[...])) * _dot(yssm_ref[...], wssm_ref[...])
    acc += _sigmoid(_dot(x, g3_ref[...])) * _dot(ydsa_ref[...], wdsa_ref[...])
    o_ref[...] = acc.astype(o_ref.dtype)


def gated_branch_merge(x, ys, ws, w_gate, tm=1024, tn=512):
    m, k = x.shape
    n = ws[0].shape[1]
    assert m % tm == 0 and n % tn == 0
    nb = n // tn
    y_specs = [pl.BlockSpec((tm, y.shape[1]), lambda i, j: (i, 0)) for y in ys]
    w_specs = [pl.BlockSpec((w.shape[0], tn), lambda i, j: (0, j)) for w in ws]
    g_specs = [pl.BlockSpec((k, tn), functools.partial(lambda i, j, br: (0, j + br * nb), br=br))
               for br in range(N_BRANCH)]
    return pl.pallas_call(
        _gated_merge_kernel,
        grid=(m // tm, nb),
        in_specs=[pl.BlockSpec((tm, k), lambda i, j: (i, 0))] + y_specs + w_specs + g_specs,
        out_specs=pl.BlockSpec((tm, tn), lambda i, j: (i, j)),
        out_shape=jax.ShapeDtypeStruct((m, n), BF16),
        compiler_params=_params("parallel", "arbitrary"),
        name="gated_branch_merge",
    )(x, *ys, *ws, w_gate, w_gate, w_gate, w_gate)


def _blocked_transpose(x, blk):
    bsz, seq, c = x.shape
    return jnp.swapaxes(x.reshape(bsz, seq // blk, blk, c), 2, 3)


def _is_power_of_two(x):
    return math.frexp(x)[0] == 0.5


def _sb_kernel(qT_ref, k_ref, vT_ref, o_ref, acc_ref, *, qblk, kblk, scale):
    i = pl.program_id(2)
    ratio = qblk // kblk
    fold_scale = _is_power_of_two(scale)
    qT = qT_ref[0, 0]
    if fold_scale:
        qT = qT * jnp.asarray(scale, qT.dtype)
    row = lax.broadcasted_iota(jnp.int32, (LANES, 1), 0)
    zero = jnp.zeros_like(qT)
    q_rhs = jnp.concatenate([jnp.where(row < HEAD_DIM, qT, zero), jnp.where(row >= HEAD_DIM, qT, zero)], axis=1)
    key_io = lax.broadcasted_iota(jnp.int32, (kblk, kblk), 0)
    qry_io = lax.broadcasted_iota(jnp.int32, (kblk, kblk), 1)
    later = jnp.where(qry_io > key_io, 1.0, 0.0).astype(BF16)
    acc_ref[...] = jnp.zeros_like(acc_ref)
    nsub = 2 * ratio

    def block(j, carry, diag_sub):
        start = pl.multiple_of(j * kblk, kblk)
        z_all = _dot(k_ref[0, pl.ds(start, kblk), :], q_rhs)
        vT = vT_ref[0, j]
        strict = key_io < qry_io
        stage = []
        for c in range(nsub):
            h, sub = divmod(c, ratio)
            if diag_sub is not None and sub < diag_sub:
                stage.append(None)
                continue
            masked = diag_sub is not None and sub == diag_sub
            z = z_all[:, c * kblk:(c + 1) * kblk]
            if not fold_scale:
                z = z * scale
            sp = jnp.maximum(z, 0.0) + jnp.log(1.0 + jnp.exp(-jnp.abs(z)))
            log_1mb = jnp.where(strict, -sp, 0.0) if masked else -sp
            suf = _dot(later, log_1mb.astype(BF16))
            stage.append((z - sp, suf, jnp.sum(log_1mb, axis=0, keepdims=True), masked))
        new_carry = []
        for c in range(nsub):
            h, sub = divmod(c, ratio)
            if stage[c] is None:
                new_carry.append(carry[c])
                continue
            log_beta, suf, colsum, masked = stage[c]
            att = jnp.exp(log_beta + suf + carry[c])
            if masked:
                att = jnp.where(strict, att, 0.0)
            rows = slice(h * HEAD_DIM, (h + 1) * HEAD_DIM)
            cols = slice(sub * kblk, (sub + 1) * kblk)
            acc_ref[rows, cols] += _dot(vT[rows, :], att.astype(BF16))
            new_carry.append(carry[c] + colsum)
        return tuple(new_carry)

    carry = tuple(jnp.zeros((1, kblk), F32) for _ in range(nsub))
    for sub in reversed(range(ratio)):
        carry = block(i * ratio + sub, carry, sub)
    lax.fori_loop(0, i * ratio, lambda jj, c: block(i * ratio - 1 - jj, c, None), carry)
    o_ref[0] = acc_ref[...].T.astype(o_ref.dtype)


def stick_breaking_attention(qkv, qblk=SB_QUERY_BLOCK, kblk=SB_KEY_BLOCK):
    bsz, seq, _ = qkv.shape
    width = SB_HEADS * HEAD_DIM
    npair = width // LANES
    qblk = min(qblk, seq)
    assert seq % qblk == 0 and qblk % kblk == 0
    qT = _blocked_transpose(qkv[..., :width], qblk)
    vT = _blocked_transpose(qkv[..., 2 * width:], kblk)
    return pl.pallas_call(
        functools.partial(_sb_kernel, qblk=qblk, kblk=kblk, scale=HEAD_DIM ** -0.5),
        grid=(bsz, npair, seq // qblk),
        in_specs=[pl.BlockSpec((1, 1, LANES, qblk), lambda b, p, i: (b, i, p, 0)),
                  pl.BlockSpec((1, seq, LANES), lambda b, p, i: (b, 0, npair + p)),
                  pl.BlockSpec((1, seq // kblk, LANES, kblk), lambda b, p, i: (b, 0, p, 0))],
        out_specs=pl.BlockSpec((1, qblk, LANES), lambda b, p, i: (b, i, p)),
        out_shape=jax.ShapeDtypeStruct((bsz, seq, width), BF16),
        scratch_shapes=[pltpu.VMEM((LANES, qblk), F32)],
        compiler_params=_params("parallel", "parallel", "arbitrary"),
        name="stick_breaking_attention",
    )(qT, qkv, vT)


def _t5_bucket(dist):
    max_exact = REL_BUCKETS // 2
    d = jnp.maximum(dist, 1).astype(F32)
    large = max_exact + (jnp.log(d / max_exact) / math.log(REL_MAX_DIST / max_exact)
                         * (REL_BUCKETS - max_exact)).astype(jnp.int32)
    large = jnp.minimum(large, REL_BUCKETS - 1)
    return jnp.where(dist < max_exact, dist, large)


def _bias_of_distance(rel_bias, dist):
    one_hot = jax.nn.one_hot(_t5_bucket(dist), REL_BUCKETS, dtype=F32)
    return jnp.einsum("...b,bh->...h", one_hot, rel_bias.astype(F32), precision=lax.Precision.HIGHEST)


def _dilated_bias_tiles(rel_bias_group, dil, blk):
    a = jnp.arange(blk)[:, None]
    cc = jnp.arange(2 * blk)[None, :]
    step = blk + a - cc
    valid = (step >= 0) & (step <= blk)
    bias = _bias_of_distance(rel_bias_group, dil * jnp.clip(step, 0, blk))
    tile = jnp.where(valid[..., None], bias, NEG_INF)
    return jnp.transpose(tile, (2, 0, 1))


def _dsa_bias_tiles(rel_bias_dsa, blk):
    c = jnp.arange(blk)[:, None]
    a = jnp.arange(blk)[None, :]
    own = jnp.where((a - c >= 0)[..., None], _bias_of_distance(rel_bias_dsa, jnp.maximum(a - c, 0)), NEG_INF)
    prev = _bias_of_distance(rel_bias_dsa, blk + a - c)
    half = REL_BUCKETS // 2
    assert half + int(math.log((blk + 1) / half) / math.log(REL_MAX_DIST / half) * (REL_BUCKETS - half)) >= REL_BUCKETS - 1
    far = jnp.broadcast_to(rel_bias_dsa[REL_BUCKETS - 1].astype(F32), prev.shape)
    return jnp.transpose(jnp.stack([own, prev, far]), (3, 0, 1, 2))


def _dil_kernel(q0_ref, q1_ref, k0_ref, k1_ref, v0_ref, v1_ref, bias_ref, o_ref, lse_ref, *, blk, dil, nb, units, scale):
    step = pl.program_id(1)
    lane = lax.broadcasted_iota(jnp.int32, (1, LANES), 1)
    fold_scale = _is_power_of_two(scale)

    def rows(c, block):
        start = c + dil * blk * block
        return pl.ds(start, blk, stride=dil) if dil > 1 else pl.ds(pl.multiple_of(start, blk), blk)

    work = []
    for u in range(units):
        unit = step * units + u
        c, i = unit // nb, unit % nb
        cur, prev = rows(c, i), rows(c, jnp.maximum(i - 1, 0))
        for half, (q_ref, k_ref, v_ref) in enumerate(((q0_ref, k0_ref, v0_ref), (q1_ref, k1_ref, v1_ref))):
            q = q_ref[0, cur, :]
            if fold_scale:
                q = q * scale
            q = q.astype(BF16)
            kc, kp = k_ref[0, cur, :].astype(BF16), k_ref[0, prev, :].astype(BF16)
            zero = jnp.zeros_like(q)
            scores = []
            for e in range(2):
                head = (lane < HEAD_DIM) if e == 0 else (lane >= HEAD_DIM)
                qm = jnp.where(head, q, zero)
                scores.append((_dot_nt(qm, kp), _dot_nt(qm, kc)))
            work.append((i, cur, prev, half, v_ref, scores))
    soft = []
    for i, cur, prev, half, v_ref, scores in work:
        has_prev = i > 0
        parts = []
        for e in range(2):
            h = 2 * half + e
            s_prev, s_cur = scores[e]
            if not fold_scale:
                s_prev, s_cur = s_prev * scale, s_cur * scale
            s_prev = jnp.where(has_prev, s_prev + bias_ref[h, :, :blk], NEG_INF)
            s_cur = s_cur + bias_ref[h, :, blk:]
            m = jnp.maximum(jnp.max(s_prev, axis=-1, keepdims=True), jnp.max(s_cur, axis=-1, keepdims=True))
            p_prev = jnp.exp(s_prev - m)
            p_cur = jnp.exp(s_cur - m)
            denom = jnp.sum(p_prev, axis=-1, keepdims=True) + jnp.sum(p_cur, axis=-1, keepdims=True)
            parts.append((p_prev.astype(BF16), p_cur.astype(BF16), denom, m + jnp.log(denom)))
        soft.append(parts)
    for (i, cur, prev, half, v_ref, scores), parts in zip(work, soft):
        vc, vp = v_ref[0, cur, :].astype(BF16), v_ref[0, prev, :].astype(BF16)
        out = jnp.zeros((blk, LANES), F32)
        lse_b = jnp.zeros((blk, LANES), F32)
        for e in range(2):
            head = (lane < HEAD_DIM) if e == 0 else (lane >= HEAD_DIM)
            p_prev, p_cur, denom, lse = parts[e]
            o = (_dot(p_prev, vp) + _dot(p_cur, vc)) / denom
            out = jnp.where(head, o, out)
            lse_b = jnp.where(head, lse, lse_b)
        o_ref[0, half, cur, :] = out
        lse_ref[0, half, cur, :] = lse_b


def dilated_group_attention(qkv, bias_tiles, group, dil, blk=ATT_BLOCK, units=DIL_UNITS):
    bsz, seq, cols = qkv.shape
    assert DIL_HEADS_PER_GROUP * HEAD_DIM == 2 * LANES
    nslab = cols // LANES // 3
    n = seq // dil
    assert n % blk == 0
    nb = n // blk
    assert (dil * nb) % units == 0
    slab = lambda section, half: pl.BlockSpec(
        (1, seq, LANES), functools.partial(lambda b, s, col: (b, 0, col), col=section * nslab + 2 * group + half))
    out_spec = pl.BlockSpec((1, 2, seq, LANES), lambda b, s: (b, 0, 0, 0))
    return pl.pallas_call(
        functools.partial(_dil_kernel, blk=blk, dil=dil, nb=nb, units=units, scale=HEAD_DIM ** -0.5),
        grid=(bsz, dil * nb // units),
        in_specs=[slab(0, 0), slab(0, 1), slab(1, 0), slab(1, 1), slab(2, 0), slab(2, 1),
                  pl.BlockSpec((DIL_HEADS_PER_GROUP, blk, 2 * blk), lambda b, s: (0, 0, 0))],
        out_specs=[out_spec, out_spec],
        out_shape=[jax.ShapeDtypeStruct((bsz, 2, seq, LANES), F32)] * 2,
        compiler_params=_params("parallel", "arbitrary"),
        name=f"dilated_attention_g{group}",
    )(qkv, qkv, qkv, qkv, qkv, qkv, bias_tiles)


def _dil_merge_kernel(o0_ref, o1_ref, o2_ref, l0_ref, l1_ref, l2_ref, y_ref):
    for half in range(2):
        l0, l1, l2 = l0_ref[0, half], l1_ref[0, half], l2_ref[0, half]
        m = jnp.maximum(jnp.maximum(l0, l1), l2)
        w0, w1, w2 = jnp.exp(l0 - m), jnp.exp(l1 - m), jnp.exp(l2 - m)
        y = (w0 * o0_ref[0, half] + w1 * o1_ref[0, half] + w2 * o2_ref[0, half]) / (w0 + w1 + w2)
        y_ref[0, :, half * LANES:(half + 1) * LANES] = y.astype(y_ref.dtype)


def dilated_merge(outs, lses, tm=1024):
    bsz, _, seq, _ = outs[0].shape
    spec = pl.BlockSpec((1, 2, tm, LANES), lambda b, i: (b, 0, i, 0))
    return pl.pallas_call(
        _dil_merge_kernel,
        grid=(bsz, seq // tm),
        in_specs=[spec] * 6,
        out_specs=pl.BlockSpec((1, tm, 2 * LANES), lambda b, i: (b, i, 0)),
        out_shape=jax.ShapeDtypeStruct((bsz, seq, 2 * LANES), BF16),
        compiler_params=_params("parallel", "parallel"),
        name="dilated_merge",
    )(*outs, *lses)


def _ssm_kernel(u_ref, bb_ref, a_ref, cc_ref, d_ref, wg_ref, o_ref, x_ref, carry_ref, *, steps, strip):
    @pl.when(pl.program_id(1) == 0)
    def _():
        carry_ref[...] = jnp.zeros_like(carry_ref)

    u = u_ref[0]
    ub = u.astype(BF16)
    width = u.shape[1]
    half = width // 2
    nstates = x_ref.shape[1] // 2
    hs = nstates // 2
    for part in range(2):
        bu = _dot(ub[:, part * half:(part + 1) * half], bb_ref[part])
        x_ref[:, part * hs:(part + 1) * hs] = bu[:, :hs]
        x_ref[:, nstates + part * hs:nstates + (part + 1) * hs] = bu[:, hs:]
    for s in range(nstates // strip):
        re = slice(s * strip, (s + 1) * strip)
        im = slice(nstates + s * strip, nstates + (s + 1) * strip)
        ar = jnp.broadcast_to(a_ref[0:1, re], (SUBLANES, strip))
        ai = jnp.broadcast_to(a_ref[1:2, re], (SUBLANES, strip))
        xr = carry_ref[:, re]
        xi = carry_ref[:, im]
        for t in range(steps):
            rows = slice(t * SUBLANES, (t + 1) * SUBLANES)
            xr, xi = ar * xr - ai * xi + x_ref[rows, re], ar * xi + ai * xr + x_ref[rows, im]
            x_ref[rows, re] = xr
            x_ref[rows, im] = xi
        carry_ref[:, re] = xr
        carry_ref[:, im] = xi
    y_parts = []
    for part in range(2):
        xr = x_ref[:, part * hs:(part + 1) * hs].astype(BF16)
        xi = x_ref[:, nstates + part * hs:nstates + (part + 1) * hs].astype(BF16)
        y_parts.append(_dot(xr, cc_ref[0, part]) + _dot(xi, cc_ref[1, part]))
    y = jnp.concatenate(y_parts, axis=1) + d_ref[...] * u
    y = 0.5 * y * (1.0 + jnp.tanh(math.sqrt(2.0 / math.pi) * (y + 0.044715 * (y * y * y))))
    z = _dot(y.astype(BF16), wg_ref[...])
    o_ref[0] = (z[:, :width] * _sigmoid(z[:, width:])).astype(o_ref.dtype)


def _block_diag(blocks):
    g, r, c = blocks.shape
    eye = jnp.eye(g, dtype=blocks.dtype)
    return (eye[:, None, :, None] * blocks[:, :, None, :]).reshape(g * r, g * c)


def _ssm_tables(lam_re, lam_im, log_dt, b_re, b_im, c_re, c_im):
    lr, li = lam_re.astype(F32), lam_im.astype(F32)
    dt = jnp.exp(log_dt.astype(F32))[:, None]
    mag = jnp.exp(lr * dt)
    a_re, a_im = mag * jnp.cos(li * dt), mag * jnp.sin(li * dt)
    den = lr * lr + li * li
    f_re = ((a_re - 1.0) * lr + a_im * li) / den
    f_im = (a_im * lr - (a_re - 1.0) * li) / den
    br, bi = b_re.astype(F32), b_im.astype(F32)
    bb_re = _block_diag(jnp.transpose(f_re[..., None] * br - f_im[..., None] * bi, (0, 2, 1)))
    bb_im = _block_diag(jnp.transpose(f_re[..., None] * bi + f_im[..., None] * br, (0, 2, 1)))
    cc_re = _block_diag(jnp.transpose(c_re.astype(F32), (0, 2, 1)))
    cc_im = -_block_diag(jnp.transpose(c_im.astype(F32), (0, 2, 1)))
    nstates = SSM_GROUPS * SSM_STATE
    hw, hs = SSM_WIDTH // 2, nstates // 2
    bb = jnp.stack([jnp.concatenate([bb_re[p * hw:(p + 1) * hw, p * hs:(p + 1) * hs],
                                     bb_im[p * hw:(p + 1) * hw, p * hs:(p + 1) * hs]], axis=1) for p in range(2)])
    cc = jnp.stack([jnp.stack([m[p * hs:(p + 1) * hs, p * hw:(p + 1) * hw] for p in range(2)])
                    for m in (cc_re, cc_im)])
    a = jnp.stack([a_re.reshape(nstates), a_im.reshape(nstates)])
    return bb.astype(BF16), a, cc.astype(BF16)


def s5_glu(u, bb, a, cc, d_skip, w_glu, steps=SSM_STEPS, strip=SSM_STRIP):
    bsz, seq, width = u.shape
    nstates = a.shape[1]
    assert bsz % SUBLANES == 0 and seq % steps == 0 and nstates % strip == 0
    ngrp = bsz // SUBLANES
    rows = steps * SUBLANES
    ut = u.reshape(ngrp, SUBLANES, seq, width).transpose(0, 2, 1, 3).reshape(ngrp, seq * SUBLANES, width)
    const = lambda *shape: pl.BlockSpec(shape, lambda g, c: (0,) * len(shape))
    out = pl.pallas_call(
        functools.partial(_ssm_kernel, steps=steps, strip=strip),
        grid=(ngrp, seq // steps),
        in_specs=[pl.BlockSpec((1, rows, width), lambda g, c: (g, c, 0)),
                  const(*bb.shape), const(*a.shape), const(*cc.shape),
                  const(1, width), const(*w_glu.shape)],
        out_specs=pl.BlockSpec((1, rows, width), lambda g, c: (g, c, 0)),
        out_shape=jax.ShapeDtypeStruct((ngrp, seq * SUBLANES, width), BF16),
        scratch_shapes=[pltpu.VMEM((rows, 2 * nstates), F32),
                        pltpu.VMEM((SUBLANES, 2 * nstates), F32)],
        compiler_params=_params("parallel", "arbitrary"),
        name="s5_glu",
    )(ut, bb, a, cc, d_skip.reshape(1, width).astype(F32), w_glu)
    return out.reshape(ngrp, seq, SUBLANES, width).transpose(0, 2, 1, 3).reshape(bsz * seq, width)


def _dsa_kernel(qT_ref, k_ref, vT_ref, iqT_ref, ik_ref, iwT_ref, bias_ref, o_ref,
                key_ref, madd_ref, acc_ref, *, blk, topk, scale):
    i = pl.program_id(1)
    nblk = i + 1
    key_io = lax.broadcasted_iota(jnp.int32, (blk, blk), 0)
    qry_io = lax.broadcasted_iota(jnp.int32, (blk, blk), 1)

    iqT = iqT_ref[0, 0]
    idx_rhs = jnp.concatenate([iqT[h * IDX_DIM:(h + 1) * IDX_DIM, :] for h in range(IDX_HEADS)], axis=1)
    iw = iwT_ref[0]

    def score_block(j, _):
        start = pl.multiple_of(j * blk, blk)
        kk = ik_ref[0, pl.ds(start, blk), :][:, :IDX_DIM].astype(BF16)
        d = _dot(kk, idx_rhs)
        sc = jnp.zeros((blk, blk), F32)
        for h in range(IDX_HEADS):
            sc = sc + iw[h:h + 1, :] * jnp.maximum(d[:, h * blk:(h + 1) * blk], 0.0)
        sc = jnp.where((key_io + j * blk) <= (qry_io + i * blk), sc, NEG_INF)
        bits = pltpu.bitcast(sc, jnp.int32)
        key_ref[j] = bits ^ ((bits >> 31) & 0x7FFFFFFF)
        return 0

    lax.fori_loop(0, nblk, score_block, 0)

    def count(pred):
        def body(j, acc):
            hit = jnp.where(pred(key_ref[j]), 1.0, 0.0)
            return acc + jnp.sum(hit.reshape(blk // 8, 8, blk), axis=0)
        return jnp.sum(lax.fori_loop(0, nblk, body, jnp.zeros((8, blk), F32)), axis=0, keepdims=True)

    def bit_step(b, thr):
        cand = thr + lax.shift_left(jnp.int32(1), 31 - b)
        return jnp.where(count(lambda key: key >= cand) >= topk, cand, thr)

    thr = lax.fori_loop(0, 32, bit_step, jnp.full((1, blk), INT_MIN, jnp.int32))

    need = topk - count(lambda key: key > thr)
    upto = jnp.where(qry_io <= key_io, 1.0, 0.0).astype(BF16)

    def select_block(j, run):
        key = key_ref[j]
        tie = jnp.where(key == thr, 1.0, 0.0)
        rank = _dot(upto, tie.astype(BF16)) + run
        keep_tie = jnp.where(rank <= need, 0.0, NEG_INF)
        madd_ref[j] = jnp.where(key > thr, 0.0, jnp.where(key == thr, keep_tie, NEG_INF))
        return run + jnp.sum(tie, axis=0, keepdims=True)

    lax.fori_loop(0, nblk, select_block, jnp.zeros((1, blk), F32))

    fold_scale = _is_power_of_two(scale)
    qT = qT_ref[0, 0]
    if fold_scale:
        qT = qT * jnp.asarray(scale, qT.dtype)
    row = lax.broadcasted_iota(jnp.int32, (LANES, 1), 0)
    q_rhs = []
    for g in range(DSA_HEADS // 2):
        pair = qT[g * LANES:(g + 1) * LANES, :]
        zero = jnp.zeros_like(pair)
        q_rhs.append(jnp.concatenate([jnp.where(row < HEAD_DIM, pair, zero),
                                      jnp.where(row >= HEAD_DIM, pair, zero)], axis=1))
    acc_ref[...] = jnp.zeros_like(acc_ref)

    def attend_block(j, carry):
        m_all, l_all = carry
        start = pl.multiple_of(j * blk, blk)
        kblk = k_ref[0, pl.ds(start, blk), :]
        vT = vT_ref[0, j]
        madd = madd_ref[j]
        which = jnp.minimum(i - j, 2)
        s2 = [_dot(kblk[:, g * LANES:(g + 1) * LANES], q_rhs[g]) for g in range(DSA_HEADS // 2)]
        m_out, l_out, probs, alphas = [], [], [], []
        for h in range(DSA_HEADS):
            s = s2[h // 2][:, (h % 2) * blk:(h % 2 + 1) * blk]
            if not fold_scale:
                s = s * scale
            s = s + bias_ref[h, which] + madd
            m_new = jnp.maximum(m_all[h], jnp.max(s, axis=0, keepdims=True))
            p = jnp.exp(s - m_new)
            alpha = jnp.exp(m_all[h] - m_new)
            l_out.append(alpha * l_all[h] + jnp.sum(p, axis=0, keepdims=True))
            m_out.append(m_new)
            probs.append(p.astype(BF16))
            alphas.append(alpha)
        for h in range(DSA_HEADS):
            rows = slice(h * HEAD_DIM, (h + 1) * HEAD_DIM)
            acc_ref[rows, :] = alphas[h] * acc_ref[rows, :] + _dot(vT[rows, :], probs[h])
        return tuple(m_out), tuple(l_out)

    init = (tuple(jnp.full((1, blk), NEG_INF, F32) for _ in range(DSA_HEADS)),
            tuple(jnp.zeros((1, blk), F32) for _ in range(DSA_HEADS)))
    _, l_all = lax.fori_loop(0, nblk, attend_block, init)
    for h in range(DSA_HEADS):
        rows = slice(h * HEAD_DIM, (h + 1) * HEAD_DIM)
        acc_ref[rows, :] = acc_ref[rows, :] / l_all[h]
    o_ref[0] = acc_ref[...].T.astype(o_ref.dtype)


def dsa_attention(qkv, idx_q, idx_kw, bias_tiles, blk=DSA_BLOCK):
    bsz, seq, _ = qkv.shape
    width = DSA_HEADS * HEAD_DIM
    nblk = seq // blk
    topk = min(DSA_TOPK, seq // 4)
    qT = _blocked_transpose(qkv[..., :width], blk)
    vT = _blocked_transpose(qkv[..., 2 * width:], blk)
    iqT = _blocked_transpose(idx_q, blk)
    iwT = jnp.swapaxes(idx_kw[..., IDX_DIM:IDX_DIM + IDX_HEADS], 1, 2)
    return pl.pallas_call(
        functools.partial(_dsa_kernel, blk=blk, topk=topk, scale=HEAD_DIM ** -0.5),
        grid=(bsz, nblk),
        in_specs=[pl.BlockSpec((1, 1, width, blk), lambda b, i: (b, i, 0, 0)),
                  pl.BlockSpec((1, seq, width), lambda b, i: (b, 0, 1)),
                  pl.BlockSpec((1, nblk, width, blk), lambda b, i: (b, 0, 0, 0)),
                  pl.BlockSpec((1, 1, IDX_HEADS * IDX_DIM, blk), lambda b, i: (b, i, 0, 0)),
                  pl.BlockSpec((1, seq, LANES), lambda b, i: (b, 0, 0)),
                  pl.BlockSpec((1, IDX_HEADS, blk), lambda b, i: (b, 0, i)),
                  pl.BlockSpec((DSA_HEADS, 3, blk, blk), lambda b, i: (0, 0, 0, 0))],
        out_specs=pl.BlockSpec((1, blk, width), lambda b, i: (b, i, 0)),
        out_shape=jax.ShapeDtypeStruct((bsz, seq, width), BF16),
        scratch_shapes=[pltpu.VMEM((nblk, blk, blk), jnp.int32),
                        pltpu.VMEM((nblk, blk, blk), F32),
                        pltpu.VMEM((width, blk), F32)],
        compiler_params=_params("parallel", "arbitrary"),
        name="dsa_attention",
    )(qT, qkv, vT, iqT, idx_kw, iwT, bias_tiles)


def hybrid_mixer(xbf, bsz, seq, w_in, rel_bias, ssm_params, d_skip, w_glu, w_branches):
    offs = [0]
    for width in IN_SPLITS:
        offs.append(offs[-1] + width)
    col = lambda a, b: w_in[:, offs[a]:offs[b]].astype(BF16)
    w_idx_k = w_in[:, offs[5]:offs[6]]
    w_idx_w = w_in[:, offs[6]:offs[7]]
    w_small = jnp.concatenate([w_idx_k, w_idx_w,
                               jnp.zeros((w_in.shape[0], LANES - IDX_DIM - IDX_HEADS), w_in.dtype)], axis=1).astype(BF16)

    sb_qkv = matmul(xbf, col(0, 1), BF16).reshape(bsz, seq, -1)
    dil_qkv = matmul(xbf, col(1, 2), F32, tn=768).reshape(bsz, seq, -1)
    ssm_u = matmul(xbf, col(2, 3), F32).reshape(bsz, seq, -1)
    dsa_qkv = matmul(xbf, col(3, 4), BF16).reshape(bsz, seq, -1)
    idx_q = matmul(xbf, col(4, 5), BF16).reshape(bsz, seq, -1)
    idx_kw = matmul(xbf, w_small, F32, tn=LANES).reshape(bsz, seq, -1)

    y_sb = stick_breaking_attention(sb_qkv).reshape(bsz * seq, -1)

    outs, lses = [], []
    for g, (window, dil) in enumerate(DIL_PATTERNS):
        assert window // dil == ATT_BLOCK
        tiles = _dilated_bias_tiles(rel_bias[:, g * DIL_HEADS_PER_GROUP:(g + 1) * DIL_HEADS_PER_GROUP], dil, ATT_BLOCK)
        o, lse = dilated_group_attention(dil_qkv, tiles, g, dil)
        outs.append(o)
        lses.append(lse)
    y_dil = dilated_merge(outs, lses).reshape(bsz * seq, -1)

    bb, a_bar, cc = _ssm_tables(*ssm_params)
    y_ssm = s5_glu(ssm_u, bb, a_bar, cc, d_skip, w_glu.astype(BF16))

    y_dsa = dsa_attention(dsa_qkv, idx_q, idx_kw, _dsa_bias_tiles(rel_bias[:, DIL_HEADS:], DSA_BLOCK))
    y_dsa = y_dsa.reshape(bsz * seq, -1)

    return gated_branch_merge(xbf, [y_sb, y_dil, y_ssm, y_dsa], [w.astype(BF16) for w in w_branches], col(7, 8))


def kernel(x, ln_g, ln_b, ffn1_w_up, ffn1_w_down, w_in, rel_bias, ssm_lam_re, ssm_lam_im, ssm_log_dt,
           ssm_b_re, ssm_b_im, ssm_c_re, ssm_c_im, ssm_d, ssm_w_glu, w_br_sb, w_br_dil, w_br_ssm, w_br_dsa,
           w_out, ffn2_w_up, ffn2_w_down):
    bsz, seq, d = x.shape
    xf = x.reshape(bsz * seq, d)
    xbf = xf.astype(BF16)
    for l in range(DEPTH):
        h = ffn_up(xbf, ffn1_w_up[l].astype(BF16))
        xf, xbf = matmul_residual_layernorm(h, ffn1_w_down[l].astype(BF16), xf, ln_g[l, 0], ln_b[l, 0], MACARON)
        merged = hybrid_mixer(xbf, bsz, seq, w_in[l], rel_bias,
                              (ssm_lam_re[l], ssm_lam_im[l], ssm_log_dt[l], ssm_b_re[l], ssm_b_im[l],
                               ssm_c_re[l], ssm_c_im[l]), ssm_d[l], ssm_w_glu[l],
                              (w_br_sb[l], w_br_dil[l], w_br_ssm[l], w_br_dsa[l]))
        xf, xbf = matmul_residual_layernorm(merged, w_out[l].astype(BF16), xf, ln_g[l, 1], ln_b[l, 1], 1.0)
        h = ffn_up(xbf, ffn2_w_up[l].astype(BF16))
        xf, xbf = matmul_residual_layernorm(h, ffn2_w_down[l].astype(BF16), xf, ln_g[l, 2], ln_b[l, 2], MACARON)
    return xf.reshape(bsz, seq, d)
```

```python
import functools
import math

import jax
import jax.numpy as jnp
from jax import lax
from jax.experimental import pallas as pl
from jax.experimental.pallas import tpu as pltpu

F32 = jnp.float32
BF16 = jnp.bfloat16

D_MODEL = 2048
DEPTH = 2
HEAD_DIM = 64
SB_HEADS = 8
DIL_PATTERNS = ((128, 1), (512, 4), (2048, 16))
DIL_HEADS_PER_GROUP = 4
DIL_HEADS = DIL_HEADS_PER_GROUP * len(DIL_PATTERNS)
SSM_WIDTH = 512
SSM_GROUP = 16
SSM_GROUPS = SSM_WIDTH // SSM_GROUP
SSM_STATE = 64
DSA_HEADS = 8
IDX_HEADS = 8
IDX_DIM = 64
DSA_TOPK = 256
N_BRANCH = 4
D_FF = 5632
REL_BUCKETS = 32
REL_MAX_DIST = 128
DN_ALPHA = (2.0 * DEPTH) ** 0.25
LN_EPS = 1e-5
NEG_INF = -1e30
MACARON = 0.5

IN_SPLITS = (3 * SB_HEADS * HEAD_DIM, 3 * DIL_HEADS * HEAD_DIM, SSM_WIDTH, 3 * DSA_HEADS * HEAD_DIM,
             IDX_HEADS * IDX_DIM, IDX_DIM, IDX_HEADS, N_BRANCH * D_MODEL)

LANES = 128
ATT_BLOCK = 128
DIL_UNITS = 2
DSA_BLOCK = 256
SB_QUERY_BLOCK = 1024
SB_KEY_BLOCK = 256
SUBLANES = 8
SSM_STEPS = 32
SSM_STRIP = 512
VMEM_LIMIT = 56 * 1024 * 1024
INT_MIN = -2 ** 31

_NT = (((1,), (1,)), ((), ()))


def _dot(a, b):
    return jnp.dot(a, b, preferred_element_type=F32)


def _dot_nt(a, b):
    return lax.dot_general(a, b, _NT, preferred_element_type=F32)


def _params(*sem):
    return pltpu.CompilerParams(dimension_semantics=sem, vmem_limit_bytes=VMEM_LIMIT)


def _sigmoid(x):
    return 1.0 / (1.0 + jnp.exp(-x))


def _mm_kernel(x_ref, w_ref, o_ref):
    o_ref[...] = _dot(x_ref[...], w_ref[...]).astype(o_ref.dtype)


def matmul(x, w, out_dtype, tm=512):
    m, k = x.shape
    n = w.shape[1]
    assert m % tm == 0 and n % LANES == 0
    return pl.pallas_call(
        _mm_kernel,
        grid=(m // tm,),
        in_specs=[pl.BlockSpec((tm, k), lambda i: (i, 0)),
                  pl.BlockSpec((k, n), lambda i: (0, 0), pipeline_mode=pl.Buffered(1))],
        out_specs=pl.BlockSpec((tm, n), lambda i: (i, 0)),
        out_shape=jax.ShapeDtypeStruct((m, n), out_dtype),
        compiler_params=_params("parallel"),
        name="matmul",
    )(x, w)


def _ffn_up_kernel(x_ref, wa_ref, wb_ref, o_ref):
    x = x_ref[...]
    a = _dot(x, wa_ref[...])
    b = _dot(x, wb_ref[...])
    o_ref[...] = (a * _sigmoid(a) * b).astype(o_ref.dtype)


def ffn_up(x, w_up, tm=1024, tn=512):
    m, k = x.shape
    f = w_up.shape[1] // 2
    assert m % tm == 0 and f % tn == 0
    nb = f // tn
    return pl.pallas_call(
        _ffn_up_kernel,
        grid=(m // tm, nb),
        in_specs=[pl.BlockSpec((tm, k), lambda i, j: (i, 0)),
                  pl.BlockSpec((k, tn), lambda i, j: (0, j)),
                  pl.BlockSpec((k, tn), lambda i, j: (0, j + nb))],
        out_specs=pl.BlockSpec((tm, tn), lambda i, j: (i, j)),
        out_shape=jax.ShapeDtypeStruct((m, f), BF16),
        compiler_params=_params("parallel", "arbitrary"),
        name="ffn_up",
    )(x, w_up, w_up)


def _mm_res_ln_kernel(h_ref, w_ref, x_ref, g_ref, b_ref, o_ref, obf_ref, *, scale):
    y = DN_ALPHA * x_ref[...] + scale * _dot(h_ref[...], w_ref[...])
    mu = jnp.mean(y, axis=-1, keepdims=True)
    yc = y - mu
    var = jnp.mean(yc * yc, axis=-1, keepdims=True)
    out = yc * lax.rsqrt(var + LN_EPS) * g_ref[...] + b_ref[...]
    o_ref[...] = out
    obf_ref[...] = out.astype(BF16)


def matmul_residual_layernorm(h, w, x, g, b, scale, tm=256):
    m, k = h.shape
    n = w.shape[1]
    assert m % tm == 0
    return pl.pallas_call(
        functools.partial(_mm_res_ln_kernel, scale=scale),
        grid=(m // tm,),
        in_specs=[pl.BlockSpec((tm, k), lambda i: (i, 0)),
                  pl.BlockSpec((k, n), lambda i: (0, 0), pipeline_mode=pl.Buffered(1)),
                  pl.BlockSpec((tm, n), lambda i: (i, 0)),
                  pl.BlockSpec((1, n), lambda i: (0, 0)),
                  pl.BlockSpec((1, n), lambda i: (0, 0))],
        out_specs=[pl.BlockSpec((tm, n), lambda i: (i, 0)),
                   pl.BlockSpec((tm, n), lambda i: (i, 0))],
        out_shape=[jax.ShapeDtypeStruct((m, n), F32), jax.ShapeDtypeStruct((m, n), BF16)],
        compiler_params=_params("parallel"),
        name="matmul_residual_layernorm",
    )(h, w, x, g.reshape(1, n), b.reshape(1, n))


def _gated_merge_kernel(x_ref, ysb_ref, ydil_ref, yssm_ref, ydsa_ref, wsb_ref, wdil_ref, wssm_ref, wdsa_ref,
                        g0_ref, g1_ref, g2_ref, g3_ref, o_ref):
    x = x_ref[...]
    acc = _sigmoid(_dot(x, g0_ref[...])) * _dot(ysb_ref[...], wsb_ref[...])
    acc += _sigmoid(_dot(x, g1_ref[...])) * _dot(ydil_ref[...], wdil_ref[...])
    acc += _sigmoid(_dot(x, g2_ref[...])) * _dot(yssm_ref[...], wssm_ref[...])
    acc += _sigmoid(_dot(x, g3_ref[...])) * _dot(ydsa_ref[...], wdsa_ref[...])
    o_ref[...] = acc.astype(o_ref.dtype)


def gated_branch_merge(x, ys, ws, w_gate, tm=1024, tn=512):
    m, k = x.shape
    n = ws[0].shape[1]
    assert m % tm == 0 and n % tn == 0
    nb = n // tn
    y_specs = [pl.BlockSpec((tm, y.shape[1]), lambda i, j: (i, 0)) for y in ys]
    w_specs = [pl.BlockSpec((w.shape[0], tn), lambda i, j: (0, j)) for w in ws]
    g_specs = [pl.BlockSpec((k, tn), functools.partial(lambda i, j, br: (0, j + br * nb), br=br))
               for br in range(N_BRANCH)]
    return pl.pallas_call(
        _gated_merge_kernel,
        grid=(m // tm, nb),
        in_specs=[pl.BlockSpec((tm, k), lambda i, j: (i, 0))] + y_specs + w_specs + g_specs,
        out_specs=pl.BlockSpec((tm, tn), lambda i, j: (i, j)),
        out_shape=jax.ShapeDtypeStruct((m, n), BF16),
        compiler_params=_params("parallel", "arbitrary"),
        name="gated_branch_merge",
    )(x, *ys, *ws, w_gate, w_gate, w_gate, w_gate)


def _blocked_transpose(x, blk):
    bsz, seq, c = x.shape
    return jnp.swapaxes(x.reshape(bsz, seq // blk, blk, c), 2, 3)


def _is_power_of_two(x):
    return math.frexp(x)[0] == 0.5


def _sb_kernel(qT_ref, k_ref, vT_ref, o_ref, acc_ref, *, qblk, kblk, scale):
    i = pl.program_id(2)
    ratio = qblk // kblk
    fold_scale = _is_power_of_two(scale)
    qT = qT_ref[0, 0]
    if fold_scale:
        qT = qT * jnp.asarray(scale, qT.dtype)
    row = lax.broadcasted_iota(jnp.int32, (LANES, 1), 0)
    zero = jnp.zeros_like(qT)
    q_rhs = jnp.concatenate([jnp.where(row < HEAD_DIM, qT, zero), jnp.where(row >= HEAD_DIM, qT, zero)], axis=1)
    key_io = lax.broadcasted_iota(jnp.int32, (kblk, kblk), 0)
    qry_io = lax.broadcasted_iota(jnp.int32, (kblk, kblk), 1)
    later = jnp.where(qry_io > key_io, 1.0, 0.0).astype(BF16)
    acc_ref[...] = jnp.zeros_like(acc_ref)
    nsub = 2 * ratio

    def block(j, carry, diag_sub):
        start = pl.multiple_of(j * kblk, kblk)
        z_all = _dot(k_ref[0, pl.ds(start, kblk), :], q_rhs)
        vT = vT_ref[0, j]
        strict = key_io < qry_io
        stage = []
        for c in range(nsub):
            h, sub = divmod(c, ratio)
            if diag_sub is not None and sub < diag_sub:
                stage.append(None)
                continue
            masked = diag_sub is not None and sub == diag_sub
            z = z_all[:, c * kblk:(c + 1) * kblk]
            if not fold_scale:
                z = z * scale
            sp = jnp.maximum(z, 0.0) + jnp.log(1.0 + jnp.exp(-jnp.abs(z)))
            log_1mb = jnp.where(strict, -sp, 0.0) if masked else -sp
            suf = _dot(later, log_1mb.astype(BF16))
            stage.append((z - sp, suf, jnp.sum(log_1mb, axis=0, keepdims=True), masked))
        new_carry = []
        for c in range(nsub):
            h, sub = divmod(c, ratio)
            if stage[c] is None:
                new_carry.append(carry[c])
                continue
            log_beta, suf, colsum, masked = stage[c]
            att = jnp.exp(log_beta + suf + carry[c])
            if masked:
                att = jnp.where(strict, att, 0.0)
            rows = slice(h * HEAD_DIM, (h + 1) * HEAD_DIM)
            cols = slice(sub * kblk, (sub + 1) * kblk)
            acc_ref[rows, cols] += _dot(vT[rows, :], att.astype(BF16))
            new_carry.append(carry[c] + colsum)
        return tuple(new_carry)

    carry = tuple(jnp.zeros((1, kblk), F32) for _ in range(nsub))
    for sub in reversed(range(ratio)):
        carry = block(i * ratio + sub, carry, sub)
    lax.fori_loop(0, i * ratio, lambda jj, c: block(i * ratio - 1 - jj, c, None), carry)
    o_ref[0] = acc_ref[...].T.astype(o_ref.dtype)


def stick_breaking_attention(qkv, col0=0, qblk=SB_QUERY_BLOCK, kblk=SB_KEY_BLOCK):
    bsz, seq, _ = qkv.shape
    width = SB_HEADS * HEAD_DIM
    npair = width // LANES
    qblk = min(qblk, seq)
    assert seq % qblk == 0 and qblk % kblk == 0 and col0 % LANES == 0
    kcol = col0 // LANES + npair
    qT = _blocked_transpose(qkv[..., col0:col0 + width], qblk)
    vT = _blocked_transpose(qkv[..., col0 + 2 * width:col0 + 3 * width], kblk)
    return pl.pallas_call(
        functools.partial(_sb_kernel, qblk=qblk, kblk=kblk, scale=HEAD_DIM ** -0.5),
        grid=(bsz, npair, seq // qblk),
        in_specs=[pl.BlockSpec((1, 1, LANES, qblk), lambda b, p, i: (b, i, p, 0)),
                  pl.BlockSpec((1, seq, LANES), lambda b, p, i: (b, 0, kcol + p)),
                  pl.BlockSpec((1, seq // kblk, LANES, kblk), lambda b, p, i: (b, 0, p, 0))],
        out_specs=pl.BlockSpec((1, qblk, LANES), lambda b, p, i: (b, i, p)),
        out_shape=jax.ShapeDtypeStruct((bsz, seq, width), BF16),
        scratch_shapes=[pltpu.VMEM((LANES, qblk), F32)],
        compiler_params=_params("parallel", "parallel", "arbitrary"),
        name="stick_breaking_attention",
    )(qT, qkv, vT)


def _t5_bucket(dist):
    max_exact = REL_BUCKETS // 2
    d = jnp.maximum(dist, 1).astype(F32)
    large = max_exact + (jnp.log(d / max_exact) / math.log(REL_MAX_DIST / max_exact)
                         * (REL_BUCKETS - max_exact)).astype(jnp.int32)
    large = jnp.minimum(large, REL_BUCKETS - 1)
    return jnp.where(dist < max_exact, dist, large)


def _bias_of_distance(rel_bias, dist):
    one_hot = jax.nn.one_hot(_t5_bucket(dist), REL_BUCKETS, dtype=F32)
    return jnp.einsum("...b,bh->...h", one_hot, rel_bias.astype(F32), precision=lax.Precision.HIGHEST)


def _dilated_bias_tiles(rel_bias_group, dil, blk):
    a = jnp.arange(blk)[:, None]
    cc = jnp.arange(2 * blk)[None, :]
    step = blk + a - cc
    valid = (step >= 0) & (step <= blk)
    bias = _bias_of_distance(rel_bias_group, dil * jnp.clip(step, 0, blk))
    tile = jnp.where(valid[..., None], bias, NEG_INF)
    return jnp.transpose(tile, (2, 0, 1))


def _dsa_bias_tiles(rel_bias_dsa, blk):
    c = jnp.arange(blk)[:, None]
    a = jnp.arange(blk)[None, :]
    own = jnp.where((a - c >= 0)[..., None], _bias_of_distance(rel_bias_dsa, jnp.maximum(a - c, 0)), NEG_INF)
    prev = _bias_of_distance(rel_bias_dsa, blk + a - c)
    half = REL_BUCKETS // 2
    assert half + int(math.log((blk + 1) / half) / math.log(REL_MAX_DIST / half) * (REL_BUCKETS - half)) >= REL_BUCKETS - 1
    far = jnp.broadcast_to(rel_bias_dsa[REL_BUCKETS - 1].astype(F32), prev.shape)
    return jnp.transpose(jnp.stack([own, prev, far]), (3, 0, 1, 2))


def _dil_kernel(q0_ref, q1_ref, k0_ref, k1_ref, v0_ref, v1_ref, bias_ref, o_ref, lse_ref, *, blk, dil, nb, units, scale):
    step = pl.program_id(1)
    lane = lax.broadcasted_iota(jnp.int32, (1, LANES), 1)
    fold_scale = _is_power_of_two(scale)

    def rows(c, block):
        start = c + dil * blk * block
        return pl.ds(start, blk, stride=dil) if dil > 1 else pl.ds(pl.multiple_of(start, blk), blk)

    work = []
    for u in range(units):
        unit = step * units + u
        c, i = unit // nb, unit % nb
        cur, prev = rows(c, i), rows(c, jnp.maximum(i - 1, 0))
        for half, (q_ref, k_ref, v_ref) in enumerate(((q0_ref, k0_ref, v0_ref), (q1_ref, k1_ref, v1_ref))):
            q = q_ref[0, cur, :]
            if fold_scale:
                q = q * scale
            q = q.astype(BF16)
            kc, kp = k_ref[0, cur, :].astype(BF16), k_ref[0, prev, :].astype(BF16)
            zero = jnp.zeros_like(q)
            scores = []
            for e in range(2):
                head = (lane < HEAD_DIM) if e == 0 else (lane >= HEAD_DIM)
                qm = jnp.where(head, q, zero)
                scores.append((_dot_nt(qm, kp), _dot_nt(qm, kc)))
            work.append((i, cur, prev, half, v_ref, scores))
    soft = []
    for i, cur, prev, half, v_ref, scores in work:
        has_prev = i > 0
        parts = []
        for e in range(2):
            h = 2 * half + e
            s_prev, s_cur = scores[e]
            if not fold_scale:
                s_prev, s_cur = s_prev * scale, s_cur * scale
            s_prev = jnp.where(has_prev, s_prev + bias_ref[h, :, :blk], NEG_INF)
            s_cur = s_cur + bias_ref[h, :, blk:]
            m = jnp.maximum(jnp.max(s_prev, axis=-1, keepdims=True), jnp.max(s_cur, axis=-1, keepdims=True))
            p_prev = jnp.exp(s_prev - m)
            p_cur = jnp.exp(s_cur - m)
            denom = jnp.sum(p_prev, axis=-1, keepdims=True) + jnp.sum(p_cur, axis=-1, keepdims=True)
            parts.append((p_prev.astype(BF16), p_cur.astype(BF16), denom, m + jnp.log(denom)))
        soft.append(parts)
    for (i, cur, prev, half, v_ref, scores), parts in zip(work, soft):
        vc, vp = v_ref[0, cur, :].astype(BF16), v_ref[0, prev, :].astype(BF16)
        out = jnp.zeros((blk, LANES), F32)
        lse_b = jnp.zeros((blk, LANES), F32)
        for e in range(2):
            head = (lane < HEAD_DIM) if e == 0 else (lane >= HEAD_DIM)
            p_prev, p_cur, denom, lse = parts[e]
            o = (_dot(p_prev, vp) + _dot(p_cur, vc)) / denom
            out = jnp.where(head, o, out)
            lse_b = jnp.where(head, lse, lse_b)
        o_ref[0, half, cur, :] = out
        lse_ref[0, half, cur, :] = lse_b


def dilated_group_attention(qkv, bias_tiles, group, dil, blk=ATT_BLOCK, units=DIL_UNITS):
    bsz, seq, _ = qkv.shape
    assert DIL_HEADS_PER_GROUP * HEAD_DIM == 2 * LANES
    nslab = DIL_HEADS * HEAD_DIM // LANES
    n = seq // dil
    assert n % blk == 0
    nb = n // blk
    assert (dil * nb) % units == 0
    slab = lambda section, half: pl.BlockSpec(
        (1, seq, LANES), functools.partial(lambda b, s, col: (b, 0, col), col=section * nslab + 2 * group + half))
    out_spec = pl.BlockSpec((1, 2, seq, LANES), lambda b, s: (b, 0, 0, 0))
    return pl.pallas_call(
        functools.partial(_dil_kernel, blk=blk, dil=dil, nb=nb, units=units, scale=HEAD_DIM ** -0.5),
        grid=(bsz, dil * nb // units),
        in_specs=[slab(0, 0), slab(0, 1), slab(1, 0), slab(1, 1), slab(2, 0), slab(2, 1),
                  pl.BlockSpec((DIL_HEADS_PER_GROUP, blk, 2 * blk), lambda b, s: (0, 0, 0))],
        out_specs=[out_spec, out_spec],
        out_shape=[jax.ShapeDtypeStruct((bsz, 2, seq, LANES), F32)] * 2,
        compiler_params=_params("parallel", "arbitrary"),
        name=f"dilated_attention_g{group}",
    )(qkv, qkv, qkv, qkv, qkv, qkv, bias_tiles)


def _dil_merge_kernel(o0_ref, o1_ref, o2_ref, l0_ref, l1_ref, l2_ref, y_ref):
    for half in range(2):
        l0, l1, l2 = l0_ref[0, half], l1_ref[0, half], l2_ref[0, half]
        m = jnp.maximum(jnp.maximum(l0, l1), l2)
        w0, w1, w2 = jnp.exp(l0 - m), jnp.exp(l1 - m), jnp.exp(l2 - m)
        y = (w0 * o0_ref[0, half] + w1 * o1_ref[0, half] + w2 * o2_ref[0, half]) / (w0 + w1 + w2)
        y_ref[0, :, half * LANES:(half + 1) * LANES] = y.astype(y_ref.dtype)


def dilated_merge(outs, lses, tm=1024):
    bsz, _, seq, _ = outs[0].shape
    spec = pl.BlockSpec((1, 2, tm, LANES), lambda b, i: (b, 0, i, 0))
    return pl.pallas_call(
        _dil_merge_kernel,
        grid=(bsz, seq // tm),
        in_specs=[spec] * 6,
        out_specs=pl.BlockSpec((1, tm, 2 * LANES), lambda b, i: (b, i, 0)),
        out_shape=jax.ShapeDtypeStruct((bsz, seq, 2 * LANES), BF16),
        compiler_params=_params("parallel", "parallel"),
        name="dilated_merge",
    )(*outs, *lses)


def _ssm_kernel(u_ref, bb_ref, a_ref, cc_ref, d_ref, wg_ref, o_ref, x_ref, carry_ref, *, steps, strip):
    @pl.when(pl.program_id(1) == 0)
    def _():
        carry_ref[...] = jnp.zeros_like(carry_ref)

    u = u_ref[0]
    ub = u.astype(BF16)
    width = u.shape[1]
    half = width // 2
    nstates = x_ref.shape[1] // 2
    hs = nstates // 2
    for part in range(2):
        bu = _dot(ub[:, part * half:(part + 1) * half], bb_ref[part])
        x_ref[:, part * hs:(part + 1) * hs] = bu[:, :hs]
        x_ref[:, nstates + part * hs:nstates + (part + 1) * hs] = bu[:, hs:]
    for s in range(nstates // strip):
        re = slice(s * strip, (s + 1) * strip)
        im = slice(nstates + s * strip, nstates + (s + 1) * strip)
        ar = jnp.broadcast_to(a_ref[0:1, re], (SUBLANES, strip))
        ai = jnp.broadcast_to(a_ref[1:2, re], (SUBLANES, strip))
        xr = carry_ref[:, re]
        xi = carry_ref[:, im]
        for t in range(steps):
            rows = slice(t * SUBLANES, (t + 1) * SUBLANES)
            xr, xi = ar * xr - ai * xi + x_ref[rows, re], ar * xi + ai * xr + x_ref[rows, im]
            x_ref[rows, re] = xr
            x_ref[rows, im] = xi
        carry_ref[:, re] = xr
        carry_ref[:, im] = xi
    y_parts = []
    for part in range(2):
        xr = x_ref[:, part * hs:(part + 1) * hs].astype(BF16)
        xi = x_ref[:, nstates + part * hs:nstates + (part + 1) * hs].astype(BF16)
        y_parts.append(_dot(xr, cc_ref[0, part]) + _dot(xi, cc_ref[1, part]))
    y = jnp.concatenate(y_parts, axis=1) + d_ref[...] * u
    y = 0.5 * y * (1.0 + jnp.tanh(math.sqrt(2.0 / math.pi) * (y + 0.044715 * (y * y * y))))
    z = _dot(y.astype(BF16), wg_ref[...])
    o_ref[0] = (z[:, :width] * _sigmoid(z[:, width:])).astype(o_ref.dtype)


def _block_diag(blocks):
    g, r, c = blocks.shape
    eye = jnp.eye(g, dtype=blocks.dtype)
    return (eye[:, None, :, None] * blocks[:, :, None, :]).reshape(g * r, g * c)


def _ssm_tables(lam_re, lam_im, log_dt, b_re, b_im, c_re, c_im):
    lr, li = lam_re.astype(F32), lam_im.astype(F32)
    dt = jnp.exp(log_dt.astype(F32))[:, None]
    mag = jnp.exp(lr * dt)
    a_re, a_im = mag * jnp.cos(li * dt), mag * jnp.sin(li * dt)
    den = lr * lr + li * li
    f_re = ((a_re - 1.0) * lr + a_im * li) / den
    f_im = (a_im * lr - (a_re - 1.0) * li) / den
    br, bi = b_re.astype(F32), b_im.astype(F32)
    bb_re = _block_diag(jnp.transpose(f_re[..., None] * br - f_im[..., None] * bi, (0, 2, 1)))
    bb_im = _block_diag(jnp.transpose(f_re[..., None] * bi + f_im[..., None] * br, (0, 2, 1)))
    cc_re = _block_diag(jnp.transpose(c_re.astype(F32), (0, 2, 1)))
    cc_im = -_block_diag(jnp.transpose(c_im.astype(F32), (0, 2, 1)))
    nstates = SSM_GROUPS * SSM_STATE
    hw, hs = SSM_WIDTH // 2, nstates // 2
    bb = jnp.stack([jnp.concatenate([bb_re[p * hw:(p + 1) * hw, p * hs:(p + 1) * hs],
                                     bb_im[p * hw:(p + 1) * hw, p * hs:(p + 1) * hs]], axis=1) for p in range(2)])
    cc = jnp.stack([jnp.stack([m[p * hs:(p + 1) * hs, p * hw:(p + 1) * hw] for p in range(2)])
                    for m in (cc_re, cc_im)])
    a = jnp.stack([a_re.reshape(nstates), a_im.reshape(nstates)])
    return bb.astype(BF16), a, cc.astype(BF16)


def s5_glu(u, bb, a, cc, d_skip, w_glu, steps=SSM_STEPS, strip=SSM_STRIP):
    bsz, seq, width = u.shape
    nstates = a.shape[1]
    assert bsz % SUBLANES == 0 and seq % steps == 0 and nstates % strip == 0
    ngrp = bsz // SUBLANES
    rows = steps * SUBLANES
    ut = u.reshape(ngrp, SUBLANES, seq, width).transpose(0, 2, 1, 3).reshape(ngrp, seq * SUBLANES, width)
    const = lambda *shape: pl.BlockSpec(shape, lambda g, c: (0,) * len(shape))
    out = pl.pallas_call(
        functools.partial(_ssm_kernel, steps=steps, strip=strip),
        grid=(ngrp, seq // steps),
        in_specs=[pl.BlockSpec((1, rows, width), lambda g, c: (g, c, 0)),
                  const(*bb.shape), const(*a.shape), const(*cc.shape),
                  const(1, width), const(*w_glu.shape)],
        out_specs=pl.BlockSpec((1, rows, width), lambda g, c: (g, c, 0)),
        out_shape=jax.ShapeDtypeStruct((ngrp, seq * SUBLANES, width), BF16),
        scratch_shapes=[pltpu.VMEM((rows, 2 * nstates), F32),
                        pltpu.VMEM((SUBLANES, 2 * nstates), F32)],
        compiler_params=_params("parallel", "arbitrary"),
        name="s5_glu",
    )(ut, bb, a, cc, d_skip.reshape(1, width).astype(F32), w_glu)
    return out.reshape(ngrp, seq, SUBLANES, width).transpose(0, 2, 1, 3).reshape(bsz * seq, width)


def _dsa_kernel(qT_ref, k_ref, vT_ref, iqT_ref, ik_ref, iwT_ref, bias_ref, o_ref,
                key_ref, madd_ref, acc_ref, *, blk, topk, scale):
    i = pl.program_id(1)
    nblk = i + 1
    key_io = lax.broadcasted_iota(jnp.int32, (blk, blk), 0)
    qry_io = lax.broadcasted_iota(jnp.int32, (blk, blk), 1)

    iqT = iqT_ref[0, 0]
    idx_rhs = jnp.concatenate([iqT[h * IDX_DIM:(h + 1) * IDX_DIM, :] for h in range(IDX_HEADS)], axis=1)
    iw = iwT_ref[0]

    def score_block(j, _):
        start = pl.multiple_of(j * blk, blk)
        kk = ik_ref[0, pl.ds(start, blk), :][:, :IDX_DIM].astype(BF16)
        d = _dot(kk, idx_rhs)
        sc = jnp.zeros((blk, blk), F32)
        for h in range(IDX_HEADS):
            sc = sc + iw[h:h + 1, :] * jnp.maximum(d[:, h * blk:(h + 1) * blk], 0.0)
        sc = jnp.where((key_io + j * blk) <= (qry_io + i * blk), sc, NEG_INF)
        bits = pltpu.bitcast(sc, jnp.int32)
        key_ref[j] = bits ^ ((bits >> 31) & 0x7FFFFFFF)
        return 0

    lax.fori_loop(0, nblk, score_block, 0)

    def count(pred):
        def body(j, acc):
            hit = jnp.where(pred(key_ref[j]), 1.0, 0.0)
            return acc + jnp.sum(hit.reshape(blk // 8, 8, blk), axis=0)
        return jnp.sum(lax.fori_loop(0, nblk, body, jnp.zeros((8, blk), F32)), axis=0, keepdims=True)

    def bit_step(b, thr):
        cand = thr + lax.shift_left(jnp.int32(1), 31 - b)
        return jnp.where(count(lambda key: key >= cand) >= topk, cand, thr)

    thr = lax.fori_loop(0, 32, bit_step, jnp.full((1, blk), INT_MIN, jnp.int32))

    need = topk - count(lambda key: key > thr)
    upto = jnp.where(qry_io <= key_io, 1.0, 0.0).astype(BF16)

    def select_block(j, run):
        key = key_ref[j]
        tie = jnp.where(key == thr, 1.0, 0.0)
        rank = _dot(upto, tie.astype(BF16)) + run
        keep_tie = jnp.where(rank <= need, 0.0, NEG_INF)
        madd_ref[j] = jnp.where(key > thr, 0.0, jnp.where(key == thr, keep_tie, NEG_INF))
        return run + jnp.sum(tie, axis=0, keepdims=True)

    lax.fori_loop(0, nblk, select_block, jnp.zeros((1, blk), F32))

    fold_scale = _is_power_of_two(scale)
    qT = qT_ref[0, 0]
    if fold_scale:
        qT = qT * jnp.asarray(scale, qT.dtype)
    row = lax.broadcasted_iota(jnp.int32, (LANES, 1), 0)
    q_rhs = []
    for g in range(DSA_HEADS // 2):
        pair = qT[g * LANES:(g + 1) * LANES, :]
        zero = jnp.zeros_like(pair)
        q_rhs.append(jnp.concatenate([jnp.where(row < HEAD_DIM, pair, zero),
                                      jnp.where(row >= HEAD_DIM, pair, zero)], axis=1))
    acc_ref[...] = jnp.zeros_like(acc_ref)

    def attend_block(j, carry):
        m_all, l_all = carry
        start = pl.multiple_of(j * blk, blk)
        kblk = k_ref[0, pl.ds(start, blk), :]
        vT = vT_ref[0, j]
        madd = madd_ref[j]
        which = jnp.minimum(i - j, 2)
        s2 = [_dot(kblk[:, g * LANES:(g + 1) * LANES], q_rhs[g]) for g in range(DSA_HEADS // 2)]
        m_out, l_out, probs, alphas = [], [], [], []
        for h in range(DSA_HEADS):
            s = s2[h // 2][:, (h % 2) * blk:(h % 2 + 1) * blk]
            if not fold_scale:
                s = s * scale
            s = s + bias_ref[h, which] + madd
            m_new = jnp.maximum(m_all[h], jnp.max(s, axis=0, keepdims=True))
            p = jnp.exp(s - m_new)
            alpha = jnp.exp(m_all[h] - m_new)
            l_out.append(alpha * l_all[h] + jnp.sum(p, axis=0, keepdims=True))
            m_out.append(m_new)
            probs.append(p.astype(BF16))
            alphas.append(alpha)
        for h in range(DSA_HEADS):
            rows = slice(h * HEAD_DIM, (h + 1) * HEAD_DIM)
            acc_ref[rows, :] = alphas[h] * acc_ref[rows, :] + _dot(vT[rows, :], probs[h])
        return tuple(m_out), tuple(l_out)

    init = (tuple(jnp.full((1, blk), NEG_INF, F32) for _ in range(DSA_HEADS)),
            tuple(jnp.zeros((1, blk), F32) for _ in range(DSA_HEADS)))
    _, l_all = lax.fori_loop(0, nblk, attend_block, init)
    for h in range(DSA_HEADS):
        rows = slice(h * HEAD_DIM, (h + 1) * HEAD_DIM)
        acc_ref[rows, :] = acc_ref[rows, :] / l_all[h]
    o_ref[0] = acc_ref[...].T.astype(o_ref.dtype)


def dsa_attention(proj, qkv_col, iq_col, proj_f32, kw_col, bias_tiles, blk=DSA_BLOCK):
    bsz, seq, _ = proj.shape
    width = DSA_HEADS * HEAD_DIM
    nblk = seq // blk
    topk = min(DSA_TOPK, seq // 4)
    assert qkv_col % width == 0 and kw_col % LANES == 0
    qT = _blocked_transpose(proj[..., qkv_col:qkv_col + width], blk)
    vT = _blocked_transpose(proj[..., qkv_col + 2 * width:qkv_col + 3 * width], blk)
    iqT = _blocked_transpose(proj[..., iq_col:iq_col + IDX_HEADS * IDX_DIM], blk)
    iwT = jnp.swapaxes(proj_f32[..., kw_col + IDX_DIM:kw_col + IDX_DIM + IDX_HEADS], 1, 2)
    return pl.pallas_call(
        functools.partial(_dsa_kernel, blk=blk, topk=topk, scale=HEAD_DIM ** -0.5),
        grid=(bsz, nblk),
        in_specs=[pl.BlockSpec((1, 1, width, blk), lambda b, i: (b, i, 0, 0)),
                  pl.BlockSpec((1, seq, width), lambda b, i: (b, 0, qkv_col // width + 1)),
                  pl.BlockSpec((1, nblk, width, blk), lambda b, i: (b, 0, 0, 0)),
                  pl.BlockSpec((1, 1, IDX_HEADS * IDX_DIM, blk), lambda b, i: (b, i, 0, 0)),
                  pl.BlockSpec((1, seq, LANES), lambda b, i: (b, 0, kw_col // LANES)),
                  pl.BlockSpec((1, IDX_HEADS, blk), lambda b, i: (b, 0, i)),
                  pl.BlockSpec((DSA_HEADS, 3, blk, blk), lambda b, i: (0, 0, 0, 0))],
        out_specs=pl.BlockSpec((1, blk, width), lambda b, i: (b, i, 0)),
        out_shape=jax.ShapeDtypeStruct((bsz, seq, width), BF16),
        scratch_shapes=[pltpu.VMEM((nblk, blk, blk), jnp.int32),
                        pltpu.VMEM((nblk, blk, blk), F32),
                        pltpu.VMEM((width, blk), F32)],
        compiler_params=_params("parallel", "arbitrary"),
        name="dsa_attention",
    )(qT, proj, vT, iqT, proj_f32, iwT, bias_tiles)


def hybrid_mixer(xbf, bsz, seq, w_in, rel_bias, ssm_params, d_skip, w_glu, w_branches):
    offs = [0]
    for width in IN_SPLITS:
        offs.append(offs[-1] + width)
    seg = lambda a: w_in[:, offs[a]:offs[a + 1]]
    w_bf = jnp.concatenate([seg(0), seg(3), seg(4)], axis=1).astype(BF16)
    pad = jnp.zeros((w_in.shape[0], LANES - IDX_DIM - IDX_HEADS), w_in.dtype)
    w_f32 = jnp.concatenate([seg(1), seg(2), seg(5), seg(6), pad], axis=1).astype(BF16)
    dsa_col = IN_SPLITS[0]
    iq_col = dsa_col + IN_SPLITS[3]
    ssm_col = IN_SPLITS[1]
    kw_col = ssm_col + IN_SPLITS[2]
    proj = matmul(xbf, w_bf, BF16).reshape(bsz, seq, -1)
    proj_f32 = matmul(xbf, w_f32, F32).reshape(bsz, seq, -1)

    y_sb = stick_breaking_attention(proj).reshape(bsz * seq, -1)

    outs, lses = [], []
    for g, (window, dil) in enumerate(DIL_PATTERNS):
        assert window // dil == ATT_BLOCK
        tiles = _dilated_bias_tiles(rel_bias[:, g * DIL_HEADS_PER_GROUP:(g + 1) * DIL_HEADS_PER_GROUP], dil, ATT_BLOCK)
        o, lse = dilated_group_attention(proj_f32, tiles, g, dil)
        outs.append(o)
        lses.append(lse)
    y_dil = dilated_merge(outs, lses).reshape(bsz * seq, -1)

    bb, a_bar, cc = _ssm_tables(*ssm_params)
    y_ssm = s5_glu(proj_f32[..., ssm_col:ssm_col + SSM_WIDTH], bb, a_bar, cc, d_skip, w_glu.astype(BF16))

    y_dsa = dsa_attention(proj, dsa_col, iq_col, proj_f32, kw_col, _dsa_bias_tiles(rel_bias[:, DIL_HEADS:], DSA_BLOCK))
    y_dsa = y_dsa.reshape(bsz * seq, -1)

    return gated_branch_merge(xbf, [y_sb, y_dil, y_ssm, y_dsa], [w.astype(BF16) for w in w_branches],
                              seg(7).astype(BF16))


def kernel(x, ln_g, ln_b, ffn1_w_up, ffn1_w_down, w_in, rel_bias, ssm_lam_re, ssm_lam_im, ssm_log_dt,
           ssm_b_re, ssm_b_im, ssm_c_re, ssm_c_im, ssm_d, ssm_w_glu, w_br_sb, w_br_dil, w_br_ssm, w_br_dsa,
           w_out, ffn2_w_up, ffn2_w_down):
    bsz, seq, d = x.shape
    xf = x.reshape(bsz * seq, d)
    xbf = xf.astype(BF16)
    for l in range(DEPTH):
        h = ffn_up(xbf, ffn1_w_up[l].astype(BF16))
        xf, xbf = matmul_residual_layernorm(h, ffn1_w_down[l].astype(BF16), xf, ln_g[l, 0], ln_b[l, 0], MACARON)
        merged = hybrid_mixer(xbf, bsz, seq, w_in[l], rel_bias,
                              (ssm_lam_re[l], ssm_lam_im[l], ssm_log_dt[l], ssm_b_re[l], ssm_b_im[l],
                               ssm_c_re[l], ssm_c_im[l]), ssm_d[l], ssm_w_glu[l],
                              (w_br_sb[l], w_br_dil[l], w_br_ssm[l], w_br_dsa[l]))
        xf, xbf = matmul_residual_layernorm(merged, w_out[l].astype(BF16), xf, ln_g[l, 1], ln_b[l, 1], 1.0)
        h = ffn_up(xbf, ffn2_w_up[l].astype(BF16))
        xf, xbf = matmul_residual_layernorm(h, ffn2_w_down[l].astype(BF16), xf, ln_g[l, 2], ln_b[l, 2], MACARON)
    return xf.reshape(bsz, seq, d)
```

```python
import functools
import math

import jax
import jax.numpy as jnp
from jax import lax
from jax.experimental import pallas as pl
from jax.experimental.pallas import tpu as pltpu

F32 = jnp.float32
BF16 = jnp.bfloat16

D_MODEL = 2048
DEPTH = 2
HEAD_DIM = 64
SB_HEADS = 8
DIL_PATTERNS = ((128, 1), (512, 4), (2048, 16))
DIL_HEADS_PER_GROUP = 4
DIL_HEADS = DIL_HEADS_PER_GROUP * len(DIL_PATTERNS)
SSM_WIDTH = 512
SSM_GROUP = 16
SSM_GROUPS = SSM_WIDTH // SSM_GROUP
SSM_STATE = 64
DSA_HEADS = 8
IDX_HEADS = 8
IDX_DIM = 64
DSA_TOPK = 256
N_BRANCH = 4
D_FF = 5632
REL_BUCKETS = 32
REL_MAX_DIST = 128
DN_ALPHA = (2.0 * DEPTH) ** 0.25
LN_EPS = 1e-5
NEG_INF = -1e30
MACARON = 0.5

IN_SPLITS = (3 * SB_HEADS * HEAD_DIM, 3 * DIL_HEADS * HEAD_DIM, SSM_WIDTH, 3 * DSA_HEADS * HEAD_DIM,
             IDX_HEADS * IDX_DIM, IDX_DIM, IDX_HEADS, N_BRANCH * D_MODEL)

LANES = 128
ATT_BLOCK = 128
DIL_UNITS = 2
DSA_BLOCK = 256
SB_QUERY_BLOCK = 1024
SB_KEY_BLOCK = 256
SUBLANES = 8
SSM_STEPS = 32
SSM_STRIP = 512
VMEM_LIMIT = 56 * 1024 * 1024
INT_MIN = -2 ** 31

_NT = (((1,), (1,)), ((), ()))


def _dot(a, b):
    return jnp.dot(a, b, preferred_element_type=F32)


def _dot_nt(a, b):
    return lax.dot_general(a, b, _NT, preferred_element_type=F32)


def _params(*sem):
    return pltpu.CompilerParams(dimension_semantics=sem, vmem_limit_bytes=VMEM_LIMIT)


def _sigmoid(x):
    return 1.0 / (1.0 + jnp.exp(-x))


def _mm_kernel(x_ref, w_ref, o_ref):
    o_ref[...] = _dot(x_ref[...], w_ref[...]).astype(o_ref.dtype)


def matmul(x, w, out_dtype, tm=512):
    m, k = x.shape
    n = w.shape[1]
    assert m % tm == 0 and n % LANES == 0
    return pl.pallas_call(
        _mm_kernel,
        grid=(m // tm,),
        in_specs=[pl.BlockSpec((tm, k), lambda i: (i, 0)),
                  pl.BlockSpec((k, n), lambda i: (0, 0), pipeline_mode=pl.Buffered(1))],
        out_specs=pl.BlockSpec((tm, n), lambda i: (i, 0)),
        out_shape=jax.ShapeDtypeStruct((m, n), out_dtype),
        compiler_params=_params("parallel"),
        name="matmul",
    )(x, w)


def _ffn_up_kernel(x_ref, wa_ref, wb_ref, o_ref):
    x = x_ref[...]
    a = _dot(x, wa_ref[...])
    b = _dot(x, wb_ref[...])
    o_ref[...] = (a * _sigmoid(a) * b).astype(o_ref.dtype)


def ffn_up(x, w_up, tm=1024, tn=512):
    m, k = x.shape
    f = w_up.shape[1] // 2
    assert m % tm == 0 and f % tn == 0
    nb = f // tn
    return pl.pallas_call(
        _ffn_up_kernel,
        grid=(m // tm, nb),
        in_specs=[pl.BlockSpec((tm, k), lambda i, j: (i, 0)),
                  pl.BlockSpec((k, tn), lambda i, j: (0, j)),
                  pl.BlockSpec((k, tn), lambda i, j: (0, j + nb))],
        out_specs=pl.BlockSpec((tm, tn), lambda i, j: (i, j)),
        out_shape=jax.ShapeDtypeStruct((m, f), BF16),
        compiler_params=_params("parallel", "arbitrary"),
        name="ffn_up",
    )(x, w_up, w_up)


def _mm_res_ln_kernel(h_ref, w_ref, x_ref, g_ref, b_ref, o_ref, obf_ref, *, scale):
    y = DN_ALPHA * x_ref[...] + scale * _dot(h_ref[...], w_ref[...])
    mu = jnp.mean(y, axis=-1, keepdims=True)
    yc = y - mu
    var = jnp.mean(yc * yc, axis=-1, keepdims=True)
    out = yc * lax.rsqrt(var + LN_EPS) * g_ref[...] + b_ref[...]
    o_ref[...] = out
    obf_ref[...] = out.astype(BF16)


def matmul_residual_layernorm(h, w, x, g, b, scale, tm=256):
    m, k = h.shape
    n = w.shape[1]
    assert m % tm == 0
    return pl.pallas_call(
        functools.partial(_mm_res_ln_kernel, scale=scale),
        grid=(m // tm,),
        in_specs=[pl.BlockSpec((tm, k), lambda i: (i, 0)),
                  pl.BlockSpec((k, n), lambda i: (0, 0), pipeline_mode=pl.Buffered(1)),
                  pl.BlockSpec((tm, n), lambda i: (i, 0)),
                  pl.BlockSpec((1, n), lambda i: (0, 0)),
                  pl.BlockSpec((1, n), lambda i: (0, 0))],
        out_specs=[pl.BlockSpec((tm, n), lambda i: (i, 0)),
                   pl.BlockSpec((tm, n), lambda i: (i, 0))],
        out_shape=[jax.ShapeDtypeStruct((m, n), F32), jax.ShapeDtypeStruct((m, n), BF16)],
        compiler_params=_params("parallel"),
        name="matmul_residual_layernorm",
    )(h, w, x, g.reshape(1, n), b.reshape(1, n))


def _gated_merge_kernel(x_ref, ysb_ref, ydil_ref, yssm_ref, ydsa_ref, wsb_ref, wdil_ref, wssm_ref, wdsa_ref,
                        g0_ref, g1_ref, g2_ref, g3_ref, o_ref):
    x = x_ref[...]
    acc = _sigmoid(_dot(x, g0_ref[...])) * _dot(ysb_ref[...], wsb_ref[...])
    acc += _sigmoid(_dot(x, g1_ref[...])) * _dot(ydil_ref[...], wdil_ref[...])
    acc += _sigmoid(_dot(x, g2_ref[...])) * _dot(yssm_ref[...], wssm_ref[...])
    acc += _sigmoid(_dot(x, g3_ref[...])) * _dot(ydsa_ref[...], wdsa_ref[...])
    o_ref[...] = acc.astype(o_ref.dtype)


def gated_branch_merge(x, ys, ws, w_gate, tm=1024, tn=512):
    m, k = x.shape
    n = ws[0].shape[1]
    assert m % tm == 0 and n % tn == 0
    nb = n // tn
    y_specs = [pl.BlockSpec((tm, y.shape[1]), lambda i, j: (i, 0)) for y in ys]
    w_specs = [pl.BlockSpec((w.shape[0], tn), lambda i, j: (0, j)) for w in ws]
    g_specs = [pl.BlockSpec((k, tn), functools.partial(lambda i, j, br: (0, j + br * nb), br=br))
               for br in range(N_BRANCH)]
    return pl.pallas_call(
        _gated_merge_kernel,
        grid=(m // tm, nb),
        in_specs=[pl.BlockSpec((tm, k), lambda i, j: (i, 0))] + y_specs + w_specs + g_specs,
        out_specs=pl.BlockSpec((tm, tn), lambda i, j: (i, j)),
        out_shape=jax.ShapeDtypeStruct((m, n), BF16),
        compiler_params=_params("parallel", "arbitrary"),
        name="gated_branch_merge",
    )(x, *ys, *ws, w_gate, w_gate, w_gate, w_gate)


def _is_power_of_two(x):
    return math.frexp(x)[0] == 0.5


def _sb_kernel(q_ref, k_ref, v_ref, o_ref, vT_ref, acc_ref, *, qblk, kblk, scale):
    i = pl.program_id(2)
    ratio = qblk // kblk
    fold_scale = _is_power_of_two(scale)

    @pl.when(i == 0)
    def _():
        for j in range(vT_ref.shape[0]):
            vT_ref[j] = v_ref[0, j * kblk:(j + 1) * kblk, :].T

    qT = q_ref[0].T
    if fold_scale:
        qT = qT * jnp.asarray(scale, qT.dtype)
    row = lax.broadcasted_iota(jnp.int32, (LANES, 1), 0)
    zero = jnp.zeros_like(qT)
    q_rhs = jnp.concatenate([jnp.where(row < HEAD_DIM, qT, zero), jnp.where(row >= HEAD_DIM, qT, zero)], axis=1)
    key_io = lax.broadcasted_iota(jnp.int32, (kblk, kblk), 0)
    qry_io = lax.broadcasted_iota(jnp.int32, (kblk, kblk), 1)
    later = jnp.where(qry_io > key_io, 1.0, 0.0).astype(BF16)
    acc_ref[...] = jnp.zeros_like(acc_ref)
    nsub = 2 * ratio

    def block(j, carry, diag_sub):
        start = pl.multiple_of(j * kblk, kblk)
        z_all = _dot(k_ref[0, pl.ds(start, kblk), :], q_rhs)
        vT = vT_ref[j]
        strict = key_io < qry_io
        stage = []
        for c in range(nsub):
            h, sub = divmod(c, ratio)
            if diag_sub is not None and sub < diag_sub:
                stage.append(None)
                continue
            masked = diag_sub is not None and sub == diag_sub
            z = z_all[:, c * kblk:(c + 1) * kblk]
            if not fold_scale:
                z = z * scale
            sp = jnp.maximum(z, 0.0) + jnp.log(1.0 + jnp.exp(-jnp.abs(z)))
            log_1mb = jnp.where(strict, -sp, 0.0) if masked else -sp
            suf = _dot(later, log_1mb.astype(BF16))
            stage.append((z - sp, suf, jnp.sum(log_1mb, axis=0, keepdims=True), masked))
        new_carry = []
        for c in range(nsub):
            h, sub = divmod(c, ratio)
            if stage[c] is None:
                new_carry.append(carry[c])
                continue
            log_beta, suf, colsum, masked = stage[c]
            att = jnp.exp(log_beta + suf + carry[c])
            if masked:
                att = jnp.where(strict, att, 0.0)
            rows = slice(h * HEAD_DIM, (h + 1) * HEAD_DIM)
            cols = slice(sub * kblk, (sub + 1) * kblk)
            acc_ref[rows, cols] += _dot(vT[rows, :], att.astype(BF16))
            new_carry.append(carry[c] + colsum)
        return tuple(new_carry)

    carry = tuple(jnp.zeros((1, kblk), F32) for _ in range(nsub))
    for sub in reversed(range(ratio)):
        carry = block(i * ratio + sub, carry, sub)
    lax.fori_loop(0, i * ratio, lambda jj, c: block(i * ratio - 1 - jj, c, None), carry)
    o_ref[0] = acc_ref[...].T.astype(o_ref.dtype)


def stick_breaking_attention(qkv, col0=0, qblk=SB_QUERY_BLOCK, kblk=SB_KEY_BLOCK):
    bsz, seq, _ = qkv.shape
    width = SB_HEADS * HEAD_DIM
    npair = width // LANES
    qblk = min(qblk, seq)
    assert seq % qblk == 0 and qblk % kblk == 0 and col0 % LANES == 0
    qcol = col0 // LANES
    return pl.pallas_call(
        functools.partial(_sb_kernel, qblk=qblk, kblk=kblk, scale=HEAD_DIM ** -0.5),
        grid=(bsz, npair, seq // qblk),
        in_specs=[pl.BlockSpec((1, qblk, LANES), lambda b, p, i: (b, i, qcol + p)),
                  pl.BlockSpec((1, seq, LANES), lambda b, p, i: (b, 0, qcol + npair + p)),
                  pl.BlockSpec((1, seq, LANES), lambda b, p, i: (b, 0, qcol + 2 * npair + p))],
        out_specs=pl.BlockSpec((1, qblk, LANES), lambda b, p, i: (b, i, p)),
        out_shape=jax.ShapeDtypeStruct((bsz, seq, width), BF16),
        scratch_shapes=[pltpu.VMEM((seq // kblk, LANES, kblk), BF16),
                        pltpu.VMEM((LANES, qblk), F32)],
        compiler_params=_params("parallel", "parallel", "arbitrary"),
        name="stick_breaking_attention",
    )(qkv, qkv, qkv)


def _t5_bucket(dist):
    max_exact = REL_BUCKETS // 2
    d = jnp.maximum(dist, 1).astype(F32)
    large = max_exact + (jnp.log(d / max_exact) / math.log(REL_MAX_DIST / max_exact)
                         * (REL_BUCKETS - max_exact)).astype(jnp.int32)
    large = jnp.minimum(large, REL_BUCKETS - 1)
    return jnp.where(dist < max_exact, dist, large)


def _bias_of_distance(rel_bias, dist):
    one_hot = jax.nn.one_hot(_t5_bucket(dist), REL_BUCKETS, dtype=F32)
    return jnp.einsum("...b,bh->...h", one_hot, rel_bias.astype(F32), precision=lax.Precision.HIGHEST)


def _dilated_bias_tiles(rel_bias_group, dil, blk):
    a = jnp.arange(blk)[:, None]
    cc = jnp.arange(2 * blk)[None, :]
    step = blk + a - cc
    valid = (step >= 0) & (step <= blk)
    bias = _bias_of_distance(rel_bias_group, dil * jnp.clip(step, 0, blk))
    tile = jnp.where(valid[..., None], bias, NEG_INF)
    return jnp.transpose(tile, (2, 0, 1))


def _dsa_bias_tiles(rel_bias_dsa, blk):
    c = jnp.arange(blk)[:, None]
    a = jnp.arange(blk)[None, :]
    own = jnp.where((a - c >= 0)[..., None], _bias_of_distance(rel_bias_dsa, jnp.maximum(a - c, 0)), NEG_INF)
    prev = _bias_of_distance(rel_bias_dsa, blk + a - c)
    half = REL_BUCKETS // 2
    assert half + int(math.log((blk + 1) / half) / math.log(REL_MAX_DIST / half) * (REL_BUCKETS - half)) >= REL_BUCKETS - 1
    far = jnp.broadcast_to(rel_bias_dsa[REL_BUCKETS - 1].astype(F32), prev.shape)
    return jnp.transpose(jnp.stack([own, prev, far]), (3, 0, 1, 2))


def _dil_kernel(q0_ref, q1_ref, k0_ref, k1_ref, v0_ref, v1_ref, bias_ref, o_ref, lse_ref, *, blk, dil, nb, units, scale):
    step = pl.program_id(1)
    lane = lax.broadcasted_iota(jnp.int32, (1, LANES), 1)
    fold_scale = _is_power_of_two(scale)

    def rows(c, block):
        start = c + dil * blk * block
        return pl.ds(start, blk, stride=dil) if dil > 1 else pl.ds(pl.multiple_of(start, blk), blk)

    work = []
    for u in range(units):
        unit = step * units + u
        c, i = unit // nb, unit % nb
        cur, prev = rows(c, i), rows(c, jnp.maximum(i - 1, 0))
        for half, (q_ref, k_ref, v_ref) in enumerate(((q0_ref, k0_ref, v0_ref), (q1_ref, k1_ref, v1_ref))):
            q = q_ref[0, cur, :]
            if fold_scale:
                q = q * scale
            q = q.astype(BF16)
            kc, kp = k_ref[0, cur, :].astype(BF16), k_ref[0, prev, :].astype(BF16)
            zero = jnp.zeros_like(q)
            scores = []
            for e in range(2):
                head = (lane < HEAD_DIM) if e == 0 else (lane >= HEAD_DIM)
                qm = jnp.where(head, q, zero)
                scores.append((_dot_nt(qm, kp), _dot_nt(qm, kc)))
            work.append((i, cur, prev, half, v_ref, scores))
    soft = []
    for i, cur, prev, half, v_ref, scores in work:
        has_prev = i > 0
        parts = []
        for e in range(2):
            h = 2 * half + e
            s_prev, s_cur = scores[e]
            if not fold_scale:
                s_prev, s_cur = s_prev * scale, s_cur * scale
            s_prev = jnp.where(has_prev, s_prev + bias_ref[h, :, :blk], NEG_INF)
            s_cur = s_cur + bias_ref[h, :, blk:]
            m = jnp.maximum(jnp.max(s_prev, axis=-1, keepdims=True), jnp.max(s_cur, axis=-1, keepdims=True))
            p_prev = jnp.exp(s_prev - m)
            p_cur = jnp.exp(s_cur - m)
            denom = jnp.sum(p_prev, axis=-1, keepdims=True) + jnp.sum(p_cur, axis=-1, keepdims=True)
            parts.append((p_prev.astype(BF16), p_cur.astype(BF16), denom, m + jnp.log(denom)))
        soft.append(parts)
    for (i, cur, prev, half, v_ref, scores), parts in zip(work, soft):
        vc, vp = v_ref[0, cur, :].astype(BF16), v_ref[0, prev, :].astype(BF16)
        out = jnp.zeros((blk, LANES), F32)
        lse_b = jnp.zeros((blk, LANES), F32)
        for e in range(2):
            head = (lane < HEAD_DIM) if e == 0 else (lane >= HEAD_DIM)
            p_prev, p_cur, denom, lse = parts[e]
            o = (_dot(p_prev, vp) + _dot(p_cur, vc)) / denom
            out = jnp.where(head, o, out)
            lse_b = jnp.where(head, lse, lse_b)
        o_ref[0, half, cur, :] = out
        lse_ref[0, half, cur, :] = lse_b


def dilated_group_attention(qkv, bias_tiles, group, dil, blk=ATT_BLOCK, units=DIL_UNITS):
    bsz, seq, _ = qkv.shape
    assert DIL_HEADS_PER_GROUP * HEAD_DIM == 2 * LANES
    nslab = DIL_HEADS * HEAD_DIM // LANES
    n = seq // dil
    assert n % blk == 0
    nb = n // blk
    assert (dil * nb) % units == 0
    slab = lambda section, half: pl.BlockSpec(
        (1, seq, LANES), functools.partial(lambda b, s, col: (b, 0, col), col=section * nslab + 2 * group + half))
    out_spec = pl.BlockSpec((1, 2, seq, LANES), lambda b, s: (b, 0, 0, 0))
    return pl.pallas_call(
        functools.partial(_dil_kernel, blk=blk, dil=dil, nb=nb, units=units, scale=HEAD_DIM ** -0.5),
        grid=(bsz, dil * nb // units),
        in_specs=[slab(0, 0), slab(0, 1), slab(1, 0), slab(1, 1), slab(2, 0), slab(2, 1),
                  pl.BlockSpec((DIL_HEADS_PER_GROUP, blk, 2 * blk), lambda b, s: (0, 0, 0))],
        out_specs=[out_spec, out_spec],
        out_shape=[jax.ShapeDtypeStruct((bsz, 2, seq, LANES), F32)] * 2,
        compiler_params=_params("parallel", "arbitrary"),
        name=f"dilated_attention_g{group}",
    )(qkv, qkv, qkv, qkv, qkv, qkv, bias_tiles)


def _dil_merge_kernel(o0_ref, o1_ref, o2_ref, l0_ref, l1_ref, l2_ref, y_ref):
    for half in range(2):
        l0, l1, l2 = l0_ref[0, half], l1_ref[0, half], l2_ref[0, half]
        m = jnp.maximum(jnp.maximum(l0, l1), l2)
        w0, w1, w2 = jnp.exp(l0 - m), jnp.exp(l1 - m), jnp.exp(l2 - m)
        y = (w0 * o0_ref[0, half] + w1 * o1_ref[0, half] + w2 * o2_ref[0, half]) / (w0 + w1 + w2)
        y_ref[0, :, half * LANES:(half + 1) * LANES] = y.astype(y_ref.dtype)


def dilated_merge(outs, lses, tm=1024):
    bsz, _, seq, _ = outs[0].shape
    spec = pl.BlockSpec((1, 2, tm, LANES), lambda b, i: (b, 0, i, 0))
    return pl.pallas_call(
        _dil_merge_kernel,
        grid=(bsz, seq // tm),
        in_specs=[spec] * 6,
        out_specs=pl.BlockSpec((1, tm, 2 * LANES), lambda b, i: (b, i, 0)),
        out_shape=jax.ShapeDtypeStruct((bsz, seq, 2 * LANES), BF16),
        compiler_params=_params("parallel", "parallel"),
        name="dilated_merge",
    )(*outs, *lses)


def _ssm_kernel(u_ref, bb_ref, a_ref, cc_ref, d_ref, wg_ref, o_ref, x_ref, carry_ref, *, steps, strip):
    @pl.when(pl.program_id(1) == 0)
    def _():
        carry_ref[...] = jnp.zeros_like(carry_ref)

    u = u_ref[0]
    ub = u.astype(BF16)
    width = u.shape[1]
    half = width // 2
    nstates = x_ref.shape[1] // 2
    hs = nstates // 2
    for part in range(2):
        bu = _dot(ub[:, part * half:(part + 1) * half], bb_ref[part])
        x_ref[:, part * hs:(part + 1) * hs] = bu[:, :hs]
        x_ref[:, nstates + part * hs:nstates + (part + 1) * hs] = bu[:, hs:]
    for s in range(nstates // strip):
        re = slice(s * strip, (s + 1) * strip)
        im = slice(nstates + s * strip, nstates + (s + 1) * strip)
        ar = jnp.broadcast_to(a_ref[0:1, re], (SUBLANES, strip))
        ai = jnp.broadcast_to(a_ref[1:2, re], (SUBLANES, strip))
        xr = carry_ref[:, re]
        xi = carry_ref[:, im]
        for t in range(steps):
            rows = slice(t * SUBLANES, (t + 1) * SUBLANES)
            xr, xi = ar * xr - ai * xi + x_ref[rows, re], ar * xi + ai * xr + x_ref[rows, im]
            x_ref[rows, re] = xr
            x_ref[rows, im] = xi
        carry_ref[:, re] = xr
        carry_ref[:, im] = xi
    y_parts = []
    for part in range(2):
        xr = x_ref[:, part * hs:(part + 1) * hs].astype(BF16)
        xi = x_ref[:, nstates + part * hs:nstates + (part + 1) * hs].astype(BF16)
        y_parts.append(_dot(xr, cc_ref[0, part]) + _dot(xi, cc_ref[1, part]))
    y = jnp.concatenate(y_parts, axis=1) + d_ref[...] * u
    y = 0.5 * y * (1.0 + jnp.tanh(math.sqrt(2.0 / math.pi) * (y + 0.044715 * (y * y * y))))
    z = _dot(y.astype(BF16), wg_ref[...])
    o_ref[0] = (z[:, :width] * _sigmoid(z[:, width:])).astype(o_ref.dtype)


def _block_diag(blocks):
    g, r, c = blocks.shape
    eye = jnp.eye(g, dtype=blocks.dtype)
    return (eye[:, None, :, None] * blocks[:, :, None, :]).reshape(g * r, g * c)


def _ssm_tables(lam_re, lam_im, log_dt, b_re, b_im, c_re, c_im):
    lr, li = lam_re.astype(F32), lam_im.astype(F32)
    dt = jnp.exp(log_dt.astype(F32))[:, None]
    mag = jnp.exp(lr * dt)
    a_re, a_im = mag * jnp.cos(li * dt), mag * jnp.sin(li * dt)
    den = lr * lr + li * li
    f_re = ((a_re - 1.0) * lr + a_im * li) / den
    f_im = (a_im * lr - (a_re - 1.0) * li) / den
    br, bi = b_re.astype(F32), b_im.astype(F32)
    bb_re = _block_diag(jnp.transpose(f_re[..., None] * br - f_im[..., None] * bi, (0, 2, 1)))
    bb_im = _block_diag(jnp.transpose(f_re[..., None] * bi + f_im[..., None] * br, (0, 2, 1)))
    cc_re = _block_diag(jnp.transpose(c_re.astype(F32), (0, 2, 1)))
    cc_im = -_block_diag(jnp.transpose(c_im.astype(F32), (0, 2, 1)))
    nstates = SSM_GROUPS * SSM_STATE
    hw, hs = SSM_WIDTH // 2, nstates // 2
    bb = jnp.stack([jnp.concatenate([bb_re[p * hw:(p + 1) * hw, p * hs:(p + 1) * hs],
                                     bb_im[p * hw:(p + 1) * hw, p * hs:(p + 1) * hs]], axis=1) for p in range(2)])
    cc = jnp.stack([jnp.stack([m[p * hs:(p + 1) * hs, p * hw:(p + 1) * hw] for p in range(2)])
                    for m in (cc_re, cc_im)])
    a = jnp.stack([a_re.reshape(nstates), a_im.reshape(nstates)])
    return bb.astype(BF16), a, cc.astype(BF16)


def s5_glu(u, bb, a, cc, d_skip, w_glu, steps=SSM_STEPS, strip=SSM_STRIP):
    bsz, seq, width = u.shape
    nstates = a.shape[1]
    assert bsz % SUBLANES == 0 and seq % steps == 0 and nstates % strip == 0
    ngrp = bsz // SUBLANES
    rows = steps * SUBLANES
    ut = u.reshape(ngrp, SUBLANES, seq, width).transpose(0, 2, 1, 3).reshape(ngrp, seq * SUBLANES, width)
    const = lambda *shape: pl.BlockSpec(shape, lambda g, c: (0,) * len(shape))
    out = pl.pallas_call(
        functools.partial(_ssm_kernel, steps=steps, strip=strip),
        grid=(ngrp, seq // steps),
        in_specs=[pl.BlockSpec((1, rows, width), lambda g, c: (g, c, 0)),
                  const(*bb.shape), const(*a.shape), const(*cc.shape),
                  const(1, width), const(*w_glu.shape)],
        out_specs=pl.BlockSpec((1, rows, width), lambda g, c: (g, c, 0)),
        out_shape=jax.ShapeDtypeStruct((ngrp, seq * SUBLANES, width), BF16),
        scratch_shapes=[pltpu.VMEM((rows, 2 * nstates), F32),
                        pltpu.VMEM((SUBLANES, 2 * nstates), F32)],
        compiler_params=_params("parallel", "arbitrary"),
        name="s5_glu",
    )(ut, bb, a, cc, d_skip.reshape(1, width).astype(F32), w_glu)
    return out.reshape(ngrp, seq, SUBLANES, width).transpose(0, 2, 1, 3).reshape(bsz * seq, width)


def _dsa_kernel(q_ref, k_ref, v_ref, iq_ref, ik_ref, ikq_ref, bias_ref, o_ref,
                vT_ref, key_ref, madd_ref, acc_ref, *, blk, topk, scale):
    i = pl.program_id(1)
    nblk = i + 1
    key_io = lax.broadcasted_iota(jnp.int32, (blk, blk), 0)
    qry_io = lax.broadcasted_iota(jnp.int32, (blk, blk), 1)

    @pl.when(i == 0)
    def _():
        for j in range(vT_ref.shape[0]):
            vT_ref[j] = v_ref[0, j * blk:(j + 1) * blk, :].T

    iqT = iq_ref[0].T
    idx_rhs = jnp.concatenate([iqT[h * IDX_DIM:(h + 1) * IDX_DIM, :] for h in range(IDX_HEADS)], axis=1)
    iw = ikq_ref[0].T[IDX_DIM:IDX_DIM + IDX_HEADS, :]

    def score_block(j, _):
        start = pl.multiple_of(j * blk, blk)
        kk = ik_ref[0, pl.ds(start, blk), :][:, :IDX_DIM].astype(BF16)
        d = _dot(kk, idx_rhs)
        sc = jnp.zeros((blk, blk), F32)
        for h in range(IDX_HEADS):
            sc = sc + iw[h:h + 1, :] * jnp.maximum(d[:, h * blk:(h + 1) * blk], 0.0)
        sc = jnp.where((key_io + j * blk) <= (qry_io + i * blk), sc, NEG_INF)
        bits = pltpu.bitcast(sc, jnp.int32)
        key_ref[j] = bits ^ ((bits >> 31) & 0x7FFFFFFF)
        return 0

    lax.fori_loop(0, nblk, score_block, 0)

    def count(pred):
        def body(j, acc):
            hit = jnp.where(pred(key_ref[j]), 1.0, 0.0)
            return acc + jnp.sum(hit.reshape(blk // 8, 8, blk), axis=0)
        return jnp.sum(lax.fori_loop(0, nblk, body, jnp.zeros((8, blk), F32)), axis=0, keepdims=True)

    def bit_step(b, thr):
        cand = thr + lax.shift_left(jnp.int32(1), 31 - b)
        return jnp.where(count(lambda key: key >= cand) >= topk, cand, thr)

    thr = lax.fori_loop(0, 32, bit_step, jnp.full((1, blk), INT_MIN, jnp.int32))

    need = topk - count(lambda key: key > thr)
    upto = jnp.where(qry_io <= key_io, 1.0, 0.0).astype(BF16)

    def select_block(j, run):
        key = key_ref[j]
        tie = jnp.where(key == thr, 1.0, 0.0)
        rank = _dot(upto, tie.astype(BF16)) + run
        keep_tie = jnp.where(rank <= need, 0.0, NEG_INF)
        madd_ref[j] = jnp.where(key > thr, 0.0, jnp.where(key == thr, keep_tie, NEG_INF))
        return run + jnp.sum(tie, axis=0, keepdims=True)

    lax.fori_loop(0, nblk, select_block, jnp.zeros((1, blk), F32))

    fold_scale = _is_power_of_two(scale)
    qT = q_ref[0].T
    if fold_scale:
        qT = qT * jnp.asarray(scale, qT.dtype)
    row = lax.broadcasted_iota(jnp.int32, (LANES, 1), 0)
    q_rhs = []
    for g in range(DSA_HEADS // 2):
        pair = qT[g * LANES:(g + 1) * LANES, :]
        zero = jnp.zeros_like(pair)
        q_rhs.append(jnp.concatenate([jnp.where(row < HEAD_DIM, pair, zero),
                                      jnp.where(row >= HEAD_DIM, pair, zero)], axis=1))
    acc_ref[...] = jnp.zeros_like(acc_ref)

    def attend_block(j, carry):
        m_all, l_all = carry
        start = pl.multiple_of(j * blk, blk)
        kblk = k_ref[0, pl.ds(start, blk), :]
        vT = vT_ref[j]
        madd = madd_ref[j]
        which = jnp.minimum(i - j, 2)
        s2 = [_dot(kblk[:, g * LANES:(g + 1) * LANES], q_rhs[g]) for g in range(DSA_HEADS // 2)]
        m_out, l_out, probs, alphas = [], [], [], []
        for h in range(DSA_HEADS):
            s = s2[h // 2][:, (h % 2) * blk:(h % 2 + 1) * blk]
            if not fold_scale:
                s = s * scale
            s = s + bias_ref[h, which] + madd
            m_new = jnp.maximum(m_all[h], jnp.max(s, axis=0, keepdims=True))
            p = jnp.exp(s - m_new)
            alpha = jnp.exp(m_all[h] - m_new)
            l_out.append(alpha * l_all[h] + jnp.sum(p, axis=0, keepdims=True))
            m_out.append(m_new)
            probs.append(p.astype(BF16))
            alphas.append(alpha)
        for h in range(DSA_HEADS):
            rows = slice(h * HEAD_DIM, (h + 1) * HEAD_DIM)
            acc_ref[rows, :] = alphas[h] * acc_ref[rows, :] + _dot(vT[rows, :], probs[h])
        return tuple(m_out), tuple(l_out)

    init = (tuple(jnp.full((1, blk), NEG_INF, F32) for _ in range(DSA_HEADS)),
            tuple(jnp.zeros((1, blk), F32) for _ in range(DSA_HEADS)))
    _, l_all = lax.fori_loop(0, nblk, attend_block, init)
    for h in range(DSA_HEADS):
        rows = slice(h * HEAD_DIM, (h + 1) * HEAD_DIM)
        acc_ref[rows, :] = acc_ref[rows, :] / l_all[h]
    o_ref[0] = acc_ref[...].T.astype(o_ref.dtype)


def dsa_attention(proj, qkv_col, iq_col, proj_f32, kw_col, bias_tiles, blk=DSA_BLOCK):
    bsz, seq, _ = proj.shape
    width = DSA_HEADS * HEAD_DIM
    nblk = seq // blk
    topk = min(DSA_TOPK, seq // 4)
    iq_width = IDX_HEADS * IDX_DIM
    assert qkv_col % width == 0 and iq_col % iq_width == 0 and kw_col % LANES == 0
    qcol = qkv_col // width
    return pl.pallas_call(
        functools.partial(_dsa_kernel, blk=blk, topk=topk, scale=HEAD_DIM ** -0.5),
        grid=(bsz, nblk),
        in_specs=[pl.BlockSpec((1, blk, width), lambda b, i: (b, i, qcol)),
                  pl.BlockSpec((1, seq, width), lambda b, i: (b, 0, qcol + 1)),
                  pl.BlockSpec((1, seq, width), lambda b, i: (b, 0, qcol + 2)),
                  pl.BlockSpec((1, blk, iq_width), lambda b, i: (b, i, iq_col // iq_width)),
                  pl.BlockSpec((1, seq, LANES), lambda b, i: (b, 0, kw_col // LANES)),
                  pl.BlockSpec((1, blk, LANES), lambda b, i: (b, i, kw_col // LANES)),
                  pl.BlockSpec((DSA_HEADS, 3, blk, blk), lambda b, i: (0, 0, 0, 0))],
        out_specs=pl.BlockSpec((1, blk, width), lambda b, i: (b, i, 0)),
        out_shape=jax.ShapeDtypeStruct((bsz, seq, width), BF16),
        scratch_shapes=[pltpu.VMEM((nblk, width, blk), BF16),
                        pltpu.VMEM((nblk, blk, blk), jnp.int32),
                        pltpu.VMEM((nblk, blk, blk), F32),
                        pltpu.VMEM((width, blk), F32)],
        compiler_params=_params("parallel", "arbitrary"),
        name="dsa_attention",
    )(proj, proj, proj, proj, proj_f32, proj_f32, bias_tiles)


def hybrid_mixer(xbf, bsz, seq, w_in, rel_bias, ssm_params, d_skip, w_glu, w_branches):
    offs = [0]
    for width in IN_SPLITS:
        offs.append(offs[-1] + width)
    seg = lambda a: w_in[:, offs[a]:offs[a + 1]]
    w_bf = jnp.concatenate([seg(0), seg(3), seg(4)], axis=1).astype(BF16)
    pad = jnp.zeros((w_in.shape[0], LANES - IDX_DIM - IDX_HEADS), w_in.dtype)
    w_f32 = jnp.concatenate([seg(1), seg(2), seg(5), seg(6), pad], axis=1).astype(BF16)
    dsa_col = IN_SPLITS[0]
    iq_col = dsa_col + IN_SPLITS[3]
    ssm_col = IN_SPLITS[1]
    kw_col = ssm_col + IN_SPLITS[2]
    proj = matmul(xbf, w_bf, BF16).reshape(bsz, seq, -1)
    proj_f32 = matmul(xbf, w_f32, F32).reshape(bsz, seq, -1)

    y_sb = stick_breaking_attention(proj).reshape(bsz * seq, -1)

    outs, lses = [], []
    for g, (window, dil) in enumerate(DIL_PATTERNS):
        assert window // dil == ATT_BLOCK
        tiles = _dilated_bias_tiles(rel_bias[:, g * DIL_HEADS_PER_GROUP:(g + 1) * DIL_HEADS_PER_GROUP], dil, ATT_BLOCK)
        o, lse = dilated_group_attention(proj_f32, tiles, g, dil)
        outs.append(o)
        lses.append(lse)
    y_dil = dilated_merge(outs, lses).reshape(bsz * seq, -1)

    bb, a_bar, cc = _ssm_tables(*ssm_params)
    y_ssm = s5_glu(proj_f32[..., ssm_col:ssm_col + SSM_WIDTH], bb, a_bar, cc, d_skip, w_glu.astype(BF16))

    y_dsa = dsa_attention(proj, dsa_col, iq_col, proj_f32, kw_col, _dsa_bias_tiles(rel_bias[:, DIL_HEADS:], DSA_BLOCK))
    y_dsa = y_dsa.reshape(bsz * seq, -1)

    return gated_branch_merge(xbf, [y_sb, y_dil, y_ssm, y_dsa], [w.astype(BF16) for w in w_branches],
                              seg(7).astype(BF16))


def kernel(x, ln_g, ln_b, ffn1_w_up, ffn1_w_down, w_in, rel_bias, ssm_lam_re, ssm_lam_im, ssm_log_dt,
           ssm_b_re, ssm_b_im, ssm_c_re, ssm_c_im, ssm_d, ssm_w_glu, w_br_sb, w_br_dil, w_br_ssm, w_br_dsa,
           w_out, ffn2_w_up, ffn2_w_down):
    bsz, seq, d = x.shape
    xf = x.reshape(bsz * seq, d)
    xbf = xf.astype(BF16)
    for l in range(DEPTH):
        h = ffn_up(xbf, ffn1_w_up[l].astype(BF16))
        xf, xbf = matmul_residual_layernorm(h, ffn1_w_down[l].astype(BF16), xf, ln_g[l, 0], ln_b[l, 0], MACARON)
        merged = hybrid_mixer(xbf, bsz, seq, w_in[l], rel_bias,
                              (ssm_lam_re[l], ssm_lam_im[l], ssm_log_dt[l], ssm_b_re[l], ssm_b_im[l],
                               ssm_c_re[l], ssm_c_im[l]), ssm_d[l], ssm_w_glu[l],
                              (w_br_sb[l], w_br_dil[l], w_br_ssm[l], w_br_dsa[l]))
        xf, xbf = matmul_residual_layernorm(merged, w_out[l].astype(BF16), xf, ln_g[l, 1], ln_b[l, 1], 1.0)
        h = ffn_up(xbf, ffn2_w_up[l].astype(BF16))
        xf, xbf = matmul_residual_layernorm(h, ffn2_w_down[l].astype(BF16), xf, ln_g[l, 2], ln_b[l, 2], MACARON)
    return xf.reshape(bsz, seq, d)
```

```python
import functools
import math

import jax
import jax.numpy as jnp
from jax import lax
from jax.experimental import pallas as pl
from jax.experimental.pallas import tpu as pltpu

F32 = jnp.float32
BF16 = jnp.bfloat16

D_MODEL = 2048
DEPTH = 2
HEAD_DIM = 64
SB_HEADS = 8
DIL_PATTERNS = ((128, 1), (512, 4), (2048, 16))
DIL_HEADS_PER_GROUP = 4
DIL_HEADS = DIL_HEADS_PER_GROUP * len(DIL_PATTERNS)
SSM_WIDTH = 512
SSM_GROUP = 16
SSM_GROUPS = SSM_WIDTH // SSM_GROUP
SSM_STATE = 64
DSA_HEADS = 8
IDX_HEADS = 8
IDX_DIM = 64
DSA_TOPK = 256
N_BRANCH = 4
D_FF = 5632
REL_BUCKETS = 32
REL_MAX_DIST = 128
DN_ALPHA = (2.0 * DEPTH) ** 0.25
LN_EPS = 1e-5
NEG_INF = -1e30
MACARON = 0.5

IN_SPLITS = (3 * SB_HEADS * HEAD_DIM, 3 * DIL_HEADS * HEAD_DIM, SSM_WIDTH, 3 * DSA_HEADS * HEAD_DIM,
             IDX_HEADS * IDX_DIM, IDX_DIM, IDX_HEADS, N_BRANCH * D_MODEL)

LANES = 128
ATT_BLOCK = 128
DIL_UNITS = 2
DSA_BLOCK = 256
SB_QUERY_BLOCK = 1024
SB_KEY_BLOCK = 256
SUBLANES = 8
SSM_STEPS = 32
SSM_STRIP = 512
VMEM_LIMIT = 56 * 1024 * 1024
INT_MIN = -2 ** 31
SOFTPLUS_CLAMP = 80.0

_NT = (((1,), (1,)), ((), ()))


def _dot(a, b):
    return jnp.dot(a, b, preferred_element_type=F32)


def _dot_nt(a, b):
    return lax.dot_general(a, b, _NT, preferred_element_type=F32)


def _params(*sem):
    return pltpu.CompilerParams(dimension_semantics=sem, vmem_limit_bytes=VMEM_LIMIT)


def _sigmoid(x):
    return 1.0 / (1.0 + jnp.exp(-x))


def _mm_kernel(x_ref, w_ref, o_ref):
    o_ref[...] = _dot(x_ref[...], w_ref[...]).astype(o_ref.dtype)


def matmul(x, w, out_dtype, tm=512):
    m, k = x.shape
    n = w.shape[1]
    assert m % tm == 0 and n % LANES == 0
    return pl.pallas_call(
        _mm_kernel,
        grid=(m // tm,),
        in_specs=[pl.BlockSpec((tm, k), lambda i: (i, 0)),
                  pl.BlockSpec((k, n), lambda i: (0, 0), pipeline_mode=pl.Buffered(1))],
        out_specs=pl.BlockSpec((tm, n), lambda i: (i, 0)),
        out_shape=jax.ShapeDtypeStruct((m, n), out_dtype),
        compiler_params=_params("parallel"),
        name="matmul",
    )(x, w)


def _ffn_up_kernel(x_ref, wa_ref, wb_ref, o_ref):
    x = x_ref[...].astype(BF16)
    a = _dot(x, wa_ref[...])
    b = _dot(x, wb_ref[...])
    o_ref[...] = (a * _sigmoid(a) * b).astype(o_ref.dtype)


def ffn_up(x, w_up, layer, tm=1024, tn=512):
    m, k = x.shape
    f = w_up.shape[2] // 2
    assert m % tm == 0 and f % tn == 0
    nb = f // tn
    return pl.pallas_call(
        _ffn_up_kernel,
        grid=(m // tm, nb),
        in_specs=[pl.BlockSpec((tm, k), lambda i, j: (i, 0)),
                  pl.BlockSpec((None, k, tn), lambda i, j: (layer, 0, j)),
                  pl.BlockSpec((None, k, tn), lambda i, j: (layer, 0, j + nb))],
        out_specs=pl.BlockSpec((tm, tn), lambda i, j: (i, j)),
        out_shape=jax.ShapeDtypeStruct((m, f), BF16),
        compiler_params=_params("parallel", "arbitrary"),
        name="ffn_up",
    )(x, w_up, w_up)


def _mm_res_ln_kernel(h_ref, w_ref, x_ref, g_ref, b_ref, o_ref, obf_ref, *, scale):
    y = DN_ALPHA * x_ref[...] + scale * _dot(h_ref[...], w_ref[...])
    mu = jnp.mean(y, axis=-1, keepdims=True)
    yc = y - mu
    var = jnp.mean(yc * yc, axis=-1, keepdims=True)
    out = yc * lax.rsqrt(var + LN_EPS) * g_ref[...] + b_ref[...]
    o_ref[...] = out
    obf_ref[...] = out.astype(BF16)


def matmul_residual_layernorm(h, w, layer, x, g, b, scale, tm=256):
    m, k = h.shape
    n = w.shape[2]
    assert m % tm == 0
    return pl.pallas_call(
        functools.partial(_mm_res_ln_kernel, scale=scale),
        grid=(m // tm,),
        in_specs=[pl.BlockSpec((tm, k), lambda i: (i, 0)),
                  pl.BlockSpec((None, k, n), lambda i: (layer, 0, 0), pipeline_mode=pl.Buffered(1)),
                  pl.BlockSpec((tm, n), lambda i: (i, 0)),
                  pl.BlockSpec((1, n), lambda i: (0, 0)),
                  pl.BlockSpec((1, n), lambda i: (0, 0))],
        out_specs=[pl.BlockSpec((tm, n), lambda i: (i, 0)),
                   pl.BlockSpec((tm, n), lambda i: (i, 0))],
        out_shape=[jax.ShapeDtypeStruct((m, n), F32), jax.ShapeDtypeStruct((m, n), BF16)],
        compiler_params=_params("parallel"),
        name="matmul_residual_layernorm",
    )(h, w, x, g.reshape(1, n), b.reshape(1, n))


def _gated_merge_kernel(x_ref, ysb_ref, ydil_ref, yssm_ref, ydsa_ref, wsb_ref, wdil_ref, wssm_ref, wdsa_ref,
                        g0_ref, g1_ref, g2_ref, g3_ref, o_ref):
    x = x_ref[...]
    acc = _sigmoid(_dot(x, g0_ref[...])) * _dot(ysb_ref[...], wsb_ref[...])
    acc += _sigmoid(_dot(x, g1_ref[...])) * _dot(ydil_ref[...], wdil_ref[...])
    acc += _sigmoid(_dot(x, g2_ref[...])) * _dot(yssm_ref[...], wssm_ref[...])
    acc += _sigmoid(_dot(x, g3_ref[...])) * _dot(ydsa_ref[...], wdsa_ref[...])
    o_ref[...] = acc.astype(o_ref.dtype)


def gated_branch_merge(x, ys, ws, w_gate, tm=1024, tn=512):
    m, k = x.shape
    n = ws[0].shape[1]
    assert m % tm == 0 and n % tn == 0
    nb = n // tn
    y_specs = [pl.BlockSpec((tm, y.shape[1]), lambda i, j: (i, 0)) for y in ys]
    w_specs = [pl.BlockSpec((w.shape[0], tn), lambda i, j: (0, j)) for w in ws]
    g_specs = [pl.BlockSpec((k, tn), functools.partial(lambda i, j, br: (0, j + br * nb), br=br))
               for br in range(N_BRANCH)]
    return pl.pallas_call(
        _gated_merge_kernel,
        grid=(m // tm, nb),
        in_specs=[pl.BlockSpec((tm, k), lambda i, j: (i, 0))] + y_specs + w_specs + g_specs,
        out_specs=pl.BlockSpec((tm, tn), lambda i, j: (i, j)),
        out_shape=jax.ShapeDtypeStruct((m, n), BF16),
        compiler_params=_params("parallel", "arbitrary"),
        name="gated_branch_merge",
    )(x, *ys, *ws, w_gate, w_gate, w_gate, w_gate)


def _is_power_of_two(x):
    return math.frexp(x)[0] == 0.5


def _sb_kernel(q_ref, k_ref, v_ref, o_ref, vT_ref, acc_ref, *, qblk, kblk, scale):
    i = pl.program_id(2)
    ratio = qblk // kblk
    fold_scale = _is_power_of_two(scale)

    @pl.when(i == 0)
    def _():
        for j in range(vT_ref.shape[0]):
            vT_ref[j] = v_ref[0, j * kblk:(j + 1) * kblk, :].T

    qT = q_ref[0].T
    if fold_scale:
        qT = qT * jnp.asarray(scale, qT.dtype)
    row = lax.broadcasted_iota(jnp.int32, (LANES, 1), 0)
    zero = jnp.zeros_like(qT)
    q_rhs = jnp.concatenate([jnp.where(row < HEAD_DIM, qT, zero), jnp.where(row >= HEAD_DIM, qT, zero)], axis=1)
    key_io = lax.broadcasted_iota(jnp.int32, (kblk, kblk), 0)
    qry_io = lax.broadcasted_iota(jnp.int32, (kblk, kblk), 1)
    later = jnp.where(qry_io > key_io, 1.0, 0.0).astype(BF16)
    acc_ref[...] = jnp.zeros_like(acc_ref)
    nsub = 2 * ratio

    def block(j, carry, diag_sub):
        start = pl.multiple_of(j * kblk, kblk)
        z_all = _dot(k_ref[0, pl.ds(start, kblk), :], q_rhs)
        vT = vT_ref[j]
        strict = key_io < qry_io
        stage = []
        for c in range(nsub):
            h, sub = divmod(c, ratio)
            if diag_sub is not None and sub < diag_sub:
                stage.append(None)
                continue
            masked = diag_sub is not None and sub == diag_sub
            z = z_all[:, c * kblk:(c + 1) * kblk]
            if not fold_scale:
                z = z * scale
            sp = jnp.maximum(jnp.log(1.0 + jnp.exp(jnp.minimum(z, SOFTPLUS_CLAMP))), z)
            log_1mb = jnp.where(strict, -sp, 0.0) if masked else -sp
            suf = _dot(later, log_1mb.astype(BF16))
            stage.append((z - sp, suf, jnp.sum(log_1mb, axis=0, keepdims=True), masked))
        new_carry = []
        for c in range(nsub):
            h, sub = divmod(c, ratio)
            if stage[c] is None:
                new_carry.append(carry[c])
                continue
            log_beta, suf, colsum, masked = stage[c]
            att = jnp.exp(log_beta + suf + carry[c])
            if masked:
                att = jnp.where(strict, att, 0.0)
            rows = slice(h * HEAD_DIM, (h + 1) * HEAD_DIM)
            cols = slice(sub * kblk, (sub + 1) * kblk)
            acc_ref[rows, cols] += _dot(vT[rows, :], att.astype(BF16))
            new_carry.append(carry[c] + colsum)
        return tuple(new_carry)

    carry = tuple(jnp.zeros((1, kblk), F32) for _ in range(nsub))
    for sub in reversed(range(ratio)):
        carry = block(i * ratio + sub, carry, sub)
    lax.fori_loop(0, i * ratio, lambda jj, c: block(i * ratio - 1 - jj, c, None), carry)
    o_ref[0] = acc_ref[...].T.astype(o_ref.dtype)


def stick_breaking_attention(qkv, col0=0, qblk=SB_QUERY_BLOCK, kblk=SB_KEY_BLOCK):
    bsz, seq, _ = qkv.shape
    width = SB_HEADS * HEAD_DIM
    npair = width // LANES
    qblk = min(qblk, seq)
    assert seq % qblk == 0 and qblk % kblk == 0 and col0 % LANES == 0
    qcol = col0 // LANES
    return pl.pallas_call(
        functools.partial(_sb_kernel, qblk=qblk, kblk=kblk, scale=HEAD_DIM ** -0.5),
        grid=(bsz, npair, seq // qblk),
        in_specs=[pl.BlockSpec((1, qblk, LANES), lambda b, p, i: (b, i, qcol + p)),
                  pl.BlockSpec((1, seq, LANES), lambda b, p, i: (b, 0, qcol + npair + p)),
                  pl.BlockSpec((1, seq, LANES), lambda b, p, i: (b, 0, qcol + 2 * npair + p))],
        out_specs=pl.BlockSpec((1, qblk, LANES), lambda b, p, i: (b, i, p)),
        out_shape=jax.ShapeDtypeStruct((bsz, seq, width), BF16),
        scratch_shapes=[pltpu.VMEM((seq // kblk, LANES, kblk), BF16),
                        pltpu.VMEM((LANES, qblk), F32)],
        compiler_params=_params("parallel", "parallel", "arbitrary"),
        name="stick_breaking_attention",
    )(qkv, qkv, qkv)


def _t5_bucket(dist):
    max_exact = REL_BUCKETS // 2
    d = jnp.maximum(dist, 1).astype(F32)
    large = max_exact + (jnp.log(d / max_exact) / math.log(REL_MAX_DIST / max_exact)
                         * (REL_BUCKETS - max_exact)).astype(jnp.int32)
    large = jnp.minimum(large, REL_BUCKETS - 1)
    return jnp.where(dist < max_exact, dist, large)


def _bias_of_distance(rel_bias, dist):
    one_hot = jax.nn.one_hot(_t5_bucket(dist), REL_BUCKETS, dtype=F32)
    return jnp.einsum("...b,bh->...h", one_hot, rel_bias.astype(F32), precision=lax.Precision.HIGHEST)


def _dilated_bias_tiles(rel_bias_group, dil, blk):
    a = jnp.arange(blk)[:, None]
    cc = jnp.arange(2 * blk)[None, :]
    step = blk + a - cc
    valid = (step >= 0) & (step <= blk)
    bias = _bias_of_distance(rel_bias_group, dil * jnp.clip(step, 0, blk))
    tile = jnp.where(valid[..., None], bias, NEG_INF)
    return jnp.transpose(tile, (2, 0, 1))


def _dsa_bias_tiles(rel_bias_dsa, blk):
    c = jnp.arange(blk)[:, None]
    a = jnp.arange(blk)[None, :]
    own = jnp.where((a - c >= 0)[..., None], _bias_of_distance(rel_bias_dsa, jnp.maximum(a - c, 0)), NEG_INF)
    prev = _bias_of_distance(rel_bias_dsa, blk + a - c)
    half = REL_BUCKETS // 2
    assert half + int(math.log((blk + 1) / half) / math.log(REL_MAX_DIST / half) * (REL_BUCKETS - half)) >= REL_BUCKETS - 1
    far = jnp.broadcast_to(rel_bias_dsa[REL_BUCKETS - 1].astype(F32), prev.shape)
    return jnp.transpose(jnp.stack([own, prev, far]), (3, 0, 1, 2))


def _dil_kernel(q0_ref, q1_ref, k0_ref, k1_ref, v0_ref, v1_ref, bias_ref, o_ref, lse_ref, *, blk, dil, nb, units, scale):
    step = pl.program_id(1)
    lane = lax.broadcasted_iota(jnp.int32, (1, LANES), 1)
    fold_scale = _is_power_of_two(scale)

    def rows(c, block):
        start = c + dil * blk * block
        return pl.ds(start, blk, stride=dil) if dil > 1 else pl.ds(pl.multiple_of(start, blk), blk)

    work = []
    for u in range(units):
        unit = step * units + u
        c, i = unit // nb, unit % nb
        cur, prev = rows(c, i), rows(c, jnp.maximum(i - 1, 0))
        for half, (q_ref, k_ref, v_ref) in enumerate(((q0_ref, k0_ref, v0_ref), (q1_ref, k1_ref, v1_ref))):
            q = q_ref[0, cur, :]
            if fold_scale:
                q = q * scale
            q = q.astype(BF16)
            kc, kp = k_ref[0, cur, :].astype(BF16), k_ref[0, prev, :].astype(BF16)
            zero = jnp.zeros_like(q)
            scores = []
            for e in range(2):
                head = (lane < HEAD_DIM) if e == 0 else (lane >= HEAD_DIM)
                qm = jnp.where(head, q, zero)
                scores.append((_dot_nt(qm, kp), _dot_nt(qm, kc)))
            work.append((i, cur, prev, half, v_ref, scores))
    soft = []
    for i, cur, prev, half, v_ref, scores in work:
        has_prev = i > 0
        parts = []
        for e in range(2):
            h = 2 * half + e
            s_prev, s_cur = scores[e]
            if not fold_scale:
                s_prev, s_cur = s_prev * scale, s_cur * scale
            s_prev = jnp.where(has_prev, s_prev + bias_ref[h, :, :blk], NEG_INF)
            s_cur = s_cur + bias_ref[h, :, blk:]
            m = jnp.maximum(jnp.max(s_prev, axis=-1, keepdims=True), jnp.max(s_cur, axis=-1, keepdims=True))
            p_prev = jnp.exp(s_prev - m)
            p_cur = jnp.exp(s_cur - m)
            denom = jnp.sum(p_prev, axis=-1, keepdims=True) + jnp.sum(p_cur, axis=-1, keepdims=True)
            parts.append((p_prev.astype(BF16), p_cur.astype(BF16), denom, m + jnp.log(denom)))
        soft.append(parts)
    for (i, cur, prev, half, v_ref, scores), parts in zip(work, soft):
        vc, vp = v_ref[0, cur, :].astype(BF16), v_ref[0, prev, :].astype(BF16)
        out = jnp.zeros((blk, LANES), F32)
        lse_b = jnp.zeros((blk, LANES), F32)
        for e in range(2):
            head = (lane < HEAD_DIM) if e == 0 else (lane >= HEAD_DIM)
            p_prev, p_cur, denom, lse = parts[e]
            o = (_dot(p_prev, vp) + _dot(p_cur, vc)) / denom
            out = jnp.where(head, o, out)
            lse_b = jnp.where(head, lse, lse_b)
        o_ref[0, half, cur, :] = out
        lse_ref[0, half, cur, :] = lse_b


def dilated_group_attention(qkv, bias_tiles, group, dil, blk=ATT_BLOCK, units=DIL_UNITS):
    bsz, seq, _ = qkv.shape
    assert DIL_HEADS_PER_GROUP * HEAD_DIM == 2 * LANES
    nslab = DIL_HEADS * HEAD_DIM // LANES
    n = seq // dil
    assert n % blk == 0
    nb = n // blk
    assert (dil * nb) % units == 0
    slab = lambda section, half: pl.BlockSpec(
        (1, seq, LANES), functools.partial(lambda b, s, col: (b, 0, col), col=section * nslab + 2 * group + half))
    out_spec = pl.BlockSpec((1, 2, seq, LANES), lambda b, s: (b, 0, 0, 0))
    return pl.pallas_call(
        functools.partial(_dil_kernel, blk=blk, dil=dil, nb=nb, units=units, scale=HEAD_DIM ** -0.5),
        grid=(bsz, dil * nb // units),
        in_specs=[slab(0, 0), slab(0, 1), slab(1, 0), slab(1, 1), slab(2, 0), slab(2, 1),
                  pl.BlockSpec((DIL_HEADS_PER_GROUP, blk, 2 * blk), lambda b, s: (0, 0, 0))],
        out_specs=[out_spec, out_spec],
        out_shape=[jax.ShapeDtypeStruct((bsz, 2, seq, LANES), F32)] * 2,
        compiler_params=_params("parallel", "arbitrary"),
        name=f"dilated_attention_g{group}",
    )(qkv, qkv, qkv, qkv, qkv, qkv, bias_tiles)


def _dil_merge_kernel(o0_ref, o1_ref, o2_ref, l0_ref, l1_ref, l2_ref, y_ref):
    for half in range(2):
        l0, l1, l2 = l0_ref[0, half], l1_ref[0, half], l2_ref[0, half]
        m = jnp.maximum(jnp.maximum(l0, l1), l2)
        w0, w1, w2 = jnp.exp(l0 - m), jnp.exp(l1 - m), jnp.exp(l2 - m)
        y = (w0 * o0_ref[0, half] + w1 * o1_ref[0, half] + w2 * o2_ref[0, half]) / (w0 + w1 + w2)
        y_ref[0, :, half * LANES:(half + 1) * LANES] = y.astype(y_ref.dtype)


def dilated_merge(outs, lses, tm=1024):
    bsz, _, seq, _ = outs[0].shape
    spec = pl.BlockSpec((1, 2, tm, LANES), lambda b, i: (b, 0, i, 0))
    return pl.pallas_call(
        _dil_merge_kernel,
        grid=(bsz, seq // tm),
        in_specs=[spec] * 6,
        out_specs=pl.BlockSpec((1, tm, 2 * LANES), lambda b, i: (b, i, 0)),
        out_shape=jax.ShapeDtypeStruct((bsz, seq, 2 * LANES), BF16),
        compiler_params=_params("parallel", "parallel"),
        name="dilated_merge",
    )(*outs, *lses)


def _ssm_kernel(u_ref, bb_ref, a_ref, cc_ref, d_ref, wg_ref, o_ref, x_ref, carry_ref, *, steps, strip):
    @pl.when(pl.program_id(1) == 0)
    def _():
        carry_ref[...] = jnp.zeros_like(carry_ref)

    u = u_ref[0]
    ub = u.astype(BF16)
    width = u.shape[1]
    half = width // 2
    nstates = x_ref.shape[1] // 2
    hs = nstates // 2
    for part in range(2):
        bu = _dot(ub[:, part * half:(part + 1) * half], bb_ref[part])
        x_ref[:, part * hs:(part + 1) * hs] = bu[:, :hs]
        x_ref[:, nstates + part * hs:nstates + (part + 1) * hs] = bu[:, hs:]
    for s in range(nstates // strip):
        re = slice(s * strip, (s + 1) * strip)
        im = slice(nstates + s * strip, nstates + (s + 1) * strip)
        ar = jnp.broadcast_to(a_ref[0:1, re], (SUBLANES, strip))
        ai = jnp.broadcast_to(a_ref[1:2, re], (SUBLANES, strip))
        xr = carry_ref[:, re]
        xi = carry_ref[:, im]
        for t in range(steps):
            rows = slice(t * SUBLANES, (t + 1) * SUBLANES)
            xr, xi = ar * xr - ai * xi + x_ref[rows, re], ar * xi + ai * xr + x_ref[rows, im]
            x_ref[rows, re] = xr
            x_ref[rows, im] = xi
        carry_ref[:, re] = xr
        carry_ref[:, im] = xi
    y_parts = []
    for part in range(2):
        xr = x_ref[:, part * hs:(part + 1) * hs].astype(BF16)
        xi = x_ref[:, nstates + part * hs:nstates + (part + 1) * hs].astype(BF16)
        y_parts.append(_dot(xr, cc_ref[0, part]) + _dot(xi, cc_ref[1, part]))
    y = jnp.concatenate(y_parts, axis=1) + d_ref[...] * u
    y = 0.5 * y * (1.0 + jnp.tanh(math.sqrt(2.0 / math.pi) * (y + 0.044715 * (y * y * y))))
    z = _dot(y.astype(BF16), wg_ref[...])
    o_ref[0] = (z[:, :width] * _sigmoid(z[:, width:])).astype(o_ref.dtype)


def _block_diag(blocks):
    g, r, c = blocks.shape
    eye = jnp.eye(g, dtype=blocks.dtype)
    return (eye[:, None, :, None] * blocks[:, :, None, :]).reshape(g * r, g * c)


def _ssm_tables(lam_re, lam_im, log_dt, b_re, b_im, c_re, c_im):
    lr, li = lam_re.astype(F32), lam_im.astype(F32)
    dt = jnp.exp(log_dt.astype(F32))[:, None]
    mag = jnp.exp(lr * dt)
    a_re, a_im = mag * jnp.cos(li * dt), mag * jnp.sin(li * dt)
    den = lr * lr + li * li
    f_re = ((a_re - 1.0) * lr + a_im * li) / den
    f_im = (a_im * lr - (a_re - 1.0) * li) / den
    br, bi = b_re.astype(F32), b_im.astype(F32)
    bb_re = _block_diag(jnp.transpose(f_re[..., None] * br - f_im[..., None] * bi, (0, 2, 1)))
    bb_im = _block_diag(jnp.transpose(f_re[..., None] * bi + f_im[..., None] * br, (0, 2, 1)))
    cc_re = _block_diag(jnp.transpose(c_re.astype(F32), (0, 2, 1)))
    cc_im = -_block_diag(jnp.transpose(c_im.astype(F32), (0, 2, 1)))
    nstates = SSM_GROUPS * SSM_STATE
    hw, hs = SSM_WIDTH // 2, nstates // 2
    bb = jnp.stack([jnp.concatenate([bb_re[p * hw:(p + 1) * hw, p * hs:(p + 1) * hs],
                                     bb_im[p * hw:(p + 1) * hw, p * hs:(p + 1) * hs]], axis=1) for p in range(2)])
    cc = jnp.stack([jnp.stack([m[p * hs:(p + 1) * hs, p * hw:(p + 1) * hw] for p in range(2)])
                    for m in (cc_re, cc_im)])
    a = jnp.stack([a_re.reshape(nstates), a_im.reshape(nstates)])
    return bb.astype(BF16), a, cc.astype(BF16)


def s5_glu(u, bb, a, cc, d_skip, w_glu, steps=SSM_STEPS, strip=SSM_STRIP):
    bsz, seq, width = u.shape
    nstates = a.shape[1]
    assert bsz % SUBLANES == 0 and seq % steps == 0 and nstates % strip == 0
    ngrp = bsz // SUBLANES
    rows = steps * SUBLANES
    ut = u.reshape(ngrp, SUBLANES, seq, width).transpose(0, 2, 1, 3).reshape(ngrp, seq * SUBLANES, width)
    const = lambda *shape: pl.BlockSpec(shape, lambda g, c: (0,) * len(shape))
    out = pl.pallas_call(
        functools.partial(_ssm_kernel, steps=steps, strip=strip),
        grid=(ngrp, seq // steps),
        in_specs=[pl.BlockSpec((1, rows, width), lambda g, c: (g, c, 0)),
                  const(*bb.shape), const(*a.shape), const(*cc.shape),
                  const(1, width), const(*w_glu.shape)],
        out_specs=pl.BlockSpec((1, rows, width), lambda g, c: (g, c, 0)),
        out_shape=jax.ShapeDtypeStruct((ngrp, seq * SUBLANES, width), BF16),
        scratch_shapes=[pltpu.VMEM((rows, 2 * nstates), F32),
                        pltpu.VMEM((SUBLANES, 2 * nstates), F32)],
        compiler_params=_params("parallel", "arbitrary"),
        name="s5_glu",
    )(ut, bb, a, cc, d_skip.reshape(1, width).astype(F32), w_glu)
    return out.reshape(ngrp, seq, SUBLANES, width).transpose(0, 2, 1, 3).reshape(bsz * seq, width)


def _dsa_kernel(q_ref, k_ref, v_ref, iq_ref, ik_ref, ikq_ref, bias_ref, o_ref,
                vT_ref, key_ref, madd_ref, acc_ref, *, blk, topk, scale):
    i = pl.program_id(1)
    nblk = i + 1
    key_io = lax.broadcasted_iota(jnp.int32, (blk, blk), 0)
    qry_io = lax.broadcasted_iota(jnp.int32, (blk, blk), 1)

    @pl.when(i == 0)
    def _():
        for j in range(vT_ref.shape[0]):
            vT_ref[j] = v_ref[0, j * blk:(j + 1) * blk, :].T

    iqT = iq_ref[0].T
    idx_rhs = jnp.concatenate([iqT[h * IDX_DIM:(h + 1) * IDX_DIM, :] for h in range(IDX_HEADS)], axis=1)
    iw = ikq_ref[0].T[IDX_DIM:IDX_DIM + IDX_HEADS, :]

    def score_block(j, _):
        start = pl.multiple_of(j * blk, blk)
        kk = ik_ref[0, pl.ds(start, blk), :][:, :IDX_DIM].astype(BF16)
        d = _dot(kk, idx_rhs)
        sc = jnp.zeros((blk, blk), F32)
        for h in range(IDX_HEADS):
            sc = sc + iw[h:h + 1, :] * jnp.maximum(d[:, h * blk:(h + 1) * blk], 0.0)
        sc = jnp.where((key_io + j * blk) <= (qry_io + i * blk), sc, NEG_INF)
        bits = pltpu.bitcast(sc, jnp.int32)
        key_ref[j] = bits ^ ((bits >> 31) & 0x7FFFFFFF)
        return 0

    lax.fori_loop(0, nblk, score_block, 0)

    def count(pred):
        def body(j, acc):
            hit = jnp.where(pred(key_ref[j]), 1.0, 0.0)
            return acc + jnp.sum(hit.reshape(blk // 8, 8, blk), axis=0)
        return jnp.sum(lax.fori_loop(0, nblk, body, jnp.zeros((8, blk), F32)), axis=0, keepdims=True)

    def bit_step(b, thr):
        cand = thr + lax.shift_left(jnp.int32(1), 31 - b)
        return jnp.where(count(lambda key: key >= cand) >= topk, cand, thr)

    thr = lax.fori_loop(0, 32, bit_step, jnp.full((1, blk), INT_MIN, jnp.int32))

    need = topk - count(lambda key: key > thr)
    upto = jnp.where(qry_io <= key_io, 1.0, 0.0).astype(BF16)

    def select_block(j, run):
        key = key_ref[j]
        tie = jnp.where(key == thr, 1.0, 0.0)
        rank = _dot(upto, tie.astype(BF16)) + run
        keep_tie = jnp.where(rank <= need, 0.0, NEG_INF)
        madd_ref[j] = jnp.where(key > thr, 0.0, jnp.where(key == thr, keep_tie, NEG_INF))
        return run + jnp.sum(tie, axis=0, keepdims=True)

    lax.fori_loop(0, nblk, select_block, jnp.zeros((1, blk), F32))

    fold_scale = _is_power_of_two(scale)
    qT = q_ref[0].T
    if fold_scale:
        qT = qT * jnp.asarray(scale, qT.dtype)
    row = lax.broadcasted_iota(jnp.int32, (LANES, 1), 0)
    q_rhs = []
    for g in range(DSA_HEADS // 2):
        pair = qT[g * LANES:(g + 1) * LANES, :]
        zero = jnp.zeros_like(pair)
        q_rhs.append(jnp.concatenate([jnp.where(row < HEAD_DIM, pair, zero),
                                      jnp.where(row >= HEAD_DIM, pair, zero)], axis=1))
    acc_ref[...] = jnp.zeros_like(acc_ref)

    def attend_block(j, carry):
        m_all, l_all = carry
        start = pl.multiple_of(j * blk, blk)
        kblk = k_ref[0, pl.ds(start, blk), :]
        vT = vT_ref[j]
        madd = madd_ref[j]
        which = jnp.minimum(i - j, 2)
        s2 = [_dot(kblk[:, g * LANES:(g + 1) * LANES], q_rhs[g]) for g in range(DSA_HEADS // 2)]
        m_out, l_out, probs, alphas = [], [], [], []
        for h in range(DSA_HEADS):
            s = s2[h // 2][:, (h % 2) * blk:(h % 2 + 1) * blk]
            if not fold_scale:
                s = s * scale
            s = s + bias_ref[h, which] + madd
            m_new = jnp.maximum(m_all[h], jnp.max(s, axis=0, keepdims=True))
            p = jnp.exp(s - m_new)
            alpha = jnp.exp(m_all[h] - m_new)
            l_out.append(alpha * l_all[h] + jnp.sum(p, axis=0, keepdims=True))
            m_out.append(m_new)
            probs.append(p.astype(BF16))
            alphas.append(alpha)
        for h in range(DSA_HEADS):
            rows = slice(h * HEAD_DIM, (h + 1) * HEAD_DIM)
            acc_ref[rows, :] = alphas[h] * acc_ref[rows, :] + _dot(vT[rows, :], probs[h])
        return tuple(m_out), tuple(l_out)

    init = (tuple(jnp.full((1, blk), NEG_INF, F32) for _ in range(DSA_HEADS)),
            tuple(jnp.zeros((1, blk), F32) for _ in range(DSA_HEADS)))
    _, l_all = lax.fori_loop(0, nblk, attend_block, init)
    for h in range(DSA_HEADS):
        rows = slice(h * HEAD_DIM, (h + 1) * HEAD_DIM)
        acc_ref[rows, :] = acc_ref[rows, :] / l_all[h]
    o_ref[0] = acc_ref[...].T.astype(o_ref.dtype)


def dsa_attention(proj, qkv_col, iq_col, proj_f32, kw_col, bias_tiles, blk=DSA_BLOCK):
    bsz, seq, _ = proj.shape
    width = DSA_HEADS * HEAD_DIM
    nblk = seq // blk
    topk = min(DSA_TOPK, seq // 4)
    iq_width = IDX_HEADS * IDX_DIM
    assert qkv_col % width == 0 and iq_col % iq_width == 0 and kw_col % LANES == 0
    qcol = qkv_col // width
    return pl.pallas_call(
        functools.partial(_dsa_kernel, blk=blk, topk=topk, scale=HEAD_DIM ** -0.5),
        grid=(bsz, nblk),
        in_specs=[pl.BlockSpec((1, blk, width), lambda b, i: (b, i, qcol)),
                  pl.BlockSpec((1, seq, width), lambda b, i: (b, 0, qcol + 1)),
                  pl.BlockSpec((1, seq, width), lambda b, i: (b, 0, qcol + 2)),
                  pl.BlockSpec((1, blk, iq_width), lambda b, i: (b, i, iq_col // iq_width)),
                  pl.BlockSpec((1, seq, LANES), lambda b, i: (b, 0, kw_col // LANES)),
                  pl.BlockSpec((1, blk, LANES), lambda b, i: (b, i, kw_col // LANES)),
                  pl.BlockSpec((DSA_HEADS, 3, blk, blk), lambda b, i: (0, 0, 0, 0))],
        out_specs=pl.BlockSpec((1, blk, width), lambda b, i: (b, i, 0)),
        out_shape=jax.ShapeDtypeStruct((bsz, seq, width), BF16),
        scratch_shapes=[pltpu.VMEM((nblk, width, blk), BF16),
                        pltpu.VMEM((nblk, blk, blk), jnp.int32),
                        pltpu.VMEM((nblk, blk, blk), F32),
                        pltpu.VMEM((width, blk), F32)],
        compiler_params=_params("parallel", "arbitrary"),
        name="dsa_attention",
    )(proj, proj, proj, proj, proj_f32, proj_f32, bias_tiles)


def hybrid_mixer(xbf, bsz, seq, w_in, rel_bias, ssm_params, d_skip, w_glu, w_branches):
    offs = [0]
    for width in IN_SPLITS:
        offs.append(offs[-1] + width)
    seg = lambda a: w_in[:, offs[a]:offs[a + 1]]
    w_bf = jnp.concatenate([seg(0), seg(3), seg(4)], axis=1).astype(BF16)
    pad = jnp.zeros((w_in.shape[0], LANES - IDX_DIM - IDX_HEADS), w_in.dtype)
    w_f32 = jnp.concatenate([seg(1), seg(2), seg(5), seg(6), pad], axis=1).astype(BF16)
    dsa_col = IN_SPLITS[0]
    iq_col = dsa_col + IN_SPLITS[3]
    ssm_col = IN_SPLITS[1]
    kw_col = ssm_col + IN_SPLITS[2]
    proj = matmul(xbf, w_bf, BF16).reshape(bsz, seq, -1)
    proj_f32 = matmul(xbf, w_f32, F32).reshape(bsz, seq, -1)

    y_sb = stick_breaking_attention(proj).reshape(bsz * seq, -1)

    outs, lses = [], []
    for g, (window, dil) in enumerate(DIL_PATTERNS):
        assert window // dil == ATT_BLOCK
        tiles = _dilated_bias_tiles(rel_bias[:, g * DIL_HEADS_PER_GROUP:(g + 1) * DIL_HEADS_PER_GROUP], dil, ATT_BLOCK)
        o, lse = dilated_group_attention(proj_f32, tiles, g, dil)
        outs.append(o)
        lses.append(lse)
    y_dil = dilated_merge(outs, lses).reshape(bsz * seq, -1)

    bb, a_bar, cc = _ssm_tables(*ssm_params)
    y_ssm = s5_glu(proj_f32[..., ssm_col:ssm_col + SSM_WIDTH], bb, a_bar, cc, d_skip, w_glu.astype(BF16))

    y_dsa = dsa_attention(proj, dsa_col, iq_col, proj_f32, kw_col, _dsa_bias_tiles(rel_bias[:, DIL_HEADS:], DSA_BLOCK))
    y_dsa = y_dsa.reshape(bsz * seq, -1)

    return gated_branch_merge(xbf, [y_sb, y_dil, y_ssm, y_dsa], [w.astype(BF16) for w in w_branches],
                              seg(7).astype(BF16))


def kernel(x, ln_g, ln_b, ffn1_w_up, ffn1_w_down, w_in, rel_bias, ssm_lam_re, ssm_lam_im, ssm_log_dt,
           ssm_b_re, ssm_b_im, ssm_c_re, ssm_c_im, ssm_d, ssm_w_glu, w_br_sb, w_br_dil, w_br_ssm, w_br_dsa,
           w_out, ffn2_w_up, ffn2_w_down):
    bsz, seq, d = x.shape
    xf = x.reshape(bsz * seq, d)
    xbf = xf
    up1, down1 = ffn1_w_up.astype(BF16), ffn1_w_down.astype(BF16)
    up2, down2 = ffn2_w_up.astype(BF16), ffn2_w_down.astype(BF16)
    out_w = w_out.astype(BF16)
    for l in range(DEPTH):
        h = ffn_up(xbf, up1, l)
        xf, xbf = matmul_residual_layernorm(h, down1, l, xf, ln_g[l, 0], ln_b[l, 0], MACARON)
        merged = hybrid_mixer(xbf, bsz, seq, w_in[l], rel_bias,
                              (ssm_lam_re[l], ssm_lam_im[l], ssm_log_dt[l], ssm_b_re[l], ssm_b_im[l],
                               ssm_c_re[l], ssm_c_im[l]), ssm_d[l], ssm_w_glu[l],
                              (w_br_sb[l], w_br_dil[l], w_br_ssm[l], w_br_dsa[l]))
        xf, xbf = matmul_residual_layernorm(merged, out_w, l, xf, ln_g[l, 1], ln_b[l, 1], 1.0)
        h = ffn_up(xbf, up2, l)
        xf, xbf = matmul_residual_layernorm(h, down2, l, xf, ln_g[l, 2], ln_b[l, 2], MACARON)
    return xf.reshape(bsz, seq, d)
```

```python
import functools
import math

import jax
import jax.numpy as jnp
from jax import lax
from jax.experimental import pallas as pl
from jax.experimental.pallas import tpu as pltpu

F32 = jnp.float32
BF16 = jnp.bfloat16

D_MODEL = 2048
DEPTH = 2
HEAD_DIM = 64
SB_HEADS = 8
DIL_PATTERNS = ((128, 1), (512, 4), (2048, 16))
DIL_HEADS_PER_GROUP = 4
DIL_HEADS = DIL_HEADS_PER_GROUP * len(DIL_PATTERNS)
SSM_WIDTH = 512
SSM_GROUP = 16
SSM_GROUPS = SSM_WIDTH // SSM_GROUP
SSM_STATE = 64
DSA_HEADS = 8
IDX_HEADS = 8
IDX_DIM = 64
DSA_TOPK = 256
N_BRANCH = 4
D_FF = 5632
REL_BUCKETS = 32
REL_MAX_DIST = 128
DN_ALPHA = (2.0 * DEPTH) ** 0.25
LN_EPS = 1e-5
NEG_INF = -1e30
MACARON = 0.5

IN_SPLITS = (3 * SB_HEADS * HEAD_DIM, 3 * DIL_HEADS * HEAD_DIM, SSM_WIDTH, 3 * DSA_HEADS * HEAD_DIM,
             IDX_HEADS * IDX_DIM, IDX_DIM, IDX_HEADS, N_BRANCH * D_MODEL)

LANES = 128
ATT_BLOCK = 128
DIL_MERGE_ROWS = 512
DSA_BLOCK = 256
SB_QUERY_BLOCK = 1024
SB_KEY_BLOCK = 256
SUBLANES = 8
SSM_STEPS = 32
SSM_STRIP = 512
VMEM_LIMIT = 56 * 1024 * 1024
INT_MIN = -2 ** 31
SOFTPLUS_CLAMP = 80.0

_NT = (((1,), (1,)), ((), ()))


def _dot(a, b):
    return jnp.dot(a, b, preferred_element_type=F32)


def _dot_nt(a, b):
    return lax.dot_general(a, b, _NT, preferred_element_type=F32)


def _params(*sem):
    return pltpu.CompilerParams(dimension_semantics=sem, vmem_limit_bytes=VMEM_LIMIT)


def _sigmoid(x):
    return 1.0 / (1.0 + jnp.exp(-x))


def _mm_kernel(x_ref, w_ref, o_ref):
    o_ref[...] = _dot(x_ref[...], w_ref[...]).astype(o_ref.dtype)


def matmul(x, w, out_dtype, tm=512):
    m, k = x.shape
    n = w.shape[1]
    assert m % tm == 0 and n % LANES == 0
    return pl.pallas_call(
        _mm_kernel,
        grid=(m // tm,),
        in_specs=[pl.BlockSpec((tm, k), lambda i: (i, 0)),
                  pl.BlockSpec((k, n), lambda i: (0, 0), pipeline_mode=pl.Buffered(1))],
        out_specs=pl.BlockSpec((tm, n), lambda i: (i, 0)),
        out_shape=jax.ShapeDtypeStruct((m, n), out_dtype),
        compiler_params=_params("parallel"),
        name="matmul",
    )(x, w)


def _ffn_up_kernel(x_ref, wa_ref, wb_ref, o_ref):
    x = x_ref[...].astype(BF16)
    a = _dot(x, wa_ref[...])
    b = _dot(x, wb_ref[...])
    o_ref[...] = (a * _sigmoid(a) * b).astype(o_ref.dtype)


def ffn_up(x, w_up, layer, tm=1024, tn=512):
    m, k = x.shape
    f = w_up.shape[2] // 2
    assert m % tm == 0 and f % tn == 0
    nb = f // tn
    return pl.pallas_call(
        _ffn_up_kernel,
        grid=(m // tm, nb),
        in_specs=[pl.BlockSpec((tm, k), lambda i, j: (i, 0)),
                  pl.BlockSpec((None, k, tn), lambda i, j: (layer, 0, j)),
                  pl.BlockSpec((None, k, tn), lambda i, j: (layer, 0, j + nb))],
        out_specs=pl.BlockSpec((tm, tn), lambda i, j: (i, j)),
        out_shape=jax.ShapeDtypeStruct((m, f), BF16),
        compiler_params=_params("parallel", "arbitrary"),
        name="ffn_up",
    )(x, w_up, w_up)


def _mm_res_ln_kernel(h_ref, w_ref, x_ref, g_ref, b_ref, o_ref, obf_ref, *, scale):
    y = DN_ALPHA * x_ref[...] + scale * _dot(h_ref[...], w_ref[...])
    mu = jnp.mean(y, axis=-1, keepdims=True)
    yc = y - mu
    var = jnp.mean(yc * yc, axis=-1, keepdims=True)
    out = yc * lax.rsqrt(var + LN_EPS) * g_ref[...] + b_ref[...]
    o_ref[...] = out
    obf_ref[...] = out.astype(BF16)


def matmul_residual_layernorm(h, w, layer, x, g, b, scale, tm=256):
    m, k = h.shape
    n = w.shape[2]
    assert m % tm == 0
    return pl.pallas_call(
        functools.partial(_mm_res_ln_kernel, scale=scale),
        grid=(m // tm,),
        in_specs=[pl.BlockSpec((tm, k), lambda i: (i, 0)),
                  pl.BlockSpec((None, k, n), lambda i: (layer, 0, 0), pipeline_mode=pl.Buffered(1)),
                  pl.BlockSpec((tm, n), lambda i: (i, 0)),
                  pl.BlockSpec((1, n), lambda i: (0, 0)),
                  pl.BlockSpec((1, n), lambda i: (0, 0))],
        out_specs=[pl.BlockSpec((tm, n), lambda i: (i, 0)),
                   pl.BlockSpec((tm, n), lambda i: (i, 0))],
        out_shape=[jax.ShapeDtypeStruct((m, n), F32), jax.ShapeDtypeStruct((m, n), BF16)],
        compiler_params=_params("parallel"),
        name="matmul_residual_layernorm",
    )(h, w, x, g.reshape(1, n), b.reshape(1, n))


def _gated_merge_kernel(x_ref, ysb_ref, ydil_ref, yssm_ref, ydsa_ref, wsb_ref, wdil_ref, wssm_ref, wdsa_ref,
                        g0_ref, g1_ref, g2_ref, g3_ref, o_ref):
    x = x_ref[...]
    acc = _sigmoid(_dot(x, g0_ref[...])) * _dot(ysb_ref[...], wsb_ref[...])
    acc += _sigmoid(_dot(x, g1_ref[...])) * _dot(ydil_ref[...], wdil_ref[...])
    acc += _sigmoid(_dot(x, g2_ref[...])) * _dot(yssm_ref[...], wssm_ref[...])
    acc += _sigmoid(_dot(x, g3_ref[...])) * _dot(ydsa_ref[...], wdsa_ref[...])
    o_ref[...] = acc.astype(o_ref.dtype)


def gated_branch_merge(x, ys, ws, w_gate, tm=1024, tn=512):
    m, k = x.shape
    n = ws[0].shape[1]
    assert m % tm == 0 and n % tn == 0
    nb = n // tn
    y_specs = [pl.BlockSpec((tm, y.shape[1]), lambda i, j: (i, 0)) for y in ys]
    w_specs = [pl.BlockSpec((w.shape[0], tn), lambda i, j: (0, j)) for w in ws]
    g_specs = [pl.BlockSpec((k, tn), functools.partial(lambda i, j, br: (0, j + br * nb), br=br))
               for br in range(N_BRANCH)]
    return pl.pallas_call(
        _gated_merge_kernel,
        grid=(m // tm, nb),
        in_specs=[pl.BlockSpec((tm, k), lambda i, j: (i, 0))] + y_specs + w_specs + g_specs,
        out_specs=pl.BlockSpec((tm, tn), lambda i, j: (i, j)),
        out_shape=jax.ShapeDtypeStruct((m, n), BF16),
        compiler_params=_params("parallel", "arbitrary"),
        name="gated_branch_merge",
    )(x, *ys, *ws, w_gate, w_gate, w_gate, w_gate)


def _is_power_of_two(x):
    return math.frexp(x)[0] == 0.5


def _sb_kernel(q_ref, k_ref, v_ref, o_ref, vT_ref, acc_ref, *, qblk, kblk, scale):
    i = pl.program_id(2)
    ratio = qblk // kblk
    fold_scale = _is_power_of_two(scale)

    @pl.when(i == 0)
    def _():
        for j in range(vT_ref.shape[0]):
            vT_ref[j] = v_ref[0, j * kblk:(j + 1) * kblk, :].T

    qT = q_ref[0].T
    if fold_scale:
        qT = qT * jnp.asarray(scale, qT.dtype)
    row = lax.broadcasted_iota(jnp.int32, (LANES, 1), 0)
    zero = jnp.zeros_like(qT)
    q_rhs = jnp.concatenate([jnp.where(row < HEAD_DIM, qT, zero), jnp.where(row >= HEAD_DIM, qT, zero)], axis=1)
    key_io = lax.broadcasted_iota(jnp.int32, (kblk, kblk), 0)
    qry_io = lax.broadcasted_iota(jnp.int32, (kblk, kblk), 1)
    later = jnp.where(qry_io > key_io, 1.0, 0.0).astype(BF16)
    acc_ref[...] = jnp.zeros_like(acc_ref)
    nsub = 2 * ratio

    def block(j, carry, diag_sub):
        start = pl.multiple_of(j * kblk, kblk)
        z_all = _dot(k_ref[0, pl.ds(start, kblk), :], q_rhs)
        vT = vT_ref[j]
        strict = key_io < qry_io
        stage = []
        for c in range(nsub):
            h, sub = divmod(c, ratio)
            if diag_sub is not None and sub < diag_sub:
                stage.append(None)
                continue
            masked = diag_sub is not None and sub == diag_sub
            z = z_all[:, c * kblk:(c + 1) * kblk]
            if not fold_scale:
                z = z * scale
            sp = jnp.maximum(jnp.log(1.0 + jnp.exp(jnp.minimum(z, SOFTPLUS_CLAMP))), z)
            log_1mb = jnp.where(strict, -sp, 0.0) if masked else -sp
            suf = _dot(later, log_1mb.astype(BF16))
            stage.append((z - sp, suf, jnp.sum(log_1mb, axis=0, keepdims=True), masked))
        new_carry = []
        for c in range(nsub):
            h, sub = divmod(c, ratio)
            if stage[c] is None:
                new_carry.append(carry[c])
                continue
            log_beta, suf, colsum, masked = stage[c]
            att = jnp.exp(log_beta + suf + carry[c])
            if masked:
                att = jnp.where(strict, att, 0.0)
            rows = slice(h * HEAD_DIM, (h + 1) * HEAD_DIM)
            cols = slice(sub * kblk, (sub + 1) * kblk)
            acc_ref[rows, cols] += _dot(vT[rows, :], att.astype(BF16))
            new_carry.append(carry[c] + colsum)
        return tuple(new_carry)

    carry = tuple(jnp.zeros((1, kblk), F32) for _ in range(nsub))
    for sub in reversed(range(ratio)):
        carry = block(i * ratio + sub, carry, sub)
    lax.fori_loop(0, i * ratio, lambda jj, c: block(i * ratio - 1 - jj, c, None), carry)
    o_ref[0] = acc_ref[...].T.astype(o_ref.dtype)


def stick_breaking_attention(qkv, col0=0, qblk=SB_QUERY_BLOCK, kblk=SB_KEY_BLOCK):
    bsz, seq, _ = qkv.shape
    width = SB_HEADS * HEAD_DIM
    npair = width // LANES
    qblk = min(qblk, seq)
    assert seq % qblk == 0 and qblk % kblk == 0 and col0 % LANES == 0
    qcol = col0 // LANES
    return pl.pallas_call(
        functools.partial(_sb_kernel, qblk=qblk, kblk=kblk, scale=HEAD_DIM ** -0.5),
        grid=(bsz, npair, seq // qblk),
        in_specs=[pl.BlockSpec((1, qblk, LANES), lambda b, p, i: (b, i, qcol + p)),
                  pl.BlockSpec((1, seq, LANES), lambda b, p, i: (b, 0, qcol + npair + p)),
                  pl.BlockSpec((1, seq, LANES), lambda b, p, i: (b, 0, qcol + 2 * npair + p))],
        out_specs=pl.BlockSpec((1, qblk, LANES), lambda b, p, i: (b, i, p)),
        out_shape=jax.ShapeDtypeStruct((bsz, seq, width), BF16),
        scratch_shapes=[pltpu.VMEM((seq // kblk, LANES, kblk), BF16),
                        pltpu.VMEM((LANES, qblk), F32)],
        compiler_params=_params("parallel", "parallel", "arbitrary"),
        name="stick_breaking_attention",
    )(qkv, qkv, qkv)


def _t5_bucket(dist):
    max_exact = REL_BUCKETS // 2
    d = jnp.maximum(dist, 1).astype(F32)
    large = max_exact + (jnp.log(d / max_exact) / math.log(REL_MAX_DIST / max_exact)
                         * (REL_BUCKETS - max_exact)).astype(jnp.int32)
    large = jnp.minimum(large, REL_BUCKETS - 1)
    return jnp.where(dist < max_exact, dist, large)


def _bias_of_distance(rel_bias, dist):
    one_hot = jax.nn.one_hot(_t5_bucket(dist), REL_BUCKETS, dtype=F32)
    return jnp.einsum("...b,bh->...h", one_hot, rel_bias.astype(F32), precision=lax.Precision.HIGHEST)


def _dilated_bias_tiles(rel_bias_group, dil, blk):
    a = jnp.arange(blk)[:, None]
    cc = jnp.arange(2 * blk)[None, :]
    step = blk + a - cc
    valid = (step >= 0) & (step <= blk)
    bias = _bias_of_distance(rel_bias_group, dil * jnp.clip(step, 0, blk))
    tile = jnp.where(valid[..., None], bias, NEG_INF)
    return jnp.transpose(tile, (2, 0, 1))


def _dsa_bias_tiles(rel_bias_dsa, blk):
    c = jnp.arange(blk)[:, None]
    a = jnp.arange(blk)[None, :]
    own = jnp.where((a - c >= 0)[..., None], _bias_of_distance(rel_bias_dsa, jnp.maximum(a - c, 0)), NEG_INF)
    prev = _bias_of_distance(rel_bias_dsa, blk + a - c)
    half = REL_BUCKETS // 2
    assert half + int(math.log((blk + 1) / half) / math.log(REL_MAX_DIST / half) * (REL_BUCKETS - half)) >= REL_BUCKETS - 1
    far = jnp.broadcast_to(rel_bias_dsa[REL_BUCKETS - 1].astype(F32), prev.shape)
    return jnp.transpose(jnp.stack([own, prev, far]), (3, 0, 1, 2))


def _dil_kernel(*refs, blk, dils, nbs, scale):
    ngroups = len(dils)
    nslab = 2 * ngroups
    q_refs, k_refs, v_refs = refs[:nslab], refs[nslab:2 * nslab], refs[2 * nslab:3 * nslab]
    bias_refs = refs[3 * nslab:3 * nslab + ngroups]
    y_ref, o_ref, lse_ref = refs[3 * nslab + ngroups:]
    step = pl.program_id(1)
    lane = lax.broadcasted_iota(jnp.int32, (1, LANES), 1)
    fold_scale = _is_power_of_two(scale)

    work = []
    for g in range(ngroups):
        dil, nb = dils[g], nbs[g]
        c, i = step // nb, step % nb

        def rows(block, dil=dil, c=c):
            start = c + dil * blk * block
            return pl.ds(start, blk, stride=dil) if dil > 1 else pl.ds(pl.multiple_of(start, blk), blk)

        cur, prev = rows(i), rows(jnp.maximum(i - 1, 0))
        for half in range(2):
            slab = 2 * g + half
            q = q_refs[slab][0, cur, :]
            if fold_scale:
                q = q * scale
            q = q.astype(BF16)
            kc, kp = k_refs[slab][0, cur, :].astype(BF16), k_refs[slab][0, prev, :].astype(BF16)
            zero = jnp.zeros_like(q)
            scores = []
            for e in range(2):
                head = (lane < HEAD_DIM) if e == 0 else (lane >= HEAD_DIM)
                qm = jnp.where(head, q, zero)
                scores.append((_dot_nt(qm, kp), _dot_nt(qm, kc)))
            work.append((g, i, cur, prev, half, scores))
    soft = []
    for g, i, cur, prev, half, scores in work:
        has_prev = i > 0
        parts = []
        for e in range(2):
            h = 2 * half + e
            s_prev, s_cur = scores[e]
            if not fold_scale:
                s_prev, s_cur = s_prev * scale, s_cur * scale
            s_prev = jnp.where(has_prev, s_prev + bias_refs[g][h, :, :blk], NEG_INF)
            s_cur = s_cur + bias_refs[g][h, :, blk:]
            m = jnp.maximum(jnp.max(s_prev, axis=-1, keepdims=True), jnp.max(s_cur, axis=-1, keepdims=True))
            p_prev = jnp.exp(s_prev - m)
            p_cur = jnp.exp(s_cur - m)
            denom = jnp.sum(p_prev, axis=-1, keepdims=True) + jnp.sum(p_cur, axis=-1, keepdims=True)
            parts.append((p_prev.astype(BF16), p_cur.astype(BF16), denom, m + jnp.log(denom)))
        soft.append(parts)
    for (g, i, cur, prev, half, scores), parts in zip(work, soft):
        v_ref = v_refs[2 * g + half]
        vc, vp = v_ref[0, cur, :].astype(BF16), v_ref[0, prev, :].astype(BF16)
        out = jnp.zeros((blk, LANES), F32)
        lse_b = jnp.zeros((blk, LANES), F32)
        for e in range(2):
            head = (lane < HEAD_DIM) if e == 0 else (lane >= HEAD_DIM)
            p_prev, p_cur, denom, lse = parts[e]
            o = (_dot(p_prev, vp) + _dot(p_cur, vc)) / denom
            out = jnp.where(head, o, out)
            lse_b = jnp.where(head, lse, lse_b)
        o_ref[g, half, cur, :] = out
        lse_ref[g, half, cur, :] = lse_b

    @pl.when(step == pl.num_programs(1) - 1)
    def _():
        def merge(r, _):
            rows_ = pl.ds(pl.multiple_of(r * DIL_MERGE_ROWS, DIL_MERGE_ROWS), DIL_MERGE_ROWS)
            for half in range(2):
                lses = [lse_ref[g, half, rows_, :] for g in range(ngroups)]
                top = functools.reduce(jnp.maximum, lses)
                ws = [jnp.exp(x - top) for x in lses]
                num = functools.reduce(lambda a, b: a + b, [ws[g] * o_ref[g, half, rows_, :] for g in range(ngroups)])
                den = functools.reduce(lambda a, b: a + b, ws)
                y_ref[0, rows_, half * LANES:(half + 1) * LANES] = (num / den).astype(y_ref.dtype)
            return 0

        lax.fori_loop(0, o_ref.shape[2] // DIL_MERGE_ROWS, merge, 0)


def dilated_attention(qkv, rel_bias_dil, blk=ATT_BLOCK):
    bsz, seq, _ = qkv.shape
    hpg = DIL_HEADS_PER_GROUP
    assert hpg * HEAD_DIM == 2 * LANES and seq % DIL_MERGE_ROWS == 0
    ngroups = len(DIL_PATTERNS)
    nslab = DIL_HEADS * HEAD_DIM // LANES
    dils = tuple(d for _, d in DIL_PATTERNS)
    nbs = tuple(seq // d // blk for d in dils)
    nsteps = dils[0] * nbs[0]
    assert all(w // d == blk for w, d in DIL_PATTERNS) and all(d * n == nsteps for d, n in zip(dils, nbs))
    tiles = [_dilated_bias_tiles(rel_bias_dil[:, g * hpg:(g + 1) * hpg], d, blk) for g, d in enumerate(dils)]
    once = pl.Buffered(1)
    slab = lambda col: pl.BlockSpec((1, seq, LANES), functools.partial(lambda b, s, col: (b, 0, col), col=col),
                                    pipeline_mode=once)
    in_specs = [slab(section * nslab + s) for section in range(3) for s in range(nslab)]
    in_specs += [pl.BlockSpec(t.shape, lambda b, s: (0, 0, 0), pipeline_mode=once) for t in tiles]
    return pl.pallas_call(
        functools.partial(_dil_kernel, blk=blk, dils=dils, nbs=nbs, scale=HEAD_DIM ** -0.5),
        grid=(bsz, nsteps),
        in_specs=in_specs,
        out_specs=pl.BlockSpec((1, seq, 2 * LANES), lambda b, s: (b, 0, 0)),
        out_shape=jax.ShapeDtypeStruct((bsz, seq, 2 * LANES), BF16),
        scratch_shapes=[pltpu.VMEM((ngroups, 2, seq, LANES), F32),
                        pltpu.VMEM((ngroups, 2, seq, LANES), F32)],
        compiler_params=_params("parallel", "arbitrary"),
        name="dilated_attention",
    )(*([qkv] * (3 * nslab)), *tiles)


def _ssm_kernel(u_ref, bb_ref, a_ref, cc_ref, d_ref, wg_ref, o_ref, x_ref, carry_ref, *, steps, strip):
    @pl.when(pl.program_id(1) == 0)
    def _():
        carry_ref[...] = jnp.zeros_like(carry_ref)

    u = u_ref[0]
    ub = u.astype(BF16)
    width = u.shape[1]
    half = width // 2
    nstates = x_ref.shape[1] // 2
    hs = nstates // 2
    for part in range(2):
        bu = _dot(ub[:, part * half:(part + 1) * half], bb_ref[part])
        x_ref[:, part * hs:(part + 1) * hs] = bu[:, :hs]
        x_ref[:, nstates + part * hs:nstates + (part + 1) * hs] = bu[:, hs:]
    for s in range(nstates // strip):
        re = slice(s * strip, (s + 1) * strip)
        im = slice(nstates + s * strip, nstates + (s + 1) * strip)
        ar = jnp.broadcast_to(a_ref[0:1, re], (SUBLANES, strip))
        ai = jnp.broadcast_to(a_ref[1:2, re], (SUBLANES, strip))
        xr = carry_ref[:, re]
        xi = carry_ref[:, im]
        for t in range(steps):
            rows = slice(t * SUBLANES, (t + 1) * SUBLANES)
            xr, xi = ar * xr - ai * xi + x_ref[rows, re], ar * xi + ai * xr + x_ref[rows, im]
            x_ref[rows, re] = xr
            x_ref[rows, im] = xi
        carry_ref[:, re] = xr
        carry_ref[:, im] = xi
    y_parts = []
    for part in range(2):
        xr = x_ref[:, part * hs:(part + 1) * hs].astype(BF16)
        xi = x_ref[:, nstates + part * hs:nstates + (part + 1) * hs].astype(BF16)
        y_parts.append(_dot(xr, cc_ref[0, part]) + _dot(xi, cc_ref[1, part]))
    y = jnp.concatenate(y_parts, axis=1) + d_ref[...] * u
    y = 0.5 * y * (1.0 + jnp.tanh(math.sqrt(2.0 / math.pi) * (y + 0.044715 * (y * y * y))))
    z = _dot(y.astype(BF16), wg_ref[...])
    o_ref[0] = (z[:, :width] * _sigmoid(z[:, width:])).astype(o_ref.dtype)


def _block_diag(blocks):
    g, r, c = blocks.shape
    eye = jnp.eye(g, dtype=blocks.dtype)
    return (eye[:, None, :, None] * blocks[:, :, None, :]).reshape(g * r, g * c)


def _ssm_tables(lam_re, lam_im, log_dt, b_re, b_im, c_re, c_im):
    lr, li = lam_re.astype(F32), lam_im.astype(F32)
    dt = jnp.exp(log_dt.astype(F32))[:, None]
    mag = jnp.exp(lr * dt)
    a_re, a_im = mag * jnp.cos(li * dt), mag * jnp.sin(li * dt)
    den = lr * lr + li * li
    f_re = ((a_re - 1.0) * lr + a_im * li) / den
    f_im = (a_im * lr - (a_re - 1.0) * li) / den
    br, bi = b_re.astype(F32), b_im.astype(F32)
    bb_re = _block_diag(jnp.transpose(f_re[..., None] * br - f_im[..., None] * bi, (0, 2, 1)))
    bb_im = _block_diag(jnp.transpose(f_re[..., None] * bi + f_im[..., None] * br, (0, 2, 1)))
    cc_re = _block_diag(jnp.transpose(c_re.astype(F32), (0, 2, 1)))
    cc_im = -_block_diag(jnp.transpose(c_im.astype(F32), (0, 2, 1)))
    nstates = SSM_GROUPS * SSM_STATE
    hw, hs = SSM_WIDTH // 2, nstates // 2
    bb = jnp.stack([jnp.concatenate([bb_re[p * hw:(p + 1) * hw, p * hs:(p + 1) * hs],
                                     bb_im[p * hw:(p + 1) * hw, p * hs:(p + 1) * hs]], axis=1) for p in range(2)])
    cc = jnp.stack([jnp.stack([m[p * hs:(p + 1) * hs, p * hw:(p + 1) * hw] for p in range(2)])
                    for m in (cc_re, cc_im)])
    a = jnp.stack([a_re.reshape(nstates), a_im.reshape(nstates)])
    return bb.astype(BF16), a, cc.astype(BF16)


def s5_glu(u, bb, a, cc, d_skip, w_glu, steps=SSM_STEPS, strip=SSM_STRIP):
    bsz, seq, width = u.shape
    nstates = a.shape[1]
    assert bsz % SUBLANES == 0 and seq % steps == 0 and nstates % strip == 0
    ngrp = bsz // SUBLANES
    rows = steps * SUBLANES
    ut = u.reshape(ngrp, SUBLANES, seq, width).transpose(0, 2, 1, 3).reshape(ngrp, seq * SUBLANES, width)
    const = lambda *shape: pl.BlockSpec(shape, lambda g, c: (0,) * len(shape))
    out = pl.pallas_call(
        functools.partial(_ssm_kernel, steps=steps, strip=strip),
        grid=(ngrp, seq // steps),
        in_specs=[pl.BlockSpec((1, rows, width), lambda g, c: (g, c, 0)),
                  const(*bb.shape), const(*a.shape), const(*cc.shape),
                  const(1, width), const(*w_glu.shape)],
        out_specs=pl.BlockSpec((1, rows, width), lambda g, c: (g, c, 0)),
        out_shape=jax.ShapeDtypeStruct((ngrp, seq * SUBLANES, width), BF16),
        scratch_shapes=[pltpu.VMEM((rows, 2 * nstates), F32),
                        pltpu.VMEM((SUBLANES, 2 * nstates), F32)],
        compiler_params=_params("parallel", "arbitrary"),
        name="s5_glu",
    )(ut, bb, a, cc, d_skip.reshape(1, width).astype(F32), w_glu)
    return out.reshape(ngrp, seq, SUBLANES, width).transpose(0, 2, 1, 3).reshape(bsz * seq, width)


def _dsa_kernel(q_ref, k_ref, v_ref, iq_ref, ik_ref, ikq_ref, bias_ref, o_ref,
                vT_ref, key_ref, madd_ref, acc_ref, *, blk, topk, scale):
    i = pl.program_id(1)
    nblk = i + 1
    key_io = lax.broadcasted_iota(jnp.int32, (blk, blk), 0)
    qry_io = lax.broadcasted_iota(jnp.int32, (blk, blk), 1)

    @pl.when(i == 0)
    def _():
        for j in range(vT_ref.shape[0]):
            vT_ref[j] = v_ref[0, j * blk:(j + 1) * blk, :].T

    iqT = iq_ref[0].T
    idx_rhs = jnp.concatenate([iqT[h * IDX_DIM:(h + 1) * IDX_DIM, :] for h in range(IDX_HEADS)], axis=1)
    iw = ikq_ref[0].T[IDX_DIM:IDX_DIM + IDX_HEADS, :]

    def score_block(j, _):
        start = pl.multiple_of(j * blk, blk)
        kk = ik_ref[0, pl.ds(start, blk), :][:, :IDX_DIM].astype(BF16)
        d = _dot(kk, idx_rhs)
        sc = jnp.zeros((blk, blk), F32)
        for h in range(IDX_HEADS):
            sc = sc + iw[h:h + 1, :] * jnp.maximum(d[:, h * blk:(h + 1) * blk], 0.0)
        sc = jnp.where((key_io + j * blk) <= (qry_io + i * blk), sc, NEG_INF)
        bits = pltpu.bitcast(sc, jnp.int32)
        key_ref[j] = bits ^ ((bits >> 31) & 0x7FFFFFFF)
        return 0

    lax.fori_loop(0, nblk, score_block, 0)

    def count(pred):
        def body(j, acc):
            hit = jnp.where(pred(key_ref[j]), 1.0, 0.0)
            return acc + jnp.sum(hit.reshape(blk // 8, 8, blk), axis=0)
        return jnp.sum(lax.fori_loop(0, nblk, body, jnp.zeros((8, blk), F32)), axis=0, keepdims=True)

    def bit_step(b, thr):
        cand = thr + lax.shift_left(jnp.int32(1), 31 - b)
        return jnp.where(count(lambda key: key >= cand) >= topk, cand, thr)

    thr = lax.fori_loop(0, 32, bit_step, jnp.full((1, blk), INT_MIN, jnp.int32))

    need = topk - count(lambda key: key > thr)
    upto = jnp.where(qry_io <= key_io, 1.0, 0.0).astype(BF16)

    def select_block(j, run):
        key = key_ref[j]
        tie = jnp.where(key == thr, 1.0, 0.0)
        rank = _dot(upto, tie.astype(BF16)) + run
        keep_tie = jnp.where(rank <= need, 0.0, NEG_INF)
        madd_ref[j] = jnp.where(key > thr, 0.0, jnp.where(key == thr, keep_tie, NEG_INF))
        return run + jnp.sum(tie, axis=0, keepdims=True)

    lax.fori_loop(0, nblk, select_block, jnp.zeros((1, blk), F32))

    fold_scale = _is_power_of_two(scale)
    qT = q_ref[0].T
    if fold_scale:
        qT = qT * jnp.asarray(scale, qT.dtype)
    row = lax.broadcasted_iota(jnp.int32, (LANES, 1), 0)
    q_rhs = []
    for g in range(DSA_HEADS // 2):
        pair = qT[g * LANES:(g + 1) * LANES, :]
        zero = jnp.zeros_like(pair)
        q_rhs.append(jnp.concatenate([jnp.where(row < HEAD_DIM, pair, zero),
                                      jnp.where(row >= HEAD_DIM, pair, zero)], axis=1))
    acc_ref[...] = jnp.zeros_like(acc_ref)

    def attend_block(j, carry):
        m_all, l_all = carry
        start = pl.multiple_of(j * blk, blk)
        kblk = k_ref[0, pl.ds(start, blk), :]
        vT = vT_ref[j]
        madd = madd_ref[j]
        which = jnp.minimum(i - j, 2)
        s2 = [_dot(kblk[:, g * LANES:(g + 1) * LANES], q_rhs[g]) for g in range(DSA_HEADS // 2)]
        m_out, l_out, probs, alphas = [], [], [], []
        for h in range(DSA_HEADS):
            s = s2[h // 2][:, (h % 2) * blk:(h % 2 + 1) * blk]
            if not fold_scale:
                s = s * scale
            s = s + bias_ref[h, which] + madd
            m_new = jnp.maximum(m_all[h], jnp.max(s, axis=0, keepdims=True))
            p = jnp.exp(s - m_new)
            alpha = jnp.exp(m_all[h] - m_new)
            l_out.append(alpha * l_all[h] + jnp.sum(p, axis=0, keepdims=True))
            m_out.append(m_new)
            probs.append(p.astype(BF16))
            alphas.append(alpha)
        for h in range(DSA_HEADS):
            rows = slice(h * HEAD_DIM, (h + 1) * HEAD_DIM)
            acc_ref[rows, :] = alphas[h] * acc_ref[rows, :] + _dot(vT[rows, :], probs[h])
        return tuple(m_out), tuple(l_out)

    init = (tuple(jnp.full((1, blk), NEG_INF, F32) for _ in range(DSA_HEADS)),
            tuple(jnp.zeros((1, blk), F32) for _ in range(DSA_HEADS)))
    _, l_all = lax.fori_loop(0, nblk, attend_block, init)
    for h in range(DSA_HEADS):
        rows = slice(h * HEAD_DIM, (h + 1) * HEAD_DIM)
        acc_ref[rows, :] = acc_ref[rows, :] / l_all[h]
    o_ref[0] = acc_ref[...].T.astype(o_ref.dtype)


def dsa_attention(proj, qkv_col, iq_col, proj_f32, kw_col, bias_tiles, blk=DSA_BLOCK):
    bsz, seq, _ = proj.shape
    width = DSA_HEADS * HEAD_DIM
    nblk = seq // blk
    topk = min(DSA_TOPK, seq // 4)
    iq_width = IDX_HEADS * IDX_DIM
    assert qkv_col % width == 0 and iq_col % iq_width == 0 and kw_col % LANES == 0
    qcol = qkv_col // width
    return pl.pallas_call(
        functools.partial(_dsa_kernel, blk=blk, topk=topk, scale=HEAD_DIM ** -0.5),
        grid=(bsz, nblk),
        in_specs=[pl.BlockSpec((1, blk, width), lambda b, i: (b, i, qcol)),
                  pl.BlockSpec((1, seq, width), lambda b, i: (b, 0, qcol + 1)),
                  pl.BlockSpec((1, seq, width), lambda b, i: (b, 0, qcol + 2)),
                  pl.BlockSpec((1, blk, iq_width), lambda b, i: (b, i, iq_col // iq_width)),
                  pl.BlockSpec((1, seq, LANES), lambda b, i: (b, 0, kw_col // LANES)),
                  pl.BlockSpec((1, blk, LANES), lambda b, i: (b, i, kw_col // LANES)),
                  pl.BlockSpec((DSA_HEADS, 3, blk, blk), lambda b, i: (0, 0, 0, 0))],
        out_specs=pl.BlockSpec((1, blk, width), lambda b, i: (b, i, 0)),
        out_shape=jax.ShapeDtypeStruct((bsz, seq, width), BF16),
        scratch_shapes=[pltpu.VMEM((nblk, width, blk), BF16),
                        pltpu.VMEM((nblk, blk, blk), jnp.int32),
                        pltpu.VMEM((nblk, blk, blk), F32),
                        pltpu.VMEM((width, blk), F32)],
        compiler_params=_params("parallel", "arbitrary"),
        name="dsa_attention",
    )(proj, proj, proj, proj, proj_f32, proj_f32, bias_tiles)


def hybrid_mixer(xbf, bsz, seq, w_in, rel_bias, ssm_params, d_skip, w_glu, w_branches):
    offs = [0]
    for width in IN_SPLITS:
        offs.append(offs[-1] + width)
    seg = lambda a: w_in[:, offs[a]:offs[a + 1]]
    w_bf = jnp.concatenate([seg(0), seg(3), seg(4)], axis=1).astype(BF16)
    pad = jnp.zeros((w_in.shape[0], LANES - IDX_DIM - IDX_HEADS), w_in.dtype)
    w_f32 = jnp.concatenate([seg(1), seg(2), seg(5), seg(6), pad], axis=1).astype(BF16)
    dsa_col = IN_SPLITS[0]
    iq_col = dsa_col + IN_SPLITS[3]
    ssm_col = IN_SPLITS[1]
    kw_col = ssm_col + IN_SPLITS[2]
    proj = matmul(xbf, w_bf, BF16).reshape(bsz, seq, -1)
    proj_f32 = matmul(xbf, w_f32, F32).reshape(bsz, seq, -1)

    y_sb = stick_breaking_attention(proj).reshape(bsz * seq, -1)

    y_dil = dilated_attention(proj_f32, rel_bias[:, :DIL_HEADS]).reshape(bsz * seq, -1)

    bb, a_bar, cc = _ssm_tables(*ssm_params)
    y_ssm = s5_glu(proj_f32[..., ssm_col:ssm_col + SSM_WIDTH], bb, a_bar, cc, d_skip, w_glu.astype(BF16))

    y_dsa = dsa_attention(proj, dsa_col, iq_col, proj_f32, kw_col, _dsa_bias_tiles(rel_bias[:, DIL_HEADS:], DSA_BLOCK))
    y_dsa = y_dsa.reshape(bsz * seq, -1)

    return gated_branch_merge(xbf, [y_sb, y_dil, y_ssm, y_dsa], [w.astype(BF16) for w in w_branches],
                              seg(7).astype(BF16))


def kernel(x, ln_g, ln_b, ffn1_w_up, ffn1_w_down, w_in, rel_bias, ssm_lam_re, ssm_lam_im, ssm_log_dt,
           ssm_b_re, ssm_b_im, ssm_c_re, ssm_c_im, ssm_d, ssm_w_glu, w_br_sb, w_br_dil, w_br_ssm, w_br_dsa,
           w_out, ffn2_w_up, ffn2_w_down):
    bsz, seq, d = x.shape
    xf = x.reshape(bsz * seq, d)
    xbf = xf
    up1, down1 = ffn1_w_up.astype(BF16), ffn1_w_down.astype(BF16)
    up2, down2 = ffn2_w_up.astype(BF16), ffn2_w_down.astype(BF16)
    out_w = w_out.astype(BF16)
    for l in range(DEPTH):
        h = ffn_up(xbf, up1, l)
        xf, xbf = matmul_residual_layernorm(h, down1, l, xf, ln_g[l, 0], ln_b[l, 0], MACARON)
        merged = hybrid_mixer(xbf, bsz, seq, w_in[l], rel_bias,
                              (ssm_lam_re[l], ssm_lam_im[l], ssm_log_dt[l], ssm_b_re[l], ssm_b_im[l],
                               ssm_c_re[l], ssm_c_im[l]), ssm_d[l], ssm_w_glu[l],
                              (w_br_sb[l], w_br_dil[l], w_br_ssm[l], w_br_dsa[l]))
        xf, xbf = matmul_residual_layernorm(merged, out_w, l, xf, ln_g[l, 1], ln_b[l, 1], 1.0)
        h = ffn_up(xbf, up2, l)
        xf, xbf = matmul_residual_layernorm(h, down2, l, xf, ln_g[l, 2], ln_b[l, 2], MACARON)
    return xf.reshape(bsz, seq, d)
```

```python
import functools
import math

import jax
import jax.numpy as jnp
from jax import lax
from jax.experimental import pallas as pl
from jax.experimental.pallas import tpu as pltpu

F32 = jnp.float32
BF16 = jnp.bfloat16

D_MODEL = 2048
DEPTH = 2
HEAD_DIM = 64
SB_HEADS = 8
DIL_PATTERNS = ((128, 1), (512, 4), (2048, 16))
DIL_HEADS_PER_GROUP = 4
DIL_HEADS = DIL_HEADS_PER_GROUP * len(DIL_PATTERNS)
SSM_WIDTH = 512
SSM_GROUP = 16
SSM_GROUPS = SSM_WIDTH // SSM_GROUP
SSM_STATE = 64
DSA_HEADS = 8
IDX_HEADS = 8
IDX_DIM = 64
DSA_TOPK = 256
N_BRANCH = 4
D_FF = 5632
REL_BUCKETS = 32
REL_MAX_DIST = 128
DN_ALPHA = (2.0 * DEPTH) ** 0.25
LN_EPS = 1e-5
NEG_INF = -1e30
MACARON = 0.5

IN_SPLITS = (3 * SB_HEADS * HEAD_DIM, 3 * DIL_HEADS * HEAD_DIM, SSM_WIDTH, 3 * DSA_HEADS * HEAD_DIM,
             IDX_HEADS * IDX_DIM, IDX_DIM, IDX_HEADS, N_BRANCH * D_MODEL)

LANES = 128
ATT_BLOCK = 128
DIL_UNITS = 2
DIL_MERGE_ROWS = 512
DSA_BLOCK = 256
SB_QUERY_BLOCK = 1024
SB_KEY_BLOCK = 256
SUBLANES = 8
SSM_STEPS = 32
SSM_STRIP = 512
VMEM_LIMIT = 56 * 1024 * 1024
INT_MIN = -2 ** 31
SOFTPLUS_CLAMP = 80.0

_NT = (((1,), (1,)), ((), ()))


def _dot(a, b):
    return jnp.dot(a, b, preferred_element_type=F32)


def _dot_nt(a, b):
    return lax.dot_general(a, b, _NT, preferred_element_type=F32)


def _params(*sem):
    return pltpu.CompilerParams(dimension_semantics=sem, vmem_limit_bytes=VMEM_LIMIT)


def _sigmoid(x):
    return 1.0 / (1.0 + jnp.exp(-x))


def _mm_kernel(x_ref, w_ref, o_ref):
    o_ref[...] = _dot(x_ref[...], w_ref[...]).astype(o_ref.dtype)


def matmul(x, w, out_dtype, tm=512):
    m, k = x.shape
    n = w.shape[1]
    assert m % tm == 0 and n % LANES == 0
    return pl.pallas_call(
        _mm_kernel,
        grid=(m // tm,),
        in_specs=[pl.BlockSpec((tm, k), lambda i: (i, 0)),
                  pl.BlockSpec((k, n), lambda i: (0, 0), pipeline_mode=pl.Buffered(1))],
        out_specs=pl.BlockSpec((tm, n), lambda i: (i, 0)),
        out_shape=jax.ShapeDtypeStruct((m, n), out_dtype),
        compiler_params=_params("parallel"),
        name="matmul",
    )(x, w)


def _ffn_up_kernel(x_ref, wa_ref, wb_ref, o_ref):
    x = x_ref[...].astype(BF16)
    a = _dot(x, wa_ref[...])
    b = _dot(x, wb_ref[...])
    o_ref[...] = (a * _sigmoid(a) * b).astype(o_ref.dtype)


def ffn_up(x, w_up, layer, tm=1024, tn=512):
    m, k = x.shape
    f = w_up.shape[2] // 2
    assert m % tm == 0 and f % tn == 0
    nb = f // tn
    return pl.pallas_call(
        _ffn_up_kernel,
        grid=(m // tm, nb),
        in_specs=[pl.BlockSpec((tm, k), lambda i, j: (i, 0)),
                  pl.BlockSpec((None, k, tn), lambda i, j: (layer, 0, j)),
                  pl.BlockSpec((None, k, tn), lambda i, j: (layer, 0, j + nb))],
        out_specs=pl.BlockSpec((tm, tn), lambda i, j: (i, j)),
        out_shape=jax.ShapeDtypeStruct((m, f), BF16),
        compiler_params=_params("parallel", "arbitrary"),
        name="ffn_up",
    )(x, w_up, w_up)


def _mm_res_ln_kernel(h_ref, w_ref, x_ref, g_ref, b_ref, o_ref, obf_ref, *, scale):
    y = DN_ALPHA * x_ref[...] + scale * _dot(h_ref[...], w_ref[...])
    mu = jnp.mean(y, axis=-1, keepdims=True)
    yc = y - mu
    var = jnp.mean(yc * yc, axis=-1, keepdims=True)
    out = yc * lax.rsqrt(var + LN_EPS) * g_ref[...] + b_ref[...]
    o_ref[...] = out
    obf_ref[...] = out.astype(BF16)


def matmul_residual_layernorm(h, w, layer, x, g, b, scale, tm=256):
    m, k = h.shape
    n = w.shape[2]
    assert m % tm == 0
    return pl.pallas_call(
        functools.partial(_mm_res_ln_kernel, scale=scale),
        grid=(m // tm,),
        in_specs=[pl.BlockSpec((tm, k), lambda i: (i, 0)),
                  pl.BlockSpec((None, k, n), lambda i: (layer, 0, 0), pipeline_mode=pl.Buffered(1)),
                  pl.BlockSpec((tm, n), lambda i: (i, 0)),
                  pl.BlockSpec((1, n), lambda i: (0, 0)),
                  pl.BlockSpec((1, n), lambda i: (0, 0))],
        out_specs=[pl.BlockSpec((tm, n), lambda i: (i, 0)),
                   pl.BlockSpec((tm, n), lambda i: (i, 0))],
        out_shape=[jax.ShapeDtypeStruct((m, n), F32), jax.ShapeDtypeStruct((m, n), BF16)],
        compiler_params=_params("parallel"),
        name="matmul_residual_layernorm",
    )(h, w, x, g.reshape(1, n), b.reshape(1, n))


def _gated_merge_kernel(x_ref, ysb_ref, ydil_ref, yssm_ref, ydsa_ref, wsb_ref, wdil_ref, wssm_ref, wdsa_ref,
                        g0_ref, g1_ref, g2_ref, g3_ref, o_ref):
    x = x_ref[...]
    acc = _sigmoid(_dot(x, g0_ref[...])) * _dot(ysb_ref[...], wsb_ref[...])
    acc += _sigmoid(_dot(x, g1_ref[...])) * _dot(ydil_ref[...], wdil_ref[...])
    acc += _sigmoid(_dot(x, g2_ref[...])) * _dot(yssm_ref[...], wssm_ref[...])
    acc += _sigmoid(_dot(x, g3_ref[...])) * _dot(ydsa_ref[...], wdsa_ref[...])
    o_ref[...] = acc.astype(o_ref.dtype)


def gated_branch_merge(x, ys, ws, w_gate, tm=1024, tn=512):
    m, k = x.shape
    n = ws[0].shape[1]
    assert m % tm == 0 and n % tn == 0
    nb = n // tn
    y_specs = [pl.BlockSpec((tm, y.shape[1]), lambda i, j: (i, 0)) for y in ys]
    w_specs = [pl.BlockSpec((w.shape[0], tn), lambda i, j: (0, j)) for w in ws]
    g_specs = [pl.BlockSpec((k, tn), functools.partial(lambda i, j, br: (0, j + br * nb), br=br))
               for br in range(N_BRANCH)]
    return pl.pallas_call(
        _gated_merge_kernel,
        grid=(m // tm, nb),
        in_specs=[pl.BlockSpec((tm, k), lambda i, j: (i, 0))] + y_specs + w_specs + g_specs,
        out_specs=pl.BlockSpec((tm, tn), lambda i, j: (i, j)),
        out_shape=jax.ShapeDtypeStruct((m, n), BF16),
        compiler_params=_params("parallel", "arbitrary"),
        name="gated_branch_merge",
    )(x, *ys, *ws, w_gate, w_gate, w_gate, w_gate)


def _is_power_of_two(x):
    return math.frexp(x)[0] == 0.5


def _sb_kernel(q_ref, k_ref, v_ref, o_ref, vT_ref, acc_ref, *, qblk, kblk, scale):
    i = pl.program_id(2)
    ratio = qblk // kblk
    fold_scale = _is_power_of_two(scale)

    @pl.when(i == 0)
    def _():
        for j in range(vT_ref.shape[0]):
            vT_ref[j] = v_ref[0, j * kblk:(j + 1) * kblk, :].T

    qT = q_ref[0].T
    if fold_scale:
        qT = qT * jnp.asarray(scale, qT.dtype)
    row = lax.broadcasted_iota(jnp.int32, (LANES, 1), 0)
    zero = jnp.zeros_like(qT)
    q_rhs = jnp.concatenate([jnp.where(row < HEAD_DIM, qT, zero), jnp.where(row >= HEAD_DIM, qT, zero)], axis=1)
    key_io = lax.broadcasted_iota(jnp.int32, (kblk, kblk), 0)
    qry_io = lax.broadcasted_iota(jnp.int32, (kblk, kblk), 1)
    later = jnp.where(qry_io > key_io, 1.0, 0.0).astype(BF16)
    acc_ref[...] = jnp.zeros_like(acc_ref)
    nsub = 2 * ratio

    def block(j, carry, diag_sub):
        start = pl.multiple_of(j * kblk, kblk)
        z_all = _dot(k_ref[0, pl.ds(start, kblk), :], q_rhs)
        vT = vT_ref[j]
        strict = key_io < qry_io
        stage = []
        for c in range(nsub):
            h, sub = divmod(c, ratio)
            if diag_sub is not None and sub < diag_sub:
                stage.append(None)
                continue
            masked = diag_sub is not None and sub == diag_sub
            z = z_all[:, c * kblk:(c + 1) * kblk]
            if not fold_scale:
                z = z * scale
            sp = jnp.maximum(jnp.log(1.0 + jnp.exp(jnp.minimum(z, SOFTPLUS_CLAMP))), z)
            log_1mb = jnp.where(strict, -sp, 0.0) if masked else -sp
            suf = _dot(later, log_1mb.astype(BF16))
            stage.append((z - sp, suf, jnp.sum(log_1mb, axis=0, keepdims=True), masked))
        new_carry = []
        for c in range(nsub):
            h, sub = divmod(c, ratio)
            if stage[c] is None:
                new_carry.append(carry[c])
                continue
            log_beta, suf, colsum, masked = stage[c]
            att = jnp.exp(log_beta + suf + carry[c])
            if masked:
                att = jnp.where(strict, att, 0.0)
            rows = slice(h * HEAD_DIM, (h + 1) * HEAD_DIM)
            cols = slice(sub * kblk, (sub + 1) * kblk)
            acc_ref[rows, cols] += _dot(vT[rows, :], att.astype(BF16))
            new_carry.append(carry[c] + colsum)
        return tuple(new_carry)

    carry = tuple(jnp.zeros((1, kblk), F32) for _ in range(nsub))
    for sub in reversed(range(ratio)):
        carry = block(i * ratio + sub, carry, sub)
    lax.fori_loop(0, i * ratio, lambda jj, c: block(i * ratio - 1 - jj, c, None), carry)
    o_ref[0] = acc_ref[...].T.astype(o_ref.dtype)


def stick_breaking_attention(qkv, col0=0, qblk=SB_QUERY_BLOCK, kblk=SB_KEY_BLOCK):
    bsz, seq, _ = qkv.shape
    width = SB_HEADS * HEAD_DIM
    npair = width // LANES
    qblk = min(qblk, seq)
    assert seq % qblk == 0 and qblk % kblk == 0 and col0 % LANES == 0
    qcol = col0 // LANES
    return pl.pallas_call(
        functools.partial(_sb_kernel, qblk=qblk, kblk=kblk, scale=HEAD_DIM ** -0.5),
        grid=(bsz, npair, seq // qblk),
        in_specs=[pl.BlockSpec((1, qblk, LANES), lambda b, p, i: (b, i, qcol + p)),
                  pl.BlockSpec((1, seq, LANES), lambda b, p, i: (b, 0, qcol + npair + p)),
                  pl.BlockSpec((1, seq, LANES), lambda b, p, i: (b, 0, qcol + 2 * npair + p))],
        out_specs=pl.BlockSpec((1, qblk, LANES), lambda b, p, i: (b, i, p)),
        out_shape=jax.ShapeDtypeStruct((bsz, seq, width), BF16),
        scratch_shapes=[pltpu.VMEM((seq // kblk, LANES, kblk), BF16),
                        pltpu.VMEM((LANES, qblk), F32)],
        compiler_params=_params("parallel", "parallel", "arbitrary"),
        name="stick_breaking_attention",
    )(qkv, qkv, qkv)


def _t5_bucket(dist):
    max_exact = REL_BUCKETS // 2
    d = jnp.maximum(dist, 1).astype(F32)
    large = max_exact + (jnp.log(d / max_exact) / math.log(REL_MAX_DIST / max_exact)
                         * (REL_BUCKETS - max_exact)).astype(jnp.int32)
    large = jnp.minimum(large, REL_BUCKETS - 1)
    return jnp.where(dist < max_exact, dist, large)


def _bias_of_distance(rel_bias, dist):
    one_hot = jax.nn.one_hot(_t5_bucket(dist), REL_BUCKETS, dtype=F32)
    return jnp.einsum("...b,bh->...h", one_hot, rel_bias.astype(F32), precision=lax.Precision.HIGHEST)


def _dilated_bias_tiles(rel_bias_group, dil, blk):
    a = jnp.arange(blk)[:, None]
    cc = jnp.arange(2 * blk)[None, :]
    step = blk + a - cc
    valid = (step >= 0) & (step <= blk)
    bias = _bias_of_distance(rel_bias_group, dil * jnp.clip(step, 0, blk))
    tile = jnp.where(valid[..., None], bias, NEG_INF)
    return jnp.transpose(tile, (2, 0, 1))


def _dsa_bias_tiles(rel_bias_dsa, blk):
    c = jnp.arange(blk)[:, None]
    a = jnp.arange(blk)[None, :]
    own = jnp.where((a - c >= 0)[..., None], _bias_of_distance(rel_bias_dsa, jnp.maximum(a - c, 0)), NEG_INF)
    prev = _bias_of_distance(rel_bias_dsa, blk + a - c)
    half = REL_BUCKETS // 2
    assert half + int(math.log((blk + 1) / half) / math.log(REL_MAX_DIST / half) * (REL_BUCKETS - half)) >= REL_BUCKETS - 1
    far = jnp.broadcast_to(rel_bias_dsa[REL_BUCKETS - 1].astype(F32), prev.shape)
    return jnp.transpose(jnp.stack([own, prev, far]), (3, 0, 1, 2))


def _dil_kernel(*refs, blk, dils, nbs, units, scale):
    ngroups = len(dils)
    nslab = 2 * ngroups
    q_refs, k_refs, v_refs = refs[:nslab], refs[nslab:2 * nslab], refs[2 * nslab:3 * nslab]
    bias_refs = refs[3 * nslab:3 * nslab + ngroups]
    y_ref, o_ref, lse_ref = refs[3 * nslab + ngroups:]
    step = pl.program_id(1)
    lane = lax.broadcasted_iota(jnp.int32, (1, LANES), 1)
    fold_scale = _is_power_of_two(scale)

    work = []
    for g, u in [(g, u) for g in range(ngroups) for u in range(units)]:
        dil, nb = dils[g], nbs[g]
        unit = step * units + u
        c, i = unit // nb, unit % nb

        def rows(block, dil=dil, c=c):
            start = c + dil * blk * block
            return pl.ds(start, blk, stride=dil) if dil > 1 else pl.ds(pl.multiple_of(start, blk), blk)

        cur, prev = rows(i), rows(jnp.maximum(i - 1, 0))
        for half in range(2):
            slab = 2 * g + half
            q = q_refs[slab][0, cur, :]
            if fold_scale:
                q = q * scale
            q = q.astype(BF16)
            kc, kp = k_refs[slab][0, cur, :].astype(BF16), k_refs[slab][0, prev, :].astype(BF16)
            zero = jnp.zeros_like(q)
            scores = []
            for e in range(2):
                head = (lane < HEAD_DIM) if e == 0 else (lane >= HEAD_DIM)
                qm = jnp.where(head, q, zero)
                scores.append((_dot_nt(qm, kp), _dot_nt(qm, kc)))
            work.append((g, i, cur, prev, half, scores))
    soft = []
    for g, i, cur, prev, half, scores in work:
        has_prev = i > 0
        parts = []
        for e in range(2):
            h = 2 * half + e
            s_prev, s_cur = scores[e]
            if not fold_scale:
                s_prev, s_cur = s_prev * scale, s_cur * scale
            s_prev = jnp.where(has_prev, s_prev + bias_refs[g][h, :, :blk], NEG_INF)
            s_cur = s_cur + bias_refs[g][h, :, blk:]
            m = jnp.maximum(jnp.max(s_prev, axis=-1, keepdims=True), jnp.max(s_cur, axis=-1, keepdims=True))
            p_prev = jnp.exp(s_prev - m)
            p_cur = jnp.exp(s_cur - m)
            denom = jnp.sum(p_prev, axis=-1, keepdims=True) + jnp.sum(p_cur, axis=-1, keepdims=True)
            parts.append((p_prev.astype(BF16), p_cur.astype(BF16), denom, m + jnp.log(denom)))
        soft.append(parts)
    for (g, i, cur, prev, half, scores), parts in zip(work, soft):
        v_ref = v_refs[2 * g + half]
        vc, vp = v_ref[0, cur, :].astype(BF16), v_ref[0, prev, :].astype(BF16)
        out = jnp.zeros((blk, LANES), F32)
        lse_b = jnp.zeros((blk, LANES), F32)
        for e in range(2):
            head = (lane < HEAD_DIM) if e == 0 else (lane >= HEAD_DIM)
            p_prev, p_cur, denom, lse = parts[e]
            o = (_dot(p_prev, vp) + _dot(p_cur, vc)) / denom
            out = jnp.where(head, o, out)
            lse_b = jnp.where(head, lse, lse_b)
        o_ref[g, half, cur, :] = out
        lse_ref[g, half, cur, :] = lse_b

    @pl.when(step == pl.num_programs(1) - 1)
    def _():
        def merge(r, _):
            rows_ = pl.ds(pl.multiple_of(r * DIL_MERGE_ROWS, DIL_MERGE_ROWS), DIL_MERGE_ROWS)
            for half in range(2):
                lses = [lse_ref[g, half, rows_, :] for g in range(ngroups)]
                top = functools.reduce(jnp.maximum, lses)
                ws = [jnp.exp(x - top) for x in lses]
                num = functools.reduce(lambda a, b: a + b, [ws[g] * o_ref[g, half, rows_, :] for g in range(ngroups)])
                den = functools.reduce(lambda a, b: a + b, ws)
                y_ref[0, rows_, half * LANES:(half + 1) * LANES] = (num / den).astype(y_ref.dtype)
            return 0

        lax.fori_loop(0, o_ref.shape[2] // DIL_MERGE_ROWS, merge, 0)


def dilated_attention(qkv, rel_bias_dil, blk=ATT_BLOCK, units=DIL_UNITS):
    bsz, seq, _ = qkv.shape
    hpg = DIL_HEADS_PER_GROUP
    assert hpg * HEAD_DIM == 2 * LANES and seq % DIL_MERGE_ROWS == 0
    ngroups = len(DIL_PATTERNS)
    nslab = DIL_HEADS * HEAD_DIM // LANES
    dils = tuple(d for _, d in DIL_PATTERNS)
    nbs = tuple(seq // d // blk for d in dils)
    nsteps = dils[0] * nbs[0]
    assert all(w // d == blk for w, d in DIL_PATTERNS) and all(d * n == nsteps for d, n in zip(dils, nbs))
    assert nsteps % units == 0
    tiles = [_dilated_bias_tiles(rel_bias_dil[:, g * hpg:(g + 1) * hpg], d, blk) for g, d in enumerate(dils)]
    once = pl.Buffered(1)
    slab = lambda col: pl.BlockSpec((1, seq, LANES), functools.partial(lambda b, s, col: (b, 0, col), col=col),
                                    pipeline_mode=once)
    in_specs = [slab(section * nslab + s) for section in range(3) for s in range(nslab)]
    in_specs += [pl.BlockSpec(t.shape, lambda b, s: (0, 0, 0), pipeline_mode=once) for t in tiles]
    return pl.pallas_call(
        functools.partial(_dil_kernel, blk=blk, dils=dils, nbs=nbs, units=units, scale=HEAD_DIM ** -0.5),
        grid=(bsz, nsteps // units),
        in_specs=in_specs,
        out_specs=pl.BlockSpec((1, seq, 2 * LANES), lambda b, s: (b, 0, 0)),
        out_shape=jax.ShapeDtypeStruct((bsz, seq, 2 * LANES), BF16),
        scratch_shapes=[pltpu.VMEM((ngroups, 2, seq, LANES), F32),
                        pltpu.VMEM((ngroups, 2, seq, LANES), F32)],
        compiler_params=_params("parallel", "arbitrary"),
        name="dilated_attention",
    )(*([qkv] * (3 * nslab)), *tiles)


def _ssm_kernel(u_ref, bb_ref, a_ref, cc_ref, d_ref, wg_ref, o_ref, x_ref, carry_ref, *, steps, strip):
    @pl.when(pl.program_id(1) == 0)
    def _():
        carry_ref[...] = jnp.zeros_like(carry_ref)

    u = u_ref[0]
    ub = u.astype(BF16)
    width = u.shape[1]
    half = width // 2
    nstates = x_ref.shape[1] // 2
    hs = nstates // 2
    for part in range(2):
        bu = _dot(ub[:, part * half:(part + 1) * half], bb_ref[part])
        x_ref[:, part * hs:(part + 1) * hs] = bu[:, :hs]
        x_ref[:, nstates + part * hs:nstates + (part + 1) * hs] = bu[:, hs:]
    for s in range(nstates // strip):
        re = slice(s * strip, (s + 1) * strip)
        im = slice(nstates + s * strip, nstates + (s + 1) * strip)
        ar = jnp.broadcast_to(a_ref[0:1, re], (SUBLANES, strip))
        ai = jnp.broadcast_to(a_ref[1:2, re], (SUBLANES, strip))
        xr = carry_ref[:, re]
        xi = carry_ref[:, im]
        for t in range(steps):
            rows = slice(t * SUBLANES, (t + 1) * SUBLANES)
            xr, xi = ar * xr - ai * xi + x_ref[rows, re], ar * xi + ai * xr + x_ref[rows, im]
            x_ref[rows, re] = xr
            x_ref[rows, im] = xi
        carry_ref[:, re] = xr
        carry_ref[:, im] = xi
    y_parts = []
    for part in range(2):
        xr = x_ref[:, part * hs:(part + 1) * hs].astype(BF16)
        xi = x_ref[:, nstates + part * hs:nstates + (part + 1) * hs].astype(BF16)
        y_parts.append(_dot(xr, cc_ref[0, part]) + _dot(xi, cc_ref[1, part]))
    y = jnp.concatenate(y_parts, axis=1) + d_ref[...] * u
    y = 0.5 * y * (1.0 + jnp.tanh(math.sqrt(2.0 / math.pi) * (y + 0.044715 * (y * y * y))))
    z = _dot(y.astype(BF16), wg_ref[...])
    o_ref[0] = (z[:, :width] * _sigmoid(z[:, width:])).astype(o_ref.dtype)


def _block_diag(blocks):
    g, r, c = blocks.shape
    eye = jnp.eye(g, dtype=blocks.dtype)
    return (eye[:, None, :, None] * blocks[:, :, None, :]).reshape(g * r, g * c)


def _ssm_tables(lam_re, lam_im, log_dt, b_re, b_im, c_re, c_im):
    lr, li = lam_re.astype(F32), lam_im.astype(F32)
    dt = jnp.exp(log_dt.astype(F32))[:, None]
    mag = jnp.exp(lr * dt)
    a_re, a_im = mag * jnp.cos(li * dt), mag * jnp.sin(li * dt)
    den = lr * lr + li * li
    f_re = ((a_re - 1.0) * lr + a_im * li) / den
    f_im = (a_im * lr - (a_re - 1.0) * li) / den
    br, bi = b_re.astype(F32), b_im.astype(F32)
    bb_re = _block_diag(jnp.transpose(f_re[..., None] * br - f_im[..., None] * bi, (0, 2, 1)))
    bb_im = _block_diag(jnp.transpose(f_re[..., None] * bi + f_im[..., None] * br, (0, 2, 1)))
    cc_re = _block_diag(jnp.transpose(c_re.astype(F32), (0, 2, 1)))
    cc_im = -_block_diag(jnp.transpose(c_im.astype(F32), (0, 2, 1)))
    nstates = SSM_GROUPS * SSM_STATE
    hw, hs = SSM_WIDTH // 2, nstates // 2
    bb = jnp.stack([jnp.concatenate([bb_re[p * hw:(p + 1) * hw, p * hs:(p + 1) * hs],
                                     bb_im[p * hw:(p + 1) * hw, p * hs:(p + 1) * hs]], axis=1) for p in range(2)])
    cc = jnp.stack([jnp.stack([m[p * hs:(p + 1) * hs, p * hw:(p + 1) * hw] for p in range(2)])
                    for m in (cc_re, cc_im)])
    a = jnp.stack([a_re.reshape(nstates), a_im.reshape(nstates)])
    return bb.astype(BF16), a, cc.astype(BF16)


def s5_glu(u, bb, a, cc, d_skip, w_glu, steps=SSM_STEPS, strip=SSM_STRIP):
    bsz, seq, width = u.shape
    nstates = a.shape[1]
    assert bsz % SUBLANES == 0 and seq % steps == 0 and nstates % strip == 0
    ngrp = bsz // SUBLANES
    rows = steps * SUBLANES
    ut = u.reshape(ngrp, SUBLANES, seq, width).transpose(0, 2, 1, 3).reshape(ngrp, seq * SUBLANES, width)
    const = lambda *shape: pl.BlockSpec(shape, lambda g, c: (0,) * len(shape))
    out = pl.pallas_call(
        functools.partial(_ssm_kernel, steps=steps, strip=strip),
        grid=(ngrp, seq // steps),
        in_specs=[pl.BlockSpec((1, rows, width), lambda g, c: (g, c, 0)),
                  const(*bb.shape), const(*a.shape), const(*cc.shape),
                  const(1, width), const(*w_glu.shape)],
        out_specs=pl.BlockSpec((1, rows, width), lambda g, c: (g, c, 0)),
        out_shape=jax.ShapeDtypeStruct((ngrp, seq * SUBLANES, width), BF16),
        scratch_shapes=[pltpu.VMEM((rows, 2 * nstates), F32),
                        pltpu.VMEM((SUBLANES, 2 * nstates), F32)],
        compiler_params=_params("parallel", "arbitrary"),
        name="s5_glu",
    )(ut, bb, a, cc, d_skip.reshape(1, width).astype(F32), w_glu)
    return out.reshape(ngrp, seq, SUBLANES, width).transpose(0, 2, 1, 3).reshape(bsz * seq, width)


def _dsa_kernel(q_ref, k_ref, v_ref, iq_ref, ik_ref, ikq_ref, bias_ref, o_ref,
                vT_ref, key_ref, madd_ref, acc_ref, *, blk, topk, scale):
    i = pl.program_id(1)
    nblk = i + 1
    key_io = lax.broadcasted_iota(jnp.int32, (blk, blk), 0)
    qry_io = lax.broadcasted_iota(jnp.int32, (blk, blk), 1)

    @pl.when(i == 0)
    def _():
        for j in range(vT_ref.shape[0]):
            vT_ref[j] = v_ref[0, j * blk:(j + 1) * blk, :].T

    iqT = iq_ref[0].T
    idx_rhs = jnp.concatenate([iqT[h * IDX_DIM:(h + 1) * IDX_DIM, :] for h in range(IDX_HEADS)], axis=1)
    iw = ikq_ref[0].T[IDX_DIM:IDX_DIM + IDX_HEADS, :]

    def score_block(j, _):
        start = pl.multiple_of(j * blk, blk)
        kk = ik_ref[0, pl.ds(start, blk), :][:, :IDX_DIM].astype(BF16)
        d = _dot(kk, idx_rhs)
        sc = jnp.zeros((blk, blk), F32)
        for h in range(IDX_HEADS):
            sc = sc + iw[h:h + 1, :] * jnp.maximum(d[:, h * blk:(h + 1) * blk], 0.0)
        sc = jnp.where((key_io + j * blk) <= (qry_io + i * blk), sc, NEG_INF)
        bits = pltpu.bitcast(sc, jnp.int32)
        key_ref[j] = bits ^ ((bits >> 31) & 0x7FFFFFFF)
        return 0

    lax.fori_loop(0, nblk, score_block, 0)

    @pl.when(nblk % 2 == 1)
    def _():
        key_ref[nblk] = jnp.full((blk, blk), INT_MIN, jnp.int32)

    def count(pred):
        def body(p, acc):
            for j in (2 * p, 2 * p + 1):
                hit = jnp.where(pred(key_ref[j]), 1.0, 0.0)
                acc = acc + jnp.sum(hit.reshape(blk // 8, 8, blk), axis=0)
            return acc
        return jnp.sum(lax.fori_loop(0, (nblk + 1) // 2, body, jnp.zeros((8, blk), F32)), axis=0, keepdims=True)

    def bit_step(b, thr):
        cand = thr + lax.shift_left(jnp.int32(1), 31 - b)
        return jnp.where(count(lambda key: key >= cand) >= topk, cand, thr)

    thr = lax.fori_loop(0, 32, bit_step, jnp.full((1, blk), INT_MIN, jnp.int32))

    need = topk - count(lambda key: key > thr)
    upto = jnp.where(qry_io <= key_io, 1.0, 0.0).astype(BF16)

    def select_block(j, run):
        key = key_ref[j]
        tie = jnp.where(key == thr, 1.0, 0.0)
        rank = _dot(upto, tie.astype(BF16)) + run
        keep_tie = jnp.where(rank <= need, 0.0, NEG_INF)
        madd_ref[j] = jnp.where(key > thr, 0.0, jnp.where(key == thr, keep_tie, NEG_INF))
        return run + jnp.sum(tie, axis=0, keepdims=True)

    lax.fori_loop(0, nblk, select_block, jnp.zeros((1, blk), F32))

    fold_scale = _is_power_of_two(scale)
    qT = q_ref[0].T
    if fold_scale:
        qT = qT * jnp.asarray(scale, qT.dtype)
    row = lax.broadcasted_iota(jnp.int32, (LANES, 1), 0)
    q_rhs = []
    for g in range(DSA_HEADS // 2):
        pair = qT[g * LANES:(g + 1) * LANES, :]
        zero = jnp.zeros_like(pair)
        q_rhs.append(jnp.concatenate([jnp.where(row < HEAD_DIM, pair, zero),
                                      jnp.where(row >= HEAD_DIM, pair, zero)], axis=1))
    acc_ref[...] = jnp.zeros_like(acc_ref)

    def attend_block(j, carry):
        m_all, l_all = carry
        start = pl.multiple_of(j * blk, blk)
        kblk = k_ref[0, pl.ds(start, blk), :]
        vT = vT_ref[j]
        madd = madd_ref[j]
        which = jnp.minimum(i - j, 2)
        s2 = [_dot(kblk[:, g * LANES:(g + 1) * LANES], q_rhs[g]) for g in range(DSA_HEADS // 2)]
        m_out, l_out, probs, alphas = [], [], [], []
        for h in range(DSA_HEADS):
            s = s2[h // 2][:, (h % 2) * blk:(h % 2 + 1) * blk]
            if not fold_scale:
                s = s * scale
            s = s + bias_ref[h, which] + madd
            m_new = jnp.maximum(m_all[h], jnp.max(s, axis=0, keepdims=True))
            p = jnp.exp(s - m_new)
            alpha = jnp.exp(m_all[h] - m_new)
            l_out.append(alpha * l_all[h] + jnp.sum(p, axis=0, keepdims=True))
            m_out.append(m_new)
            probs.append(p.astype(BF16))
            alphas.append(alpha)
        for h in range(DSA_HEADS):
            rows = slice(h * HEAD_DIM, (h + 1) * HEAD_DIM)
            acc_ref[rows, :] = alphas[h] * acc_ref[rows, :] + _dot(vT[rows, :], probs[h])
        return tuple(m_out), tuple(l_out)

    init = (tuple(jnp.full((1, blk), NEG_INF, F32) for _ in range(DSA_HEADS)),
            tuple(jnp.zeros((1, blk), F32) for _ in range(DSA_HEADS)))
    _, l_all = lax.fori_loop(0, nblk, attend_block, init)
    for h in range(DSA_HEADS):
        rows = slice(h * HEAD_DIM, (h + 1) * HEAD_DIM)
        acc_ref[rows, :] = acc_ref[rows, :] / l_all[h]
    o_ref[0] = acc_ref[...].T.astype(o_ref.dtype)


def dsa_attention(proj, qkv_col, iq_col, proj_f32, kw_col, bias_tiles, blk=DSA_BLOCK):
    bsz, seq, _ = proj.shape
    width = DSA_HEADS * HEAD_DIM
    nblk = seq // blk
    topk = min(DSA_TOPK, seq // 4)
    iq_width = IDX_HEADS * IDX_DIM
    assert qkv_col % width == 0 and iq_col % iq_width == 0 and kw_col % LANES == 0
    assert nblk % 2 == 0 and topk <= blk
    qcol = qkv_col // width
    return pl.pallas_call(
        functools.partial(_dsa_kernel, blk=blk, topk=topk, scale=HEAD_DIM ** -0.5),
        grid=(bsz, nblk),
        in_specs=[pl.BlockSpec((1, blk, width), lambda b, i: (b, i, qcol)),
                  pl.BlockSpec((1, seq, width), lambda b, i: (b, 0, qcol + 1)),
                  pl.BlockSpec((1, seq, width), lambda b, i: (b, 0, qcol + 2)),
                  pl.BlockSpec((1, blk, iq_width), lambda b, i: (b, i, iq_col // iq_width)),
                  pl.BlockSpec((1, seq, LANES), lambda b, i: (b, 0, kw_col // LANES)),
                  pl.BlockSpec((1, blk, LANES), lambda b, i: (b, i, kw_col // LANES)),
                  pl.BlockSpec((DSA_HEADS, 3, blk, blk), lambda b, i: (0, 0, 0, 0))],
        out_specs=pl.BlockSpec((1, blk, width), lambda b, i: (b, i, 0)),
        out_shape=jax.ShapeDtypeStruct((bsz, seq, width), BF16),
        scratch_shapes=[pltpu.VMEM((nblk, width, blk), BF16),
                        pltpu.VMEM((nblk, blk, blk), jnp.int32),
                        pltpu.VMEM((nblk, blk, blk), F32),
                        pltpu.VMEM((width, blk), F32)],
        compiler_params=_params("parallel", "arbitrary"),
        name="dsa_attention",
    )(proj, proj, proj, proj, proj_f32, proj_f32, bias_tiles)


def hybrid_mixer(xbf, bsz, seq, w_in, rel_bias, ssm_params, d_skip, w_glu, w_branches):
    offs = [0]
    for width in IN_SPLITS:
        offs.append(offs[-1] + width)
    seg = lambda a: w_in[:, offs[a]:offs[a + 1]]
    w_bf = jnp.concatenate([seg(0), seg(3), seg(4)], axis=1).astype(BF16)
    pad = jnp.zeros((w_in.shape[0], LANES - IDX_DIM - IDX_HEADS), w_in.dtype)
    w_f32 = jnp.concatenate([seg(1), seg(2), seg(5), seg(6), pad], axis=1).astype(BF16)
    dsa_col = IN_SPLITS[0]
    iq_col = dsa_col + IN_SPLITS[3]
    ssm_col = IN_SPLITS[1]
    kw_col = ssm_col + IN_SPLITS[2]
    proj = matmul(xbf, w_bf, BF16).reshape(bsz, seq, -1)
    proj_f32 = matmul(xbf, w_f32, F32).reshape(bsz, seq, -1)

    y_sb = stick_breaking_attention(proj).reshape(bsz * seq, -1)

    y_dil = dilated_attention(proj_f32, rel_bias[:, :DIL_HEADS]).reshape(bsz * seq, -1)

    bb, a_bar, cc = _ssm_tables(*ssm_params)
    y_ssm = s5_glu(proj_f32[..., ssm_col:ssm_col + SSM_WIDTH], bb, a_bar, cc, d_skip, w_glu.astype(BF16))

    y_dsa = dsa_attention(proj, dsa_col, iq_col, proj_f32, kw_col, _dsa_bias_tiles(rel_bias[:, DIL_HEADS:], DSA_BLOCK))
    y_dsa = y_dsa.reshape(bsz * seq, -1)

    return gated_branch_merge(xbf, [y_sb, y_dil, y_ssm, y_dsa], [w.astype(BF16) for w in w_branches],
                              seg(7).astype(BF16))


def kernel(x, ln_g, ln_b, ffn1_w_up, ffn1_w_down, w_in, rel_bias, ssm_lam_re, ssm_lam_im, ssm_log_dt,
           ssm_b_re, ssm_b_im, ssm_c_re, ssm_c_im, ssm_d, ssm_w_glu, w_br_sb, w_br_dil, w_br_ssm, w_br_dsa,
           w_out, ffn2_w_up, ffn2_w_down):
    bsz, seq, d = x.shape
    xf = x.reshape(bsz * seq, d)
    xbf = xf
    up1, down1 = ffn1_w_up.astype(BF16), ffn1_w_down.astype(BF16)
    up2, down2 = ffn2_w_up.astype(BF16), ffn2_w_down.astype(BF16)
    out_w = w_out.astype(BF16)
    for l in range(DEPTH):
        h = ffn_up(xbf, up1, l)
        xf, xbf = matmul_residual_layernorm(h, down1, l, xf, ln_g[l, 0], ln_b[l, 0], MACARON)
        merged = hybrid_mixer(xbf, bsz, seq, w_in[l], rel_bias,
                              (ssm_lam_re[l], ssm_lam_im[l], ssm_log_dt[l], ssm_b_re[l], ssm_b_im[l],
                               ssm_c_re[l], ssm_c_im[l]), ssm_d[l], ssm_w_glu[l],
                              (w_br_sb[l], w_br_dil[l], w_br_ssm[l], w_br_dsa[l]))
        xf, xbf = matmul_residual_layernorm(merged, out_w, l, xf, ln_g[l, 1], ln_b[l, 1], 1.0)
        h = ffn_up(xbf, up2, l)
        xf, xbf = matmul_residual_layernorm(h, down2, l, xf, ln_g[l, 2], ln_b[l, 2], MACARON)
    return xf.reshape(bsz, seq, d)
```

```python
import functools
import math

import jax
import jax.numpy as jnp
from jax import lax
from jax.experimental import pallas as pl
from jax.experimental.pallas import tpu as pltpu

F32 = jnp.float32
BF16 = jnp.bfloat16

D_MODEL = 2048
DEPTH = 2
HEAD_DIM = 64
SB_HEADS = 8
DIL_PATTERNS = ((128, 1), (512, 4), (2048, 16))
DIL_HEADS_PER_GROUP = 4
DIL_HEADS = DIL_HEADS_PER_GROUP * len(DIL_PATTERNS)
SSM_WIDTH = 512
SSM_GROUP = 16
SSM_GROUPS = SSM_WIDTH // SSM_GROUP
SSM_STATE = 64
DSA_HEADS = 8
IDX_HEADS = 8
IDX_DIM = 64
DSA_TOPK = 256
N_BRANCH = 4
D_FF = 5632
REL_BUCKETS = 32
REL_MAX_DIST = 128
DN_ALPHA = (2.0 * DEPTH) ** 0.25
LN_EPS = 1e-5
NEG_INF = -1e30
MACARON = 0.5

IN_SPLITS = (3 * SB_HEADS * HEAD_DIM, 3 * DIL_HEADS * HEAD_DIM, SSM_WIDTH, 3 * DSA_HEADS * HEAD_DIM,
             IDX_HEADS * IDX_DIM, IDX_DIM, IDX_HEADS, N_BRANCH * D_MODEL)

LANES = 128
ATT_BLOCK = 128
DIL_UNITS = 1
DIL_MERGE_ROWS = 512
DSA_BLOCK = 256
DSA_ONES_ROWS = 16
SB_QUERY_BLOCK = 1024
SB_KEY_BLOCK = 256
SUBLANES = 8
SSM_STEPS = 32
SSM_STRIP = 512
VMEM_LIMIT = 56 * 1024 * 1024
INT_MIN = -2 ** 31
SOFTPLUS_CLAMP = 80.0

_NT = (((1,), (1,)), ((), ()))


def _dot(a, b):
    return jnp.dot(a, b, preferred_element_type=F32)


def _dot_nt(a, b):
    return lax.dot_general(a, b, _NT, preferred_element_type=F32)


def _params(*sem):
    return pltpu.CompilerParams(dimension_semantics=sem, vmem_limit_bytes=VMEM_LIMIT)


def _sigmoid(x):
    return 1.0 / (1.0 + jnp.exp(-x))


def _mm_kernel(x_ref, w_ref, o_ref):
    o_ref[...] = _dot(x_ref[...], w_ref[...]).astype(o_ref.dtype)


def matmul(x, w, out_dtype, tm=512):
    m, k = x.shape
    n = w.shape[1]
    assert m % tm == 0 and n % LANES == 0
    return pl.pallas_call(
        _mm_kernel,
        grid=(m // tm,),
        in_specs=[pl.BlockSpec((tm, k), lambda i: (i, 0)),
                  pl.BlockSpec((k, n), lambda i: (0, 0), pipeline_mode=pl.Buffered(1))],
        out_specs=pl.BlockSpec((tm, n), lambda i: (i, 0)),
        out_shape=jax.ShapeDtypeStruct((m, n), out_dtype),
        compiler_params=_params("parallel"),
        name="matmul",
    )(x, w)


def _ffn_up_kernel(x_ref, wa_ref, wb_ref, o_ref):
    x = x_ref[...].astype(BF16)
    a = _dot(x, wa_ref[...])
    b = _dot(x, wb_ref[...])
    o_ref[...] = (a * _sigmoid(a) * b).astype(o_ref.dtype)


def ffn_up(x, w_up, layer, tm=1024, tn=512):
    m, k = x.shape
    f = w_up.shape[2] // 2
    assert m % tm == 0 and f % tn == 0
    nb = f // tn
    return pl.pallas_call(
        _ffn_up_kernel,
        grid=(m // tm, nb),
        in_specs=[pl.BlockSpec((tm, k), lambda i, j: (i, 0)),
                  pl.BlockSpec((None, k, tn), lambda i, j: (layer, 0, j)),
                  pl.BlockSpec((None, k, tn), lambda i, j: (layer, 0, j + nb))],
        out_specs=pl.BlockSpec((tm, tn), lambda i, j: (i, j)),
        out_shape=jax.ShapeDtypeStruct((m, f), BF16),
        compiler_params=_params("parallel", "arbitrary"),
        name="ffn_up",
    )(x, w_up, w_up)


def _mm_res_ln_kernel(h_ref, w_ref, x_ref, g_ref, b_ref, o_ref, obf_ref, *, scale):
    y = DN_ALPHA * x_ref[...] + scale * _dot(h_ref[...], w_ref[...])
    mu = jnp.mean(y, axis=-1, keepdims=True)
    yc = y - mu
    var = jnp.mean(yc * yc, axis=-1, keepdims=True)
    out = yc * lax.rsqrt(var + LN_EPS) * g_ref[...] + b_ref[...]
    o_ref[...] = out
    obf_ref[...] = out.astype(BF16)


def matmul_residual_layernorm(h, w, layer, x, g, b, scale, tm=256):
    m, k = h.shape
    n = w.shape[2]
    assert m % tm == 0
    return pl.pallas_call(
        functools.partial(_mm_res_ln_kernel, scale=scale),
        grid=(m // tm,),
        in_specs=[pl.BlockSpec((tm, k), lambda i: (i, 0)),
                  pl.BlockSpec((None, k, n), lambda i: (layer, 0, 0), pipeline_mode=pl.Buffered(1)),
                  pl.BlockSpec((tm, n), lambda i: (i, 0)),
                  pl.BlockSpec((1, n), lambda i: (0, 0)),
                  pl.BlockSpec((1, n), lambda i: (0, 0))],
        out_specs=[pl.BlockSpec((tm, n), lambda i: (i, 0)),
                   pl.BlockSpec((tm, n), lambda i: (i, 0))],
        out_shape=[jax.ShapeDtypeStruct((m, n), F32), jax.ShapeDtypeStruct((m, n), BF16)],
        compiler_params=_params("parallel"),
        name="matmul_residual_layernorm",
    )(h, w, x, g.reshape(1, n), b.reshape(1, n))


def _gated_merge_kernel(x_ref, ysb_ref, ydil_ref, yssm_ref, ydsa_ref, wsb_ref, wdil_ref, wssm_ref, wdsa_ref,
                        g0_ref, g1_ref, g2_ref, g3_ref, o_ref):
    x = x_ref[...]
    acc = _sigmoid(_dot(x, g0_ref[...])) * _dot(ysb_ref[...], wsb_ref[...])
    acc += _sigmoid(_dot(x, g1_ref[...])) * _dot(ydil_ref[...], wdil_ref[...])
    acc += _sigmoid(_dot(x, g2_ref[...])) * _dot(yssm_ref[...], wssm_ref[...])
    acc += _sigmoid(_dot(x, g3_ref[...])) * _dot(ydsa_ref[...], wdsa_ref[...])
    o_ref[...] = acc.astype(o_ref.dtype)


def gated_branch_merge(x, ys, ws, w_gate, tm=1024, tn=512):
    m, k = x.shape
    n = ws[0].shape[1]
    assert m % tm == 0 and n % tn == 0
    nb = n // tn
    y_specs = [pl.BlockSpec((tm, y.shape[1]), lambda i, j: (i, 0)) for y in ys]
    w_specs = [pl.BlockSpec((w.shape[0], tn), lambda i, j: (0, j)) for w in ws]
    g_specs = [pl.BlockSpec((k, tn), functools.partial(lambda i, j, br: (0, j + br * nb), br=br))
               for br in range(N_BRANCH)]
    return pl.pallas_call(
        _gated_merge_kernel,
        grid=(m // tm, nb),
        in_specs=[pl.BlockSpec((tm, k), lambda i, j: (i, 0))] + y_specs + w_specs + g_specs,
        out_specs=pl.BlockSpec((tm, tn), lambda i, j: (i, j)),
        out_shape=jax.ShapeDtypeStruct((m, n), BF16),
        compiler_params=_params("parallel", "arbitrary"),
        name="gated_branch_merge",
    )(x, *ys, *ws, w_gate, w_gate, w_gate, w_gate)


def _is_power_of_two(x):
    return math.frexp(x)[0] == 0.5


def _sb_kernel(q_ref, k_ref, v_ref, o_ref, vT_ref, acc_ref, *, qblk, kblk, scale):
    i = pl.program_id(2)
    ratio = qblk // kblk
    fold_scale = _is_power_of_two(scale)

    @pl.when(i == 0)
    def _():
        for j in range(vT_ref.shape[0]):
            vT_ref[j] = v_ref[0, j * kblk:(j + 1) * kblk, :].T

    qT = q_ref[0].T
    if fold_scale:
        qT = qT * jnp.asarray(scale, qT.dtype)
    row = lax.broadcasted_iota(jnp.int32, (LANES, 1), 0)
    zero = jnp.zeros_like(qT)
    q_rhs = jnp.concatenate([jnp.where(row < HEAD_DIM, qT, zero), jnp.where(row >= HEAD_DIM, qT, zero)], axis=1)
    key_io = lax.broadcasted_iota(jnp.int32, (kblk, kblk), 0)
    qry_io = lax.broadcasted_iota(jnp.int32, (kblk, kblk), 1)
    later = jnp.where(qry_io > key_io, 1.0, 0.0).astype(BF16)
    acc_ref[...] = jnp.zeros_like(acc_ref)
    nsub = 2 * ratio

    def block(j, carry, diag_sub):
        start = pl.multiple_of(j * kblk, kblk)
        z_all = _dot(k_ref[0, pl.ds(start, kblk), :], q_rhs)
        vT = vT_ref[j]
        strict = key_io < qry_io
        stage = []
        for c in range(nsub):
            h, sub = divmod(c, ratio)
            if diag_sub is not None and sub < diag_sub:
                stage.append(None)
                continue
            masked = diag_sub is not None and sub == diag_sub
            z = z_all[:, c * kblk:(c + 1) * kblk]
            if not fold_scale:
                z = z * scale
            sp = jnp.maximum(jnp.log(1.0 + jnp.exp(jnp.minimum(z, SOFTPLUS_CLAMP))), z)
            log_1mb = jnp.where(strict, -sp, 0.0) if masked else -sp
            suf = _dot(later, log_1mb.astype(BF16))
            stage.append((z - sp, suf, jnp.sum(log_1mb, axis=0, keepdims=True), masked))
        new_carry = []
        for c in range(nsub):
            h, sub = divmod(c, ratio)
            if stage[c] is None:
                new_carry.append(carry[c])
                continue
            log_beta, suf, colsum, masked = stage[c]
            att = jnp.exp(log_beta + suf + carry[c])
            if masked:
                att = jnp.where(strict, att, 0.0)
            rows = slice(h * HEAD_DIM, (h + 1) * HEAD_DIM)
            cols = slice(sub * kblk, (sub + 1) * kblk)
            acc_ref[rows, cols] += _dot(vT[rows, :], att.astype(BF16))
            new_carry.append(carry[c] + colsum)
        return tuple(new_carry)

    carry = tuple(jnp.zeros((1, kblk), F32) for _ in range(nsub))
    for sub in reversed(range(ratio)):
        carry = block(i * ratio + sub, carry, sub)
    lax.fori_loop(0, i * ratio, lambda jj, c: block(i * ratio - 1 - jj, c, None), carry)
    o_ref[0] = acc_ref[...].T.astype(o_ref.dtype)


def stick_breaking_attention(qkv, col0=0, qblk=SB_QUERY_BLOCK, kblk=SB_KEY_BLOCK):
    bsz, seq, _ = qkv.shape
    width = SB_HEADS * HEAD_DIM
    npair = width // LANES
    qblk = min(qblk, seq)
    assert seq % qblk == 0 and qblk % kblk == 0 and col0 % LANES == 0
    qcol = col0 // LANES
    return pl.pallas_call(
        functools.partial(_sb_kernel, qblk=qblk, kblk=kblk, scale=HEAD_DIM ** -0.5),
        grid=(bsz, npair, seq // qblk),
        in_specs=[pl.BlockSpec((1, qblk, LANES), lambda b, p, i: (b, i, qcol + p)),
                  pl.BlockSpec((1, seq, LANES), lambda b, p, i: (b, 0, qcol + npair + p)),
                  pl.BlockSpec((1, seq, LANES), lambda b, p, i: (b, 0, qcol + 2 * npair + p))],
        out_specs=pl.BlockSpec((1, qblk, LANES), lambda b, p, i: (b, i, p)),
        out_shape=jax.ShapeDtypeStruct((bsz, seq, width), BF16),
        scratch_shapes=[pltpu.VMEM((seq // kblk, LANES, kblk), BF16),
                        pltpu.VMEM((LANES, qblk), F32)],
        compiler_params=_params("parallel", "parallel", "arbitrary"),
        name="stick_breaking_attention",
    )(qkv, qkv, qkv)


def _t5_bucket(dist):
    max_exact = REL_BUCKETS // 2
    d = jnp.maximum(dist, 1).astype(F32)
    large = max_exact + (jnp.log(d / max_exact) / math.log(REL_MAX_DIST / max_exact)
                         * (REL_BUCKETS - max_exact)).astype(jnp.int32)
    large = jnp.minimum(large, REL_BUCKETS - 1)
    return jnp.where(dist < max_exact, dist, large)


def _bias_of_distance(rel_bias, dist):
    one_hot = jax.nn.one_hot(_t5_bucket(dist), REL_BUCKETS, dtype=F32)
    return jnp.einsum("...b,bh->...h", one_hot, rel_bias.astype(F32), precision=lax.Precision.HIGHEST)


def _dilated_bias_tiles(rel_bias_group, dil, blk):
    a = jnp.arange(blk)[:, None]
    cc = jnp.arange(2 * blk)[None, :]
    step = blk + a - cc
    valid = (step >= 0) & (step <= blk)
    bias = _bias_of_distance(rel_bias_group, dil * jnp.clip(step, 0, blk))
    tile = jnp.where(valid[..., None], bias, NEG_INF)
    return jnp.transpose(tile, (2, 0, 1))


def _dsa_bias_tiles(rel_bias_dsa, blk):
    c = jnp.arange(blk)[:, None]
    a = jnp.arange(blk)[None, :]
    own = jnp.where((a - c >= 0)[..., None], _bias_of_distance(rel_bias_dsa, jnp.maximum(a - c, 0)), NEG_INF)
    prev = _bias_of_distance(rel_bias_dsa, blk + a - c)
    half = REL_BUCKETS // 2
    assert half + int(math.log((blk + 1) / half) / math.log(REL_MAX_DIST / half) * (REL_BUCKETS - half)) >= REL_BUCKETS - 1
    far = jnp.broadcast_to(rel_bias_dsa[REL_BUCKETS - 1].astype(F32), prev.shape)
    return jnp.transpose(jnp.stack([own, prev, far]), (3, 0, 1, 2))


def _dil_kernel(*refs, blk, dils, nbs, units, scale):
    ngroups = len(dils)
    nslab = 2 * ngroups
    q_refs, k_refs, v_refs = refs[:nslab], refs[nslab:2 * nslab], refs[2 * nslab:3 * nslab]
    bias_refs = refs[3 * nslab:3 * nslab + ngroups]
    y_ref, o_ref, lse_ref = refs[3 * nslab + ngroups:]
    step = pl.program_id(1)
    lane = lax.broadcasted_iota(jnp.int32, (1, LANES), 1)
    fold_scale = _is_power_of_two(scale)

    work = []
    for g, u in [(g, u) for g in range(ngroups) for u in range(units)]:
        dil, nb = dils[g], nbs[g]
        unit = step * units + u
        c, i = unit // nb, unit % nb

        def rows(block, dil=dil, c=c):
            start = c + dil * blk * block
            return pl.ds(start, blk, stride=dil) if dil > 1 else pl.ds(pl.multiple_of(start, blk), blk)

        cur, prev = rows(i), rows(jnp.maximum(i - 1, 0))
        for half in range(2):
            slab = 2 * g + half
            q = q_refs[slab][0, cur, :]
            if fold_scale:
                q = q * scale
            q = q.astype(BF16)
            kc, kp = k_refs[slab][0, cur, :].astype(BF16), k_refs[slab][0, prev, :].astype(BF16)
            zero = jnp.zeros_like(q)
            scores = []
            for e in range(2):
                head = (lane < HEAD_DIM) if e == 0 else (lane >= HEAD_DIM)
                qm = jnp.where(head, q, zero)
                scores.append((_dot_nt(qm, kp), _dot_nt(qm, kc)))
            work.append((g, i, cur, prev, half, scores))
    soft = []
    for g, i, cur, prev, half, scores in work:
        has_prev = i > 0
        parts = []
        for e in range(2):
            h = 2 * half + e
            s_prev, s_cur = scores[e]
            if not fold_scale:
                s_prev, s_cur = s_prev * scale, s_cur * scale
            s_prev = jnp.where(has_prev, s_prev + bias_refs[g][h, :, :blk], NEG_INF)
            s_cur = s_cur + bias_refs[g][h, :, blk:]
            m = jnp.maximum(jnp.max(s_prev, axis=-1, keepdims=True), jnp.max(s_cur, axis=-1, keepdims=True))
            parts.append((jnp.exp(s_prev - m).astype(BF16), jnp.exp(s_cur - m).astype(BF16), m))
        soft.append(parts)
    ones = jnp.ones((blk, LANES), BF16)
    for (g, i, cur, prev, half, scores), parts in zip(work, soft):
        v_ref = v_refs[2 * g + half]
        vc = jnp.concatenate([v_ref[0, cur, :].astype(BF16), ones], axis=1)
        vp = jnp.concatenate([v_ref[0, prev, :].astype(BF16), ones], axis=1)
        out = jnp.zeros((blk, LANES), F32)
        lse_b = jnp.zeros((blk, LANES), F32)
        for e in range(2):
            head = (lane < HEAD_DIM) if e == 0 else (lane >= HEAD_DIM)
            p_prev, p_cur, m = parts[e]
            o2 = _dot(p_prev, vp) + _dot(p_cur, vc)
            denom = o2[:, LANES:]
            out = jnp.where(head, o2[:, :LANES] / denom, out)
            lse_b = jnp.where(head, m + jnp.log(denom), lse_b)
        o_ref[g, half, cur, :] = out
        lse_ref[g, half, cur, :] = lse_b

    @pl.when(step == pl.num_programs(1) - 1)
    def _():
        def merge(r, _):
            rows_ = pl.ds(pl.multiple_of(r * DIL_MERGE_ROWS, DIL_MERGE_ROWS), DIL_MERGE_ROWS)
            for half in range(2):
                lses = [lse_ref[g, half, rows_, :] for g in range(ngroups)]
                top = functools.reduce(jnp.maximum, lses)
                ws = [jnp.exp(x - top) for x in lses]
                num = functools.reduce(lambda a, b: a + b, [ws[g] * o_ref[g, half, rows_, :] for g in range(ngroups)])
                den = functools.reduce(lambda a, b: a + b, ws)
                y_ref[0, rows_, half * LANES:(half + 1) * LANES] = (num / den).astype(y_ref.dtype)
            return 0

        lax.fori_loop(0, o_ref.shape[2] // DIL_MERGE_ROWS, merge, 0)


def dilated_attention(qkv, rel_bias_dil, blk=ATT_BLOCK, units=DIL_UNITS):
    bsz, seq, _ = qkv.shape
    hpg = DIL_HEADS_PER_GROUP
    assert hpg * HEAD_DIM == 2 * LANES and seq % DIL_MERGE_ROWS == 0
    ngroups = len(DIL_PATTERNS)
    nslab = DIL_HEADS * HEAD_DIM // LANES
    dils = tuple(d for _, d in DIL_PATTERNS)
    nbs = tuple(seq // d // blk for d in dils)
    nsteps = dils[0] * nbs[0]
    assert all(w // d == blk for w, d in DIL_PATTERNS) and all(d * n == nsteps for d, n in zip(dils, nbs))
    assert nsteps % units == 0
    tiles = [_dilated_bias_tiles(rel_bias_dil[:, g * hpg:(g + 1) * hpg], d, blk) for g, d in enumerate(dils)]
    once = pl.Buffered(1)
    slab = lambda col: pl.BlockSpec((1, seq, LANES), functools.partial(lambda b, s, col: (b, 0, col), col=col),
                                    pipeline_mode=once)
    in_specs = [slab(section * nslab + s) for section in range(3) for s in range(nslab)]
    in_specs += [pl.BlockSpec(t.shape, lambda b, s: (0, 0, 0), pipeline_mode=once) for t in tiles]
    return pl.pallas_call(
        functools.partial(_dil_kernel, blk=blk, dils=dils, nbs=nbs, units=units, scale=HEAD_DIM ** -0.5),
        grid=(bsz, nsteps // units),
        in_specs=in_specs,
        out_specs=pl.BlockSpec((1, seq, 2 * LANES), lambda b, s: (b, 0, 0)),
        out_shape=jax.ShapeDtypeStruct((bsz, seq, 2 * LANES), BF16),
        scratch_shapes=[pltpu.VMEM((ngroups, 2, seq, LANES), F32),
                        pltpu.VMEM((ngroups, 2, seq, LANES), F32)],
        compiler_params=_params("parallel", "arbitrary"),
        name="dilated_attention",
    )(*([qkv] * (3 * nslab)), *tiles)


def _ssm_kernel(u_ref, bb_ref, a_ref, cc_ref, d_ref, wg_ref, o_ref, x_ref, carry_ref, *, steps, strip):
    @pl.when(pl.program_id(1) == 0)
    def _():
        carry_ref[...] = jnp.zeros_like(carry_ref)

    u = u_ref[0]
    ub = u.astype(BF16)
    width = u.shape[1]
    half = width // 2
    nstates = x_ref.shape[1] // 2
    hs = nstates // 2
    for part in range(2):
        bu = _dot(ub[:, part * half:(part + 1) * half], bb_ref[part])
        x_ref[:, part * hs:(part + 1) * hs] = bu[:, :hs]
        x_ref[:, nstates + part * hs:nstates + (part + 1) * hs] = bu[:, hs:]
    for s in range(nstates // strip):
        re = slice(s * strip, (s + 1) * strip)
        im = slice(nstates + s * strip, nstates + (s + 1) * strip)
        ar = jnp.broadcast_to(a_ref[0:1, re], (SUBLANES, strip))
        ai = jnp.broadcast_to(a_ref[1:2, re], (SUBLANES, strip))
        xr = carry_ref[:, re]
        xi = carry_ref[:, im]
        for t in range(steps):
            rows = slice(t * SUBLANES, (t + 1) * SUBLANES)
            xr, xi = ar * xr - ai * xi + x_ref[rows, re], ar * xi + ai * xr + x_ref[rows, im]
            x_ref[rows, re] = xr
            x_ref[rows, im] = xi
        carry_ref[:, re] = xr
        carry_ref[:, im] = xi
    y_parts = []
    for part in range(2):
        xr = x_ref[:, part * hs:(part + 1) * hs].astype(BF16)
        xi = x_ref[:, nstates + part * hs:nstates + (part + 1) * hs].astype(BF16)
        y_parts.append(_dot(xr, cc_ref[0, part]) + _dot(xi, cc_ref[1, part]))
    y = jnp.concatenate(y_parts, axis=1) + d_ref[...] * u
    y = 0.5 * y * (1.0 + jnp.tanh(math.sqrt(2.0 / math.pi) * (y + 0.044715 * (y * y * y))))
    z = _dot(y.astype(BF16), wg_ref[...])
    o_ref[0] = (z[:, :width] * _sigmoid(z[:, width:])).astype(o_ref.dtype)


def _block_diag(blocks):
    g, r, c = blocks.shape
    eye = jnp.eye(g, dtype=blocks.dtype)
    return (eye[:, None, :, None] * blocks[:, :, None, :]).reshape(g * r, g * c)


def _ssm_tables(lam_re, lam_im, log_dt, b_re, b_im, c_re, c_im):
    lr, li = lam_re.astype(F32), lam_im.astype(F32)
    dt = jnp.exp(log_dt.astype(F32))[:, None]
    mag = jnp.exp(lr * dt)
    a_re, a_im = mag * jnp.cos(li * dt), mag * jnp.sin(li * dt)
    den = lr * lr + li * li
    f_re = ((a_re - 1.0) * lr + a_im * li) / den
    f_im = (a_im * lr - (a_re - 1.0) * li) / den
    br, bi = b_re.astype(F32), b_im.astype(F32)
    bb_re = _block_diag(jnp.transpose(f_re[..., None] * br - f_im[..., None] * bi, (0, 2, 1)))
    bb_im = _block_diag(jnp.transpose(f_re[..., None] * bi + f_im[..., None] * br, (0, 2, 1)))
    cc_re = _block_diag(jnp.transpose(c_re.astype(F32), (0, 2, 1)))
    cc_im = -_block_diag(jnp.transpose(c_im.astype(F32), (0, 2, 1)))
    nstates = SSM_GROUPS * SSM_STATE
    hw, hs = SSM_WIDTH // 2, nstates // 2
    bb = jnp.stack([jnp.concatenate([bb_re[p * hw:(p + 1) * hw, p * hs:(p + 1) * hs],
                                     bb_im[p * hw:(p + 1) * hw, p * hs:(p + 1) * hs]], axis=1) for p in range(2)])
    cc = jnp.stack([jnp.stack([m[p * hs:(p + 1) * hs, p * hw:(p + 1) * hw] for p in range(2)])
                    for m in (cc_re, cc_im)])
    a = jnp.stack([a_re.reshape(nstates), a_im.reshape(nstates)])
    return bb.astype(BF16), a, cc.astype(BF16)


def s5_glu(u, bb, a, cc, d_skip, w_glu, steps=SSM_STEPS, strip=SSM_STRIP):
    bsz, seq, width = u.shape
    nstates = a.shape[1]
    assert bsz % SUBLANES == 0 and seq % steps == 0 and nstates % strip == 0
    ngrp = bsz // SUBLANES
    rows = steps * SUBLANES
    ut = u.reshape(ngrp, SUBLANES, seq, width).transpose(0, 2, 1, 3).reshape(ngrp, seq * SUBLANES, width)
    const = lambda *shape: pl.BlockSpec(shape, lambda g, c: (0,) * len(shape))
    out = pl.pallas_call(
        functools.partial(_ssm_kernel, steps=steps, strip=strip),
        grid=(ngrp, seq // steps),
        in_specs=[pl.BlockSpec((1, rows, width), lambda g, c: (g, c, 0)),
                  const(*bb.shape), const(*a.shape), const(*cc.shape),
                  const(1, width), const(*w_glu.shape)],
        out_specs=pl.BlockSpec((1, rows, width), lambda g, c: (g, c, 0)),
        out_shape=jax.ShapeDtypeStruct((ngrp, seq * SUBLANES, width), BF16),
        scratch_shapes=[pltpu.VMEM((rows, 2 * nstates), F32),
                        pltpu.VMEM((SUBLANES, 2 * nstates), F32)],
        compiler_params=_params("parallel", "arbitrary"),
        name="s5_glu",
    )(ut, bb, a, cc, d_skip.reshape(1, width).astype(F32), w_glu)
    return out.reshape(ngrp, seq, SUBLANES, width).transpose(0, 2, 1, 3).reshape(bsz * seq, width)


def _dsa_kernel(q_ref, k_ref, v_ref, iq_ref, ik_ref, ikq_ref, bias_ref, o_ref,
                vT_ref, key_ref, madd_ref, acc_ref, *, blk, topk, scale):
    i = pl.program_id(1)
    nblk = i + 1
    key_io = lax.broadcasted_iota(jnp.int32, (blk, blk), 0)
    qry_io = lax.broadcasted_iota(jnp.int32, (blk, blk), 1)

    vrows = HEAD_DIM + DSA_ONES_ROWS

    @pl.when(i == 0)
    def _():
        ones = jnp.ones((DSA_ONES_ROWS, blk), vT_ref.dtype)
        for j in range(vT_ref.shape[0]):
            vt = v_ref[0, j * blk:(j + 1) * blk, :].T
            for h in range(DSA_HEADS):
                vT_ref[j, h * vrows:h * vrows + HEAD_DIM, :] = vt[h * HEAD_DIM:(h + 1) * HEAD_DIM, :]
                vT_ref[j, h * vrows + HEAD_DIM:(h + 1) * vrows, :] = ones

    iqT = iq_ref[0].T
    idx_rhs = jnp.concatenate([iqT[h * IDX_DIM:(h + 1) * IDX_DIM, :] for h in range(IDX_HEADS)], axis=1)
    iw = ikq_ref[0].T[IDX_DIM:IDX_DIM + IDX_HEADS, :]

    def score_block(j, _):
        start = pl.multiple_of(j * blk, blk)
        kk = ik_ref[0, pl.ds(start, blk), :][:, :IDX_DIM].astype(BF16)
        d = _dot(kk, idx_rhs)
        sc = jnp.zeros((blk, blk), F32)
        for h in range(IDX_HEADS):
            sc = sc + iw[h:h + 1, :] * jnp.maximum(d[:, h * blk:(h + 1) * blk], 0.0)
        sc = jnp.where((key_io + j * blk) <= (qry_io + i * blk), sc, NEG_INF)
        bits = pltpu.bitcast(sc, jnp.int32)
        key_ref[j] = bits ^ ((bits >> 31) & 0x7FFFFFFF)
        return 0

    lax.fori_loop(0, nblk, score_block, 0)

    @pl.when(nblk % 2 == 1)
    def _():
        key_ref[nblk] = jnp.full((blk, blk), INT_MIN, jnp.int32)

    def count(pred):
        def body(p, acc):
            for j in (2 * p, 2 * p + 1):
                hit = jnp.where(pred(key_ref[j]), 1.0, 0.0)
                acc = acc + jnp.sum(hit.reshape(blk // 8, 8, blk), axis=0)
            return acc
        return jnp.sum(lax.fori_loop(0, (nblk + 1) // 2, body, jnp.zeros((8, blk), F32)), axis=0, keepdims=True)

    def bit_step(b, thr):
        cand = thr + lax.shift_left(jnp.int32(1), 31 - b)
        return jnp.where(count(lambda key: key >= cand) >= topk, cand, thr)

    thr = lax.fori_loop(0, 32, bit_step, jnp.full((1, blk), INT_MIN, jnp.int32))

    need = topk - count(lambda key: key > thr)
    upto = jnp.where(qry_io <= key_io, 1.0, 0.0).astype(BF16)

    def select_block(j, run):
        key = key_ref[j]
        tie = jnp.where(key == thr, 1.0, 0.0)
        rank = _dot(upto, tie.astype(BF16)) + run
        keep_tie = jnp.where(rank <= need, 0.0, NEG_INF)
        madd_ref[j] = jnp.where(key > thr, 0.0, jnp.where(key == thr, keep_tie, NEG_INF))
        return run + jnp.sum(tie, axis=0, keepdims=True)

    lax.fori_loop(0, nblk, select_block, jnp.zeros((1, blk), F32))

    fold_scale = _is_power_of_two(scale)
    qT = q_ref[0].T
    if fold_scale:
        qT = qT * jnp.asarray(scale, qT.dtype)
    row = lax.broadcasted_iota(jnp.int32, (LANES, 1), 0)
    q_rhs = []
    for g in range(DSA_HEADS // 2):
        pair = qT[g * LANES:(g + 1) * LANES, :]
        zero = jnp.zeros_like(pair)
        q_rhs.append(jnp.concatenate([jnp.where(row < HEAD_DIM, pair, zero),
                                      jnp.where(row >= HEAD_DIM, pair, zero)], axis=1))
    acc_ref[...] = jnp.zeros_like(acc_ref)

    def attend_block(j, m_all):
        start = pl.multiple_of(j * blk, blk)
        kblk = k_ref[0, pl.ds(start, blk), :]
        vT = vT_ref[j]
        madd = madd_ref[j]
        which = jnp.minimum(i - j, 2)
        s2 = [_dot(kblk[:, g * LANES:(g + 1) * LANES], q_rhs[g]) for g in range(DSA_HEADS // 2)]
        m_out, probs, alphas = [], [], []
        for h in range(DSA_HEADS):
            s = s2[h // 2][:, (h % 2) * blk:(h % 2 + 1) * blk]
            if not fold_scale:
                s = s * scale
            s = s + bias_ref[h, which] + madd
            m_new = jnp.maximum(m_all[h], jnp.max(s, axis=0, keepdims=True))
            probs.append(jnp.exp(s - m_new).astype(BF16))
            alphas.append(jnp.exp(m_all[h] - m_new))
            m_out.append(m_new)
        for h in range(DSA_HEADS):
            rows = slice(h * vrows, (h + 1) * vrows)
            acc_ref[rows, :] = alphas[h] * acc_ref[rows, :] + _dot(vT[rows, :], probs[h])
        return tuple(m_out)

    lax.fori_loop(0, nblk, attend_block, tuple(jnp.full((1, blk), NEG_INF, F32) for _ in range(DSA_HEADS)))
    out = [acc_ref[h * vrows:h * vrows + HEAD_DIM, :] / acc_ref[h * vrows + HEAD_DIM:h * vrows + HEAD_DIM + 1, :]
           for h in range(DSA_HEADS)]
    o_ref[0] = jnp.concatenate(out, axis=0).T.astype(o_ref.dtype)


def dsa_attention(proj, qkv_col, iq_col, proj_f32, kw_col, bias_tiles, blk=DSA_BLOCK):
    bsz, seq, _ = proj.shape
    width = DSA_HEADS * HEAD_DIM
    nblk = seq // blk
    topk = min(DSA_TOPK, seq // 4)
    iq_width = IDX_HEADS * IDX_DIM
    assert qkv_col % width == 0 and iq_col % iq_width == 0 and kw_col % LANES == 0
    assert nblk % 2 == 0 and topk <= blk
    vrows_all = DSA_HEADS * (HEAD_DIM + DSA_ONES_ROWS)
    qcol = qkv_col // width
    return pl.pallas_call(
        functools.partial(_dsa_kernel, blk=blk, topk=topk, scale=HEAD_DIM ** -0.5),
        grid=(bsz, nblk),
        in_specs=[pl.BlockSpec((1, blk, width), lambda b, i: (b, i, qcol)),
                  pl.BlockSpec((1, seq, width), lambda b, i: (b, 0, qcol + 1)),
                  pl.BlockSpec((1, seq, width), lambda b, i: (b, 0, qcol + 2)),
                  pl.BlockSpec((1, blk, iq_width), lambda b, i: (b, i, iq_col // iq_width)),
                  pl.BlockSpec((1, seq, LANES), lambda b, i: (b, 0, kw_col // LANES)),
                  pl.BlockSpec((1, blk, LANES), lambda b, i: (b, i, kw_col // LANES)),
                  pl.BlockSpec((DSA_HEADS, 3, blk, blk), lambda b, i: (0, 0, 0, 0))],
        out_specs=pl.BlockSpec((1, blk, width), lambda b, i: (b, i, 0)),
        out_shape=jax.ShapeDtypeStruct((bsz, seq, width), BF16),
        scratch_shapes=[pltpu.VMEM((nblk, vrows_all, blk), BF16),
                        pltpu.VMEM((nblk, blk, blk), jnp.int32),
                        pltpu.VMEM((nblk, blk, blk), F32),
                        pltpu.VMEM((vrows_all, blk), F32)],
        compiler_params=_params("parallel", "arbitrary"),
        name="dsa_attention",
    )(proj, proj, proj, proj, proj_f32, proj_f32, bias_tiles)


def hybrid_mixer(xbf, bsz, seq, w_in, rel_bias, ssm_params, d_skip, w_glu, w_branches):
    offs = [0]
    for width in IN_SPLITS:
        offs.append(offs[-1] + width)
    seg = lambda a: w_in[:, offs[a]:offs[a + 1]]
    w_bf = jnp.concatenate([seg(0), seg(3), seg(4)], axis=1).astype(BF16)
    pad = jnp.zeros((w_in.shape[0], LANES - IDX_DIM - IDX_HEADS), w_in.dtype)
    w_f32 = jnp.concatenate([seg(1), seg(2), seg(5), seg(6), pad], axis=1).astype(BF16)
    dsa_col = IN_SPLITS[0]
    iq_col = dsa_col + IN_SPLITS[3]
    ssm_col = IN_SPLITS[1]
    kw_col = ssm_col + IN_SPLITS[2]
    proj = matmul(xbf, w_bf, BF16).reshape(bsz, seq, -1)
    proj_f32 = matmul(xbf, w_f32, F32).reshape(bsz, seq, -1)

    y_sb = stick_breaking_attention(proj).reshape(bsz * seq, -1)

    y_dil = dilated_attention(proj_f32, rel_bias[:, :DIL_HEADS]).reshape(bsz * seq, -1)

    bb, a_bar, cc = _ssm_tables(*ssm_params)
    y_ssm = s5_glu(proj_f32[..., ssm_col:ssm_col + SSM_WIDTH], bb, a_bar, cc, d_skip, w_glu.astype(BF16))

    y_dsa = dsa_attention(proj, dsa_col, iq_col, proj_f32, kw_col, _dsa_bias_tiles(rel_bias[:, DIL_HEADS:], DSA_BLOCK))
    y_dsa = y_dsa.reshape(bsz * seq, -1)

    return gated_branch_merge(xbf, [y_sb, y_dil, y_ssm, y_dsa], [w.astype(BF16) for w in w_branches],
                              seg(7).astype(BF16))


def kernel(x, ln_g, ln_b, ffn1_w_up, ffn1_w_down, w_in, rel_bias, ssm_lam_re, ssm_lam_im, ssm_log_dt,
           ssm_b_re, ssm_b_im, ssm_c_re, ssm_c_im, ssm_d, ssm_w_glu, w_br_sb, w_br_dil, w_br_ssm, w_br_dsa,
           w_out, ffn2_w_up, ffn2_w_down):
    bsz, seq, d = x.shape
    xf = x.reshape(bsz * seq, d)
    xbf = xf
    up1, down1 = ffn1_w_up.astype(BF16), ffn1_w_down.astype(BF16)
    up2, down2 = ffn2_w_up.astype(BF16), ffn2_w_down.astype(BF16)
    out_w = w_out.astype(BF16)
    for l in range(DEPTH):
        h = ffn_up(xbf, up1, l)
        xf, xbf = matmul_residual_layernorm(h, down1, l, xf, ln_g[l, 0], ln_b[l, 0], MACARON)
        merged = hybrid_mixer(xbf, bsz, seq, w_in[l], rel_bias,
                              (ssm_lam_re[l], ssm_lam_im[l], ssm_log_dt[l], ssm_b_re[l], ssm_b_im[l],
                               ssm_c_re[l], ssm_c_im[l]), ssm_d[l], ssm_w_glu[l],
                              (w_br_sb[l], w_br_dil[l], w_br_ssm[l], w_br_dsa[l]))
        xf, xbf = matmul_residual_layernorm(merged, out_w, l, xf, ln_g[l, 1], ln_b[l, 1], 1.0)
        h = ffn_up(xbf, up2, l)
        xf, xbf = matmul_residual_layernorm(h, down2, l, xf, ln_g[l, 2], ln_b[l, 2], MACARON)
    return xf.reshape(bsz, seq, d)
```

```python
import functools
import math

import jax
import jax.numpy as jnp
from jax import lax
from jax.experimental import pallas as pl
from jax.experimental.pallas import tpu as pltpu

F32 = jnp.float32
BF16 = jnp.bfloat16

D_MODEL = 2048
DEPTH = 2
HEAD_DIM = 64
SB_HEADS = 8
DIL_PATTERNS = ((128, 1), (512, 4), (2048, 16))
DIL_HEADS_PER_GROUP = 4
DIL_HEADS = DIL_HEADS_PER_GROUP * len(DIL_PATTERNS)
SSM_WIDTH = 512
SSM_GROUP = 16
SSM_GROUPS = SSM_WIDTH // SSM_GROUP
SSM_STATE = 64
DSA_HEADS = 8
IDX_HEADS = 8
IDX_DIM = 64
DSA_TOPK = 256
N_BRANCH = 4
D_FF = 5632
REL_BUCKETS = 32
REL_MAX_DIST = 128
DN_ALPHA = (2.0 * DEPTH) ** 0.25
LN_EPS = 1e-5
NEG_INF = -1e30
MACARON = 0.5

IN_SPLITS = (3 * SB_HEADS * HEAD_DIM, 3 * DIL_HEADS * HEAD_DIM, SSM_WIDTH, 3 * DSA_HEADS * HEAD_DIM,
             IDX_HEADS * IDX_DIM, IDX_DIM, IDX_HEADS, N_BRANCH * D_MODEL)

LANES = 128
ATT_BLOCK = 128
DIL_UNITS = 1
DIL_MERGE_ROWS = 512
DSA_BLOCK = 256
SB_QUERY_BLOCK = 2048
SB_KEY_BLOCK = 256
SUBLANES = 8
SSM_STEPS = 32
SSM_STRIP = 512
VMEM_LIMIT = 56 * 1024 * 1024
INT_MIN = -2 ** 31
SOFTPLUS_CLAMP = 80.0

_NT = (((1,), (1,)), ((), ()))


def _dot(a, b):
    return jnp.dot(a, b, preferred_element_type=F32)


def _dot_nt(a, b):
    return lax.dot_general(a, b, _NT, preferred_element_type=F32)


def _params(*sem):
    return pltpu.CompilerParams(dimension_semantics=sem, vmem_limit_bytes=VMEM_LIMIT)


def _sigmoid(x):
    return 1.0 / (1.0 + jnp.exp(-x))


def _mm_kernel(x_ref, w_ref, o_ref):
    o_ref[...] = _dot(x_ref[...], w_ref[...]).astype(o_ref.dtype)


def matmul(x, w, out_dtype, tm=512):
    m, k = x.shape
    n = w.shape[1]
    assert m % tm == 0 and n % LANES == 0
    return pl.pallas_call(
        _mm_kernel,
        grid=(m // tm,),
        in_specs=[pl.BlockSpec((tm, k), lambda i: (i, 0)),
                  pl.BlockSpec((k, n), lambda i: (0, 0), pipeline_mode=pl.Buffered(1))],
        out_specs=pl.BlockSpec((tm, n), lambda i: (i, 0)),
        out_shape=jax.ShapeDtypeStruct((m, n), out_dtype),
        compiler_params=_params("parallel"),
        name="matmul",
    )(x, w)


def _ffn_up_kernel(x_ref, wa_ref, wb_ref, o_ref):
    x = x_ref[...].astype(BF16)
    a = _dot(x, wa_ref[...])
    b = _dot(x, wb_ref[...])
    o_ref[...] = (a * _sigmoid(a) * b).astype(o_ref.dtype)


def ffn_up(x, w_up, layer, tm=1024, tn=512):
    m, k = x.shape
    f = w_up.shape[2] // 2
    assert m % tm == 0 and f % tn == 0
    nb = f // tn
    return pl.pallas_call(
        _ffn_up_kernel,
        grid=(m // tm, nb),
        in_specs=[pl.BlockSpec((tm, k), lambda i, j: (i, 0)),
                  pl.BlockSpec((None, k, tn), lambda i, j: (layer, 0, j)),
                  pl.BlockSpec((None, k, tn), lambda i, j: (layer, 0, j + nb))],
        out_specs=pl.BlockSpec((tm, tn), lambda i, j: (i, j)),
        out_shape=jax.ShapeDtypeStruct((m, f), BF16),
        compiler_params=_params("parallel", "arbitrary"),
        name="ffn_up",
    )(x, w_up, w_up)


def _mm_res_ln_kernel(h_ref, w_ref, x_ref, g_ref, b_ref, o_ref, obf_ref, *, scale):
    y = DN_ALPHA * x_ref[...] + scale * _dot(h_ref[...], w_ref[...])
    mu = jnp.mean(y, axis=-1, keepdims=True)
    yc = y - mu
    var = jnp.mean(yc * yc, axis=-1, keepdims=True)
    out = yc * lax.rsqrt(var + LN_EPS) * g_ref[...] + b_ref[...]
    o_ref[...] = out
    obf_ref[...] = out.astype(BF16)


def matmul_residual_layernorm(h, w, layer, x, g, b, scale, tm=256):
    m, k = h.shape
    n = w.shape[2]
    assert m % tm == 0
    return pl.pallas_call(
        functools.partial(_mm_res_ln_kernel, scale=scale),
        grid=(m // tm,),
        in_specs=[pl.BlockSpec((tm, k), lambda i: (i, 0)),
                  pl.BlockSpec((None, k, n), lambda i: (layer, 0, 0), pipeline_mode=pl.Buffered(1)),
                  pl.BlockSpec((tm, n), lambda i: (i, 0)),
                  pl.BlockSpec((1, n), lambda i: (0, 0)),
                  pl.BlockSpec((1, n), lambda i: (0, 0))],
        out_specs=[pl.BlockSpec((tm, n), lambda i: (i, 0)),
                   pl.BlockSpec((tm, n), lambda i: (i, 0))],
        out_shape=[jax.ShapeDtypeStruct((m, n), F32), jax.ShapeDtypeStruct((m, n), BF16)],
        compiler_params=_params("parallel"),
        name="matmul_residual_layernorm",
    )(h, w, x, g.reshape(1, n), b.reshape(1, n))


def _gated_merge_kernel(x_ref, ysb_ref, ydil_ref, yssm_ref, ydsa_ref, wsb_ref, wdil_ref, wssm_ref, wdsa_ref,
                        g0_ref, g1_ref, g2_ref, g3_ref, o_ref):
    x = x_ref[...]
    acc = _sigmoid(_dot(x, g0_ref[...])) * _dot(ysb_ref[...], wsb_ref[...])
    acc += _sigmoid(_dot(x, g1_ref[...])) * _dot(ydil_ref[...], wdil_ref[...])
    acc += _sigmoid(_dot(x, g2_ref[...])) * _dot(yssm_ref[...], wssm_ref[...])
    acc += _sigmoid(_dot(x, g3_ref[...])) * _dot(ydsa_ref[...], wdsa_ref[...])
    o_ref[...] = acc.astype(o_ref.dtype)


def gated_branch_merge(x, ys, ws, w_gate, tm=1024, tn=512):
    m, k = x.shape
    n = ws[0].shape[1]
    assert m % tm == 0 and n % tn == 0
    nb = n // tn
    y_specs = [pl.BlockSpec((tm, y.shape[1]), lambda i, j: (i, 0)) for y in ys]
    w_specs = [pl.BlockSpec((w.shape[0], tn), lambda i, j: (0, j)) for w in ws]
    g_specs = [pl.BlockSpec((k, tn), functools.partial(lambda i, j, br: (0, j + br * nb), br=br))
               for br in range(N_BRANCH)]
    return pl.pallas_call(
        _gated_merge_kernel,
        grid=(m // tm, nb),
        in_specs=[pl.BlockSpec((tm, k), lambda i, j: (i, 0))] + y_specs + w_specs + g_specs,
        out_specs=pl.BlockSpec((tm, tn), lambda i, j: (i, j)),
        out_shape=jax.ShapeDtypeStruct((m, n), BF16),
        compiler_params=_params("parallel", "arbitrary"),
        name="gated_branch_merge",
    )(x, *ys, *ws, w_gate, w_gate, w_gate, w_gate)


def _is_power_of_two(x):
    return math.frexp(x)[0] == 0.5


def _sb_kernel(q_ref, k_ref, v_ref, o_ref, vT_ref, acc_ref, *, qblk, kblk, scale):
    i = pl.program_id(2)
    ratio = qblk // kblk
    fold_scale = _is_power_of_two(scale)

    @pl.when(i == 0)
    def _():
        for j in range(vT_ref.shape[0]):
            vT_ref[j] = v_ref[0, j * kblk:(j + 1) * kblk, :].T

    qT = q_ref[0].T
    if fold_scale:
        qT = qT * jnp.asarray(scale, qT.dtype)
    row = lax.broadcasted_iota(jnp.int32, (LANES, 1), 0)
    zero = jnp.zeros_like(qT)
    q_rhs = jnp.concatenate([jnp.where(row < HEAD_DIM, qT, zero), jnp.where(row >= HEAD_DIM, qT, zero)], axis=1)
    key_io = lax.broadcasted_iota(jnp.int32, (kblk, kblk), 0)
    qry_io = lax.broadcasted_iota(jnp.int32, (kblk, kblk), 1)
    later = jnp.where(qry_io > key_io, 1.0, 0.0).astype(BF16)
    acc_ref[...] = jnp.zeros_like(acc_ref)
    nsub = 2 * ratio

    def block(j, carry, diag_sub):
        start = pl.multiple_of(j * kblk, kblk)
        z_all = _dot(k_ref[0, pl.ds(start, kblk), :], q_rhs)
        vT = vT_ref[j]
        strict = key_io < qry_io
        stage = []
        for c in range(nsub):
            h, sub = divmod(c, ratio)
            if diag_sub is not None and sub < diag_sub:
                stage.append(None)
                continue
            masked = diag_sub is not None and sub == diag_sub
            z = z_all[:, c * kblk:(c + 1) * kblk]
            if not fold_scale:
                z = z * scale
            sp = jnp.maximum(jnp.log(1.0 + jnp.exp(jnp.minimum(z, SOFTPLUS_CLAMP))), z)
            log_1mb = jnp.where(strict, -sp, 0.0) if masked else -sp
            suf = _dot(later, log_1mb.astype(BF16))
            stage.append((z - sp, suf, jnp.sum(log_1mb, axis=0, keepdims=True), masked))
        new_carry = []
        for c in range(nsub):
            h, sub = divmod(c, ratio)
            if stage[c] is None:
                new_carry.append(carry[c])
                continue
            log_beta, suf, colsum, masked = stage[c]
            att = jnp.exp(log_beta + suf + carry[c])
            if masked:
                att = jnp.where(strict, att, 0.0)
            rows = slice(h * HEAD_DIM, (h + 1) * HEAD_DIM)
            cols = slice(sub * kblk, (sub + 1) * kblk)
            acc_ref[rows, cols] += _dot(vT[rows, :], att.astype(BF16))
            new_carry.append(carry[c] + colsum)
        return tuple(new_carry)

    carry = tuple(jnp.zeros((1, kblk), F32) for _ in range(nsub))
    for sub in reversed(range(ratio)):
        carry = block(i * ratio + sub, carry, sub)
    lax.fori_loop(0, i * ratio, lambda jj, c: block(i * ratio - 1 - jj, c, None), carry)
    o_ref[0] = acc_ref[...].T.astype(o_ref.dtype)


def stick_breaking_attention(qkv, col0=0, qblk=SB_QUERY_BLOCK, kblk=SB_KEY_BLOCK):
    bsz, seq, _ = qkv.shape
    width = SB_HEADS * HEAD_DIM
    npair = width // LANES
    qblk = min(qblk, seq)
    assert seq % qblk == 0 and qblk % kblk == 0 and col0 % LANES == 0
    qcol = col0 // LANES
    return pl.pallas_call(
        functools.partial(_sb_kernel, qblk=qblk, kblk=kblk, scale=HEAD_DIM ** -0.5),
        grid=(bsz, npair, seq // qblk),
        in_specs=[pl.BlockSpec((1, qblk, LANES), lambda b, p, i: (b, i, qcol + p)),
                  pl.BlockSpec((1, seq, LANES), lambda b, p, i: (b, 0, qcol + npair + p)),
                  pl.BlockSpec((1, seq, LANES), lambda b, p, i: (b, 0, qcol + 2 * npair + p))],
        out_specs=pl.BlockSpec((1, qblk, LANES), lambda b, p, i: (b, i, p)),
        out_shape=jax.ShapeDtypeStruct((bsz, seq, width), BF16),
        scratch_shapes=[pltpu.VMEM((seq // kblk, LANES, kblk), BF16),
                        pltpu.VMEM((LANES, qblk), F32)],
        compiler_params=_params("parallel", "parallel", "arbitrary"),
        name="stick_breaking_attention",
    )(qkv, qkv, qkv)


def _t5_bucket(dist):
    max_exact = REL_BUCKETS // 2
    d = jnp.maximum(dist, 1).astype(F32)
    large = max_exact + (jnp.log(d / max_exact) / math.log(REL_MAX_DIST / max_exact)
                         * (REL_BUCKETS - max_exact)).astype(jnp.int32)
    large = jnp.minimum(large, REL_BUCKETS - 1)
    return jnp.where(dist < max_exact, dist, large)


def _bias_of_distance(rel_bias, dist):
    one_hot = jax.nn.one_hot(_t5_bucket(dist), REL_BUCKETS, dtype=F32)
    return jnp.einsum("...b,bh->...h", one_hot, rel_bias.astype(F32), precision=lax.Precision.HIGHEST)


def _dilated_bias_tiles(rel_bias_group, dil, blk):
    a = jnp.arange(blk)[:, None]
    cc = jnp.arange(2 * blk)[None, :]
    step = blk + a - cc
    valid = (step >= 0) & (step <= blk)
    bias = _bias_of_distance(rel_bias_group, dil * jnp.clip(step, 0, blk))
    tile = jnp.where(valid[..., None], bias, NEG_INF)
    return jnp.transpose(tile, (2, 0, 1))


def _dsa_bias_tiles(rel_bias_dsa, blk):
    c = jnp.arange(blk)[:, None]
    a = jnp.arange(blk)[None, :]
    own = jnp.where((a - c >= 0)[..., None], _bias_of_distance(rel_bias_dsa, jnp.maximum(a - c, 0)), NEG_INF)
    prev = _bias_of_distance(rel_bias_dsa, blk + a - c)
    half = REL_BUCKETS // 2
    assert half + int(math.log((blk + 1) / half) / math.log(REL_MAX_DIST / half) * (REL_BUCKETS - half)) >= REL_BUCKETS - 1
    far = jnp.broadcast_to(rel_bias_dsa[REL_BUCKETS - 1].astype(F32), prev.shape)
    return jnp.transpose(jnp.stack([own, prev, far]), (3, 0, 1, 2))


def _dil_kernel(*refs, blk, dils, nbs, units, scale):
    ngroups = len(dils)
    nslab = 2 * ngroups
    q_refs, k_refs, v_refs = refs[:nslab], refs[nslab:2 * nslab], refs[2 * nslab:3 * nslab]
    bias_refs = refs[3 * nslab:3 * nslab + ngroups]
    y_ref, o_ref, lse_ref = refs[3 * nslab + ngroups:]
    step = pl.program_id(1)
    lane = lax.broadcasted_iota(jnp.int32, (1, LANES), 1)
    fold_scale = _is_power_of_two(scale)

    work = []
    for g, u in [(g, u) for g in range(ngroups) for u in range(units)]:
        dil, nb = dils[g], nbs[g]
        unit = step * units + u
        c, i = unit // nb, unit % nb

        def rows(block, dil=dil, c=c):
            start = c + dil * blk * block
            return pl.ds(start, blk, stride=dil) if dil > 1 else pl.ds(pl.multiple_of(start, blk), blk)

        cur, prev = rows(i), rows(jnp.maximum(i - 1, 0))
        for half in range(2):
            slab = 2 * g + half
            q = q_refs[slab][0, cur, :]
            if fold_scale:
                q = q * scale
            q = q.astype(BF16)
            kc, kp = k_refs[slab][0, cur, :].astype(BF16), k_refs[slab][0, prev, :].astype(BF16)
            zero = jnp.zeros_like(q)
            scores = []
            for e in range(2):
                head = (lane < HEAD_DIM) if e == 0 else (lane >= HEAD_DIM)
                qm = jnp.where(head, q, zero)
                scores.append((_dot_nt(qm, kp), _dot_nt(qm, kc)))
            work.append((g, i, cur, prev, half, scores))
    soft = []
    for g, i, cur, prev, half, scores in work:
        has_prev = i > 0
        parts = []
        for e in range(2):
            h = 2 * half + e
            s_prev, s_cur = scores[e]
            if not fold_scale:
                s_prev, s_cur = s_prev * scale, s_cur * scale
            s_prev = jnp.where(has_prev, s_prev + bias_refs[g][h, :, :blk], NEG_INF)
            s_cur = s_cur + bias_refs[g][h, :, blk:]
            m = jnp.maximum(jnp.max(s_prev, axis=-1, keepdims=True), jnp.max(s_cur, axis=-1, keepdims=True))
            parts.append((jnp.exp(s_prev - m).astype(BF16), jnp.exp(s_cur - m).astype(BF16), m))
        soft.append(parts)
    ones = jnp.ones((blk, LANES), BF16)
    for (g, i, cur, prev, half, scores), parts in zip(work, soft):
        v_ref = v_refs[2 * g + half]
        vc = jnp.concatenate([v_ref[0, cur, :].astype(BF16), ones], axis=1)
        vp = jnp.concatenate([v_ref[0, prev, :].astype(BF16), ones], axis=1)
        out = jnp.zeros((blk, LANES), F32)
        lse_b = jnp.zeros((blk, LANES), F32)
        for e in range(2):
            head = (lane < HEAD_DIM) if e == 0 else (lane >= HEAD_DIM)
            p_prev, p_cur, m = parts[e]
            o2 = _dot(p_prev, vp) + _dot(p_cur, vc)
            denom = o2[:, LANES:]
            out = jnp.where(head, o2[:, :LANES] / denom, out)
            lse_b = jnp.where(head, m + jnp.log(denom), lse_b)
        o_ref[g, half, cur, :] = out
        lse_ref[g, half, cur, :] = lse_b

    @pl.when(step == pl.num_programs(1) - 1)
    def _():
        def merge(r, _):
            rows_ = pl.ds(pl.multiple_of(r * DIL_MERGE_ROWS, DIL_MERGE_ROWS), DIL_MERGE_ROWS)
            for half in range(2):
                lses = [lse_ref[g, half, rows_, :] for g in range(ngroups)]
                top = functools.reduce(jnp.maximum, lses)
                ws = [jnp.exp(x - top) for x in lses]
                num = functools.reduce(lambda a, b: a + b, [ws[g] * o_ref[g, half, rows_, :] for g in range(ngroups)])
                den = functools.reduce(lambda a, b: a + b, ws)
                y_ref[0, rows_, half * LANES:(half + 1) * LANES] = (num / den).astype(y_ref.dtype)
            return 0

        lax.fori_loop(0, o_ref.shape[2] // DIL_MERGE_ROWS, merge, 0)


def dilated_attention(qkv, rel_bias_dil, blk=ATT_BLOCK, units=DIL_UNITS):
    bsz, seq, _ = qkv.shape
    hpg = DIL_HEADS_PER_GROUP
    assert hpg * HEAD_DIM == 2 * LANES and seq % DIL_MERGE_ROWS == 0
    ngroups = len(DIL_PATTERNS)
    nslab = DIL_HEADS * HEAD_DIM // LANES
    dils = tuple(d for _, d in DIL_PATTERNS)
    nbs = tuple(seq // d // blk for d in dils)
    nsteps = dils[0] * nbs[0]
    assert all(w // d == blk for w, d in DIL_PATTERNS) and all(d * n == nsteps for d, n in zip(dils, nbs))
    assert nsteps % units == 0
    tiles = [_dilated_bias_tiles(rel_bias_dil[:, g * hpg:(g + 1) * hpg], d, blk) for g, d in enumerate(dils)]
    once = pl.Buffered(1)
    slab = lambda col: pl.BlockSpec((1, seq, LANES), functools.partial(lambda b, s, col: (b, 0, col), col=col),
                                    pipeline_mode=once)
    in_specs = [slab(section * nslab + s) for section in range(3) for s in range(nslab)]
    in_specs += [pl.BlockSpec(t.shape, lambda b, s: (0, 0, 0), pipeline_mode=once) for t in tiles]
    return pl.pallas_call(
        functools.partial(_dil_kernel, blk=blk, dils=dils, nbs=nbs, units=units, scale=HEAD_DIM ** -0.5),
        grid=(bsz, nsteps // units),
        in_specs=in_specs,
        out_specs=pl.BlockSpec((1, seq, 2 * LANES), lambda b, s: (b, 0, 0)),
        out_shape=jax.ShapeDtypeStruct((bsz, seq, 2 * LANES), BF16),
        scratch_shapes=[pltpu.VMEM((ngroups, 2, seq, LANES), F32),
                        pltpu.VMEM((ngroups, 2, seq, LANES), F32)],
        compiler_params=_params("parallel", "arbitrary"),
        name="dilated_attention",
    )(*([qkv] * (3 * nslab)), *tiles)


def _ssm_kernel(u_ref, bb_ref, a_ref, cc_ref, d_ref, wg_ref, o_ref, x_ref, carry_ref, *, steps, strip):
    @pl.when(pl.program_id(1) == 0)
    def _():
        carry_ref[...] = jnp.zeros_like(carry_ref)

    u = u_ref[0]
    ub = u.astype(BF16)
    width = u.shape[1]
    half = width // 2
    nstates = x_ref.shape[1] // 2
    hs = nstates // 2
    for part in range(2):
        bu = _dot(ub[:, part * half:(part + 1) * half], bb_ref[part])
        x_ref[:, part * hs:(part + 1) * hs] = bu[:, :hs]
        x_ref[:, nstates + part * hs:nstates + (part + 1) * hs] = bu[:, hs:]
    for s in range(nstates // strip):
        re = slice(s * strip, (s + 1) * strip)
        im = slice(nstates + s * strip, nstates + (s + 1) * strip)
        ar = jnp.broadcast_to(a_ref[0:1, re], (SUBLANES, strip))
        ai = jnp.broadcast_to(a_ref[1:2, re], (SUBLANES, strip))
        xr = carry_ref[:, re]
        xi = carry_ref[:, im]
        for t in range(steps):
            rows = slice(t * SUBLANES, (t + 1) * SUBLANES)
            xr, xi = ar * xr - ai * xi + x_ref[rows, re], ar * xi + ai * xr + x_ref[rows, im]
            x_ref[rows, re] = xr
            x_ref[rows, im] = xi
        carry_ref[:, re] = xr
        carry_ref[:, im] = xi
    y_parts = []
    for part in range(2):
        xr = x_ref[:, part * hs:(part + 1) * hs].astype(BF16)
        xi = x_ref[:, nstates + part * hs:nstates + (part + 1) * hs].astype(BF16)
        y_parts.append(_dot(xr, cc_ref[0, part]) + _dot(xi, cc_ref[1, part]))
    y = jnp.concatenate(y_parts, axis=1) + d_ref[...] * u
    y = 0.5 * y * (1.0 + jnp.tanh(math.sqrt(2.0 / math.pi) * (y + 0.044715 * (y * y * y))))
    z = _dot(y.astype(BF16), wg_ref[...])
    o_ref[0] = (z[:, :width] * _sigmoid(z[:, width:])).astype(o_ref.dtype)


def _block_diag(blocks):
    g, r, c = blocks.shape
    eye = jnp.eye(g, dtype=blocks.dtype)
    return (eye[:, None, :, None] * blocks[:, :, None, :]).reshape(g * r, g * c)


def _ssm_tables(lam_re, lam_im, log_dt, b_re, b_im, c_re, c_im):
    lr, li = lam_re.astype(F32), lam_im.astype(F32)
    dt = jnp.exp(log_dt.astype(F32))[:, None]
    mag = jnp.exp(lr * dt)
    a_re, a_im = mag * jnp.cos(li * dt), mag * jnp.sin(li * dt)
    den = lr * lr + li * li
    f_re = ((a_re - 1.0) * lr + a_im * li) / den
    f_im = (a_im * lr - (a_re - 1.0) * li) / den
    br, bi = b_re.astype(F32), b_im.astype(F32)
    bb_re = _block_diag(jnp.transpose(f_re[..., None] * br - f_im[..., None] * bi, (0, 2, 1)))
    bb_im = _block_diag(jnp.transpose(f_re[..., None] * bi + f_im[..., None] * br, (0, 2, 1)))
    cc_re = _block_diag(jnp.transpose(c_re.astype(F32), (0, 2, 1)))
    cc_im = -_block_diag(jnp.transpose(c_im.astype(F32), (0, 2, 1)))
    nstates = SSM_GROUPS * SSM_STATE
    hw, hs = SSM_WIDTH // 2, nstates // 2
    bb = jnp.stack([jnp.concatenate([bb_re[p * hw:(p + 1) * hw, p * hs:(p + 1) * hs],
                                     bb_im[p * hw:(p + 1) * hw, p * hs:(p + 1) * hs]], axis=1) for p in range(2)])
    cc = jnp.stack([jnp.stack([m[p * hs:(p + 1) * hs, p * hw:(p + 1) * hw] for p in range(2)])
                    for m in (cc_re, cc_im)])
    a = jnp.stack([a_re.reshape(nstates), a_im.reshape(nstates)])
    return bb.astype(BF16), a, cc.astype(BF16)


def s5_glu(u, bb, a, cc, d_skip, w_glu, steps=SSM_STEPS, strip=SSM_STRIP):
    bsz, seq, width = u.shape
    nstates = a.shape[1]
    assert bsz % SUBLANES == 0 and seq % steps == 0 and nstates % strip == 0
    ngrp = bsz // SUBLANES
    rows = steps * SUBLANES
    ut = u.reshape(ngrp, SUBLANES, seq, width).transpose(0, 2, 1, 3).reshape(ngrp, seq * SUBLANES, width)
    const = lambda *shape: pl.BlockSpec(shape, lambda g, c: (0,) * len(shape))
    out = pl.pallas_call(
        functools.partial(_ssm_kernel, steps=steps, strip=strip),
        grid=(ngrp, seq // steps),
        in_specs=[pl.BlockSpec((1, rows, width), lambda g, c: (g, c, 0)),
                  const(*bb.shape), const(*a.shape), const(*cc.shape),
                  const(1, width), const(*w_glu.shape)],
        out_specs=pl.BlockSpec((1, rows, width), lambda g, c: (g, c, 0)),
        out_shape=jax.ShapeDtypeStruct((ngrp, seq * SUBLANES, width), BF16),
        scratch_shapes=[pltpu.VMEM((rows, 2 * nstates), F32),
                        pltpu.VMEM((SUBLANES, 2 * nstates), F32)],
        compiler_params=_params("parallel", "arbitrary"),
        name="s5_glu",
    )(ut, bb, a, cc, d_skip.reshape(1, width).astype(F32), w_glu)
    return out.reshape(ngrp, seq, SUBLANES, width).transpose(0, 2, 1, 3).reshape(bsz * seq, width)


def _dsa_kernel(q_ref, k_ref, v_ref, iq_ref, ik_ref, ikq_ref, bias_ref, o_ref,
                vT_ref, key_ref, madd_ref, acc_ref, *, blk, topk, scale):
    i = pl.program_id(1)
    nblk = i + 1
    key_io = lax.broadcasted_iota(jnp.int32, (blk, blk), 0)
    qry_io = lax.broadcasted_iota(jnp.int32, (blk, blk), 1)

    @pl.when(i == 0)
    def _():
        for j in range(vT_ref.shape[0]):
            vT_ref[j] = v_ref[0, j * blk:(j + 1) * blk, :].T

    iqT = iq_ref[0].T
    idx_rhs = jnp.concatenate([iqT[h * IDX_DIM:(h + 1) * IDX_DIM, :] for h in range(IDX_HEADS)], axis=1)
    iw = ikq_ref[0].T[IDX_DIM:IDX_DIM + IDX_HEADS, :]

    def score_block(j, _):
        start = pl.multiple_of(j * blk, blk)
        kk = ik_ref[0, pl.ds(start, blk), :][:, :IDX_DIM].astype(BF16)
        d = _dot(kk, idx_rhs)
        sc = jnp.zeros((blk, blk), F32)
        for h in range(IDX_HEADS):
            sc = sc + iw[h:h + 1, :] * jnp.maximum(d[:, h * blk:(h + 1) * blk], 0.0)
        sc = jnp.where((key_io + j * blk) <= (qry_io + i * blk), sc, NEG_INF)
        bits = pltpu.bitcast(sc, jnp.int32)
        key_ref[j] = bits ^ ((bits >> 31) & 0x7FFFFFFF)
        return 0

    lax.fori_loop(0, nblk, score_block, 0)

    @pl.when(nblk % 2 == 1)
    def _():
        key_ref[nblk] = jnp.full((blk, blk), INT_MIN, jnp.int32)

    def count(pred):
        def body(p, acc):
            for j in (2 * p, 2 * p + 1):
                hit = jnp.where(pred(key_ref[j]), 1.0, 0.0)
                acc = acc + jnp.sum(hit.reshape(blk // 8, 8, blk), axis=0)
            return acc
        return jnp.sum(lax.fori_loop(0, (nblk + 1) // 2, body, jnp.zeros((8, blk), F32)), axis=0, keepdims=True)

    def bit_step(b, thr):
        cand = thr + lax.shift_left(jnp.int32(1), 31 - b)
        return jnp.where(count(lambda key: key >= cand) >= topk, cand, thr)

    thr = lax.fori_loop(0, 32, bit_step, jnp.full((1, blk), INT_MIN, jnp.int32))

    need = topk - count(lambda key: key > thr)
    upto = jnp.where(qry_io <= key_io, 1.0, 0.0).astype(BF16)

    def select_block(j, run):
        key = key_ref[j]
        tie = jnp.where(key == thr, 1.0, 0.0)
        rank = _dot(upto, tie.astype(BF16)) + run
        keep_tie = jnp.where(rank <= need, 0.0, NEG_INF)
        madd_ref[j] = jnp.where(key > thr, 0.0, jnp.where(key == thr, keep_tie, NEG_INF))
        return run + jnp.sum(tie, axis=0, keepdims=True)

    lax.fori_loop(0, nblk, select_block, jnp.zeros((1, blk), F32))

    fold_scale = _is_power_of_two(scale)
    qT = q_ref[0].T
    if fold_scale:
        qT = qT * jnp.asarray(scale, qT.dtype)
    row = lax.broadcasted_iota(jnp.int32, (LANES, 1), 0)
    q_rhs = []
    for g in range(DSA_HEADS // 2):
        pair = qT[g * LANES:(g + 1) * LANES, :]
        zero = jnp.zeros_like(pair)
        q_rhs.append(jnp.concatenate([jnp.where(row < HEAD_DIM, pair, zero),
                                      jnp.where(row >= HEAD_DIM, pair, zero)], axis=1))
    acc_ref[...] = jnp.zeros_like(acc_ref)

    def attend_block(j, carry):
        m_all, l_all = carry
        start = pl.multiple_of(j * blk, blk)
        kblk = k_ref[0, pl.ds(start, blk), :]
        vT = vT_ref[j]
        madd = madd_ref[j]
        which = jnp.minimum(i - j, 2)
        s2 = [_dot(kblk[:, g * LANES:(g + 1) * LANES], q_rhs[g]) for g in range(DSA_HEADS // 2)]
        m_out, l_out, probs, alphas = [], [], [], []
        for h in range(DSA_HEADS):
            s = s2[h // 2][:, (h % 2) * blk:(h % 2 + 1) * blk]
            if not fold_scale:
                s = s * scale
            s = s + bias_ref[h, which] + madd
            m_new = jnp.maximum(m_all[h], jnp.max(s, axis=0, keepdims=True))
            p = jnp.exp(s - m_new)
            alpha = jnp.exp(m_all[h] - m_new)
            l_out.append(alpha * l_all[h] + jnp.sum(p, axis=0, keepdims=True))
            m_out.append(m_new)
            probs.append(p.astype(BF16))
            alphas.append(alpha)
        for h in range(DSA_HEADS):
            rows = slice(h * HEAD_DIM, (h + 1) * HEAD_DIM)
            acc_ref[rows, :] = alphas[h] * acc_ref[rows, :] + _dot(vT[rows, :], probs[h])
        return tuple(m_out), tuple(l_out)

    init = (tuple(jnp.full((1, blk), NEG_INF, F32) for _ in range(DSA_HEADS)),
            tuple(jnp.zeros((1, blk), F32) for _ in range(DSA_HEADS)))
    _, l_all = lax.fori_loop(0, nblk, attend_block, init)
    for h in range(DSA_HEADS):
        rows = slice(h * HEAD_DIM, (h + 1) * HEAD_DIM)
        acc_ref[rows, :] = acc_ref[rows, :] / l_all[h]
    o_ref[0] = acc_ref[...].T.astype(o_ref.dtype)


def dsa_attention(proj, qkv_col, iq_col, proj_f32, kw_col, bias_tiles, blk=DSA_BLOCK):
    bsz, seq, _ = proj.shape
    width = DSA_HEADS * HEAD_DIM
    nblk = seq // blk
    topk = min(DSA_TOPK, seq // 4)
    iq_width = IDX_HEADS * IDX_DIM
    assert qkv_col % width == 0 and iq_col % iq_width == 0 and kw_col % LANES == 0
    assert nblk % 2 == 0 and topk <= blk
    qcol = qkv_col // width
    return pl.pallas_call(
        functools.partial(_dsa_kernel, blk=blk, topk=topk, scale=HEAD_DIM ** -0.5),
        grid=(bsz, nblk),
        in_specs=[pl.BlockSpec((1, blk, width), lambda b, i: (b, i, qcol)),
                  pl.BlockSpec((1, seq, width), lambda b, i: (b, 0, qcol + 1)),
                  pl.BlockSpec((1, seq, width), lambda b, i: (b, 0, qcol + 2)),
                  pl.BlockSpec((1, blk, iq_width), lambda b, i: (b, i, iq_col // iq_width)),
                  pl.BlockSpec((1, seq, LANES), lambda b, i: (b, 0, kw_col // LANES)),
                  pl.BlockSpec((1, blk, LANES), lambda b, i: (b, i, kw_col // LANES)),
                  pl.BlockSpec((DSA_HEADS, 3, blk, blk), lambda b, i: (0, 0, 0, 0))],
        out_specs=pl.BlockSpec((1, blk, width), lambda b, i: (b, i, 0)),
        out_shape=jax.ShapeDtypeStruct((bsz, seq, width), BF16),
        scratch_shapes=[pltpu.VMEM((nblk, width, blk), BF16),
                        pltpu.VMEM((nblk, blk, blk), jnp.int32),
                        pltpu.VMEM((nblk, blk, blk), F32),
                        pltpu.VMEM((width, blk), F32)],
        compiler_params=_params("parallel", "arbitrary"),
        name="dsa_attention",
    )(proj, proj, proj, proj, proj_f32, proj_f32, bias_tiles)


def hybrid_mixer(xbf, bsz, seq, w_in, rel_bias, ssm_params, d_skip, w_glu, w_branches):
    offs = [0]
    for width in IN_SPLITS:
        offs.append(offs[-1] + width)
    seg = lambda a: w_in[:, offs[a]:offs[a + 1]]
    w_bf = jnp.concatenate([seg(0), seg(3), seg(4)], axis=1).astype(BF16)
    pad = jnp.zeros((w_in.shape[0], LANES - IDX_DIM - IDX_HEADS), w_in.dtype)
    w_f32 = jnp.concatenate([seg(1), seg(2), seg(5), seg(6), pad], axis=1).astype(BF16)
    dsa_col = IN_SPLITS[0]
    iq_col = dsa_col + IN_SPLITS[3]
    ssm_col = IN_SPLITS[1]
    kw_col = ssm_col + IN_SPLITS[2]
    proj = matmul(xbf, w_bf, BF16).reshape(bsz, seq, -1)
    proj_f32 = matmul(xbf, w_f32, F32).reshape(bsz, seq, -1)

    y_sb = stick_breaking_attention(proj).reshape(bsz * seq, -1)

    y_dil = dilated_attention(proj_f32, rel_bias[:, :DIL_HEADS]).reshape(bsz * seq, -1)

    bb, a_bar, cc = _ssm_tables(*ssm_params)
    y_ssm = s5_glu(proj_f32[..., ssm_col:ssm_col + SSM_WIDTH], bb, a_bar, cc, d_skip, w_glu.astype(BF16))

    y_dsa = dsa_attention(proj, dsa_col, iq_col, proj_f32, kw_col, _dsa_bias_tiles(rel_bias[:, DIL_HEADS:], DSA_BLOCK))
    y_dsa = y_dsa.reshape(bsz * seq, -1)

    return gated_branch_merge(xbf, [y_sb, y_dil, y_ssm, y_dsa], [w.astype(BF16) for w in w_branches],
                              seg(7).astype(BF16))


def kernel(x, ln_g, ln_b, ffn1_w_up, ffn1_w_down, w_in, rel_bias, ssm_lam_re, ssm_lam_im, ssm_log_dt,
           ssm_b_re, ssm_b_im, ssm_c_re, ssm_c_im, ssm_d, ssm_w_glu, w_br_sb, w_br_dil, w_br_ssm, w_br_dsa,
           w_out, ffn2_w_up, ffn2_w_down):
    bsz, seq, d = x.shape
    xf = x.reshape(bsz * seq, d)
    xbf = xf
    up1, down1 = ffn1_w_up.astype(BF16), ffn1_w_down.astype(BF16)
    up2, down2 = ffn2_w_up.astype(BF16), ffn2_w_down.astype(BF16)
    out_w = w_out.astype(BF16)
    for l in range(DEPTH):
        h = ffn_up(xbf, up1, l)
        xf, xbf = matmul_residual_layernorm(h, down1, l, xf, ln_g[l, 0], ln_b[l, 0], MACARON)
        merged = hybrid_mixer(xbf, bsz, seq, w_in[l], rel_bias,
                              (ssm_lam_re[l], ssm_lam_im[l], ssm_log_dt[l], ssm_b_re[l], ssm_b_im[l],
                               ssm_c_re[l], ssm_c_im[l]), ssm_d[l], ssm_w_glu[l],
                              (w_br_sb[l], w_br_dil[l], w_br_ssm[l], w_br_dsa[l]))
        xf, xbf = matmul_residual_layernorm(merged, out_w, l, xf, ln_g[l, 1], ln_b[l, 1], 1.0)
        h = ffn_up(xbf, up2, l)
        xf, xbf = matmul_residual_layernorm(h, down2, l, xf, ln_g[l, 2], ln_b[l, 2], MACARON)
    return xf.reshape(bsz, seq, d)
```

```python
import functools
import math

import jax
import jax.numpy as jnp
from jax import lax
from jax.experimental import pallas as pl
from jax.experimental.pallas import tpu as pltpu

F32 = jnp.float32
BF16 = jnp.bfloat16

D_MODEL = 2048
DEPTH = 2
HEAD_DIM = 64
SB_HEADS = 8
DIL_PATTERNS = ((128, 1), (512, 4), (2048, 16))
DIL_HEADS_PER_GROUP = 4
DIL_HEADS = DIL_HEADS_PER_GROUP * len(DIL_PATTERNS)
SSM_WIDTH = 512
SSM_GROUP = 16
SSM_GROUPS = SSM_WIDTH // SSM_GROUP
SSM_STATE = 64
DSA_HEADS = 8
IDX_HEADS = 8
IDX_DIM = 64
DSA_TOPK = 256
N_BRANCH = 4
D_FF = 5632
REL_BUCKETS = 32
REL_MAX_DIST = 128
DN_ALPHA = (2.0 * DEPTH) ** 0.25
LN_EPS = 1e-5
NEG_INF = -1e30
MACARON = 0.5

IN_SPLITS = (3 * SB_HEADS * HEAD_DIM, 3 * DIL_HEADS * HEAD_DIM, SSM_WIDTH, 3 * DSA_HEADS * HEAD_DIM,
             IDX_HEADS * IDX_DIM, IDX_DIM, IDX_HEADS, N_BRANCH * D_MODEL)

LANES = 128
ATT_BLOCK = 128
DIL_UNITS = 1
DIL_MERGE_ROWS = 512
DSA_BLOCK = 256
SB_QUERY_BLOCK = 2048
SB_KEY_BLOCK = 256
SUBLANES = 8
SSM_STEPS = 32
SSM_STRIP = 512
VMEM_LIMIT = 56 * 1024 * 1024
INT_MIN = -2 ** 31
SOFTPLUS_CLAMP = 80.0

_NT = (((1,), (1,)), ((), ()))


def _dot(a, b):
    return jnp.dot(a, b, preferred_element_type=F32)


def _dot_nt(a, b):
    return lax.dot_general(a, b, _NT, preferred_element_type=F32)


def _params(*sem):
    return pltpu.CompilerParams(dimension_semantics=sem, vmem_limit_bytes=VMEM_LIMIT)


def _sigmoid(x):
    return 1.0 / (1.0 + jnp.exp(-x))


def _mm_kernel(x_ref, w_ref, o_ref):
    o_ref[...] = _dot(x_ref[...], w_ref[...]).astype(o_ref.dtype)


def matmul(x, w, out_dtype, tm=512):
    m, k = x.shape
    n = w.shape[1]
    assert m % tm == 0 and n % LANES == 0
    return pl.pallas_call(
        _mm_kernel,
        grid=(m // tm,),
        in_specs=[pl.BlockSpec((tm, k), lambda i: (i, 0)),
                  pl.BlockSpec((k, n), lambda i: (0, 0), pipeline_mode=pl.Buffered(1))],
        out_specs=pl.BlockSpec((tm, n), lambda i: (i, 0)),
        out_shape=jax.ShapeDtypeStruct((m, n), out_dtype),
        compiler_params=_params("parallel"),
        name="matmul",
    )(x, w)


def _ffn_up_kernel(x_ref, wa_ref, wb_ref, o_ref):
    x = x_ref[...].astype(BF16)
    a = _dot(x, wa_ref[...])
    b = _dot(x, wb_ref[...])
    o_ref[...] = (a * _sigmoid(a) * b).astype(o_ref.dtype)


def ffn_up(x, w_up, layer, tm=1024, tn=512):
    m, k = x.shape
    f = w_up.shape[2] // 2
    assert m % tm == 0 and f % tn == 0
    nb = f // tn
    return pl.pallas_call(
        _ffn_up_kernel,
        grid=(m // tm, nb),
        in_specs=[pl.BlockSpec((tm, k), lambda i, j: (i, 0)),
                  pl.BlockSpec((None, k, tn), lambda i, j: (layer, 0, j)),
                  pl.BlockSpec((None, k, tn), lambda i, j: (layer, 0, j + nb))],
        out_specs=pl.BlockSpec((tm, tn), lambda i, j: (i, j)),
        out_shape=jax.ShapeDtypeStruct((m, f), BF16),
        compiler_params=_params("parallel", "arbitrary"),
        name="ffn_up",
    )(x, w_up, w_up)


def _mm_res_ln_kernel(h_ref, w_ref, x_ref, g_ref, b_ref, o_ref, obf_ref, *, scale):
    y = DN_ALPHA * x_ref[...] + scale * _dot(h_ref[...], w_ref[...])
    mu = jnp.mean(y, axis=-1, keepdims=True)
    yc = y - mu
    var = jnp.mean(yc * yc, axis=-1, keepdims=True)
    out = yc * lax.rsqrt(var + LN_EPS) * g_ref[...] + b_ref[...]
    o_ref[...] = out
    obf_ref[...] = out.astype(BF16)


def matmul_residual_layernorm(h, w, layer, x, g, b, scale, tm=256):
    m, k = h.shape
    n = w.shape[2]
    assert m % tm == 0
    return pl.pallas_call(
        functools.partial(_mm_res_ln_kernel, scale=scale),
        grid=(m // tm,),
        in_specs=[pl.BlockSpec((tm, k), lambda i: (i, 0)),
                  pl.BlockSpec((None, k, n), lambda i: (layer, 0, 0), pipeline_mode=pl.Buffered(1)),
                  pl.BlockSpec((tm, n), lambda i: (i, 0)),
                  pl.BlockSpec((1, n), lambda i: (0, 0)),
                  pl.BlockSpec((1, n), lambda i: (0, 0))],
        out_specs=[pl.BlockSpec((tm, n), lambda i: (i, 0)),
                   pl.BlockSpec((tm, n), lambda i: (i, 0))],
        out_shape=[jax.ShapeDtypeStruct((m, n), F32), jax.ShapeDtypeStruct((m, n), BF16)],
        compiler_params=_params("parallel"),
        name="matmul_residual_layernorm",
    )(h, w, x, g.reshape(1, n), b.reshape(1, n))


def _gated_merge_kernel(x_ref, ysb_ref, ydil_ref, yssm_ref, ydsa_ref, wsb_ref, wdil_ref, wssm_ref, wdsa_ref,
                        g0_ref, g1_ref, g2_ref, g3_ref, o_ref):
    x = x_ref[...]
    acc = _sigmoid(_dot(x, g0_ref[...])) * _dot(ysb_ref[...], wsb_ref[...])
    acc += _sigmoid(_dot(x, g1_ref[...])) * _dot(ydil_ref[...], wdil_ref[...])
    acc += _sigmoid(_dot(x, g2_ref[...])) * _dot(yssm_ref[...], wssm_ref[...])
    acc += _sigmoid(_dot(x, g3_ref[...])) * _dot(ydsa_ref[...], wdsa_ref[...])
    o_ref[...] = acc.astype(o_ref.dtype)


def gated_branch_merge(x, ys, ws, w_gate, tm=1024, tn=512):
    m, k = x.shape
    n = ws[0].shape[1]
    assert m % tm == 0 and n % tn == 0
    nb = n // tn
    y_specs = [pl.BlockSpec((tm, y.shape[1]), lambda i, j: (i, 0)) for y in ys]
    w_specs = [pl.BlockSpec((w.shape[0], tn), lambda i, j: (0, j)) for w in ws]
    g_specs = [pl.BlockSpec((k, tn), functools.partial(lambda i, j, br: (0, j + br * nb), br=br))
               for br in range(N_BRANCH)]
    return pl.pallas_call(
        _gated_merge_kernel,
        grid=(m // tm, nb),
        in_specs=[pl.BlockSpec((tm, k), lambda i, j: (i, 0))] + y_specs + w_specs + g_specs,
        out_specs=pl.BlockSpec((tm, tn), lambda i, j: (i, j)),
        out_shape=jax.ShapeDtypeStruct((m, n), BF16),
        compiler_params=_params("parallel", "arbitrary"),
        name="gated_branch_merge",
    )(x, *ys, *ws, w_gate, w_gate, w_gate, w_gate)


def _is_power_of_two(x):
    return math.frexp(x)[0] == 0.5


def _sb_kernel(q_ref, k_ref, v_ref, o_ref, vT_ref, acc_ref, *, qblk, kblk, scale):
    i = pl.program_id(2)
    ratio = qblk // kblk
    fold_scale = _is_power_of_two(scale)

    @pl.when(i == 0)
    def _():
        for j in range(vT_ref.shape[0]):
            vT_ref[j] = v_ref[0, j * kblk:(j + 1) * kblk, :].T

    qT = q_ref[0].T
    if fold_scale:
        qT = qT * jnp.asarray(scale, qT.dtype)
    row = lax.broadcasted_iota(jnp.int32, (LANES, 1), 0)
    zero = jnp.zeros_like(qT)
    q_rhs = jnp.concatenate([jnp.where(row < HEAD_DIM, qT, zero), jnp.where(row >= HEAD_DIM, qT, zero)], axis=1)
    key_io = lax.broadcasted_iota(jnp.int32, (kblk, kblk), 0)
    qry_io = lax.broadcasted_iota(jnp.int32, (kblk, kblk), 1)
    later = jnp.where(qry_io > key_io, 1.0, 0.0).astype(BF16)
    acc_ref[...] = jnp.zeros_like(acc_ref)
    nsub = 2 * ratio

    def block(j, carry, diag_sub):
        start = pl.multiple_of(j * kblk, kblk)
        z_all = _dot(k_ref[0, pl.ds(start, kblk), :], q_rhs)
        vT = vT_ref[j]
        strict = key_io < qry_io
        stage = []
        for c in range(nsub):
            h, sub = divmod(c, ratio)
            if diag_sub is not None and sub < diag_sub:
                stage.append(None)
                continue
            masked = diag_sub is not None and sub == diag_sub
            z = z_all[:, c * kblk:(c + 1) * kblk]
            if not fold_scale:
                z = z * scale
            sp = jnp.maximum(jnp.log(1.0 + jnp.exp(jnp.minimum(z, SOFTPLUS_CLAMP))), z)
            log_1mb = jnp.where(strict, -sp, 0.0) if masked else -sp
            suf = _dot(later, log_1mb.astype(BF16))
            stage.append((z - sp, suf, jnp.sum(log_1mb, axis=0, keepdims=True), masked))
        new_carry = []
        for c in range(nsub):
            h, sub = divmod(c, ratio)
            if stage[c] is None:
                new_carry.append(carry[c])
                continue
            log_beta, suf, colsum, masked = stage[c]
            att = jnp.exp(log_beta + suf + carry[c])
            if masked:
                att = jnp.where(strict, att, 0.0)
            rows = slice(h * HEAD_DIM, (h + 1) * HEAD_DIM)
            cols = slice(sub * kblk, (sub + 1) * kblk)
            acc_ref[rows, cols] += _dot(vT[rows, :], att.astype(BF16))
            new_carry.append(carry[c] + colsum)
        return tuple(new_carry)

    carry = tuple(jnp.zeros((1, kblk), F32) for _ in range(nsub))
    for sub in reversed(range(ratio)):
        carry = block(i * ratio + sub, carry, sub)
    lax.fori_loop(0, i * ratio, lambda jj, c: block(i * ratio - 1 - jj, c, None), carry)
    o_ref[0] = acc_ref[...].T.astype(o_ref.dtype)


def stick_breaking_attention(qkv, col0=0, qblk=SB_QUERY_BLOCK, kblk=SB_KEY_BLOCK):
    bsz, seq, _ = qkv.shape
    width = SB_HEADS * HEAD_DIM
    npair = width // LANES
    qblk = min(qblk, seq)
    assert seq % qblk == 0 and qblk % kblk == 0 and col0 % LANES == 0
    qcol = col0 // LANES
    return pl.pallas_call(
        functools.partial(_sb_kernel, qblk=qblk, kblk=kblk, scale=HEAD_DIM ** -0.5),
        grid=(bsz, npair, seq // qblk),
        in_specs=[pl.BlockSpec((1, qblk, LANES), lambda b, p, i: (b, i, qcol + p)),
                  pl.BlockSpec((1, seq, LANES), lambda b, p, i: (b, 0, qcol + npair + p)),
                  pl.BlockSpec((1, seq, LANES), lambda b, p, i: (b, 0, qcol + 2 * npair + p))],
        out_specs=pl.BlockSpec((1, qblk, LANES), lambda b, p, i: (b, i, p)),
        out_shape=jax.ShapeDtypeStruct((bsz, seq, width), BF16),
        scratch_shapes=[pltpu.VMEM((seq // kblk, LANES, kblk), BF16),
                        pltpu.VMEM((LANES, qblk), F32)],
        compiler_params=_params("parallel", "parallel", "arbitrary"),
        name="stick_breaking_attention",
    )(qkv, qkv, qkv)


def _t5_bucket(dist):
    max_exact = REL_BUCKETS // 2
    d = jnp.maximum(dist, 1).astype(F32)
    large = max_exact + (jnp.log(d / max_exact) / math.log(REL_MAX_DIST / max_exact)
                         * (REL_BUCKETS - max_exact)).astype(jnp.int32)
    large = jnp.minimum(large, REL_BUCKETS - 1)
    return jnp.where(dist < max_exact, dist, large)


def _bias_of_distance(rel_bias, dist):
    one_hot = jax.nn.one_hot(_t5_bucket(dist), REL_BUCKETS, dtype=F32)
    return jnp.einsum("...b,bh->...h", one_hot, rel_bias.astype(F32), precision=lax.Precision.HIGHEST)


def _dilated_bias_tiles(rel_bias_group, dil, blk):
    a = jnp.arange(blk)[:, None]
    cc = jnp.arange(2 * blk)[None, :]
    step = blk + a - cc
    valid = (step >= 0) & (step <= blk)
    bias = _bias_of_distance(rel_bias_group, dil * jnp.clip(step, 0, blk))
    tile = jnp.where(valid[..., None], bias, NEG_INF)
    return jnp.transpose(tile, (2, 0, 1))


def _dsa_bias_tiles(rel_bias_dsa, blk):
    c = jnp.arange(blk)[:, None]
    a = jnp.arange(blk)[None, :]
    own = jnp.where((a - c >= 0)[..., None], _bias_of_distance(rel_bias_dsa, jnp.maximum(a - c, 0)), NEG_INF)
    prev = _bias_of_distance(rel_bias_dsa, blk + a - c)
    half = REL_BUCKETS // 2
    assert half + int(math.log((blk + 1) / half) / math.log(REL_MAX_DIST / half) * (REL_BUCKETS - half)) >= REL_BUCKETS - 1
    far = jnp.broadcast_to(rel_bias_dsa[REL_BUCKETS - 1].astype(F32), prev.shape)
    return jnp.transpose(jnp.stack([own, prev, far]), (3, 0, 1, 2))


def _dil_kernel(*refs, blk, dils, nbs, units, scale):
    ngroups = len(dils)
    nslab = 2 * ngroups
    q_refs, k_refs, v_refs = refs[:nslab], refs[nslab:2 * nslab], refs[2 * nslab:3 * nslab]
    bias_refs = refs[3 * nslab:3 * nslab + ngroups]
    y_ref, o_ref, lse_ref = refs[3 * nslab + ngroups:]
    step = pl.program_id(1)
    lane = lax.broadcasted_iota(jnp.int32, (1, LANES), 1)
    fold_scale = _is_power_of_two(scale)

    work = []
    for g, u in [(g, u) for g in range(ngroups) for u in range(units)]:
        dil, nb = dils[g], nbs[g]
        unit = step * units + u
        c, i = unit // nb, unit % nb

        def rows(block, dil=dil, c=c):
            start = c + dil * blk * block
            return pl.ds(start, blk, stride=dil) if dil > 1 else pl.ds(pl.multiple_of(start, blk), blk)

        cur, prev = rows(i), rows(jnp.maximum(i - 1, 0))
        for half in range(2):
            slab = 2 * g + half
            q = q_refs[slab][0, cur, :]
            if fold_scale:
                q = q * scale
            q = q.astype(BF16)
            kc, kp = k_refs[slab][0, cur, :].astype(BF16), k_refs[slab][0, prev, :].astype(BF16)
            zero = jnp.zeros_like(q)
            scores = []
            for e in range(2):
                head = (lane < HEAD_DIM) if e == 0 else (lane >= HEAD_DIM)
                qm = jnp.where(head, q, zero)
                scores.append((_dot_nt(qm, kp), _dot_nt(qm, kc)))
            work.append((g, i, cur, prev, half, scores))
    soft = []
    for g, i, cur, prev, half, scores in work:
        has_prev = i > 0
        parts = []
        for e in range(2):
            h = 2 * half + e
            s_prev, s_cur = scores[e]
            if not fold_scale:
                s_prev, s_cur = s_prev * scale, s_cur * scale
            s_prev = jnp.where(has_prev, s_prev + bias_refs[g][h, :, :blk], NEG_INF)
            s_cur = s_cur + bias_refs[g][h, :, blk:]
            m = jnp.maximum(jnp.max(s_prev, axis=-1, keepdims=True), jnp.max(s_cur, axis=-1, keepdims=True))
            parts.append((jnp.exp(s_prev - m).astype(BF16), jnp.exp(s_cur - m).astype(BF16), m))
        soft.append(parts)
    ones = jnp.ones((blk, LANES), BF16)
    for (g, i, cur, prev, half, scores), parts in zip(work, soft):
        v_ref = v_refs[2 * g + half]
        vc = jnp.concatenate([v_ref[0, cur, :].astype(BF16), ones], axis=1)
        vp = jnp.concatenate([v_ref[0, prev, :].astype(BF16), ones], axis=1)
        out = jnp.zeros((blk, LANES), F32)
        lse_b = jnp.zeros((blk, LANES), F32)
        for e in range(2):
            head = (lane < HEAD_DIM) if e == 0 else (lane >= HEAD_DIM)
            p_prev, p_cur, m = parts[e]
            o2 = _dot(p_prev, vp) + _dot(p_cur, vc)
            denom = o2[:, LANES:]
            out = jnp.where(head, o2[:, :LANES] / denom, out)
            lse_b = jnp.where(head, m + jnp.log(denom), lse_b)
        o_ref[g, half, cur, :] = out
        lse_ref[g, half, cur, :] = lse_b

    @pl.when(step == pl.num_programs(1) - 1)
    def _():
        def merge(r, _):
            rows_ = pl.ds(pl.multiple_of(r * DIL_MERGE_ROWS, DIL_MERGE_ROWS), DIL_MERGE_ROWS)
            for half in range(2):
                lses = [lse_ref[g, half, rows_, :] for g in range(ngroups)]
                top = functools.reduce(jnp.maximum, lses)
                ws = [jnp.exp(x - top) for x in lses]
                num = functools.reduce(lambda a, b: a + b, [ws[g] * o_ref[g, half, rows_, :] for g in range(ngroups)])
                den = functools.reduce(lambda a, b: a + b, ws)
                y_ref[0, rows_, half * LANES:(half + 1) * LANES] = (num / den).astype(y_ref.dtype)
            return 0

        lax.fori_loop(0, o_ref.shape[2] // DIL_MERGE_ROWS, merge, 0)


def dilated_attention(qkv, rel_bias_dil, blk=ATT_BLOCK, units=DIL_UNITS):
    bsz, seq, _ = qkv.shape
    hpg = DIL_HEADS_PER_GROUP
    assert hpg * HEAD_DIM == 2 * LANES and seq % DIL_MERGE_ROWS == 0
    ngroups = len(DIL_PATTERNS)
    nslab = DIL_HEADS * HEAD_DIM // LANES
    dils = tuple(d for _, d in DIL_PATTERNS)
    nbs = tuple(seq // d // blk for d in dils)
    nsteps = dils[0] * nbs[0]
    assert all(w // d == blk for w, d in DIL_PATTERNS) and all(d * n == nsteps for d, n in zip(dils, nbs))
    assert nsteps % units == 0
    tiles = [_dilated_bias_tiles(rel_bias_dil[:, g * hpg:(g + 1) * hpg], d, blk) for g, d in enumerate(dils)]
    once = pl.Buffered(1)
    slab = lambda col: pl.BlockSpec((1, seq, LANES), functools.partial(lambda b, s, col: (b, 0, col), col=col),
                                    pipeline_mode=once)
    in_specs = [slab(section * nslab + s) for section in range(3) for s in range(nslab)]
    in_specs += [pl.BlockSpec(t.shape, lambda b, s: (0, 0, 0), pipeline_mode=once) for t in tiles]
    return pl.pallas_call(
        functools.partial(_dil_kernel, blk=blk, dils=dils, nbs=nbs, units=units, scale=HEAD_DIM ** -0.5),
        grid=(bsz, nsteps // units),
        in_specs=in_specs,
        out_specs=pl.BlockSpec((1, seq, 2 * LANES), lambda b, s: (b, 0, 0)),
        out_shape=jax.ShapeDtypeStruct((bsz, seq, 2 * LANES), BF16),
        scratch_shapes=[pltpu.VMEM((ngroups, 2, seq, LANES), F32),
                        pltpu.VMEM((ngroups, 2, seq, LANES), F32)],
        compiler_params=_params("parallel", "arbitrary"),
        name="dilated_attention",
    )(*([qkv] * (3 * nslab)), *tiles)


def _ssm_kernel(u_ref, bb_ref, a_ref, cc_ref, d_ref, wg_ref, o_ref, x_ref, carry_ref, *, steps, strip):
    @pl.when(pl.program_id(1) == 0)
    def _():
        carry_ref[...] = jnp.zeros_like(carry_ref)

    u = u_ref[0]
    ub = u.astype(BF16)
    width = u.shape[1]
    half = width // 2
    nstates = x_ref.shape[1] // 2
    hs = nstates // 2
    for part in range(2):
        bu = _dot(ub[:, part * half:(part + 1) * half], bb_ref[part])
        x_ref[:, part * hs:(part + 1) * hs] = bu[:, :hs]
        x_ref[:, nstates + part * hs:nstates + (part + 1) * hs] = bu[:, hs:]
    for s in range(nstates // strip):
        re = slice(s * strip, (s + 1) * strip)
        im = slice(nstates + s * strip, nstates + (s + 1) * strip)
        ar = jnp.broadcast_to(a_ref[0:1, re], (SUBLANES, strip))
        ai = jnp.broadcast_to(a_ref[1:2, re], (SUBLANES, strip))
        xr = carry_ref[:, re]
        xi = carry_ref[:, im]
        for t in range(steps):
            rows = slice(t * SUBLANES, (t + 1) * SUBLANES)
            xr, xi = ar * xr - ai * xi + x_ref[rows, re], ar * xi + ai * xr + x_ref[rows, im]
            x_ref[rows, re] = xr
            x_ref[rows, im] = xi
        carry_ref[:, re] = xr
        carry_ref[:, im] = xi
    y_parts = []
    for part in range(2):
        xr = x_ref[:, part * hs:(part + 1) * hs].astype(BF16)
        xi = x_ref[:, nstates + part * hs:nstates + (part + 1) * hs].astype(BF16)
        y_parts.append(_dot(xr, cc_ref[0, part]) + _dot(xi, cc_ref[1, part]))
    y = jnp.concatenate(y_parts, axis=1) + d_ref[...] * u
    y = 0.5 * y * (1.0 + jnp.tanh(math.sqrt(2.0 / math.pi) * (y + 0.044715 * (y * y * y))))
    z = _dot(y.astype(BF16), wg_ref[...])
    o_ref[0] = (z[:, :width] * _sigmoid(z[:, width:])).astype(o_ref.dtype)


def _block_diag(blocks):
    g, r, c = blocks.shape
    eye = jnp.eye(g, dtype=blocks.dtype)
    return (eye[:, None, :, None] * blocks[:, :, None, :]).reshape(g * r, g * c)


def _ssm_tables(lam_re, lam_im, log_dt, b_re, b_im, c_re, c_im):
    lr, li = lam_re.astype(F32), lam_im.astype(F32)
    dt = jnp.exp(log_dt.astype(F32))[:, None]
    mag = jnp.exp(lr * dt)
    a_re, a_im = mag * jnp.cos(li * dt), mag * jnp.sin(li * dt)
    den = lr * lr + li * li
    f_re = ((a_re - 1.0) * lr + a_im * li) / den
    f_im = (a_im * lr - (a_re - 1.0) * li) / den
    br, bi = b_re.astype(F32), b_im.astype(F32)
    bb_re = _block_diag(jnp.transpose(f_re[..., None] * br - f_im[..., None] * bi, (0, 2, 1)))
    bb_im = _block_diag(jnp.transpose(f_re[..., None] * bi + f_im[..., None] * br, (0, 2, 1)))
    cc_re = _block_diag(jnp.transpose(c_re.astype(F32), (0, 2, 1)))
    cc_im = -_block_diag(jnp.transpose(c_im.astype(F32), (0, 2, 1)))
    nstates = SSM_GROUPS * SSM_STATE
    hw, hs = SSM_WIDTH // 2, nstates // 2
    bb = jnp.stack([jnp.concatenate([bb_re[p * hw:(p + 1) * hw, p * hs:(p + 1) * hs],
                                     bb_im[p * hw:(p + 1) * hw, p * hs:(p + 1) * hs]], axis=1) for p in range(2)])
    cc = jnp.stack([jnp.stack([m[p * hs:(p + 1) * hs, p * hw:(p + 1) * hw] for p in range(2)])
                    for m in (cc_re, cc_im)])
    a = jnp.stack([a_re.reshape(nstates), a_im.reshape(nstates)])
    return bb.astype(BF16), a, cc.astype(BF16)


def s5_glu(u, bb, a, cc, d_skip, w_glu, steps=SSM_STEPS, strip=SSM_STRIP):
    bsz, seq, width = u.shape
    nstates = a.shape[1]
    assert bsz % SUBLANES == 0 and seq % steps == 0 and nstates % strip == 0
    ngrp = bsz // SUBLANES
    rows = steps * SUBLANES
    ut = u.reshape(ngrp, SUBLANES, seq, width).transpose(0, 2, 1, 3).reshape(ngrp, seq * SUBLANES, width)
    const = lambda *shape: pl.BlockSpec(shape, lambda g, c: (0,) * len(shape))
    out = pl.pallas_call(
        functools.partial(_ssm_kernel, steps=steps, strip=strip),
        grid=(ngrp, seq // steps),
        in_specs=[pl.BlockSpec((1, rows, width), lambda g, c: (g, c, 0)),
                  const(*bb.shape), const(*a.shape), const(*cc.shape),
                  const(1, width), const(*w_glu.shape)],
        out_specs=pl.BlockSpec((1, rows, width), lambda g, c: (g, c, 0)),
        out_shape=jax.ShapeDtypeStruct((ngrp, seq * SUBLANES, width), BF16),
        scratch_shapes=[pltpu.VMEM((rows, 2 * nstates), F32),
                        pltpu.VMEM((SUBLANES, 2 * nstates), F32)],
        compiler_params=_params("parallel", "arbitrary"),
        name="s5_glu",
    )(ut, bb, a, cc, d_skip.reshape(1, width).astype(F32), w_glu)
    return out.reshape(ngrp, seq, SUBLANES, width).transpose(0, 2, 1, 3).reshape(bsz * seq, width)


def _dsa_kernel(q_ref, k_ref, v_ref, iq_ref, ik_ref, ikq_ref, bias_ref, o_ref,
                vT_ref, key_ref, acc_ref, *, blk, topk, scale):
    i = pl.program_id(1)
    nblk = i + 1
    key_io = lax.broadcasted_iota(jnp.int32, (blk, blk), 0)
    qry_io = lax.broadcasted_iota(jnp.int32, (blk, blk), 1)

    @pl.when(i == 0)
    def _():
        for j in range(vT_ref.shape[0]):
            vT_ref[j] = v_ref[0, j * blk:(j + 1) * blk, :].T

    iqT = iq_ref[0].T
    idx_rhs = jnp.concatenate([iqT[h * IDX_DIM:(h + 1) * IDX_DIM, :] for h in range(IDX_HEADS)], axis=1)
    iw = ikq_ref[0].T[IDX_DIM:IDX_DIM + IDX_HEADS, :]

    def score_block(j, _):
        start = pl.multiple_of(j * blk, blk)
        kk = ik_ref[0, pl.ds(start, blk), :][:, :IDX_DIM].astype(BF16)
        d = _dot(kk, idx_rhs)
        sc = jnp.zeros((blk, blk), F32)
        for h in range(IDX_HEADS):
            sc = sc + iw[h:h + 1, :] * jnp.maximum(d[:, h * blk:(h + 1) * blk], 0.0)
        sc = jnp.where((key_io + j * blk) <= (qry_io + i * blk), sc, NEG_INF)
        bits = pltpu.bitcast(sc, jnp.int32)
        key_ref[j] = bits ^ ((bits >> 31) & 0x7FFFFFFF)
        return 0

    lax.fori_loop(0, nblk, score_block, 0)

    @pl.when(nblk % 2 == 1)
    def _():
        key_ref[nblk] = jnp.full((blk, blk), INT_MIN, jnp.int32)

    def count(pred):
        def body(p, acc):
            for j in (2 * p, 2 * p + 1):
                hit = jnp.where(pred(key_ref[j]), 1.0, 0.0)
                acc = acc + jnp.sum(hit.reshape(blk // 8, 8, blk), axis=0)
            return acc
        return jnp.sum(lax.fori_loop(0, (nblk + 1) // 2, body, jnp.zeros((8, blk), F32)), axis=0, keepdims=True)

    def bit_step(b, thr):
        cand = thr + lax.shift_left(jnp.int32(1), 31 - b)
        return jnp.where(count(lambda key: key >= cand) >= topk, cand, thr)

    thr = lax.fori_loop(0, 32, bit_step, jnp.full((1, blk), INT_MIN, jnp.int32))

    need = topk - count(lambda key: key > thr)
    upto = jnp.where(qry_io <= key_io, 1.0, 0.0).astype(BF16)

    def select_block(j, run):
        key = key_ref[j]
        tie = jnp.where(key == thr, 1.0, 0.0)
        rank = _dot(upto, tie.astype(BF16)) + run
        keep_tie = jnp.where(rank <= need, 0.0, NEG_INF)
        madd = jnp.where(key > thr, 0.0, jnp.where(key == thr, keep_tie, NEG_INF))
        return madd, run + jnp.sum(tie, axis=0, keepdims=True)

    fold_scale = _is_power_of_two(scale)
    qT = q_ref[0].T
    if fold_scale:
        qT = qT * jnp.asarray(scale, qT.dtype)
    row = lax.broadcasted_iota(jnp.int32, (LANES, 1), 0)
    q_rhs = []
    for g in range(DSA_HEADS // 2):
        pair = qT[g * LANES:(g + 1) * LANES, :]
        zero = jnp.zeros_like(pair)
        q_rhs.append(jnp.concatenate([jnp.where(row < HEAD_DIM, pair, zero),
                                      jnp.where(row >= HEAD_DIM, pair, zero)], axis=1))
    acc_ref[...] = jnp.zeros_like(acc_ref)

    def attend_block(j, carry):
        m_all, l_all, run = carry
        start = pl.multiple_of(j * blk, blk)
        kblk = k_ref[0, pl.ds(start, blk), :]
        vT = vT_ref[j]
        madd, run = select_block(j, run)
        which = jnp.minimum(i - j, 2)
        s2 = [_dot(kblk[:, g * LANES:(g + 1) * LANES], q_rhs[g]) for g in range(DSA_HEADS // 2)]
        m_out, l_out, probs, alphas = [], [], [], []
        for h in range(DSA_HEADS):
            s = s2[h // 2][:, (h % 2) * blk:(h % 2 + 1) * blk]
            if not fold_scale:
                s = s * scale
            s = s + bias_ref[h, which] + madd
            m_new = jnp.maximum(m_all[h], jnp.max(s, axis=0, keepdims=True))
            p = jnp.exp(s - m_new)
            alpha = jnp.exp(m_all[h] - m_new)
            l_out.append(alpha * l_all[h] + jnp.sum(p, axis=0, keepdims=True))
            m_out.append(m_new)
            probs.append(p.astype(BF16))
            alphas.append(alpha)
        for h in range(DSA_HEADS):
            rows = slice(h * HEAD_DIM, (h + 1) * HEAD_DIM)
            acc_ref[rows, :] = alphas[h] * acc_ref[rows, :] + _dot(vT[rows, :], probs[h])
        return tuple(m_out), tuple(l_out), run

    init = (tuple(jnp.full((1, blk), NEG_INF, F32) for _ in range(DSA_HEADS)),
            tuple(jnp.zeros((1, blk), F32) for _ in range(DSA_HEADS)), jnp.zeros((1, blk), F32))
    _, l_all, _ = lax.fori_loop(0, nblk, attend_block, init)
    for h in range(DSA_HEADS):
        rows = slice(h * HEAD_DIM, (h + 1) * HEAD_DIM)
        acc_ref[rows, :] = acc_ref[rows, :] / l_all[h]
    o_ref[0] = acc_ref[...].T.astype(o_ref.dtype)


def dsa_attention(proj, qkv_col, iq_col, proj_f32, kw_col, bias_tiles, blk=DSA_BLOCK):
    bsz, seq, _ = proj.shape
    width = DSA_HEADS * HEAD_DIM
    nblk = seq // blk
    topk = min(DSA_TOPK, seq // 4)
    iq_width = IDX_HEADS * IDX_DIM
    assert qkv_col % width == 0 and iq_col % iq_width == 0 and kw_col % LANES == 0
    assert nblk % 2 == 0 and topk <= blk
    qcol = qkv_col // width
    return pl.pallas_call(
        functools.partial(_dsa_kernel, blk=blk, topk=topk, scale=HEAD_DIM ** -0.5),
        grid=(bsz, nblk),
        in_specs=[pl.BlockSpec((1, blk, width), lambda b, i: (b, i, qcol)),
                  pl.BlockSpec((1, seq, width), lambda b, i: (b, 0, qcol + 1)),
                  pl.BlockSpec((1, seq, width), lambda b, i: (b, 0, qcol + 2)),
                  pl.BlockSpec((1, blk, iq_width), lambda b, i: (b, i, iq_col // iq_width)),
                  pl.BlockSpec((1, seq, LANES), lambda b, i: (b, 0, kw_col // LANES)),
                  pl.BlockSpec((1, blk, LANES), lambda b, i: (b, i, kw_col // LANES)),
                  pl.BlockSpec((DSA_HEADS, 3, blk, blk), lambda b, i: (0, 0, 0, 0))],
        out_specs=pl.BlockSpec((1, blk, width), lambda b, i: (b, i, 0)),
        out_shape=jax.ShapeDtypeStruct((bsz, seq, width), BF16),
        scratch_shapes=[pltpu.VMEM((nblk, width, blk), BF16),
                        pltpu.VMEM((nblk, blk, blk), jnp.int32),
                        pltpu.VMEM((width, blk), F32)],
        compiler_params=_params("parallel", "arbitrary"),
        name="dsa_attention",
    )(proj, proj, proj, proj, proj_f32, proj_f32, bias_tiles)


def hybrid_mixer(xbf, bsz, seq, w_in, rel_bias, ssm_params, d_skip, w_glu, w_branches):
    offs = [0]
    for width in IN_SPLITS:
        offs.append(offs[-1] + width)
    seg = lambda a: w_in[:, offs[a]:offs[a + 1]]
    w_bf = jnp.concatenate([seg(0), seg(3), seg(4)], axis=1).astype(BF16)
    pad = jnp.zeros((w_in.shape[0], LANES - IDX_DIM - IDX_HEADS), w_in.dtype)
    w_f32 = jnp.concatenate([seg(1), seg(2), seg(5), seg(6), pad], axis=1).astype(BF16)
    dsa_col = IN_SPLITS[0]
    iq_col = dsa_col + IN_SPLITS[3]
    ssm_col = IN_SPLITS[1]
    kw_col = ssm_col + IN_SPLITS[2]
    proj = matmul(xbf, w_bf, BF16).reshape(bsz, seq, -1)
    proj_f32 = matmul(xbf, w_f32, F32).reshape(bsz, seq, -1)

    y_sb = stick_breaking_attention(proj).reshape(bsz * seq, -1)

    y_dil = dilated_attention(proj_f32, rel_bias[:, :DIL_HEADS]).reshape(bsz * seq, -1)

    bb, a_bar, cc = _ssm_tables(*ssm_params)
    y_ssm = s5_glu(proj_f32[..., ssm_col:ssm_col + SSM_WIDTH], bb, a_bar, cc, d_skip, w_glu.astype(BF16))

    y_dsa = dsa_attention(proj, dsa_col, iq_col, proj_f32, kw_col, _dsa_bias_tiles(rel_bias[:, DIL_HEADS:], DSA_BLOCK))
    y_dsa = y_dsa.reshape(bsz * seq, -1)

    return gated_branch_merge(xbf, [y_sb, y_dil, y_ssm, y_dsa], [w.astype(BF16) for w in w_branches],
                              seg(7).astype(BF16))


def kernel(x, ln_g, ln_b, ffn1_w_up, ffn1_w_down, w_in, rel_bias, ssm_lam_re, ssm_lam_im, ssm_log_dt,
           ssm_b_re, ssm_b_im, ssm_c_re, ssm_c_im, ssm_d, ssm_w_glu, w_br_sb, w_br_dil, w_br_ssm, w_br_dsa,
           w_out, ffn2_w_up, ffn2_w_down):
    bsz, seq, d = x.shape
    xf = x.reshape(bsz * seq, d)
    xbf = xf
    up1, down1 = ffn1_w_up.astype(BF16), ffn1_w_down.astype(BF16)
    up2, down2 = ffn2_w_up.astype(BF16), ffn2_w_down.astype(BF16)
    out_w = w_out.astype(BF16)
    for l in range(DEPTH):
        h = ffn_up(xbf, up1, l)
        xf, xbf = matmul_residual_layernorm(h, down1, l, xf, ln_g[l, 0], ln_b[l, 0], MACARON)
        merged = hybrid_mixer(xbf, bsz, seq, w_in[l], rel_bias,
                              (ssm_lam_re[l], ssm_lam_im[l], ssm_log_dt[l], ssm_b_re[l], ssm_b_im[l],
                               ssm_c_re[l], ssm_c_im[l]), ssm_d[l], ssm_w_glu[l],
                              (w_br_sb[l], w_br_dil[l], w_br_ssm[l], w_br_dsa[l]))
        xf, xbf = matmul_residual_layernorm(merged, out_w, l, xf, ln_g[l, 1], ln_b[l, 1], 1.0)
        h = ffn_up(xbf, up2, l)
        xf, xbf = matmul_residual_layernorm(h, down2, l, xf, ln_g[l, 2], ln_b[l, 2], MACARON)
    return xf.reshape(bsz, seq, d)
```

```python
import functools
import math

import jax
import jax.numpy as jnp
from jax import lax
from jax.experimental import pallas as pl
from jax.experimental.pallas import tpu as pltpu

F32 = jnp.float32
BF16 = jnp.bfloat16

D_MODEL = 2048
DEPTH = 2
HEAD_DIM = 64
SB_HEADS = 8
DIL_PATTERNS = ((128, 1), (512, 4), (2048, 16))
DIL_HEADS_PER_GROUP = 4
DIL_HEADS = DIL_HEADS_PER_GROUP * len(DIL_PATTERNS)
SSM_WIDTH = 512
SSM_GROUP = 16
SSM_GROUPS = SSM_WIDTH // SSM_GROUP
SSM_STATE = 64
DSA_HEADS = 8
IDX_HEADS = 8
IDX_DIM = 64
DSA_TOPK = 256
N_BRANCH = 4
D_FF = 5632
REL_BUCKETS = 32
REL_MAX_DIST = 128
DN_ALPHA = (2.0 * DEPTH) ** 0.25
LN_EPS = 1e-5
NEG_INF = -1e30
MACARON = 0.5

IN_SPLITS = (3 * SB_HEADS * HEAD_DIM, 3 * DIL_HEADS * HEAD_DIM, SSM_WIDTH, 3 * DSA_HEADS * HEAD_DIM,
             IDX_HEADS * IDX_DIM, IDX_DIM, IDX_HEADS, N_BRANCH * D_MODEL)

LANES = 128
ATT_BLOCK = 128
DIL_UNITS = 1
DIL_MERGE_ROWS = 512
DSA_BLOCK = 256
SB_QUERY_BLOCK = 2048
SB_KEY_BLOCK = 256
SUBLANES = 8
SSM_STEPS = 32
SSM_STRIP = 512
VMEM_LIMIT = 56 * 1024 * 1024
INT_MIN = -2 ** 31
HALF_BITS = 16
INT16_MIN = -2 ** 15
SOFTPLUS_CLAMP = 80.0

_NT = (((1,), (1,)), ((), ()))


def _dot(a, b):
    return jnp.dot(a, b, preferred_element_type=F32)


def _dot_nt(a, b):
    return lax.dot_general(a, b, _NT, preferred_element_type=F32)


def _params(*sem):
    return pltpu.CompilerParams(dimension_semantics=sem, vmem_limit_bytes=VMEM_LIMIT)


def _sigmoid(x):
    return 1.0 / (1.0 + jnp.exp(-x))


def _mm_kernel(x_ref, w_ref, o_ref):
    o_ref[...] = _dot(x_ref[...], w_ref[...]).astype(o_ref.dtype)


def matmul(x, w, out_dtype, tm=512):
    m, k = x.shape
    n = w.shape[1]
    assert m % tm == 0 and n % LANES == 0
    return pl.pallas_call(
        _mm_kernel,
        grid=(m // tm,),
        in_specs=[pl.BlockSpec((tm, k), lambda i: (i, 0)),
                  pl.BlockSpec((k, n), lambda i: (0, 0), pipeline_mode=pl.Buffered(1))],
        out_specs=pl.BlockSpec((tm, n), lambda i: (i, 0)),
        out_shape=jax.ShapeDtypeStruct((m, n), out_dtype),
        compiler_params=_params("parallel"),
        name="matmul",
    )(x, w)


def _ffn_up_kernel(x_ref, wa_ref, wb_ref, o_ref):
    x = x_ref[...].astype(BF16)
    a = _dot(x, wa_ref[...])
    b = _dot(x, wb_ref[...])
    o_ref[...] = (a * _sigmoid(a) * b).astype(o_ref.dtype)


def ffn_up(x, w_up, layer, tm=1024, tn=512):
    m, k = x.shape
    f = w_up.shape[2] // 2
    assert m % tm == 0 and f % tn == 0
    nb = f // tn
    return pl.pallas_call(
        _ffn_up_kernel,
        grid=(m // tm, nb),
        in_specs=[pl.BlockSpec((tm, k), lambda i, j: (i, 0)),
                  pl.BlockSpec((None, k, tn), lambda i, j: (layer, 0, j)),
                  pl.BlockSpec((None, k, tn), lambda i, j: (layer, 0, j + nb))],
        out_specs=pl.BlockSpec((tm, tn), lambda i, j: (i, j)),
        out_shape=jax.ShapeDtypeStruct((m, f), BF16),
        compiler_params=_params("parallel", "arbitrary"),
        name="ffn_up",
    )(x, w_up, w_up)


def _mm_res_ln_kernel(h_ref, w_ref, x_ref, g_ref, b_ref, o_ref, obf_ref, *, scale):
    y = DN_ALPHA * x_ref[...] + scale * _dot(h_ref[...], w_ref[...])
    mu = jnp.mean(y, axis=-1, keepdims=True)
    yc = y - mu
    var = jnp.mean(yc * yc, axis=-1, keepdims=True)
    out = yc * lax.rsqrt(var + LN_EPS) * g_ref[...] + b_ref[...]
    o_ref[...] = out
    obf_ref[...] = out.astype(BF16)


def matmul_residual_layernorm(h, w, layer, x, g, b, scale, tm=256):
    m, k = h.shape
    n = w.shape[2]
    assert m % tm == 0
    return pl.pallas_call(
        functools.partial(_mm_res_ln_kernel, scale=scale),
        grid=(m // tm,),
        in_specs=[pl.BlockSpec((tm, k), lambda i: (i, 0)),
                  pl.BlockSpec((None, k, n), lambda i: (layer, 0, 0), pipeline_mode=pl.Buffered(1)),
                  pl.BlockSpec((tm, n), lambda i: (i, 0)),
                  pl.BlockSpec((1, n), lambda i: (0, 0)),
                  pl.BlockSpec((1, n), lambda i: (0, 0))],
        out_specs=[pl.BlockSpec((tm, n), lambda i: (i, 0)),
                   pl.BlockSpec((tm, n), lambda i: (i, 0))],
        out_shape=[jax.ShapeDtypeStruct((m, n), F32), jax.ShapeDtypeStruct((m, n), BF16)],
        compiler_params=_params("parallel"),
        name="matmul_residual_layernorm",
    )(h, w, x, g.reshape(1, n), b.reshape(1, n))


def _gated_merge_kernel(x_ref, ysb_ref, ydil_ref, yssm_ref, ydsa_ref, wsb_ref, wdil_ref, wssm_ref, wdsa_ref,
                        g0_ref, g1_ref, g2_ref, g3_ref, o_ref):
    x = x_ref[...]
    acc = _sigmoid(_dot(x, g0_ref[...])) * _dot(ysb_ref[...], wsb_ref[...])
    acc += _sigmoid(_dot(x, g1_ref[...])) * _dot(ydil_ref[...], wdil_ref[...])
    acc += _sigmoid(_dot(x, g2_ref[...])) * _dot(yssm_ref[...], wssm_ref[...])
    acc += _sigmoid(_dot(x, g3_ref[...])) * _dot(ydsa_ref[...], wdsa_ref[...])
    o_ref[...] = acc.astype(o_ref.dtype)


def gated_branch_merge(x, ys, ws, w_gate, tm=1024, tn=512):
    m, k = x.shape
    n = ws[0].shape[1]
    assert m % tm == 0 and n % tn == 0
    nb = n // tn
    y_specs = [pl.BlockSpec((tm, y.shape[1]), lambda i, j: (i, 0)) for y in ys]
    w_specs = [pl.BlockSpec((w.shape[0], tn), lambda i, j: (0, j)) for w in ws]
    g_specs = [pl.BlockSpec((k, tn), functools.partial(lambda i, j, br: (0, j + br * nb), br=br))
               for br in range(N_BRANCH)]
    return pl.pallas_call(
        _gated_merge_kernel,
        grid=(m // tm, nb),
        in_specs=[pl.BlockSpec((tm, k), lambda i, j: (i, 0))] + y_specs + w_specs + g_specs,
        out_specs=pl.BlockSpec((tm, tn), lambda i, j: (i, j)),
        out_shape=jax.ShapeDtypeStruct((m, n), BF16),
        compiler_params=_params("parallel", "arbitrary"),
        name="gated_branch_merge",
    )(x, *ys, *ws, w_gate, w_gate, w_gate, w_gate)


def _is_power_of_two(x):
    return math.frexp(x)[0] == 0.5


def _sb_kernel(q_ref, k_ref, v_ref, o_ref, vT_ref, acc_ref, *, qblk, kblk, scale):
    i = pl.program_id(2)
    ratio = qblk // kblk
    fold_scale = _is_power_of_two(scale)

    @pl.when(i == 0)
    def _():
        for j in range(vT_ref.shape[0]):
            vT_ref[j] = v_ref[0, j * kblk:(j + 1) * kblk, :].T

    qT = q_ref[0].T
    if fold_scale:
        qT = qT * jnp.asarray(scale, qT.dtype)
    row = lax.broadcasted_iota(jnp.int32, (LANES, 1), 0)
    zero = jnp.zeros_like(qT)
    q_rhs = jnp.concatenate([jnp.where(row < HEAD_DIM, qT, zero), jnp.where(row >= HEAD_DIM, qT, zero)], axis=1)
    key_io = lax.broadcasted_iota(jnp.int32, (kblk, kblk), 0)
    qry_io = lax.broadcasted_iota(jnp.int32, (kblk, kblk), 1)
    later = jnp.where(qry_io > key_io, 1.0, 0.0).astype(BF16)
    acc_ref[...] = jnp.zeros_like(acc_ref)
    nsub = 2 * ratio

    def block(j, carry, diag_sub):
        start = pl.multiple_of(j * kblk, kblk)
        z_all = _dot(k_ref[0, pl.ds(start, kblk), :], q_rhs)
        vT = vT_ref[j]
        strict = key_io < qry_io
        stage = []
        for c in range(nsub):
            h, sub = divmod(c, ratio)
            if diag_sub is not None and sub < diag_sub:
                stage.append(None)
                continue
            masked = diag_sub is not None and sub == diag_sub
            z = z_all[:, c * kblk:(c + 1) * kblk]
            if not fold_scale:
                z = z * scale
            sp = jnp.maximum(jnp.log(1.0 + jnp.exp(jnp.minimum(z, SOFTPLUS_CLAMP))), z)
            log_1mb = jnp.where(strict, -sp, 0.0) if masked else -sp
            suf = _dot(later, log_1mb.astype(BF16))
            stage.append((z - sp, suf, jnp.sum(log_1mb, axis=0, keepdims=True), masked))
        new_carry = []
        for c in range(nsub):
            h, sub = divmod(c, ratio)
            if stage[c] is None:
                new_carry.append(carry[c])
                continue
            log_beta, suf, colsum, masked = stage[c]
            att = jnp.exp(log_beta + suf + carry[c])
            if masked:
                att = jnp.where(strict, att, 0.0)
            rows = slice(h * HEAD_DIM, (h + 1) * HEAD_DIM)
            cols = slice(sub * kblk, (sub + 1) * kblk)
            acc_ref[rows, cols] += _dot(vT[rows, :], att.astype(BF16))
            new_carry.append(carry[c] + colsum)
        return tuple(new_carry)

    carry = tuple(jnp.zeros((1, kblk), F32) for _ in range(nsub))
    for sub in reversed(range(ratio)):
        carry = block(i * ratio + sub, carry, sub)
    lax.fori_loop(0, i * ratio, lambda jj, c: block(i * ratio - 1 - jj, c, None), carry)
    o_ref[0] = acc_ref[...].T.astype(o_ref.dtype)


def stick_breaking_attention(qkv, col0=0, qblk=SB_QUERY_BLOCK, kblk=SB_KEY_BLOCK):
    bsz, seq, _ = qkv.shape
    width = SB_HEADS * HEAD_DIM
    npair = width // LANES
    qblk = min(qblk, seq)
    assert seq % qblk == 0 and qblk % kblk == 0 and col0 % LANES == 0
    qcol = col0 // LANES
    return pl.pallas_call(
        functools.partial(_sb_kernel, qblk=qblk, kblk=kblk, scale=HEAD_DIM ** -0.5),
        grid=(bsz, npair, seq // qblk),
        in_specs=[pl.BlockSpec((1, qblk, LANES), lambda b, p, i: (b, i, qcol + p)),
                  pl.BlockSpec((1, seq, LANES), lambda b, p, i: (b, 0, qcol + npair + p)),
                  pl.BlockSpec((1, seq, LANES), lambda b, p, i: (b, 0, qcol + 2 * npair + p))],
        out_specs=pl.BlockSpec((1, qblk, LANES), lambda b, p, i: (b, i, p)),
        out_shape=jax.ShapeDtypeStruct((bsz, seq, width), BF16),
        scratch_shapes=[pltpu.VMEM((seq // kblk, LANES, kblk), BF16),
                        pltpu.VMEM((LANES, qblk), F32)],
        compiler_params=_params("parallel", "parallel", "arbitrary"),
        name="stick_breaking_attention",
    )(qkv, qkv, qkv)


def _t5_bucket(dist):
    max_exact = REL_BUCKETS // 2
    d = jnp.maximum(dist, 1).astype(F32)
    large = max_exact + (jnp.log(d / max_exact) / math.log(REL_MAX_DIST / max_exact)
                         * (REL_BUCKETS - max_exact)).astype(jnp.int32)
    large = jnp.minimum(large, REL_BUCKETS - 1)
    return jnp.where(dist < max_exact, dist, large)


def _bias_of_distance(rel_bias, dist):
    one_hot = jax.nn.one_hot(_t5_bucket(dist), REL_BUCKETS, dtype=F32)
    return jnp.einsum("...b,bh->...h", one_hot, rel_bias.astype(F32), precision=lax.Precision.HIGHEST)


def _dilated_bias_tiles(rel_bias_group, dil, blk):
    a = jnp.arange(blk)[:, None]
    cc = jnp.arange(2 * blk)[None, :]
    step = blk + a - cc
    valid = (step >= 0) & (step <= blk)
    bias = _bias_of_distance(rel_bias_group, dil * jnp.clip(step, 0, blk))
    tile = jnp.where(valid[..., None], bias, NEG_INF)
    return jnp.transpose(tile, (2, 0, 1))


def _dsa_bias_tiles(rel_bias_dsa, blk):
    c = jnp.arange(blk)[:, None]
    a = jnp.arange(blk)[None, :]
    own = jnp.where((a - c >= 0)[..., None], _bias_of_distance(rel_bias_dsa, jnp.maximum(a - c, 0)), NEG_INF)
    prev = _bias_of_distance(rel_bias_dsa, blk + a - c)
    half = REL_BUCKETS // 2
    assert half + int(math.log((blk + 1) / half) / math.log(REL_MAX_DIST / half) * (REL_BUCKETS - half)) >= REL_BUCKETS - 1
    far = jnp.broadcast_to(rel_bias_dsa[REL_BUCKETS - 1].astype(F32), prev.shape)
    return jnp.transpose(jnp.stack([own, prev, far]), (3, 0, 1, 2))


def _dil_kernel(*refs, blk, dils, nbs, units, scale):
    ngroups = len(dils)
    nslab = 2 * ngroups
    q_refs, k_refs, v_refs = refs[:nslab], refs[nslab:2 * nslab], refs[2 * nslab:3 * nslab]
    bias_refs = refs[3 * nslab:3 * nslab + ngroups]
    y_ref, o_ref, lse_ref = refs[3 * nslab + ngroups:]
    step = pl.program_id(1)
    lane = lax.broadcasted_iota(jnp.int32, (1, LANES), 1)
    fold_scale = _is_power_of_two(scale)

    work = []
    for g, u in [(g, u) for g in range(ngroups) for u in range(units)]:
        dil, nb = dils[g], nbs[g]
        unit = step * units + u
        c, i = unit // nb, unit % nb

        def rows(block, dil=dil, c=c):
            start = c + dil * blk * block
            return pl.ds(start, blk, stride=dil) if dil > 1 else pl.ds(pl.multiple_of(start, blk), blk)

        cur, prev = rows(i), rows(jnp.maximum(i - 1, 0))
        for half in range(2):
            slab = 2 * g + half
            q = q_refs[slab][0, cur, :]
            if fold_scale:
                q = q * scale
            q = q.astype(BF16)
            kc, kp = k_refs[slab][0, cur, :].astype(BF16), k_refs[slab][0, prev, :].astype(BF16)
            zero = jnp.zeros_like(q)
            scores = []
            for e in range(2):
                head = (lane < HEAD_DIM) if e == 0 else (lane >= HEAD_DIM)
                qm = jnp.where(head, q, zero)
                scores.append((_dot_nt(qm, kp), _dot_nt(qm, kc)))
            work.append((g, i, cur, prev, half, scores))
    soft = []
    for g, i, cur, prev, half, scores in work:
        has_prev = i > 0
        parts = []
        for e in range(2):
            h = 2 * half + e
            s_prev, s_cur = scores[e]
            if not fold_scale:
                s_prev, s_cur = s_prev * scale, s_cur * scale
            s_prev = jnp.where(has_prev, s_prev + bias_refs[g][h, :, :blk], NEG_INF)
            s_cur = s_cur + bias_refs[g][h, :, blk:]
            m = jnp.maximum(jnp.max(s_prev, axis=-1, keepdims=True), jnp.max(s_cur, axis=-1, keepdims=True))
            parts.append((jnp.exp(s_prev - m).astype(BF16), jnp.exp(s_cur - m).astype(BF16), m))
        soft.append(parts)
    ones = jnp.ones((blk, LANES), BF16)
    for (g, i, cur, prev, half, scores), parts in zip(work, soft):
        v_ref = v_refs[2 * g + half]
        vc = jnp.concatenate([v_ref[0, cur, :].astype(BF16), ones], axis=1)
        vp = jnp.concatenate([v_ref[0, prev, :].astype(BF16), ones], axis=1)
        out = jnp.zeros((blk, LANES), F32)
        lse_b = jnp.zeros((blk, LANES), F32)
        for e in range(2):
            head = (lane < HEAD_DIM) if e == 0 else (lane >= HEAD_DIM)
            p_prev, p_cur, m = parts[e]
            o2 = _dot(p_prev, vp) + _dot(p_cur, vc)
            denom = o2[:, LANES:]
            out = jnp.where(head, o2[:, :LANES] / denom, out)
            lse_b = jnp.where(head, m + jnp.log(denom), lse_b)
        o_ref[g, half, cur, :] = out
        lse_ref[g, half, cur, :] = lse_b

    @pl.when(step == pl.num_programs(1) - 1)
    def _():
        def merge(r, _):
            rows_ = pl.ds(pl.multiple_of(r * DIL_MERGE_ROWS, DIL_MERGE_ROWS), DIL_MERGE_ROWS)
            for half in range(2):
                lses = [lse_ref[g, half, rows_, :] for g in range(ngroups)]
                top = functools.reduce(jnp.maximum, lses)
                ws = [jnp.exp(x - top) for x in lses]
                num = functools.reduce(lambda a, b: a + b, [ws[g] * o_ref[g, half, rows_, :] for g in range(ngroups)])
                den = functools.reduce(lambda a, b: a + b, ws)
                y_ref[0, rows_, half * LANES:(half + 1) * LANES] = (num / den).astype(y_ref.dtype)
            return 0

        lax.fori_loop(0, o_ref.shape[2] // DIL_MERGE_ROWS, merge, 0)


def dilated_attention(qkv, rel_bias_dil, blk=ATT_BLOCK, units=DIL_UNITS):
    bsz, seq, _ = qkv.shape
    hpg = DIL_HEADS_PER_GROUP
    assert hpg * HEAD_DIM == 2 * LANES and seq % DIL_MERGE_ROWS == 0
    ngroups = len(DIL_PATTERNS)
    nslab = DIL_HEADS * HEAD_DIM // LANES
    dils = tuple(d for _, d in DIL_PATTERNS)
    nbs = tuple(seq // d // blk for d in dils)
    nsteps = dils[0] * nbs[0]
    assert all(w // d == blk for w, d in DIL_PATTERNS) and all(d * n == nsteps for d, n in zip(dils, nbs))
    assert nsteps % units == 0
    tiles = [_dilated_bias_tiles(rel_bias_dil[:, g * hpg:(g + 1) * hpg], d, blk) for g, d in enumerate(dils)]
    once = pl.Buffered(1)
    slab = lambda col: pl.BlockSpec((1, seq, LANES), functools.partial(lambda b, s, col: (b, 0, col), col=col),
                                    pipeline_mode=once)
    in_specs = [slab(section * nslab + s) for section in range(3) for s in range(nslab)]
    in_specs += [pl.BlockSpec(t.shape, lambda b, s: (0, 0, 0), pipeline_mode=once) for t in tiles]
    return pl.pallas_call(
        functools.partial(_dil_kernel, blk=blk, dils=dils, nbs=nbs, units=units, scale=HEAD_DIM ** -0.5),
        grid=(bsz, nsteps // units),
        in_specs=in_specs,
        out_specs=pl.BlockSpec((1, seq, 2 * LANES), lambda b, s: (b, 0, 0)),
        out_shape=jax.ShapeDtypeStruct((bsz, seq, 2 * LANES), BF16),
        scratch_shapes=[pltpu.VMEM((ngroups, 2, seq, LANES), F32),
                        pltpu.VMEM((ngroups, 2, seq, LANES), F32)],
        compiler_params=_params("parallel", "arbitrary"),
        name="dilated_attention",
    )(*([qkv] * (3 * nslab)), *tiles)


def _ssm_kernel(u_ref, bb_ref, a_ref, cc_ref, d_ref, wg_ref, o_ref, x_ref, carry_ref, *, steps, strip):
    @pl.when(pl.program_id(1) == 0)
    def _():
        carry_ref[...] = jnp.zeros_like(carry_ref)

    u = u_ref[0]
    ub = u.astype(BF16)
    width = u.shape[1]
    half = width // 2
    nstates = x_ref.shape[1] // 2
    hs = nstates // 2
    for part in range(2):
        bu = _dot(ub[:, part * half:(part + 1) * half], bb_ref[part])
        x_ref[:, part * hs:(part + 1) * hs] = bu[:, :hs]
        x_ref[:, nstates + part * hs:nstates + (part + 1) * hs] = bu[:, hs:]
    for s in range(nstates // strip):
        re = slice(s * strip, (s + 1) * strip)
        im = slice(nstates + s * strip, nstates + (s + 1) * strip)
        ar = jnp.broadcast_to(a_ref[0:1, re], (SUBLANES, strip))
        ai = jnp.broadcast_to(a_ref[1:2, re], (SUBLANES, strip))
        xr = carry_ref[:, re]
        xi = carry_ref[:, im]
        for t in range(steps):
            rows = slice(t * SUBLANES, (t + 1) * SUBLANES)
            xr, xi = ar * xr - ai * xi + x_ref[rows, re], ar * xi + ai * xr + x_ref[rows, im]
            x_ref[rows, re] = xr
            x_ref[rows, im] = xi
        carry_ref[:, re] = xr
        carry_ref[:, im] = xi
    y_parts = []
    for part in range(2):
        xr = x_ref[:, part * hs:(part + 1) * hs].astype(BF16)
        xi = x_ref[:, nstates + part * hs:nstates + (part + 1) * hs].astype(BF16)
        y_parts.append(_dot(xr, cc_ref[0, part]) + _dot(xi, cc_ref[1, part]))
    y = jnp.concatenate(y_parts, axis=1) + d_ref[...] * u
    y = 0.5 * y * (1.0 + jnp.tanh(math.sqrt(2.0 / math.pi) * (y + 0.044715 * (y * y * y))))
    z = _dot(y.astype(BF16), wg_ref[...])
    o_ref[0] = (z[:, :width] * _sigmoid(z[:, width:])).astype(o_ref.dtype)


def _block_diag(blocks):
    g, r, c = blocks.shape
    eye = jnp.eye(g, dtype=blocks.dtype)
    return (eye[:, None, :, None] * blocks[:, :, None, :]).reshape(g * r, g * c)


def _ssm_tables(lam_re, lam_im, log_dt, b_re, b_im, c_re, c_im):
    lr, li = lam_re.astype(F32), lam_im.astype(F32)
    dt = jnp.exp(log_dt.astype(F32))[:, None]
    mag = jnp.exp(lr * dt)
    a_re, a_im = mag * jnp.cos(li * dt), mag * jnp.sin(li * dt)
    den = lr * lr + li * li
    f_re = ((a_re - 1.0) * lr + a_im * li) / den
    f_im = (a_im * lr - (a_re - 1.0) * li) / den
    br, bi = b_re.astype(F32), b_im.astype(F32)
    bb_re = _block_diag(jnp.transpose(f_re[..., None] * br - f_im[..., None] * bi, (0, 2, 1)))
    bb_im = _block_diag(jnp.transpose(f_re[..., None] * bi + f_im[..., None] * br, (0, 2, 1)))
    cc_re = _block_diag(jnp.transpose(c_re.astype(F32), (0, 2, 1)))
    cc_im = -_block_diag(jnp.transpose(c_im.astype(F32), (0, 2, 1)))
    nstates = SSM_GROUPS * SSM_STATE
    hw, hs = SSM_WIDTH // 2, nstates // 2
    bb = jnp.stack([jnp.concatenate([bb_re[p * hw:(p + 1) * hw, p * hs:(p + 1) * hs],
                                     bb_im[p * hw:(p + 1) * hw, p * hs:(p + 1) * hs]], axis=1) for p in range(2)])
    cc = jnp.stack([jnp.stack([m[p * hs:(p + 1) * hs, p * hw:(p + 1) * hw] for p in range(2)])
                    for m in (cc_re, cc_im)])
    a = jnp.stack([a_re.reshape(nstates), a_im.reshape(nstates)])
    return bb.astype(BF16), a, cc.astype(BF16)


def s5_glu(u, bb, a, cc, d_skip, w_glu, steps=SSM_STEPS, strip=SSM_STRIP):
    bsz, seq, width = u.shape
    nstates = a.shape[1]
    assert bsz % SUBLANES == 0 and seq % steps == 0 and nstates % strip == 0
    ngrp = bsz // SUBLANES
    rows = steps * SUBLANES
    ut = u.reshape(ngrp, SUBLANES, seq, width).transpose(0, 2, 1, 3).reshape(ngrp, seq * SUBLANES, width)
    const = lambda *shape: pl.BlockSpec(shape, lambda g, c: (0,) * len(shape))
    out = pl.pallas_call(
        functools.partial(_ssm_kernel, steps=steps, strip=strip),
        grid=(ngrp, seq // steps),
        in_specs=[pl.BlockSpec((1, rows, width), lambda g, c: (g, c, 0)),
                  const(*bb.shape), const(*a.shape), const(*cc.shape),
                  const(1, width), const(*w_glu.shape)],
        out_specs=pl.BlockSpec((1, rows, width), lambda g, c: (g, c, 0)),
        out_shape=jax.ShapeDtypeStruct((ngrp, seq * SUBLANES, width), BF16),
        scratch_shapes=[pltpu.VMEM((rows, 2 * nstates), F32),
                        pltpu.VMEM((SUBLANES, 2 * nstates), F32)],
        compiler_params=_params("parallel", "arbitrary"),
        name="s5_glu",
    )(ut, bb, a, cc, d_skip.reshape(1, width).astype(F32), w_glu)
    return out.reshape(ngrp, seq, SUBLANES, width).transpose(0, 2, 1, 3).reshape(bsz * seq, width)


def _dsa_kernel(q_ref, k_ref, v_ref, iq_ref, ik_ref, ikq_ref, bias_ref, o_ref,
                vT_ref, key_ref, khi_ref, acc_ref, *, blk, topk, scale):
    i = pl.program_id(1)
    nblk = i + 1
    key_io = lax.broadcasted_iota(jnp.int32, (blk, blk), 0)
    qry_io = lax.broadcasted_iota(jnp.int32, (blk, blk), 1)

    @pl.when(i == 0)
    def _():
        for j in range(vT_ref.shape[0]):
            vT_ref[j] = v_ref[0, j * blk:(j + 1) * blk, :].T

    iqT = iq_ref[0].T
    idx_rhs = jnp.concatenate([iqT[h * IDX_DIM:(h + 1) * IDX_DIM, :] for h in range(IDX_HEADS)], axis=1)
    iw = ikq_ref[0].T[IDX_DIM:IDX_DIM + IDX_HEADS, :]

    def score_block(j, _):
        start = pl.multiple_of(j * blk, blk)
        kk = ik_ref[0, pl.ds(start, blk), :][:, :IDX_DIM].astype(BF16)
        d = _dot(kk, idx_rhs)
        sc = jnp.zeros((blk, blk), F32)
        for h in range(IDX_HEADS):
            sc = sc + iw[h:h + 1, :] * jnp.maximum(d[:, h * blk:(h + 1) * blk], 0.0)
        sc = jnp.where((key_io + j * blk) <= (qry_io + i * blk), sc, NEG_INF)
        bits = pltpu.bitcast(sc, jnp.int32)
        key = bits ^ ((bits >> 31) & 0x7FFFFFFF)
        key_ref[j] = key
        khi_ref[j] = (key >> HALF_BITS).astype(jnp.int16)
        return 0

    lax.fori_loop(0, nblk, score_block, 0)

    @pl.when(nblk % 2 == 1)
    def _():
        key_ref[nblk] = jnp.full((blk, blk), INT_MIN, jnp.int32)
        khi_ref[nblk] = jnp.full((blk, blk), INT16_MIN, jnp.int16)

    npairs = (nblk + 1) // 2

    def count(pred):
        def body(p, acc):
            for j in (2 * p, 2 * p + 1):
                hit = jnp.where(pred(key_ref[j]), 1.0, 0.0)
                acc = acc + jnp.sum(hit.reshape(blk // 8, 8, blk), axis=0)
            return acc
        return jnp.sum(lax.fori_loop(0, npairs, body, jnp.zeros((8, blk), F32)), axis=0, keepdims=True)

    def count_upper(cand):
        one, zero = jnp.ones((), BF16), jnp.zeros((), BF16)
        rows = 2 * SUBLANES

        def body(p, acc):
            for j in (2 * p, 2 * p + 1):
                hit = jnp.where(khi_ref[j] >= cand, one, zero)
                for r in range(blk // rows):
                    acc = acc + hit[r * rows:(r + 1) * rows, :]
            return acc
        acc = lax.fori_loop(0, npairs, body, jnp.zeros((rows, blk), BF16))
        return jnp.sum(acc.astype(F32), axis=0, keepdims=True)

    def upper_step(b, thr):
        cand = thr + lax.shift_left(jnp.int32(1), HALF_BITS - 1 - b)
        return jnp.where(count_upper(cand.astype(jnp.int16)) >= topk, cand, thr)

    def bit_step(b, thr):
        cand = thr + lax.shift_left(jnp.int32(1), 31 - b)
        return jnp.where(count(lambda key: key >= cand) >= topk, cand, thr)

    thr = lax.fori_loop(0, HALF_BITS, upper_step, jnp.full((1, blk), INT16_MIN, jnp.int32))
    thr = lax.fori_loop(HALF_BITS, 32, bit_step, lax.shift_left(thr, HALF_BITS))

    need = topk - count(lambda key: key > thr)
    upto = jnp.where(qry_io <= key_io, 1.0, 0.0).astype(BF16)

    def select_block(j, run):
        key = key_ref[j]
        tie = jnp.where(key == thr, 1.0, 0.0)
        rank = _dot(upto, tie.astype(BF16)) + run
        keep_tie = jnp.where(rank <= need, 0.0, NEG_INF)
        madd = jnp.where(key > thr, 0.0, jnp.where(key == thr, keep_tie, NEG_INF))
        return madd, run + jnp.sum(tie, axis=0, keepdims=True)

    fold_scale = _is_power_of_two(scale)
    qT = q_ref[0].T
    if fold_scale:
        qT = qT * jnp.asarray(scale, qT.dtype)
    row = lax.broadcasted_iota(jnp.int32, (LANES, 1), 0)
    q_rhs = []
    for g in range(DSA_HEADS // 2):
        pair = qT[g * LANES:(g + 1) * LANES, :]
        zero = jnp.zeros_like(pair)
        q_rhs.append(jnp.concatenate([jnp.where(row < HEAD_DIM, pair, zero),
                                      jnp.where(row >= HEAD_DIM, pair, zero)], axis=1))
    acc_ref[...] = jnp.zeros_like(acc_ref)

    def attend_block(j, carry):
        m_all, l_all, run = carry
        start = pl.multiple_of(j * blk, blk)
        kblk = k_ref[0, pl.ds(start, blk), :]
        vT = vT_ref[j]
        madd, run = select_block(j, run)
        which = jnp.minimum(i - j, 2)
        s2 = [_dot(kblk[:, g * LANES:(g + 1) * LANES], q_rhs[g]) for g in range(DSA_HEADS // 2)]
        m_out, l_out, probs, alphas = [], [], [], []
        for h in range(DSA_HEADS):
            s = s2[h // 2][:, (h % 2) * blk:(h % 2 + 1) * blk]
            if not fold_scale:
                s = s * scale
            s = s + bias_ref[h, which] + madd
            m_new = jnp.maximum(m_all[h], jnp.max(s, axis=0, keepdims=True))
            p = jnp.exp(s - m_new)
            alpha = jnp.exp(m_all[h] - m_new)
            l_out.append(alpha * l_all[h] + jnp.sum(p, axis=0, keepdims=True))
            m_out.append(m_new)
            probs.append(p.astype(BF16))
            alphas.append(alpha)
        for h in range(DSA_HEADS):
            rows = slice(h * HEAD_DIM, (h + 1) * HEAD_DIM)
            acc_ref[rows, :] = alphas[h] * acc_ref[rows, :] + _dot(vT[rows, :], probs[h])
        return tuple(m_out), tuple(l_out), run

    init = (tuple(jnp.full((1, blk), NEG_INF, F32) for _ in range(DSA_HEADS)),
            tuple(jnp.zeros((1, blk), F32) for _ in range(DSA_HEADS)), jnp.zeros((1, blk), F32))
    _, l_all, _ = lax.fori_loop(0, nblk, attend_block, init)
    for h in range(DSA_HEADS):
        rows = slice(h * HEAD_DIM, (h + 1) * HEAD_DIM)
        acc_ref[rows, :] = acc_ref[rows, :] / l_all[h]
    o_ref[0] = acc_ref[...].T.astype(o_ref.dtype)


def dsa_attention(proj, qkv_col, iq_col, proj_f32, kw_col, bias_tiles, blk=DSA_BLOCK):
    bsz, seq, _ = proj.shape
    width = DSA_HEADS * HEAD_DIM
    nblk = seq // blk
    topk = min(DSA_TOPK, seq // 4)
    iq_width = IDX_HEADS * IDX_DIM
    assert qkv_col % width == 0 and iq_col % iq_width == 0 and kw_col % LANES == 0
    assert nblk % 2 == 0 and topk <= blk
    assert nblk * blk // (2 * SUBLANES) <= 256
    qcol = qkv_col // width
    return pl.pallas_call(
        functools.partial(_dsa_kernel, blk=blk, topk=topk, scale=HEAD_DIM ** -0.5),
        grid=(bsz, nblk),
        in_specs=[pl.BlockSpec((1, blk, width), lambda b, i: (b, i, qcol)),
                  pl.BlockSpec((1, seq, width), lambda b, i: (b, 0, qcol + 1)),
                  pl.BlockSpec((1, seq, width), lambda b, i: (b, 0, qcol + 2)),
                  pl.BlockSpec((1, blk, iq_width), lambda b, i: (b, i, iq_col // iq_width)),
                  pl.BlockSpec((1, seq, LANES), lambda b, i: (b, 0, kw_col // LANES)),
                  pl.BlockSpec((1, blk, LANES), lambda b, i: (b, i, kw_col // LANES)),
                  pl.BlockSpec((DSA_HEADS, 3, blk, blk), lambda b, i: (0, 0, 0, 0))],
        out_specs=pl.BlockSpec((1, blk, width), lambda b, i: (b, i, 0)),
        out_shape=jax.ShapeDtypeStruct((bsz, seq, width), BF16),
        scratch_shapes=[pltpu.VMEM((nblk, width, blk), BF16),
                        pltpu.VMEM((nblk, blk, blk), jnp.int32),
                        pltpu.VMEM((nblk, blk, blk), jnp.int16),
                        pltpu.VMEM((width, blk), F32)],
        compiler_params=_params("parallel", "arbitrary"),
        name="dsa_attention",
    )(proj, proj, proj, proj, proj_f32, proj_f32, bias_tiles)


def hybrid_mixer(xbf, bsz, seq, w_in, rel_bias, ssm_params, d_skip, w_glu, w_branches):
    offs = [0]
    for width in IN_SPLITS:
        offs.append(offs[-1] + width)
    seg = lambda a: w_in[:, offs[a]:offs[a + 1]]
    w_bf = jnp.concatenate([seg(0), seg(3), seg(4)], axis=1).astype(BF16)
    pad = jnp.zeros((w_in.shape[0], LANES - IDX_DIM - IDX_HEADS), w_in.dtype)
    w_f32 = jnp.concatenate([seg(1), seg(2), seg(5), seg(6), pad], axis=1).astype(BF16)
    dsa_col = IN_SPLITS[0]
    iq_col = dsa_col + IN_SPLITS[3]
    ssm_col = IN_SPLITS[1]
    kw_col = ssm_col + IN_SPLITS[2]
    proj = matmul(xbf, w_bf, BF16).reshape(bsz, seq, -1)
    proj_f32 = matmul(xbf, w_f32, F32).reshape(bsz, seq, -1)

    y_sb = stick_breaking_attention(proj).reshape(bsz * seq, -1)

    y_dil = dilated_attention(proj_f32, rel_bias[:, :DIL_HEADS]).reshape(bsz * seq, -1)

    bb, a_bar, cc = _ssm_tables(*ssm_params)
    y_ssm = s5_glu(proj_f32[..., ssm_col:ssm_col + SSM_WIDTH], bb, a_bar, cc, d_skip, w_glu.astype(BF16))

    y_dsa = dsa_attention(proj, dsa_col, iq_col, proj_f32, kw_col, _dsa_bias_tiles(rel_bias[:, DIL_HEADS:], DSA_BLOCK))
    y_dsa = y_dsa.reshape(bsz * seq, -1)

    return gated_branch_merge(xbf, [y_sb, y_dil, y_ssm, y_dsa], [w.astype(BF16) for w in w_branches],
                              seg(7).astype(BF16))


def kernel(x, ln_g, ln_b, ffn1_w_up, ffn1_w_down, w_in, rel_bias, ssm_lam_re, ssm_lam_im, ssm_log_dt,
           ssm_b_re, ssm_b_im, ssm_c_re, ssm_c_im, ssm_d, ssm_w_glu, w_br_sb, w_br_dil, w_br_ssm, w_br_dsa,
           w_out, ffn2_w_up, ffn2_w_down):
    bsz, seq, d = x.shape
    xf = x.reshape(bsz * seq, d)
    xbf = xf
    up1, down1 = ffn1_w_up.astype(BF16), ffn1_w_down.astype(BF16)
    up2, down2 = ffn2_w_up.astype(BF16), ffn2_w_down.astype(BF16)
    out_w = w_out.astype(BF16)
    for l in range(DEPTH):
        h = ffn_up(xbf, up1, l)
        xf, xbf = matmul_residual_layernorm(h, down1, l, xf, ln_g[l, 0], ln_b[l, 0], MACARON)
        merged = hybrid_mixer(xbf, bsz, seq, w_in[l], rel_bias,
                              (ssm_lam_re[l], ssm_lam_im[l], ssm_log_dt[l], ssm_b_re[l], ssm_b_im[l],
                               ssm_c_re[l], ssm_c_im[l]), ssm_d[l], ssm_w_glu[l],
                              (w_br_sb[l], w_br_dil[l], w_br_ssm[l], w_br_dsa[l]))
        xf, xbf = matmul_residual_layernorm(merged, out_w, l, xf, ln_g[l, 1], ln_b[l, 1], 1.0)
        h = ffn_up(xbf, up2, l)
        xf, xbf = matmul_residual_layernorm(h, down2, l, xf, ln_g[l, 2], ln_b[l, 2], MACARON)
    return xf.reshape(bsz, seq, d)
```

```python
import functools
import math

import jax
import jax.numpy as jnp
from jax import lax
from jax.experimental import pallas as pl
from jax.experimental.pallas import tpu as pltpu

F32 = jnp.float32
BF16 = jnp.bfloat16

D_MODEL = 2048
DEPTH = 2
HEAD_DIM = 64
SB_HEADS = 8
DIL_PATTERNS = ((128, 1), (512, 4), (2048, 16))
DIL_HEADS_PER_GROUP = 4
DIL_HEADS = DIL_HEADS_PER_GROUP * len(DIL_PATTERNS)
SSM_WIDTH = 512
SSM_GROUP = 16
SSM_GROUPS = SSM_WIDTH // SSM_GROUP
SSM_STATE = 64
DSA_HEADS = 8
IDX_HEADS = 8
IDX_DIM = 64
DSA_TOPK = 256
N_BRANCH = 4
D_FF = 5632
REL_BUCKETS = 32
REL_MAX_DIST = 128
DN_ALPHA = (2.0 * DEPTH) ** 0.25
LN_EPS = 1e-5
NEG_INF = -1e30
MACARON = 0.5

IN_SPLITS = (3 * SB_HEADS * HEAD_DIM, 3 * DIL_HEADS * HEAD_DIM, SSM_WIDTH, 3 * DSA_HEADS * HEAD_DIM,
             IDX_HEADS * IDX_DIM, IDX_DIM, IDX_HEADS, N_BRANCH * D_MODEL)

LANES = 128
ATT_BLOCK = 128
DIL_UNITS = 1
DIL_MERGE_ROWS = 512
DSA_BLOCK = 256
SB_QUERY_BLOCK = 2048
SB_KEY_BLOCK = 256
SUBLANES = 8
SSM_STEPS = 32
SSM_STRIP = 512
VMEM_LIMIT = 56 * 1024 * 1024
INT_MIN = -2 ** 31
HALF_BITS = 16
INT16_MIN = -2 ** 15
SOFTPLUS_CLAMP = 80.0

_NT = (((1,), (1,)), ((), ()))


def _dot(a, b):
    return jnp.dot(a, b, preferred_element_type=F32)


def _dot_nt(a, b):
    return lax.dot_general(a, b, _NT, preferred_element_type=F32)


def _params(*sem):
    return pltpu.CompilerParams(dimension_semantics=sem, vmem_limit_bytes=VMEM_LIMIT)


def _sigmoid(x):
    return 1.0 / (1.0 + jnp.exp(-x))


def _mm_kernel(x_ref, w_ref, o_ref):
    o_ref[...] = _dot(x_ref[...], w_ref[...]).astype(o_ref.dtype)


def matmul(x, w, out_dtype, tm=512):
    m, k = x.shape
    n = w.shape[1]
    assert m % tm == 0 and n % LANES == 0
    return pl.pallas_call(
        _mm_kernel,
        grid=(m // tm,),
        in_specs=[pl.BlockSpec((tm, k), lambda i: (i, 0)),
                  pl.BlockSpec((k, n), lambda i: (0, 0), pipeline_mode=pl.Buffered(1))],
        out_specs=pl.BlockSpec((tm, n), lambda i: (i, 0)),
        out_shape=jax.ShapeDtypeStruct((m, n), out_dtype),
        compiler_params=_params("parallel"),
        name="matmul",
    )(x, w)


def _ffn_up_kernel(x_ref, wa_ref, wb_ref, o_ref):
    x = x_ref[...].astype(BF16)
    a = _dot(x, wa_ref[...])
    b = _dot(x, wb_ref[...])
    o_ref[...] = (a * _sigmoid(a) * b).astype(o_ref.dtype)


def ffn_up(x, w_up, layer, tm=1024, tn=512):
    m, k = x.shape
    f = w_up.shape[2] // 2
    assert m % tm == 0 and f % tn == 0
    nb = f // tn
    return pl.pallas_call(
        _ffn_up_kernel,
        grid=(m // tm, nb),
        in_specs=[pl.BlockSpec((tm, k), lambda i, j: (i, 0)),
                  pl.BlockSpec((None, k, tn), lambda i, j: (layer, 0, j)),
                  pl.BlockSpec((None, k, tn), lambda i, j: (layer, 0, j + nb))],
        out_specs=pl.BlockSpec((tm, tn), lambda i, j: (i, j)),
        out_shape=jax.ShapeDtypeStruct((m, f), BF16),
        compiler_params=_params("parallel", "arbitrary"),
        name="ffn_up",
    )(x, w_up, w_up)


def _mm_res_ln_kernel(h_ref, w_ref, x_ref, g_ref, b_ref, o_ref, obf_ref, *, scale):
    y = DN_ALPHA * x_ref[...] + scale * _dot(h_ref[...], w_ref[...])
    mu = jnp.mean(y, axis=-1, keepdims=True)
    yc = y - mu
    var = jnp.mean(yc * yc, axis=-1, keepdims=True)
    out = yc * lax.rsqrt(var + LN_EPS) * g_ref[...] + b_ref[...]
    o_ref[...] = out
    obf_ref[...] = out.astype(BF16)


def matmul_residual_layernorm(h, w, layer, x, g, b, scale, tm=256):
    m, k = h.shape
    n = w.shape[2]
    assert m % tm == 0
    return pl.pallas_call(
        functools.partial(_mm_res_ln_kernel, scale=scale),
        grid=(m // tm,),
        in_specs=[pl.BlockSpec((tm, k), lambda i: (i, 0)),
                  pl.BlockSpec((None, k, n), lambda i: (layer, 0, 0), pipeline_mode=pl.Buffered(1)),
                  pl.BlockSpec((tm, n), lambda i: (i, 0)),
                  pl.BlockSpec((1, n), lambda i: (0, 0)),
                  pl.BlockSpec((1, n), lambda i: (0, 0))],
        out_specs=[pl.BlockSpec((tm, n), lambda i: (i, 0)),
                   pl.BlockSpec((tm, n), lambda i: (i, 0))],
        out_shape=[jax.ShapeDtypeStruct((m, n), F32), jax.ShapeDtypeStruct((m, n), BF16)],
        compiler_params=_params("parallel"),
        name="matmul_residual_layernorm",
    )(h, w, x, g.reshape(1, n), b.reshape(1, n))


def _gated_merge_kernel(x_ref, ysb_ref, ydil_ref, yssm_ref, ydsa_ref, wsb_ref, wdil_ref, wssm_ref, wdsa_ref,
                        g0_ref, g1_ref, g2_ref, g3_ref, o_ref):
    x = x_ref[...]
    acc = _sigmoid(_dot(x, g0_ref[...])) * _dot(ysb_ref[...], wsb_ref[...])
    acc += _sigmoid(_dot(x, g1_ref[...])) * _dot(ydil_ref[...], wdil_ref[...])
    acc += _sigmoid(_dot(x, g2_ref[...])) * _dot(yssm_ref[...], wssm_ref[...])
    acc += _sigmoid(_dot(x, g3_ref[...])) * _dot(ydsa_ref[...], wdsa_ref[...])
    o_ref[...] = acc.astype(o_ref.dtype)


def gated_branch_merge(x, ys, ws, w_gate, tm=1024, tn=512):
    m, k = x.shape
    n = ws[0].shape[1]
    assert m % tm == 0 and n % tn == 0
    nb = n // tn
    y_specs = [pl.BlockSpec((tm, y.shape[1]), lambda i, j: (i, 0)) for y in ys]
    w_specs = [pl.BlockSpec((w.shape[0], tn), lambda i, j: (0, j)) for w in ws]
    g_specs = [pl.BlockSpec((k, tn), functools.partial(lambda i, j, br: (0, j + br * nb), br=br))
               for br in range(N_BRANCH)]
    return pl.pallas_call(
        _gated_merge_kernel,
        grid=(m // tm, nb),
        in_specs=[pl.BlockSpec((tm, k), lambda i, j: (i, 0))] + y_specs + w_specs + g_specs,
        out_specs=pl.BlockSpec((tm, tn), lambda i, j: (i, j)),
        out_shape=jax.ShapeDtypeStruct((m, n), BF16),
        compiler_params=_params("parallel", "arbitrary"),
        name="gated_branch_merge",
    )(x, *ys, *ws, w_gate, w_gate, w_gate, w_gate)


def _is_power_of_two(x):
    return math.frexp(x)[0] == 0.5


def _sb_kernel(q_ref, k_ref, v_ref, o_ref, vT_ref, acc_ref, *, qblk, kblk, scale):
    i = pl.program_id(2)
    ratio = qblk // kblk
    fold_scale = _is_power_of_two(scale)

    @pl.when(i == 0)
    def _():
        for j in range(vT_ref.shape[0]):
            vT_ref[j] = v_ref[0, j * kblk:(j + 1) * kblk, :].T

    qT = q_ref[0].T
    if fold_scale:
        qT = qT * jnp.asarray(scale, qT.dtype)
    row = lax.broadcasted_iota(jnp.int32, (LANES, 1), 0)
    zero = jnp.zeros_like(qT)
    q_rhs = jnp.concatenate([jnp.where(row < HEAD_DIM, qT, zero), jnp.where(row >= HEAD_DIM, qT, zero)], axis=1)
    key_io = lax.broadcasted_iota(jnp.int32, (kblk, kblk), 0)
    qry_io = lax.broadcasted_iota(jnp.int32, (kblk, kblk), 1)
    later = jnp.where(qry_io > key_io, 1.0, 0.0).astype(BF16)
    acc_ref[...] = jnp.zeros_like(acc_ref)
    nsub = 2 * ratio

    def block(j, carry, diag_sub):
        start = pl.multiple_of(j * kblk, kblk)
        z_all = _dot(k_ref[0, pl.ds(start, kblk), :], q_rhs)
        vT = vT_ref[j]
        strict = key_io < qry_io
        stage = []
        for c in range(nsub):
            h, sub = divmod(c, ratio)
            if diag_sub is not None and sub < diag_sub:
                stage.append(None)
                continue
            masked = diag_sub is not None and sub == diag_sub
            z = z_all[:, c * kblk:(c + 1) * kblk]
            if not fold_scale:
                z = z * scale
            sp = jnp.maximum(jnp.log(1.0 + jnp.exp(jnp.minimum(z, SOFTPLUS_CLAMP))), z)
            log_1mb = jnp.where(strict, -sp, 0.0) if masked else -sp
            suf = _dot(later, log_1mb.astype(BF16))
            stage.append((z - sp, suf, jnp.sum(log_1mb, axis=0, keepdims=True), masked))
        new_carry = []
        for c in range(nsub):
            h, sub = divmod(c, ratio)
            if stage[c] is None:
                new_carry.append(carry[c])
                continue
            log_beta, suf, colsum, masked = stage[c]
            att = jnp.exp(log_beta + suf + carry[c])
            if masked:
                att = jnp.where(strict, att, 0.0)
            rows = slice(h * HEAD_DIM, (h + 1) * HEAD_DIM)
            cols = slice(sub * kblk, (sub + 1) * kblk)
            acc_ref[rows, cols] += _dot(vT[rows, :], att.astype(BF16))
            new_carry.append(carry[c] + colsum)
        return tuple(new_carry)

    carry = tuple(jnp.zeros((1, kblk), F32) for _ in range(nsub))
    for sub in reversed(range(ratio)):
        carry = block(i * ratio + sub, carry, sub)
    lax.fori_loop(0, i * ratio, lambda jj, c: block(i * ratio - 1 - jj, c, None), carry)
    o_ref[0] = acc_ref[...].T.astype(o_ref.dtype)


def stick_breaking_attention(qkv, col0=0, qblk=SB_QUERY_BLOCK, kblk=SB_KEY_BLOCK):
    bsz, seq, _ = qkv.shape
    width = SB_HEADS * HEAD_DIM
    npair = width // LANES
    qblk = min(qblk, seq)
    assert seq % qblk == 0 and qblk % kblk == 0 and col0 % LANES == 0
    qcol = col0 // LANES
    return pl.pallas_call(
        functools.partial(_sb_kernel, qblk=qblk, kblk=kblk, scale=HEAD_DIM ** -0.5),
        grid=(bsz, npair, seq // qblk),
        in_specs=[pl.BlockSpec((1, qblk, LANES), lambda b, p, i: (b, i, qcol + p)),
                  pl.BlockSpec((1, seq, LANES), lambda b, p, i: (b, 0, qcol + npair + p)),
                  pl.BlockSpec((1, seq, LANES), lambda b, p, i: (b, 0, qcol + 2 * npair + p))],
        out_specs=pl.BlockSpec((1, qblk, LANES), lambda b, p, i: (b, i, p)),
        out_shape=jax.ShapeDtypeStruct((bsz, seq, width), BF16),
        scratch_shapes=[pltpu.VMEM((seq // kblk, LANES, kblk), BF16),
                        pltpu.VMEM((LANES, qblk), F32)],
        compiler_params=_params("parallel", "parallel", "arbitrary"),
        name="stick_breaking_attention",
    )(qkv, qkv, qkv)


def _t5_bucket(dist):
    max_exact = REL_BUCKETS // 2
    d = jnp.maximum(dist, 1).astype(F32)
    large = max_exact + (jnp.log(d / max_exact) / math.log(REL_MAX_DIST / max_exact)
                         * (REL_BUCKETS - max_exact)).astype(jnp.int32)
    large = jnp.minimum(large, REL_BUCKETS - 1)
    return jnp.where(dist < max_exact, dist, large)


def _bias_of_distance(rel_bias, dist):
    one_hot = jax.nn.one_hot(_t5_bucket(dist), REL_BUCKETS, dtype=F32)
    return jnp.einsum("...b,bh->...h", one_hot, rel_bias.astype(F32), precision=lax.Precision.HIGHEST)


def _dilated_bias_tiles(rel_bias_group, dil, blk):
    a = jnp.arange(blk)[:, None]
    cc = jnp.arange(2 * blk)[None, :]
    step = blk + a - cc
    valid = (step >= 0) & (step <= blk)
    bias = _bias_of_distance(rel_bias_group, dil * jnp.clip(step, 0, blk))
    tile = jnp.where(valid[..., None], bias, NEG_INF)
    return jnp.transpose(tile, (2, 0, 1))


def _dsa_bias_tiles(rel_bias_dsa, blk):
    c = jnp.arange(blk)[:, None]
    a = jnp.arange(blk)[None, :]
    own = jnp.where((a - c >= 0)[..., None], _bias_of_distance(rel_bias_dsa, jnp.maximum(a - c, 0)), NEG_INF)
    prev = _bias_of_distance(rel_bias_dsa, blk + a - c)
    half = REL_BUCKETS // 2
    assert half + int(math.log((blk + 1) / half) / math.log(REL_MAX_DIST / half) * (REL_BUCKETS - half)) >= REL_BUCKETS - 1
    far = jnp.broadcast_to(rel_bias_dsa[REL_BUCKETS - 1].astype(F32), prev.shape)
    return jnp.transpose(jnp.stack([own, prev, far]), (3, 0, 1, 2))


def _dil_kernel(*refs, blk, dils, nbs, units, scale):
    ngroups = len(dils)
    nslab = 2 * ngroups
    q_refs, k_refs, v_refs = refs[:nslab], refs[nslab:2 * nslab], refs[2 * nslab:3 * nslab]
    bias_refs = refs[3 * nslab:3 * nslab + ngroups]
    y_ref, o_ref, lse_ref = refs[3 * nslab + ngroups:]
    step = pl.program_id(1)
    lane = lax.broadcasted_iota(jnp.int32, (1, LANES), 1)
    fold_scale = _is_power_of_two(scale)

    work = []
    for g, u in [(g, u) for g in range(ngroups) for u in range(units)]:
        dil, nb = dils[g], nbs[g]
        unit = step * units + u
        c, i = unit // nb, unit % nb

        def rows(block, dil=dil, c=c):
            start = c + dil * blk * block
            return pl.ds(start, blk, stride=dil) if dil > 1 else pl.ds(pl.multiple_of(start, blk), blk)

        cur, prev = rows(i), rows(jnp.maximum(i - 1, 0))
        for half in range(2):
            slab = 2 * g + half
            q = q_refs[slab][0, cur, :]
            if fold_scale:
                q = q * scale
            q = q.astype(BF16)
            kc, kp = k_refs[slab][0, cur, :].astype(BF16), k_refs[slab][0, prev, :].astype(BF16)
            zero = jnp.zeros_like(q)
            scores = []
            for e in range(2):
                head = (lane < HEAD_DIM) if e == 0 else (lane >= HEAD_DIM)
                qm = jnp.where(head, q, zero)
                scores.append((_dot_nt(qm, kp), _dot_nt(qm, kc)))
            work.append((g, i, cur, prev, half, scores))
    soft = []
    for g, i, cur, prev, half, scores in work:
        has_prev = i > 0
        parts = []
        for e in range(2):
            h = 2 * half + e
            s_prev, s_cur = scores[e]
            if not fold_scale:
                s_prev, s_cur = s_prev * scale, s_cur * scale
            s_prev = jnp.where(has_prev, s_prev + bias_refs[g][h, :, :blk], NEG_INF)
            s_cur = s_cur + bias_refs[g][h, :, blk:]
            m = jnp.maximum(jnp.max(s_prev, axis=-1, keepdims=True), jnp.max(s_cur, axis=-1, keepdims=True))
            parts.append((jnp.exp(s_prev - m).astype(BF16), jnp.exp(s_cur - m).astype(BF16), m))
        soft.append(parts)
    ones = jnp.ones((blk, LANES), BF16)
    for (g, i, cur, prev, half, scores), parts in zip(work, soft):
        v_ref = v_refs[2 * g + half]
        vc = jnp.concatenate([v_ref[0, cur, :].astype(BF16), ones], axis=1)
        vp = jnp.concatenate([v_ref[0, prev, :].astype(BF16), ones], axis=1)
        out = jnp.zeros((blk, LANES), F32)
        lse_b = jnp.zeros((blk, LANES), F32)
        for e in range(2):
            head = (lane < HEAD_DIM) if e == 0 else (lane >= HEAD_DIM)
            p_prev, p_cur, m = parts[e]
            o2 = _dot(p_prev, vp) + _dot(p_cur, vc)
            denom = o2[:, LANES:]
            out = jnp.where(head, o2[:, :LANES] / denom, out)
            lse_b = jnp.where(head, m + jnp.log(denom), lse_b)
        o_ref[g, half, cur, :] = out
        lse_ref[g, half, cur, :] = lse_b

    @pl.when(step == pl.num_programs(1) - 1)
    def _():
        def merge(r, _):
            rows_ = pl.ds(pl.multiple_of(r * DIL_MERGE_ROWS, DIL_MERGE_ROWS), DIL_MERGE_ROWS)
            for half in range(2):
                lses = [lse_ref[g, half, rows_, :] for g in range(ngroups)]
                top = functools.reduce(jnp.maximum, lses)
                ws = [jnp.exp(x - top) for x in lses]
                num = functools.reduce(lambda a, b: a + b, [ws[g] * o_ref[g, half, rows_, :] for g in range(ngroups)])
                den = functools.reduce(lambda a, b: a + b, ws)
                y_ref[0, rows_, half * LANES:(half + 1) * LANES] = (num / den).astype(y_ref.dtype)
            return 0

        lax.fori_loop(0, o_ref.shape[2] // DIL_MERGE_ROWS, merge, 0)


def dilated_attention(qkv, rel_bias_dil, blk=ATT_BLOCK, units=DIL_UNITS):
    bsz, seq, _ = qkv.shape
    hpg = DIL_HEADS_PER_GROUP
    assert hpg * HEAD_DIM == 2 * LANES and seq % DIL_MERGE_ROWS == 0
    ngroups = len(DIL_PATTERNS)
    nslab = DIL_HEADS * HEAD_DIM // LANES
    dils = tuple(d for _, d in DIL_PATTERNS)
    nbs = tuple(seq // d // blk for d in dils)
    nsteps = dils[0] * nbs[0]
    assert all(w // d == blk for w, d in DIL_PATTERNS) and all(d * n == nsteps for d, n in zip(dils, nbs))
    assert nsteps % units == 0
    tiles = [_dilated_bias_tiles(rel_bias_dil[:, g * hpg:(g + 1) * hpg], d, blk) for g, d in enumerate(dils)]
    once = pl.Buffered(1)
    slab = lambda col: pl.BlockSpec((1, seq, LANES), functools.partial(lambda b, s, col: (b, 0, col), col=col),
                                    pipeline_mode=once)
    in_specs = [slab(section * nslab + s) for section in range(3) for s in range(nslab)]
    in_specs += [pl.BlockSpec(t.shape, lambda b, s: (0, 0, 0), pipeline_mode=once) for t in tiles]
    return pl.pallas_call(
        functools.partial(_dil_kernel, blk=blk, dils=dils, nbs=nbs, units=units, scale=HEAD_DIM ** -0.5),
        grid=(bsz, nsteps // units),
        in_specs=in_specs,
        out_specs=pl.BlockSpec((1, seq, 2 * LANES), lambda b, s: (b, 0, 0)),
        out_shape=jax.ShapeDtypeStruct((bsz, seq, 2 * LANES), BF16),
        scratch_shapes=[pltpu.VMEM((ngroups, 2, seq, LANES), F32),
                        pltpu.VMEM((ngroups, 2, seq, LANES), F32)],
        compiler_params=_params("parallel", "arbitrary"),
        name="dilated_attention",
    )(*([qkv] * (3 * nslab)), *tiles)


def _ssm_kernel(u_ref, bb_ref, a_ref, cc_ref, d_ref, wg_ref, o_ref, x_ref, carry_ref, *, steps, strip):
    @pl.when(pl.program_id(1) == 0)
    def _():
        carry_ref[...] = jnp.zeros_like(carry_ref)

    u = u_ref[0]
    ub = u.astype(BF16)
    width = u.shape[1]
    half = width // 2
    nstates = x_ref.shape[1] // 2
    hs = nstates // 2
    for part in range(2):
        bu = _dot(ub[:, part * half:(part + 1) * half], bb_ref[part])
        x_ref[:, part * hs:(part + 1) * hs] = bu[:, :hs]
        x_ref[:, nstates + part * hs:nstates + (part + 1) * hs] = bu[:, hs:]
    for s in range(nstates // strip):
        re = slice(s * strip, (s + 1) * strip)
        im = slice(nstates + s * strip, nstates + (s + 1) * strip)
        ar = jnp.broadcast_to(a_ref[0:1, re], (SUBLANES, strip))
        ai = jnp.broadcast_to(a_ref[1:2, re], (SUBLANES, strip))
        xr = carry_ref[:, re]
        xi = carry_ref[:, im]
        for t in range(steps):
            rows = slice(t * SUBLANES, (t + 1) * SUBLANES)
            xr, xi = ar * xr - ai * xi + x_ref[rows, re], ar * xi + ai * xr + x_ref[rows, im]
            x_ref[rows, re] = xr
            x_ref[rows, im] = xi
        carry_ref[:, re] = xr
        carry_ref[:, im] = xi
    y_parts = []
    for part in range(2):
        xr = x_ref[:, part * hs:(part + 1) * hs].astype(BF16)
        xi = x_ref[:, nstates + part * hs:nstates + (part + 1) * hs].astype(BF16)
        y_parts.append(_dot(xr, cc_ref[0, part]) + _dot(xi, cc_ref[1, part]))
    y = jnp.concatenate(y_parts, axis=1) + d_ref[...] * u
    y = 0.5 * y * (1.0 + jnp.tanh(math.sqrt(2.0 / math.pi) * (y + 0.044715 * (y * y * y))))
    z = _dot(y.astype(BF16), wg_ref[...])
    o_ref[0] = (z[:, :width] * _sigmoid(z[:, width:])).astype(o_ref.dtype)


def _block_diag(blocks):
    g, r, c = blocks.shape
    eye = jnp.eye(g, dtype=blocks.dtype)
    return (eye[:, None, :, None] * blocks[:, :, None, :]).reshape(g * r, g * c)


def _ssm_tables(lam_re, lam_im, log_dt, b_re, b_im, c_re, c_im):
    lr, li = lam_re.astype(F32), lam_im.astype(F32)
    dt = jnp.exp(log_dt.astype(F32))[:, None]
    mag = jnp.exp(lr * dt)
    a_re, a_im = mag * jnp.cos(li * dt), mag * jnp.sin(li * dt)
    den = lr * lr + li * li
    f_re = ((a_re - 1.0) * lr + a_im * li) / den
    f_im = (a_im * lr - (a_re - 1.0) * li) / den
    br, bi = b_re.astype(F32), b_im.astype(F32)
    bb_re = _block_diag(jnp.transpose(f_re[..., None] * br - f_im[..., None] * bi, (0, 2, 1)))
    bb_im = _block_diag(jnp.transpose(f_re[..., None] * bi + f_im[..., None] * br, (0, 2, 1)))
    cc_re = _block_diag(jnp.transpose(c_re.astype(F32), (0, 2, 1)))
    cc_im = -_block_diag(jnp.transpose(c_im.astype(F32), (0, 2, 1)))
    nstates = SSM_GROUPS * SSM_STATE
    hw, hs = SSM_WIDTH // 2, nstates // 2
    bb = jnp.stack([jnp.concatenate([bb_re[p * hw:(p + 1) * hw, p * hs:(p + 1) * hs],
                                     bb_im[p * hw:(p + 1) * hw, p * hs:(p + 1) * hs]], axis=1) for p in range(2)])
    cc = jnp.stack([jnp.stack([m[p * hs:(p + 1) * hs, p * hw:(p + 1) * hw] for p in range(2)])
                    for m in (cc_re, cc_im)])
    a = jnp.stack([a_re.reshape(nstates), a_im.reshape(nstates)])
    return bb.astype(BF16), a, cc.astype(BF16)


def s5_glu(u, bb, a, cc, d_skip, w_glu, steps=SSM_STEPS, strip=SSM_STRIP):
    bsz, seq, width = u.shape
    nstates = a.shape[1]
    assert bsz % SUBLANES == 0 and seq % steps == 0 and nstates % strip == 0
    ngrp = bsz // SUBLANES
    rows = steps * SUBLANES
    ut = u.reshape(ngrp, SUBLANES, seq, width).transpose(0, 2, 1, 3).reshape(ngrp, seq * SUBLANES, width)
    const = lambda *shape: pl.BlockSpec(shape, lambda g, c: (0,) * len(shape))
    out = pl.pallas_call(
        functools.partial(_ssm_kernel, steps=steps, strip=strip),
        grid=(ngrp, seq // steps),
        in_specs=[pl.BlockSpec((1, rows, width), lambda g, c: (g, c, 0)),
                  const(*bb.shape), const(*a.shape), const(*cc.shape),
                  const(1, width), const(*w_glu.shape)],
        out_specs=pl.BlockSpec((1, rows, width), lambda g, c: (g, c, 0)),
        out_shape=jax.ShapeDtypeStruct((ngrp, seq * SUBLANES, width), BF16),
        scratch_shapes=[pltpu.VMEM((rows, 2 * nstates), F32),
                        pltpu.VMEM((SUBLANES, 2 * nstates), F32)],
        compiler_params=_params("parallel", "arbitrary"),
        name="s5_glu",
    )(ut, bb, a, cc, d_skip.reshape(1, width).astype(F32), w_glu)
    return out.reshape(ngrp, seq, SUBLANES, width).transpose(0, 2, 1, 3).reshape(bsz * seq, width)


def _dsa_kernel(q_ref, k_ref, v_ref, iq_ref, ik_ref, ikq_ref, bias_ref, o_ref,
                vT_ref, key_ref, khi_ref, acc_ref, *, blk, topk, scale):
    i = pl.program_id(1)
    nblk = i + 1
    key_io = lax.broadcasted_iota(jnp.int32, (blk, blk), 0)
    qry_io = lax.broadcasted_iota(jnp.int32, (blk, blk), 1)

    @pl.when(i == 0)
    def _():
        for j in range(vT_ref.shape[0]):
            vT_ref[j] = v_ref[0, j * blk:(j + 1) * blk, :].T

    iqT = iq_ref[0].T
    idx_rhs = jnp.concatenate([iqT[h * IDX_DIM:(h + 1) * IDX_DIM, :] for h in range(IDX_HEADS)], axis=1)
    iw = ikq_ref[0].T[IDX_DIM:IDX_DIM + IDX_HEADS, :]

    def score_block(j, _):
        start = pl.multiple_of(j * blk, blk)
        kk = ik_ref[0, pl.ds(start, blk), :][:, :IDX_DIM].astype(BF16)
        d = _dot(kk, idx_rhs)
        sc = jnp.zeros((blk, blk), F32)
        for h in range(IDX_HEADS):
            sc = sc + iw[h:h + 1, :] * jnp.maximum(d[:, h * blk:(h + 1) * blk], 0.0)
        sc = jnp.where((key_io + j * blk) <= (qry_io + i * blk), sc, NEG_INF)
        bits = pltpu.bitcast(sc, jnp.int32)
        key = bits ^ ((bits >> 31) & 0x7FFFFFFF)
        key_ref[j] = key
        khi_ref[j] = (key >> HALF_BITS).astype(jnp.int16)
        return 0

    lax.fori_loop(0, nblk, score_block, 0)

    @pl.when(nblk % 2 == 1)
    def _():
        key_ref[nblk] = jnp.full((blk, blk), INT_MIN, jnp.int32)
        khi_ref[nblk] = jnp.full((blk, blk), INT16_MIN, jnp.int16)

    npairs = (nblk + 1) // 2

    def count(pred):
        def body(p, acc):
            for j in (2 * p, 2 * p + 1):
                hit = jnp.where(pred(key_ref[j]), 1.0, 0.0)
                acc = acc + jnp.sum(hit.reshape(blk // 8, 8, blk), axis=0)
            return acc
        return jnp.sum(lax.fori_loop(0, npairs, body, jnp.zeros((8, blk), F32)), axis=0, keepdims=True)

    def count_half(cand):
        one, zero = jnp.ones((), BF16), jnp.zeros((), BF16)
        rows = 2 * SUBLANES

        def body(p, acc):
            for j in (2 * p, 2 * p + 1):
                hit = jnp.where(khi_ref[j] >= cand, one, zero)
                for r in range(blk // rows):
                    acc = acc + hit[r * rows:(r + 1) * rows, :]
            return acc
        acc = lax.fori_loop(0, npairs, body, jnp.zeros((rows, blk), BF16))
        return jnp.sum(acc.astype(F32), axis=0, keepdims=True)

    def half_step(b, thr):
        cand = thr + lax.shift_left(jnp.int32(1), HALF_BITS - 1 - b)
        return jnp.where(count_half(cand.astype(jnp.int16)) >= topk, cand, thr)

    start16 = jnp.full((1, blk), INT16_MIN, jnp.int32)
    thr_hi = lax.fori_loop(0, HALF_BITS, half_step, start16)
    thr_hi16 = thr_hi.astype(jnp.int16)

    def lower_halves(j, _):
        hi = khi_ref[j]
        lo = ((key_ref[j] & 0xFFFF) + INT16_MIN).astype(jnp.int16)
        khi_ref[j] = jnp.where(hi > thr_hi16, jnp.int16(-INT16_MIN - 1), jnp.where(hi == thr_hi16, lo, jnp.int16(INT16_MIN)))
        return 0

    lax.fori_loop(0, 2 * npairs, lower_halves, 0)
    thr_lo = lax.fori_loop(0, HALF_BITS, half_step, start16)
    thr = lax.shift_left(thr_hi, HALF_BITS) + (thr_lo - INT16_MIN)

    need = topk - count(lambda key: key > thr)
    upto = jnp.where(qry_io <= key_io, 1.0, 0.0).astype(BF16)

    def select_block(j, run):
        key = key_ref[j]
        tie = jnp.where(key == thr, 1.0, 0.0)
        rank = _dot(upto, tie.astype(BF16)) + run
        keep_tie = jnp.where(rank <= need, 0.0, NEG_INF)
        madd = jnp.where(key > thr, 0.0, jnp.where(key == thr, keep_tie, NEG_INF))
        return madd, run + jnp.sum(tie, axis=0, keepdims=True)

    fold_scale = _is_power_of_two(scale)
    qT = q_ref[0].T
    if fold_scale:
        qT = qT * jnp.asarray(scale, qT.dtype)
    row = lax.broadcasted_iota(jnp.int32, (LANES, 1), 0)
    q_rhs = []
    for g in range(DSA_HEADS // 2):
        pair = qT[g * LANES:(g + 1) * LANES, :]
        zero = jnp.zeros_like(pair)
        q_rhs.append(jnp.concatenate([jnp.where(row < HEAD_DIM, pair, zero),
                                      jnp.where(row >= HEAD_DIM, pair, zero)], axis=1))
    acc_ref[...] = jnp.zeros_like(acc_ref)

    def attend_block(j, carry):
        m_all, l_all, run = carry
        start = pl.multiple_of(j * blk, blk)
        kblk = k_ref[0, pl.ds(start, blk), :]
        vT = vT_ref[j]
        madd, run = select_block(j, run)
        which = jnp.minimum(i - j, 2)
        s2 = [_dot(kblk[:, g * LANES:(g + 1) * LANES], q_rhs[g]) for g in range(DSA_HEADS // 2)]
        m_out, l_out, probs, alphas = [], [], [], []
        for h in range(DSA_HEADS):
            s = s2[h // 2][:, (h % 2) * blk:(h % 2 + 1) * blk]
            if not fold_scale:
                s = s * scale
            s = s + bias_ref[h, which] + madd
            m_new = jnp.maximum(m_all[h], jnp.max(s, axis=0, keepdims=True))
            p = jnp.exp(s - m_new)
            alpha = jnp.exp(m_all[h] - m_new)
            l_out.append(alpha * l_all[h] + jnp.sum(p, axis=0, keepdims=True))
            m_out.append(m_new)
            probs.append(p.astype(BF16))
            alphas.append(alpha)
        for h in range(DSA_HEADS):
            rows = slice(h * HEAD_DIM, (h + 1) * HEAD_DIM)
            acc_ref[rows, :] = alphas[h] * acc_ref[rows, :] + _dot(vT[rows, :], probs[h])
        return tuple(m_out), tuple(l_out), run

    init = (tuple(jnp.full((1, blk), NEG_INF, F32) for _ in range(DSA_HEADS)),
            tuple(jnp.zeros((1, blk), F32) for _ in range(DSA_HEADS)), jnp.zeros((1, blk), F32))
    _, l_all, _ = lax.fori_loop(0, nblk, attend_block, init)
    for h in range(DSA_HEADS):
        rows = slice(h * HEAD_DIM, (h + 1) * HEAD_DIM)
        acc_ref[rows, :] = acc_ref[rows, :] / l_all[h]
    o_ref[0] = acc_ref[...].T.astype(o_ref.dtype)


def dsa_attention(proj, qkv_col, iq_col, proj_f32, kw_col, bias_tiles, blk=DSA_BLOCK):
    bsz, seq, _ = proj.shape
    width = DSA_HEADS * HEAD_DIM
    nblk = seq // blk
    topk = min(DSA_TOPK, seq // 4)
    iq_width = IDX_HEADS * IDX_DIM
    assert qkv_col % width == 0 and iq_col % iq_width == 0 and kw_col % LANES == 0
    assert nblk % 2 == 0 and topk <= blk
    assert nblk * blk // (2 * SUBLANES) <= 256
    qcol = qkv_col // width
    return pl.pallas_call(
        functools.partial(_dsa_kernel, blk=blk, topk=topk, scale=HEAD_DIM ** -0.5),
        grid=(bsz, nblk),
        in_specs=[pl.BlockSpec((1, blk, width), lambda b, i: (b, i, qcol)),
                  pl.BlockSpec((1, seq, width), lambda b, i: (b, 0, qcol + 1)),
                  pl.BlockSpec((1, seq, width), lambda b, i: (b, 0, qcol + 2)),
                  pl.BlockSpec((1, blk, iq_width), lambda b, i: (b, i, iq_col // iq_width)),
                  pl.BlockSpec((1, seq, LANES), lambda b, i: (b, 0, kw_col // LANES)),
                  pl.BlockSpec((1, blk, LANES), lambda b, i: (b, i, kw_col // LANES)),
                  pl.BlockSpec((DSA_HEADS, 3, blk, blk), lambda b, i: (0, 0, 0, 0))],
        out_specs=pl.BlockSpec((1, blk, width), lambda b, i: (b, i, 0)),
        out_shape=jax.ShapeDtypeStruct((bsz, seq, width), BF16),
        scratch_shapes=[pltpu.VMEM((nblk, width, blk), BF16),
                        pltpu.VMEM((nblk, blk, blk), jnp.int32),
                        pltpu.VMEM((nblk, blk, blk), jnp.int16),
                        pltpu.VMEM((width, blk), F32)],
        compiler_params=_params("parallel", "arbitrary"),
        name="dsa_attention",
    )(proj, proj, proj, proj, proj_f32, proj_f32, bias_tiles)


def hybrid_mixer(xbf, bsz, seq, w_in, rel_bias, ssm_params, d_skip, w_glu, w_branches):
    offs = [0]
    for width in IN_SPLITS:
        offs.append(offs[-1] + width)
    seg = lambda a: w_in[:, offs[a]:offs[a + 1]]
    w_bf = jnp.concatenate([seg(0), seg(3), seg(4)], axis=1).astype(BF16)
    pad = jnp.zeros((w_in.shape[0], LANES - IDX_DIM - IDX_HEADS), w_in.dtype)
    w_f32 = jnp.concatenate([seg(1), seg(2), seg(5), seg(6), pad], axis=1).astype(BF16)
    dsa_col = IN_SPLITS[0]
    iq_col = dsa_col + IN_SPLITS[3]
    ssm_col = IN_SPLITS[1]
    kw_col = ssm_col + IN_SPLITS[2]
    proj = matmul(xbf, w_bf, BF16).reshape(bsz, seq, -1)
    proj_f32 = matmul(xbf, w_f32, F32).reshape(bsz, seq, -1)

    y_sb = stick_breaking_attention(proj).reshape(bsz * seq, -1)

    y_dil = dilated_attention(proj_f32, rel_bias[:, :DIL_HEADS]).reshape(bsz * seq, -1)

    bb, a_bar, cc = _ssm_tables(*ssm_params)
    y_ssm = s5_glu(proj_f32[..., ssm_col:ssm_col + SSM_WIDTH], bb, a_bar, cc, d_skip, w_glu.astype(BF16))

    y_dsa = dsa_attention(proj, dsa_col, iq_col, proj_f32, kw_col, _dsa_bias_tiles(rel_bias[:, DIL_HEADS:], DSA_BLOCK))
    y_dsa = y_dsa.reshape(bsz * seq, -1)

    return gated_branch_merge(xbf, [y_sb, y_dil, y_ssm, y_dsa], [w.astype(BF16) for w in w_branches],
                              seg(7).astype(BF16))


def kernel(x, ln_g, ln_b, ffn1_w_up, ffn1_w_down, w_in, rel_bias, ssm_lam_re, ssm_lam_im, ssm_log_dt,
           ssm_b_re, ssm_b_im, ssm_c_re, ssm_c_im, ssm_d, ssm_w_glu, w_br_sb, w_br_dil, w_br_ssm, w_br_dsa,
           w_out, ffn2_w_up, ffn2_w_down):
    bsz, seq, d = x.shape
    xf = x.reshape(bsz * seq, d)
    xbf = xf
    up1, down1 = ffn1_w_up.astype(BF16), ffn1_w_down.astype(BF16)
    up2, down2 = ffn2_w_up.astype(BF16), ffn2_w_down.astype(BF16)
    out_w = w_out.astype(BF16)
    for l in range(DEPTH):
        h = ffn_up(xbf, up1, l)
        xf, xbf = matmul_residual_layernorm(h, down1, l, xf, ln_g[l, 0], ln_b[l, 0], MACARON)
        merged = hybrid_mixer(xbf, bsz, seq, w_in[l], rel_bias,
                              (ssm_lam_re[l], ssm_lam_im[l], ssm_log_dt[l], ssm_b_re[l], ssm_b_im[l],
                               ssm_c_re[l], ssm_c_im[l]), ssm_d[l], ssm_w_glu[l],
                              (w_br_sb[l], w_br_dil[l], w_br_ssm[l], w_br_dsa[l]))
        xf, xbf = matmul_residual_layernorm(merged, out_w, l, xf, ln_g[l, 1], ln_b[l, 1], 1.0)
        h = ffn_up(xbf, up2, l)
        xf, xbf = matmul_residual_layernorm(h, down2, l, xf, ln_g[l, 2], ln_b[l, 2], MACARON)
    return xf.reshape(bsz, seq, d)
```

```python
import functools
import math

import jax
import jax.numpy as jnp
from jax import lax
from jax.experimental import pallas as pl
from jax.experimental.pallas import tpu as pltpu

F32 = jnp.float32
BF16 = jnp.bfloat16

D_MODEL = 2048
DEPTH = 2
HEAD_DIM = 64
SB_HEADS = 8
DIL_PATTERNS = ((128, 1), (512, 4), (2048, 16))
DIL_HEADS_PER_GROUP = 4
DIL_HEADS = DIL_HEADS_PER_GROUP * len(DIL_PATTERNS)
SSM_WIDTH = 512
SSM_GROUP = 16
SSM_GROUPS = SSM_WIDTH // SSM_GROUP
SSM_STATE = 64
DSA_HEADS = 8
IDX_HEADS = 8
IDX_DIM = 64
DSA_TOPK = 256
N_BRANCH = 4
D_FF = 5632
REL_BUCKETS = 32
REL_MAX_DIST = 128
DN_ALPHA = (2.0 * DEPTH) ** 0.25
LN_EPS = 1e-5
NEG_INF = -1e30
MACARON = 0.5

IN_SPLITS = (3 * SB_HEADS * HEAD_DIM, 3 * DIL_HEADS * HEAD_DIM, SSM_WIDTH, 3 * DSA_HEADS * HEAD_DIM,
             IDX_HEADS * IDX_DIM, IDX_DIM, IDX_HEADS, N_BRANCH * D_MODEL)

LANES = 128
ATT_BLOCK = 128
DIL_UNITS = 1
DIL_MERGE_ROWS = 512
DSA_BLOCK = 256
SB_QUERY_BLOCK = 2048
SB_KEY_BLOCK = 256
SUBLANES = 8
SSM_STEPS = 32
SSM_STRIP = 512
VMEM_LIMIT = 56 * 1024 * 1024
INT_MIN = -2 ** 31
HALF_BITS = 16
INT16_MIN = -2 ** 15
SOFTPLUS_CLAMP = 80.0

_NT = (((1,), (1,)), ((), ()))


def _dot(a, b):
    return jnp.dot(a, b, preferred_element_type=F32)


def _dot_nt(a, b):
    return lax.dot_general(a, b, _NT, preferred_element_type=F32)


def _params(*sem):
    return pltpu.CompilerParams(dimension_semantics=sem, vmem_limit_bytes=VMEM_LIMIT)


def _sigmoid(x):
    return 1.0 / (1.0 + jnp.exp(-x))


def _mm_kernel(x_ref, w_ref, o_ref):
    o_ref[...] = _dot(x_ref[...], w_ref[...]).astype(o_ref.dtype)


def matmul(x, w, out_dtype, tm=512):
    m, k = x.shape
    n = w.shape[1]
    assert m % tm == 0 and n % LANES == 0
    return pl.pallas_call(
        _mm_kernel,
        grid=(m // tm,),
        in_specs=[pl.BlockSpec((tm, k), lambda i: (i, 0)),
                  pl.BlockSpec((k, n), lambda i: (0, 0), pipeline_mode=pl.Buffered(1))],
        out_specs=pl.BlockSpec((tm, n), lambda i: (i, 0)),
        out_shape=jax.ShapeDtypeStruct((m, n), out_dtype),
        compiler_params=_params("parallel"),
        name="matmul",
    )(x, w)


def _ffn_up_kernel(x_ref, wa_ref, wb_ref, o_ref):
    x = x_ref[...].astype(BF16)
    a = _dot(x, wa_ref[...])
    b = _dot(x, wb_ref[...])
    o_ref[...] = (a * _sigmoid(a) * b).astype(o_ref.dtype)


def ffn_up(x, w_up, layer, tm=1024, tn=512):
    m, k = x.shape
    f = w_up.shape[2] // 2
    assert m % tm == 0 and f % tn == 0
    nb = f // tn
    return pl.pallas_call(
        _ffn_up_kernel,
        grid=(m // tm, nb),
        in_specs=[pl.BlockSpec((tm, k), lambda i, j: (i, 0)),
                  pl.BlockSpec((None, k, tn), lambda i, j: (layer, 0, j)),
                  pl.BlockSpec((None, k, tn), lambda i, j: (layer, 0, j + nb))],
        out_specs=pl.BlockSpec((tm, tn), lambda i, j: (i, j)),
        out_shape=jax.ShapeDtypeStruct((m, f), BF16),
        compiler_params=_params("parallel", "arbitrary"),
        name="ffn_up",
    )(x, w_up, w_up)


def _mm_res_ln_kernel(h_ref, w_ref, x_ref, g_ref, b_ref, o_ref, obf_ref, *, scale):
    y = DN_ALPHA * x_ref[...] + scale * _dot(h_ref[...], w_ref[...])
    mu = jnp.mean(y, axis=-1, keepdims=True)
    yc = y - mu
    var = jnp.mean(yc * yc, axis=-1, keepdims=True)
    out = yc * lax.rsqrt(var + LN_EPS) * g_ref[...] + b_ref[...]
    o_ref[...] = out
    obf_ref[...] = out.astype(BF16)


def matmul_residual_layernorm(h, w, layer, x, g, b, scale, tm=256):
    m, k = h.shape
    n = w.shape[2]
    assert m % tm == 0
    return pl.pallas_call(
        functools.partial(_mm_res_ln_kernel, scale=scale),
        grid=(m // tm,),
        in_specs=[pl.BlockSpec((tm, k), lambda i: (i, 0)),
                  pl.BlockSpec((None, k, n), lambda i: (layer, 0, 0), pipeline_mode=pl.Buffered(1)),
                  pl.BlockSpec((tm, n), lambda i: (i, 0)),
                  pl.BlockSpec((1, n), lambda i: (0, 0)),
                  pl.BlockSpec((1, n), lambda i: (0, 0))],
        out_specs=[pl.BlockSpec((tm, n), lambda i: (i, 0)),
                   pl.BlockSpec((tm, n), lambda i: (i, 0))],
        out_shape=[jax.ShapeDtypeStruct((m, n), F32), jax.ShapeDtypeStruct((m, n), BF16)],
        compiler_params=_params("parallel"),
        name="matmul_residual_layernorm",
    )(h, w, x, g.reshape(1, n), b.reshape(1, n))


def _gated_merge_kernel(x_ref, ysb_ref, ydil_ref, yssm_ref, ydsa_ref, wsb_ref, wdil_ref, wssm_ref, wdsa_ref,
                        g0_ref, g1_ref, g2_ref, g3_ref, o_ref):
    x = x_ref[...]
    acc = _sigmoid(_dot(x, g0_ref[...])) * _dot(ysb_ref[...], wsb_ref[...])
    acc += _sigmoid(_dot(x, g1_ref[...])) * _dot(ydil_ref[...], wdil_ref[...])
    acc += _sigmoid(_dot(x, g2_ref[...])) * _dot(yssm_ref[...], wssm_ref[...])
    acc += _sigmoid(_dot(x, g3_ref[...])) * _dot(ydsa_ref[...], wdsa_ref[...])
    o_ref[...] = acc.astype(o_ref.dtype)


def gated_branch_merge(x, ys, ws, w_gate, tm=1024, tn=512):
    m, k = x.shape
    n = ws[0].shape[1]
    assert m % tm == 0 and n % tn == 0
    nb = n // tn
    y_specs = [pl.BlockSpec((tm, y.shape[1]), lambda i, j: (i, 0)) for y in ys]
    w_specs = [pl.BlockSpec((w.shape[0], tn), lambda i, j: (0, j)) for w in ws]
    g_specs = [pl.BlockSpec((k, tn), functools.partial(lambda i, j, br: (0, j + br * nb), br=br))
               for br in range(N_BRANCH)]
    return pl.pallas_call(
        _gated_merge_kernel,
        grid=(m // tm, nb),
        in_specs=[pl.BlockSpec((tm, k), lambda i, j: (i, 0))] + y_specs + w_specs + g_specs,
        out_specs=pl.BlockSpec((tm, tn), lambda i, j: (i, j)),
        out_shape=jax.ShapeDtypeStruct((m, n), BF16),
        compiler_params=_params("parallel", "arbitrary"),
        name="gated_branch_merge",
    )(x, *ys, *ws, w_gate, w_gate, w_gate, w_gate)


def _is_power_of_two(x):
    return math.frexp(x)[0] == 0.5


def _sb_kernel(q_ref, k_ref, v_ref, o_ref, vT_ref, acc_ref, *, qblk, kblk, scale):
    i = pl.program_id(2)
    ratio = qblk // kblk
    fold_scale = _is_power_of_two(scale)

    @pl.when(i == 0)
    def _():
        for j in range(vT_ref.shape[0]):
            vT_ref[j] = v_ref[0, j * kblk:(j + 1) * kblk, :].T

    qT = q_ref[0].T
    if fold_scale:
        qT = qT * jnp.asarray(scale, qT.dtype)
    row = lax.broadcasted_iota(jnp.int32, (LANES, 1), 0)
    zero = jnp.zeros_like(qT)
    q_rhs = jnp.concatenate([jnp.where(row < HEAD_DIM, qT, zero), jnp.where(row >= HEAD_DIM, qT, zero)], axis=1)
    key_io = lax.broadcasted_iota(jnp.int32, (kblk, kblk), 0)
    qry_io = lax.broadcasted_iota(jnp.int32, (kblk, kblk), 1)
    later = jnp.where(qry_io > key_io, 1.0, 0.0).astype(BF16)
    acc_ref[...] = jnp.zeros_like(acc_ref)
    nsub = 2 * ratio

    def block(j, carry, diag_sub):
        start = pl.multiple_of(j * kblk, kblk)
        z_all = _dot(k_ref[0, pl.ds(start, kblk), :], q_rhs)
        vT = vT_ref[j]
        strict = key_io < qry_io
        stage = []
        for c in range(nsub):
            h, sub = divmod(c, ratio)
            if diag_sub is not None and sub < diag_sub:
                stage.append(None)
                continue
            masked = diag_sub is not None and sub == diag_sub
            z = z_all[:, c * kblk:(c + 1) * kblk]
            if not fold_scale:
                z = z * scale
            sp = jnp.maximum(jnp.log(1.0 + jnp.exp(jnp.minimum(z, SOFTPLUS_CLAMP))), z)
            log_1mb = jnp.where(strict, -sp, 0.0) if masked else -sp
            suf = _dot(later, log_1mb.astype(BF16))
            stage.append((z - sp, suf, jnp.sum(log_1mb, axis=0, keepdims=True), masked))
        new_carry = []
        for c in range(nsub):
            h, sub = divmod(c, ratio)
            if stage[c] is None:
                new_carry.append(carry[c])
                continue
            log_beta, suf, colsum, masked = stage[c]
            att = jnp.exp(log_beta + suf + carry[c])
            if masked:
                att = jnp.where(strict, att, 0.0)
            rows = slice(h * HEAD_DIM, (h + 1) * HEAD_DIM)
            cols = slice(sub * kblk, (sub + 1) * kblk)
            acc_ref[rows, cols] += _dot(vT[rows, :], att.astype(BF16))
            new_carry.append(carry[c] + colsum)
        return tuple(new_carry)

    carry = tuple(jnp.zeros((1, kblk), F32) for _ in range(nsub))
    for sub in reversed(range(ratio)):
        carry = block(i * ratio + sub, carry, sub)
    lax.fori_loop(0, i * ratio, lambda jj, c: block(i * ratio - 1 - jj, c, None), carry)
    o_ref[0] = acc_ref[...].T.astype(o_ref.dtype)


def stick_breaking_attention(qkv, col0=0, qblk=SB_QUERY_BLOCK, kblk=SB_KEY_BLOCK):
    bsz, seq, _ = qkv.shape
    width = SB_HEADS * HEAD_DIM
    npair = width // LANES
    qblk = min(qblk, seq)
    assert seq % qblk == 0 and qblk % kblk == 0 and col0 % LANES == 0
    qcol = col0 // LANES
    return pl.pallas_call(
        functools.partial(_sb_kernel, qblk=qblk, kblk=kblk, scale=HEAD_DIM ** -0.5),
        grid=(bsz, npair, seq // qblk),
        in_specs=[pl.BlockSpec((1, qblk, LANES), lambda b, p, i: (b, i, qcol + p)),
                  pl.BlockSpec((1, seq, LANES), lambda b, p, i: (b, 0, qcol + npair + p)),
                  pl.BlockSpec((1, seq, LANES), lambda b, p, i: (b, 0, qcol + 2 * npair + p))],
        out_specs=pl.BlockSpec((1, qblk, LANES), lambda b, p, i: (b, i, p)),
        out_shape=jax.ShapeDtypeStruct((bsz, seq, width), BF16),
        scratch_shapes=[pltpu.VMEM((seq // kblk, LANES, kblk), BF16),
                        pltpu.VMEM((LANES, qblk), F32)],
        compiler_params=_params("parallel", "parallel", "arbitrary"),
        name="stick_breaking_attention",
    )(qkv, qkv, qkv)


def _t5_bucket(dist):
    max_exact = REL_BUCKETS // 2
    d = jnp.maximum(dist, 1).astype(F32)
    large = max_exact + (jnp.log(d / max_exact) / math.log(REL_MAX_DIST / max_exact)
                         * (REL_BUCKETS - max_exact)).astype(jnp.int32)
    large = jnp.minimum(large, REL_BUCKETS - 1)
    return jnp.where(dist < max_exact, dist, large)


def _bias_of_distance(rel_bias, dist):
    one_hot = jax.nn.one_hot(_t5_bucket(dist), REL_BUCKETS, dtype=F32)
    return jnp.einsum("...b,bh->...h", one_hot, rel_bias.astype(F32), precision=lax.Precision.HIGHEST)


def _dilated_bias_tiles(rel_bias_group, dil, blk):
    a = jnp.arange(blk)[:, None]
    cc = jnp.arange(2 * blk)[None, :]
    step = blk + a - cc
    valid = (step >= 0) & (step <= blk)
    bias = _bias_of_distance(rel_bias_group, dil * jnp.clip(step, 0, blk))
    tile = jnp.where(valid[..., None], bias, NEG_INF)
    return jnp.transpose(tile, (2, 0, 1))


def _dsa_bias_tiles(rel_bias_dsa, blk):
    c = jnp.arange(blk)[:, None]
    a = jnp.arange(blk)[None, :]
    own = jnp.where((a - c >= 0)[..., None], _bias_of_distance(rel_bias_dsa, jnp.maximum(a - c, 0)), NEG_INF)
    prev = _bias_of_distance(rel_bias_dsa, blk + a - c)
    half = REL_BUCKETS // 2
    assert half + int(math.log((blk + 1) / half) / math.log(REL_MAX_DIST / half) * (REL_BUCKETS - half)) >= REL_BUCKETS - 1
    far = jnp.broadcast_to(rel_bias_dsa[REL_BUCKETS - 1].astype(F32), prev.shape)
    return jnp.transpose(jnp.stack([own, prev, far]), (3, 0, 1, 2))


def _dil_kernel(*refs, blk, dils, nbs, units, scale):
    ngroups = len(dils)
    nslab = 2 * ngroups
    q_refs, k_refs, v_refs = refs[:nslab], refs[nslab:2 * nslab], refs[2 * nslab:3 * nslab]
    bias_refs = refs[3 * nslab:3 * nslab + ngroups]
    y_ref, o_ref, lse_ref = refs[3 * nslab + ngroups:]
    lane = lax.broadcasted_iota(jnp.int32, (1, LANES), 1)
    fold_scale = _is_power_of_two(scale)
    ones = jnp.ones((blk, LANES), BF16)

    def run_step(step, _):
        work = []
        for g, u in [(g, u) for g in range(ngroups) for u in range(units)]:
            dil, nb = dils[g], nbs[g]
            unit = step * units + u
            c, i = unit // nb, unit % nb

            def rows(block, dil=dil, c=c):
                start = c + dil * blk * block
                return pl.ds(start, blk, stride=dil) if dil > 1 else pl.ds(pl.multiple_of(start, blk), blk)

            cur, prev = rows(i), rows(jnp.maximum(i - 1, 0))
            for half in range(2):
                slab = 2 * g + half
                q = q_refs[slab][0, cur, :]
                if fold_scale:
                    q = q * scale
                q = q.astype(BF16)
                kc, kp = k_refs[slab][0, cur, :].astype(BF16), k_refs[slab][0, prev, :].astype(BF16)
                zero = jnp.zeros_like(q)
                scores = []
                for e in range(2):
                    head = (lane < HEAD_DIM) if e == 0 else (lane >= HEAD_DIM)
                    qm = jnp.where(head, q, zero)
                    scores.append((_dot_nt(qm, kp), _dot_nt(qm, kc)))
                work.append((g, i, cur, prev, half, scores))
        soft = []
        for g, i, cur, prev, half, scores in work:
            has_prev = i > 0
            parts = []
            for e in range(2):
                h = 2 * half + e
                s_prev, s_cur = scores[e]
                if not fold_scale:
                    s_prev, s_cur = s_prev * scale, s_cur * scale
                s_prev = jnp.where(has_prev, s_prev + bias_refs[g][h, :, :blk], NEG_INF)
                s_cur = s_cur + bias_refs[g][h, :, blk:]
                m = jnp.maximum(jnp.max(s_prev, axis=-1, keepdims=True), jnp.max(s_cur, axis=-1, keepdims=True))
                parts.append((jnp.exp(s_prev - m).astype(BF16), jnp.exp(s_cur - m).astype(BF16), m))
            soft.append(parts)
        for (g, i, cur, prev, half, scores), parts in zip(work, soft):
            v_ref = v_refs[2 * g + half]
            vc = jnp.concatenate([v_ref[0, cur, :].astype(BF16), ones], axis=1)
            vp = jnp.concatenate([v_ref[0, prev, :].astype(BF16), ones], axis=1)
            out = jnp.zeros((blk, LANES), F32)
            lse_b = jnp.zeros((blk, LANES), F32)
            for e in range(2):
                head = (lane < HEAD_DIM) if e == 0 else (lane >= HEAD_DIM)
                p_prev, p_cur, m = parts[e]
                o2 = _dot(p_prev, vp) + _dot(p_cur, vc)
                denom = o2[:, LANES:]
                out = jnp.where(head, o2[:, :LANES] / denom, out)
                lse_b = jnp.where(head, m + jnp.log(denom), lse_b)
            o_ref[g, half, cur, :] = out
            lse_ref[g, half, cur, :] = lse_b
        return 0

    lax.fori_loop(0, dils[0] * nbs[0] // units, run_step, 0)

    def merge(r, _):
        rows_ = pl.ds(pl.multiple_of(r * DIL_MERGE_ROWS, DIL_MERGE_ROWS), DIL_MERGE_ROWS)
        for half in range(2):
            lses = [lse_ref[g, half, rows_, :] for g in range(ngroups)]
            top = functools.reduce(jnp.maximum, lses)
            ws = [jnp.exp(x - top) for x in lses]
            num = functools.reduce(lambda a, b: a + b, [ws[g] * o_ref[g, half, rows_, :] for g in range(ngroups)])
            den = functools.reduce(lambda a, b: a + b, ws)
            y_ref[0, rows_, half * LANES:(half + 1) * LANES] = (num / den).astype(y_ref.dtype)
        return 0

    lax.fori_loop(0, o_ref.shape[2] // DIL_MERGE_ROWS, merge, 0)


def dilated_attention(qkv, rel_bias_dil, blk=ATT_BLOCK, units=DIL_UNITS):
    bsz, seq, _ = qkv.shape
    hpg = DIL_HEADS_PER_GROUP
    assert hpg * HEAD_DIM == 2 * LANES and seq % DIL_MERGE_ROWS == 0
    ngroups = len(DIL_PATTERNS)
    nslab = DIL_HEADS * HEAD_DIM // LANES
    dils = tuple(d for _, d in DIL_PATTERNS)
    nbs = tuple(seq // d // blk for d in dils)
    nsteps = dils[0] * nbs[0]
    assert all(w // d == blk for w, d in DIL_PATTERNS) and all(d * n == nsteps for d, n in zip(dils, nbs))
    assert nsteps % units == 0
    tiles = [_dilated_bias_tiles(rel_bias_dil[:, g * hpg:(g + 1) * hpg], d, blk) for g, d in enumerate(dils)]
    slab = lambda col: pl.BlockSpec((1, seq, LANES), functools.partial(lambda b, col: (b, 0, col), col=col))
    in_specs = [slab(section * nslab + s) for section in range(3) for s in range(nslab)]
    in_specs += [pl.BlockSpec(t.shape, lambda b: (0, 0, 0), pipeline_mode=pl.Buffered(1)) for t in tiles]
    return pl.pallas_call(
        functools.partial(_dil_kernel, blk=blk, dils=dils, nbs=nbs, units=units, scale=HEAD_DIM ** -0.5),
        grid=(bsz,),
        in_specs=in_specs,
        out_specs=pl.BlockSpec((1, seq, 2 * LANES), lambda b: (b, 0, 0)),
        out_shape=jax.ShapeDtypeStruct((bsz, seq, 2 * LANES), BF16),
        scratch_shapes=[pltpu.VMEM((ngroups, 2, seq, LANES), F32),
                        pltpu.VMEM((ngroups, 2, seq, LANES), F32)],
        compiler_params=_params("parallel"),
        name="dilated_attention",
    )(*([qkv] * (3 * nslab)), *tiles)


def _ssm_kernel(u_ref, bb_ref, a_ref, cc_ref, d_ref, wg_ref, o_ref, x_ref, carry_ref, *, steps, strip):
    @pl.when(pl.program_id(1) == 0)
    def _():
        carry_ref[...] = jnp.zeros_like(carry_ref)

    u = u_ref[0]
    ub = u.astype(BF16)
    width = u.shape[1]
    half = width // 2
    nstates = x_ref.shape[1] // 2
    hs = nstates // 2
    for part in range(2):
        bu = _dot(ub[:, part * half:(part + 1) * half], bb_ref[part])
        x_ref[:, part * hs:(part + 1) * hs] = bu[:, :hs]
        x_ref[:, nstates + part * hs:nstates + (part + 1) * hs] = bu[:, hs:]
    for s in range(nstates // strip):
        re = slice(s * strip, (s + 1) * strip)
        im = slice(nstates + s * strip, nstates + (s + 1) * strip)
        ar = jnp.broadcast_to(a_ref[0:1, re], (SUBLANES, strip))
        ai = jnp.broadcast_to(a_ref[1:2, re], (SUBLANES, strip))
        xr = carry_ref[:, re]
        xi = carry_ref[:, im]
        for t in range(steps):
            rows = slice(t * SUBLANES, (t + 1) * SUBLANES)
            xr, xi = ar * xr - ai * xi + x_ref[rows, re], ar * xi + ai * xr + x_ref[rows, im]
            x_ref[rows, re] = xr
            x_ref[rows, im] = xi
        carry_ref[:, re] = xr
        carry_ref[:, im] = xi
    y_parts = []
    for part in range(2):
        xr = x_ref[:, part * hs:(part + 1) * hs].astype(BF16)
        xi = x_ref[:, nstates + part * hs:nstates + (part + 1) * hs].astype(BF16)
        y_parts.append(_dot(xr, cc_ref[0, part]) + _dot(xi, cc_ref[1, part]))
    y = jnp.concatenate(y_parts, axis=1) + d_ref[...] * u
    y = 0.5 * y * (1.0 + jnp.tanh(math.sqrt(2.0 / math.pi) * (y + 0.044715 * (y * y * y))))
    z = _dot(y.astype(BF16), wg_ref[...])
    o_ref[0] = (z[:, :width] * _sigmoid(z[:, width:])).astype(o_ref.dtype)


def _block_diag(blocks):
    g, r, c = blocks.shape
    eye = jnp.eye(g, dtype=blocks.dtype)
    return (eye[:, None, :, None] * blocks[:, :, None, :]).reshape(g * r, g * c)


def _ssm_tables(lam_re, lam_im, log_dt, b_re, b_im, c_re, c_im):
    lr, li = lam_re.astype(F32), lam_im.astype(F32)
    dt = jnp.exp(log_dt.astype(F32))[:, None]
    mag = jnp.exp(lr * dt)
    a_re, a_im = mag * jnp.cos(li * dt), mag * jnp.sin(li * dt)
    den = lr * lr + li * li
    f_re = ((a_re - 1.0) * lr + a_im * li) / den
    f_im = (a_im * lr - (a_re - 1.0) * li) / den
    br, bi = b_re.astype(F32), b_im.astype(F32)
    bb_re = _block_diag(jnp.transpose(f_re[..., None] * br - f_im[..., None] * bi, (0, 2, 1)))
    bb_im = _block_diag(jnp.transpose(f_re[..., None] * bi + f_im[..., None] * br, (0, 2, 1)))
    cc_re = _block_diag(jnp.transpose(c_re.astype(F32), (0, 2, 1)))
    cc_im = -_block_diag(jnp.transpose(c_im.astype(F32), (0, 2, 1)))
    nstates = SSM_GROUPS * SSM_STATE
    hw, hs = SSM_WIDTH // 2, nstates // 2
    bb = jnp.stack([jnp.concatenate([bb_re[p * hw:(p + 1) * hw, p * hs:(p + 1) * hs],
                                     bb_im[p * hw:(p + 1) * hw, p * hs:(p + 1) * hs]], axis=1) for p in range(2)])
    cc = jnp.stack([jnp.stack([m[p * hs:(p + 1) * hs, p * hw:(p + 1) * hw] for p in range(2)])
                    for m in (cc_re, cc_im)])
    a = jnp.stack([a_re.reshape(nstates), a_im.reshape(nstates)])
    return bb.astype(BF16), a, cc.astype(BF16)


def s5_glu(u, bb, a, cc, d_skip, w_glu, steps=SSM_STEPS, strip=SSM_STRIP):
    bsz, seq, width = u.shape
    nstates = a.shape[1]
    assert bsz % SUBLANES == 0 and seq % steps == 0 and nstates % strip == 0
    ngrp = bsz // SUBLANES
    rows = steps * SUBLANES
    ut = u.reshape(ngrp, SUBLANES, seq, width).transpose(0, 2, 1, 3).reshape(ngrp, seq * SUBLANES, width)
    const = lambda *shape: pl.BlockSpec(shape, lambda g, c: (0,) * len(shape))
    out = pl.pallas_call(
        functools.partial(_ssm_kernel, steps=steps, strip=strip),
        grid=(ngrp, seq // steps),
        in_specs=[pl.BlockSpec((1, rows, width), lambda g, c: (g, c, 0)),
                  const(*bb.shape), const(*a.shape), const(*cc.shape),
                  const(1, width), const(*w_glu.shape)],
        out_specs=pl.BlockSpec((1, rows, width), lambda g, c: (g, c, 0)),
        out_shape=jax.ShapeDtypeStruct((ngrp, seq * SUBLANES, width), BF16),
        scratch_shapes=[pltpu.VMEM((rows, 2 * nstates), F32),
                        pltpu.VMEM((SUBLANES, 2 * nstates), F32)],
        compiler_params=_params("parallel", "arbitrary"),
        name="s5_glu",
    )(ut, bb, a, cc, d_skip.reshape(1, width).astype(F32), w_glu)
    return out.reshape(ngrp, seq, SUBLANES, width).transpose(0, 2, 1, 3).reshape(bsz * seq, width)


def _dsa_kernel(q_ref, k_ref, v_ref, iq_ref, ik_ref, ikq_ref, bias_ref, o_ref,
                vT_ref, key_ref, khi_ref, acc_ref, *, blk, topk, scale):
    i = pl.program_id(1)
    nblk = i + 1
    key_io = lax.broadcasted_iota(jnp.int32, (blk, blk), 0)
    qry_io = lax.broadcasted_iota(jnp.int32, (blk, blk), 1)

    @pl.when(i == 0)
    def _():
        for j in range(vT_ref.shape[0]):
            vT_ref[j] = v_ref[0, j * blk:(j + 1) * blk, :].T

    iqT = iq_ref[0].T
    idx_rhs = jnp.concatenate([iqT[h * IDX_DIM:(h + 1) * IDX_DIM, :] for h in range(IDX_HEADS)], axis=1)
    iw = ikq_ref[0].T[IDX_DIM:IDX_DIM + IDX_HEADS, :]

    def score_block(j, _):
        start = pl.multiple_of(j * blk, blk)
        kk = ik_ref[0, pl.ds(start, blk), :][:, :IDX_DIM].astype(BF16)
        d = _dot(kk, idx_rhs)
        sc = jnp.zeros((blk, blk), F32)
        for h in range(IDX_HEADS):
            sc = sc + iw[h:h + 1, :] * jnp.maximum(d[:, h * blk:(h + 1) * blk], 0.0)
        sc = jnp.where((key_io + j * blk) <= (qry_io + i * blk), sc, NEG_INF)
        bits = pltpu.bitcast(sc, jnp.int32)
        key = bits ^ ((bits >> 31) & 0x7FFFFFFF)
        key_ref[j] = key
        khi_ref[j] = (key >> HALF_BITS).astype(jnp.int16)
        return 0

    lax.fori_loop(0, nblk, score_block, 0)

    @pl.when(nblk % 2 == 1)
    def _():
        key_ref[nblk] = jnp.full((blk, blk), INT_MIN, jnp.int32)
        khi_ref[nblk] = jnp.full((blk, blk), INT16_MIN, jnp.int16)

    npairs = (nblk + 1) // 2

    def count(pred):
        def body(p, acc):
            for j in (2 * p, 2 * p + 1):
                hit = jnp.where(pred(key_ref[j]), 1.0, 0.0)
                acc = acc + jnp.sum(hit.reshape(blk // 8, 8, blk), axis=0)
            return acc
        return jnp.sum(lax.fori_loop(0, npairs, body, jnp.zeros((8, blk), F32)), axis=0, keepdims=True)

    def count_half(cand):
        one, zero = jnp.ones((), BF16), jnp.zeros((), BF16)
        rows = 2 * SUBLANES

        def body(p, acc):
            for j in (2 * p, 2 * p + 1):
                hit = jnp.where(khi_ref[j] >= cand, one, zero)
                for r in range(blk // rows):
                    acc = acc + hit[r * rows:(r + 1) * rows, :]
            return acc
        acc = lax.fori_loop(0, npairs, body, jnp.zeros((rows, blk), BF16))
        return jnp.sum(acc.astype(F32), axis=0, keepdims=True)

    def half_step(b, thr):
        cand = thr + lax.shift_left(jnp.int32(1), HALF_BITS - 1 - b)
        return jnp.where(count_half(cand.astype(jnp.int16)) >= topk, cand, thr)

    start16 = jnp.full((1, blk), INT16_MIN, jnp.int32)
    thr_hi = lax.fori_loop(0, HALF_BITS, half_step, start16)
    thr_hi16 = thr_hi.astype(jnp.int16)

    def lower_halves(j, _):
        hi = khi_ref[j]
        lo = ((key_ref[j] & 0xFFFF) + INT16_MIN).astype(jnp.int16)
        khi_ref[j] = jnp.where(hi > thr_hi16, jnp.int16(-INT16_MIN - 1), jnp.where(hi == thr_hi16, lo, jnp.int16(INT16_MIN)))
        return 0

    lax.fori_loop(0, 2 * npairs, lower_halves, 0)
    thr_lo = lax.fori_loop(0, HALF_BITS, half_step, start16)
    thr = lax.shift_left(thr_hi, HALF_BITS) + (thr_lo - INT16_MIN)

    need = topk - count(lambda key: key > thr)
    upto = jnp.where(qry_io <= key_io, 1.0, 0.0).astype(BF16)

    def select_block(j, run):
        key = key_ref[j]
        tie = jnp.where(key == thr, 1.0, 0.0)
        rank = _dot(upto, tie.astype(BF16)) + run
        keep_tie = jnp.where(rank <= need, 0.0, NEG_INF)
        madd = jnp.where(key > thr, 0.0, jnp.where(key == thr, keep_tie, NEG_INF))
        return madd, run + jnp.sum(tie, axis=0, keepdims=True)

    fold_scale = _is_power_of_two(scale)
    qT = q_ref[0].T
    if fold_scale:
        qT = qT * jnp.asarray(scale, qT.dtype)
    row = lax.broadcasted_iota(jnp.int32, (LANES, 1), 0)
    q_rhs = []
    for g in range(DSA_HEADS // 2):
        pair = qT[g * LANES:(g + 1) * LANES, :]
        zero = jnp.zeros_like(pair)
        q_rhs.append(jnp.concatenate([jnp.where(row < HEAD_DIM, pair, zero),
                                      jnp.where(row >= HEAD_DIM, pair, zero)], axis=1))
    acc_ref[...] = jnp.zeros_like(acc_ref)

    def attend_block(j, carry):
        m_all, l_all, run = carry
        start = pl.multiple_of(j * blk, blk)
        kblk = k_ref[0, pl.ds(start, blk), :]
        vT = vT_ref[j]
        madd, run = select_block(j, run)
        which = jnp.minimum(i - j, 2)
        s2 = [_dot(kblk[:, g * LANES:(g + 1) * LANES], q_rhs[g]) for g in range(DSA_HEADS // 2)]
        m_out, l_out, probs, alphas = [], [], [], []
        for h in range(DSA_HEADS):
            s = s2[h // 2][:, (h % 2) * blk:(h % 2 + 1) * blk]
            if not fold_scale:
                s = s * scale
            s = s + bias_ref[h, which] + madd
            m_new = jnp.maximum(m_all[h], jnp.max(s, axis=0, keepdims=True))
            p = jnp.exp(s - m_new)
            alpha = jnp.exp(m_all[h] - m_new)
            l_out.append(alpha * l_all[h] + jnp.sum(p, axis=0, keepdims=True))
            m_out.append(m_new)
            probs.append(p.astype(BF16))
            alphas.append(alpha)
        for h in range(DSA_HEADS):
            rows = slice(h * HEAD_DIM, (h + 1) * HEAD_DIM)
            acc_ref[rows, :] = alphas[h] * acc_ref[rows, :] + _dot(vT[rows, :], probs[h])
        return tuple(m_out), tuple(l_out), run

    init = (tuple(jnp.full((1, blk), NEG_INF, F32) for _ in range(DSA_HEADS)),
            tuple(jnp.zeros((1, blk), F32) for _ in range(DSA_HEADS)), jnp.zeros((1, blk), F32))
    _, l_all, _ = lax.fori_loop(0, nblk, attend_block, init)
    for h in range(DSA_HEADS):
        rows = slice(h * HEAD_DIM, (h + 1) * HEAD_DIM)
        acc_ref[rows, :] = acc_ref[rows, :] / l_all[h]
    o_ref[0] = acc_ref[...].T.astype(o_ref.dtype)


def dsa_attention(proj, qkv_col, iq_col, proj_f32, kw_col, bias_tiles, blk=DSA_BLOCK):
    bsz, seq, _ = proj.shape
    width = DSA_HEADS * HEAD_DIM
    nblk = seq // blk
    topk = min(DSA_TOPK, seq // 4)
    iq_width = IDX_HEADS * IDX_DIM
    assert qkv_col % width == 0 and iq_col % iq_width == 0 and kw_col % LANES == 0
    assert nblk % 2 == 0 and topk <= blk
    assert nblk * blk // (2 * SUBLANES) <= 256
    qcol = qkv_col // width
    return pl.pallas_call(
        functools.partial(_dsa_kernel, blk=blk, topk=topk, scale=HEAD_DIM ** -0.5),
        grid=(bsz, nblk),
        in_specs=[pl.BlockSpec((1, blk, width), lambda b, i: (b, i, qcol)),
                  pl.BlockSpec((1, seq, width), lambda b, i: (b, 0, qcol + 1)),
                  pl.BlockSpec((1, seq, width), lambda b, i: (b, 0, qcol + 2)),
                  pl.BlockSpec((1, blk, iq_width), lambda b, i: (b, i, iq_col // iq_width)),
                  pl.BlockSpec((1, seq, LANES), lambda b, i: (b, 0, kw_col // LANES)),
                  pl.BlockSpec((1, blk, LANES), lambda b, i: (b, i, kw_col // LANES)),
                  pl.BlockSpec((DSA_HEADS, 3, blk, blk), lambda b, i: (0, 0, 0, 0))],
        out_specs=pl.BlockSpec((1, blk, width), lambda b, i: (b, i, 0)),
        out_shape=jax.ShapeDtypeStruct((bsz, seq, width), BF16),
        scratch_shapes=[pltpu.VMEM((nblk, width, blk), BF16),
                        pltpu.VMEM((nblk, blk, blk), jnp.int32),
                        pltpu.VMEM((nblk, blk, blk), jnp.int16),
                        pltpu.VMEM((width, blk), F32)],
        compiler_params=_params("parallel", "arbitrary"),
        name="dsa_attention",
    )(proj, proj, proj, proj, proj_f32, proj_f32, bias_tiles)


def hybrid_mixer(xbf, bsz, seq, w_in, rel_bias, ssm_params, d_skip, w_glu, w_branches):
    offs = [0]
    for width in IN_SPLITS:
        offs.append(offs[-1] + width)
    seg = lambda a: w_in[:, offs[a]:offs[a + 1]]
    w_bf = jnp.concatenate([seg(0), seg(3), seg(4)], axis=1).astype(BF16)
    pad = jnp.zeros((w_in.shape[0], LANES - IDX_DIM - IDX_HEADS), w_in.dtype)
    w_f32 = jnp.concatenate([seg(1), seg(2), seg(5), seg(6), pad], axis=1).astype(BF16)
    dsa_col = IN_SPLITS[0]
    iq_col = dsa_col + IN_SPLITS[3]
    ssm_col = IN_SPLITS[1]
    kw_col = ssm_col + IN_SPLITS[2]
    proj = matmul(xbf, w_bf, BF16).reshape(bsz, seq, -1)
    proj_f32 = matmul(xbf, w_f32, F32).reshape(bsz, seq, -1)

    y_sb = stick_breaking_attention(proj).reshape(bsz * seq, -1)

    y_dil = dilated_attention(proj_f32, rel_bias[:, :DIL_HEADS]).reshape(bsz * seq, -1)

    bb, a_bar, cc = _ssm_tables(*ssm_params)
    y_ssm = s5_glu(proj_f32[..., ssm_col:ssm_col + SSM_WIDTH], bb, a_bar, cc, d_skip, w_glu.astype(BF16))

    y_dsa = dsa_attention(proj, dsa_col, iq_col, proj_f32, kw_col, _dsa_bias_tiles(rel_bias[:, DIL_HEADS:], DSA_BLOCK))
    y_dsa = y_dsa.reshape(bsz * seq, -1)

    return gated_branch_merge(xbf, [y_sb, y_dil, y_ssm, y_dsa], [w.astype(BF16) for w in w_branches],
                              seg(7).astype(BF16))


def kernel(x, ln_g, ln_b, ffn1_w_up, ffn1_w_down, w_in, rel_bias, ssm_lam_re, ssm_lam_im, ssm_log_dt,
           ssm_b_re, ssm_b_im, ssm_c_re, ssm_c_im, ssm_d, ssm_w_glu, w_br_sb, w_br_dil, w_br_ssm, w_br_dsa,
           w_out, ffn2_w_up, ffn2_w_down):
    bsz, seq, d = x.shape
    xf = x.reshape(bsz * seq, d)
    xbf = xf
    up1, down1 = ffn1_w_up.astype(BF16), ffn1_w_down.astype(BF16)
    up2, down2 = ffn2_w_up.astype(BF16), ffn2_w_down.astype(BF16)
    out_w = w_out.astype(BF16)
    for l in range(DEPTH):
        h = ffn_up(xbf, up1, l)
        xf, xbf = matmul_residual_layernorm(h, down1, l, xf, ln_g[l, 0], ln_b[l, 0], MACARON)
        merged = hybrid_mixer(xbf, bsz, seq, w_in[l], rel_bias,
                              (ssm_lam_re[l], ssm_lam_im[l], ssm_log_dt[l], ssm_b_re[l], ssm_b_im[l],
                               ssm_c_re[l], ssm_c_im[l]), ssm_d[l], ssm_w_glu[l],
                              (w_br_sb[l], w_br_dil[l], w_br_ssm[l], w_br_dsa[l]))
        xf, xbf = matmul_residual_layernorm(merged, out_w, l, xf, ln_g[l, 1], ln_b[l, 1], 1.0)
        h = ffn_up(xbf, up2, l)
        xf, xbf = matmul_residual_layernorm(h, down2, l, xf, ln_g[l, 2], ln_b[l, 2], MACARON)
    return xf.reshape(bsz, seq, d)
```

```python
import functools
import math

import jax
import jax.numpy as jnp
from jax import lax
from jax.experimental import pallas as pl
from jax.experimental.pallas import tpu as pltpu

F32 = jnp.float32
BF16 = jnp.bfloat16

D_MODEL = 2048
DEPTH = 2
HEAD_DIM = 64
SB_HEADS = 8
DIL_PATTERNS = ((128, 1), (512, 4), (2048, 16))
DIL_HEADS_PER_GROUP = 4
DIL_HEADS = DIL_HEADS_PER_GROUP * len(DIL_PATTERNS)
SSM_WIDTH = 512
SSM_GROUP = 16
SSM_GROUPS = SSM_WIDTH // SSM_GROUP
SSM_STATE = 64
DSA_HEADS = 8
IDX_HEADS = 8
IDX_DIM = 64
DSA_TOPK = 256
N_BRANCH = 4
D_FF = 5632
REL_BUCKETS = 32
REL_MAX_DIST = 128
DN_ALPHA = (2.0 * DEPTH) ** 0.25
LN_EPS = 1e-5
NEG_INF = -1e30
MACARON = 0.5

IN_SPLITS = (3 * SB_HEADS * HEAD_DIM, 3 * DIL_HEADS * HEAD_DIM, SSM_WIDTH, 3 * DSA_HEADS * HEAD_DIM,
             IDX_HEADS * IDX_DIM, IDX_DIM, IDX_HEADS, N_BRANCH * D_MODEL)

LANES = 128
ATT_BLOCK = 128
DIL_UNITS = 1
DIL_MERGE_ROWS = 512
DSA_BLOCK = 256
SB_QUERY_BLOCK = 2048
SB_KEY_BLOCK = 256
SUBLANES = 8
SSM_STEPS = 32
SSM_STRIP = 512
VMEM_LIMIT = 56 * 1024 * 1024
INT_MIN = -2 ** 31
HALF_BITS = 16
INT16_MIN = -2 ** 15
SOFTPLUS_CLAMP = 80.0

_NT = (((1,), (1,)), ((), ()))


def _dot(a, b):
    return jnp.dot(a, b, preferred_element_type=F32)


def _dot_nt(a, b):
    return lax.dot_general(a, b, _NT, preferred_element_type=F32)


def _params(*sem):
    return pltpu.CompilerParams(dimension_semantics=sem, vmem_limit_bytes=VMEM_LIMIT)


def _sigmoid(x):
    return 1.0 / (1.0 + jnp.exp(-x))


def _mm_kernel(x_ref, w_ref, o_ref):
    o_ref[...] = _dot(x_ref[...], w_ref[...]).astype(o_ref.dtype)


def matmul(x, w, out_dtype, tm=512):
    m, k = x.shape
    n = w.shape[1]
    assert m % tm == 0 and n % LANES == 0
    return pl.pallas_call(
        _mm_kernel,
        grid=(m // tm,),
        in_specs=[pl.BlockSpec((tm, k), lambda i: (i, 0)),
                  pl.BlockSpec((k, n), lambda i: (0, 0), pipeline_mode=pl.Buffered(1))],
        out_specs=pl.BlockSpec((tm, n), lambda i: (i, 0)),
        out_shape=jax.ShapeDtypeStruct((m, n), out_dtype),
        compiler_params=_params("parallel"),
        name="matmul",
    )(x, w)


def _ffn_up_kernel(x_ref, wa_ref, wb_ref, o_ref):
    x = x_ref[...].astype(BF16)
    a = _dot(x, wa_ref[...])
    b = _dot(x, wb_ref[...])
    o_ref[...] = (a * _sigmoid(a) * b).astype(o_ref.dtype)


def ffn_up(x, w_up, layer, tm=1024, tn=512):
    m, k = x.shape
    f = w_up.shape[2] // 2
    assert m % tm == 0 and f % tn == 0
    nb = f // tn
    return pl.pallas_call(
        _ffn_up_kernel,
        grid=(m // tm, nb),
        in_specs=[pl.BlockSpec((tm, k), lambda i, j: (i, 0)),
                  pl.BlockSpec((None, k, tn), lambda i, j: (layer, 0, j)),
                  pl.BlockSpec((None, k, tn), lambda i, j: (layer, 0, j + nb))],
        out_specs=pl.BlockSpec((tm, tn), lambda i, j: (i, j)),
        out_shape=jax.ShapeDtypeStruct((m, f), BF16),
        compiler_params=_params("parallel", "arbitrary"),
        name="ffn_up",
    )(x, w_up, w_up)


def _mm_res_ln_kernel(h_ref, w_ref, x_ref, g_ref, b_ref, o_ref, obf_ref, *, scale):
    y = DN_ALPHA * x_ref[...] + scale * _dot(h_ref[...], w_ref[...])
    mu = jnp.mean(y, axis=-1, keepdims=True)
    yc = y - mu
    var = jnp.mean(yc * yc, axis=-1, keepdims=True)
    out = yc * lax.rsqrt(var + LN_EPS) * g_ref[...] + b_ref[...]
    o_ref[...] = out
    obf_ref[...] = out.astype(BF16)


def matmul_residual_layernorm(h, w, layer, x, g, b, scale, tm=256):
    m, k = h.shape
    n = w.shape[2]
    assert m % tm == 0
    return pl.pallas_call(
        functools.partial(_mm_res_ln_kernel, scale=scale),
        grid=(m // tm,),
        in_specs=[pl.BlockSpec((tm, k), lambda i: (i, 0)),
                  pl.BlockSpec((None, k, n), lambda i: (layer, 0, 0), pipeline_mode=pl.Buffered(1)),
                  pl.BlockSpec((tm, n), lambda i: (i, 0)),
                  pl.BlockSpec((1, n), lambda i: (0, 0)),
                  pl.BlockSpec((1, n), lambda i: (0, 0))],
        out_specs=[pl.BlockSpec((tm, n), lambda i: (i, 0)),
                   pl.BlockSpec((tm, n), lambda i: (i, 0))],
        out_shape=[jax.ShapeDtypeStruct((m, n), F32), jax.ShapeDtypeStruct((m, n), BF16)],
        compiler_params=_params("parallel"),
        name="matmul_residual_layernorm",
    )(h, w, x, g.reshape(1, n), b.reshape(1, n))


def _gated_merge_kernel(x_ref, ysb_ref, ydil_ref, yssm_ref, ydsa_ref, wsb_ref, wdil_ref, wssm_ref, wdsa_ref,
                        g0_ref, g1_ref, g2_ref, g3_ref, o_ref):
    x = x_ref[...]
    acc = _sigmoid(_dot(x, g0_ref[...])) * _dot(ysb_ref[...], wsb_ref[...])
    acc += _sigmoid(_dot(x, g1_ref[...])) * _dot(ydil_ref[...], wdil_ref[...])
    acc += _sigmoid(_dot(x, g2_ref[...])) * _dot(yssm_ref[...], wssm_ref[...])
    acc += _sigmoid(_dot(x, g3_ref[...])) * _dot(ydsa_ref[...], wdsa_ref[...])
    o_ref[...] = acc.astype(o_ref.dtype)


def gated_branch_merge(x, ys, ws, w_gate, tm=1024, tn=512):
    m, k = x.shape
    n = ws[0].shape[1]
    assert m % tm == 0 and n % tn == 0
    nb = n // tn
    y_specs = [pl.BlockSpec((tm, y.shape[1]), lambda i, j: (i, 0)) for y in ys]
    w_specs = [pl.BlockSpec((w.shape[0], tn), lambda i, j: (0, j)) for w in ws]
    g_specs = [pl.BlockSpec((k, tn), functools.partial(lambda i, j, br: (0, j + br * nb), br=br))
               for br in range(N_BRANCH)]
    return pl.pallas_call(
        _gated_merge_kernel,
        grid=(m // tm, nb),
        in_specs=[pl.BlockSpec((tm, k), lambda i, j: (i, 0))] + y_specs + w_specs + g_specs,
        out_specs=pl.BlockSpec((tm, tn), lambda i, j: (i, j)),
        out_shape=jax.ShapeDtypeStruct((m, n), BF16),
        compiler_params=_params("parallel", "arbitrary"),
        name="gated_branch_merge",
    )(x, *ys, *ws, w_gate, w_gate, w_gate, w_gate)


def _is_power_of_two(x):
    return math.frexp(x)[0] == 0.5


def _sb_kernel(q_ref, k_ref, v_ref, o_ref, vT_ref, acc_ref, *, qblk, kblk, scale):
    i = pl.program_id(2)
    ratio = qblk // kblk
    fold_scale = _is_power_of_two(scale)

    @pl.when(i == 0)
    def _():
        for j in range(vT_ref.shape[0]):
            vT_ref[j] = v_ref[0, j * kblk:(j + 1) * kblk, :].T

    qT = q_ref[0].T
    if fold_scale:
        qT = qT * jnp.asarray(scale, qT.dtype)
    row = lax.broadcasted_iota(jnp.int32, (LANES, 1), 0)
    zero = jnp.zeros_like(qT)
    q_rhs = jnp.concatenate([jnp.where(row < HEAD_DIM, qT, zero), jnp.where(row >= HEAD_DIM, qT, zero)], axis=1)
    key_io = lax.broadcasted_iota(jnp.int32, (kblk, kblk), 0)
    qry_io = lax.broadcasted_iota(jnp.int32, (kblk, kblk), 1)
    later = jnp.where(qry_io > key_io, 1.0, 0.0).astype(BF16)
    acc_ref[...] = jnp.zeros_like(acc_ref)
    nsub = 2 * ratio

    def block(j, carry, diag_sub):
        start = pl.multiple_of(j * kblk, kblk)
        z_all = _dot(k_ref[0, pl.ds(start, kblk), :], q_rhs)
        vT = vT_ref[j]
        strict = key_io < qry_io
        stage = []
        for c in range(nsub):
            h, sub = divmod(c, ratio)
            if diag_sub is not None and sub < diag_sub:
                stage.append(None)
                continue
            masked = diag_sub is not None and sub == diag_sub
            z = z_all[:, c * kblk:(c + 1) * kblk]
            if not fold_scale:
                z = z * scale
            sp = jnp.maximum(jnp.log(1.0 + jnp.exp(jnp.minimum(z, SOFTPLUS_CLAMP))), z)
            log_1mb = jnp.where(strict, -sp, 0.0) if masked else -sp
            suf = _dot(later, log_1mb.astype(BF16))
            stage.append((z - sp, suf, jnp.sum(log_1mb, axis=0, keepdims=True), masked))
        new_carry = []
        for c in range(nsub):
            h, sub = divmod(c, ratio)
            if stage[c] is None:
                new_carry.append(carry[c])
                continue
            log_beta, suf, colsum, masked = stage[c]
            att = jnp.exp(log_beta + suf + carry[c])
            if masked:
                att = jnp.where(strict, att, 0.0)
            rows = slice(h * HEAD_DIM, (h + 1) * HEAD_DIM)
            cols = slice(sub * kblk, (sub + 1) * kblk)
            acc_ref[rows, cols] += _dot(vT[rows, :], att.astype(BF16))
            new_carry.append(carry[c] + colsum)
        return tuple(new_carry)

    carry = tuple(jnp.zeros((1, kblk), F32) for _ in range(nsub))
    for sub in reversed(range(ratio)):
        carry = block(i * ratio + sub, carry, sub)
    lax.fori_loop(0, i * ratio, lambda jj, c: block(i * ratio - 1 - jj, c, None), carry)
    o_ref[0] = acc_ref[...].T.astype(o_ref.dtype)


def stick_breaking_attention(qkv, col0=0, qblk=SB_QUERY_BLOCK, kblk=SB_KEY_BLOCK):
    bsz, seq, _ = qkv.shape
    width = SB_HEADS * HEAD_DIM
    npair = width // LANES
    qblk = min(qblk, seq)
    assert seq % qblk == 0 and qblk % kblk == 0 and col0 % LANES == 0
    qcol = col0 // LANES
    return pl.pallas_call(
        functools.partial(_sb_kernel, qblk=qblk, kblk=kblk, scale=HEAD_DIM ** -0.5),
        grid=(bsz, npair, seq // qblk),
        in_specs=[pl.BlockSpec((1, qblk, LANES), lambda b, p, i: (b, i, qcol + p)),
                  pl.BlockSpec((1, seq, LANES), lambda b, p, i: (b, 0, qcol + npair + p)),
                  pl.BlockSpec((1, seq, LANES), lambda b, p, i: (b, 0, qcol + 2 * npair + p))],
        out_specs=pl.BlockSpec((1, qblk, LANES), lambda b, p, i: (b, i, p)),
        out_shape=jax.ShapeDtypeStruct((bsz, seq, width), BF16),
        scratch_shapes=[pltpu.VMEM((seq // kblk, LANES, kblk), BF16),
                        pltpu.VMEM((LANES, qblk), F32)],
        compiler_params=_params("parallel", "parallel", "arbitrary"),
        name="stick_breaking_attention",
    )(qkv, qkv, qkv)


def _t5_bucket(dist):
    max_exact = REL_BUCKETS // 2
    d = jnp.maximum(dist, 1).astype(F32)
    large = max_exact + (jnp.log(d / max_exact) / math.log(REL_MAX_DIST / max_exact)
                         * (REL_BUCKETS - max_exact)).astype(jnp.int32)
    large = jnp.minimum(large, REL_BUCKETS - 1)
    return jnp.where(dist < max_exact, dist, large)


def _bias_of_distance(rel_bias, dist):
    one_hot = jax.nn.one_hot(_t5_bucket(dist), REL_BUCKETS, dtype=F32)
    return jnp.einsum("...b,bh->...h", one_hot, rel_bias.astype(F32), precision=lax.Precision.HIGHEST)


def _dilated_bias_tiles(rel_bias_group, dil, blk):
    a = jnp.arange(blk)[:, None]
    cc = jnp.arange(2 * blk)[None, :]
    step = blk + a - cc
    valid = (step >= 0) & (step <= blk)
    bias = _bias_of_distance(rel_bias_group, dil * jnp.clip(step, 0, blk))
    tile = jnp.where(valid[..., None], bias, NEG_INF)
    return jnp.transpose(tile, (2, 0, 1))


def _dsa_bias_tiles(rel_bias_dsa, blk):
    c = jnp.arange(blk)[:, None]
    a = jnp.arange(blk)[None, :]
    half = REL_BUCKETS // 2
    assert half + int(math.log((blk + 1) / half) / math.log(REL_MAX_DIST / half) * (REL_BUCKETS - half)) >= REL_BUCKETS - 1
    far = rel_bias_dsa[REL_BUCKETS - 1].astype(F32)
    own = jnp.where((a - c >= 0)[..., None], _bias_of_distance(rel_bias_dsa, jnp.maximum(a - c, 0)) - far, NEG_INF)
    prev = _bias_of_distance(rel_bias_dsa, blk + a - c) - far
    return jnp.transpose(jnp.stack([own, prev]), (3, 0, 1, 2))


def _dil_kernel(*refs, blk, dils, nbs, units, scale):
    ngroups = len(dils)
    nslab = 2 * ngroups
    q_refs, k_refs, v_refs = refs[:nslab], refs[nslab:2 * nslab], refs[2 * nslab:3 * nslab]
    bias_refs = refs[3 * nslab:3 * nslab + ngroups]
    y_ref, o_ref, lse_ref = refs[3 * nslab + ngroups:]
    lane = lax.broadcasted_iota(jnp.int32, (1, LANES), 1)
    fold_scale = _is_power_of_two(scale)
    ones = jnp.ones((blk, LANES), BF16)

    def run_step(step, _):
        work = []
        for g, u in [(g, u) for g in range(ngroups) for u in range(units)]:
            dil, nb = dils[g], nbs[g]
            unit = step * units + u
            c, i = unit // nb, unit % nb

            def rows(block, dil=dil, c=c):
                start = c + dil * blk * block
                return pl.ds(start, blk, stride=dil) if dil > 1 else pl.ds(pl.multiple_of(start, blk), blk)

            cur, prev = rows(i), rows(jnp.maximum(i - 1, 0))
            for half in range(2):
                slab = 2 * g + half
                q = q_refs[slab][0, cur, :]
                if fold_scale:
                    q = q * scale
                q = q.astype(BF16)
                kc, kp = k_refs[slab][0, cur, :].astype(BF16), k_refs[slab][0, prev, :].astype(BF16)
                zero = jnp.zeros_like(q)
                scores = []
                for e in range(2):
                    head = (lane < HEAD_DIM) if e == 0 else (lane >= HEAD_DIM)
                    qm = jnp.where(head, q, zero)
                    scores.append((_dot_nt(qm, kp), _dot_nt(qm, kc)))
                work.append((g, i, cur, prev, half, scores))
        soft = []
        for g, i, cur, prev, half, scores in work:
            has_prev = i > 0
            parts = []
            for e in range(2):
                h = 2 * half + e
                s_prev, s_cur = scores[e]
                if not fold_scale:
                    s_prev, s_cur = s_prev * scale, s_cur * scale
                s_prev = jnp.where(has_prev, s_prev + bias_refs[g][h, :, :blk], NEG_INF)
                s_cur = s_cur + bias_refs[g][h, :, blk:]
                m = jnp.maximum(jnp.max(s_prev, axis=-1, keepdims=True), jnp.max(s_cur, axis=-1, keepdims=True))
                parts.append((jnp.exp(s_prev - m).astype(BF16), jnp.exp(s_cur - m).astype(BF16), m))
            soft.append(parts)
        for (g, i, cur, prev, half, scores), parts in zip(work, soft):
            v_ref = v_refs[2 * g + half]
            vc = jnp.concatenate([v_ref[0, cur, :].astype(BF16), ones], axis=1)
            vp = jnp.concatenate([v_ref[0, prev, :].astype(BF16), ones], axis=1)
            out = jnp.zeros((blk, LANES), F32)
            lse_b = jnp.zeros((blk, LANES), F32)
            for e in range(2):
                head = (lane < HEAD_DIM) if e == 0 else (lane >= HEAD_DIM)
                p_prev, p_cur, m = parts[e]
                o2 = _dot(p_prev, vp) + _dot(p_cur, vc)
                denom = o2[:, LANES:]
                out = jnp.where(head, o2[:, :LANES] / denom, out)
                lse_b = jnp.where(head, m + jnp.log(denom), lse_b)
            o_ref[g, half, cur, :] = out
            lse_ref[g, half, cur, :] = lse_b
        return 0

    lax.fori_loop(0, dils[0] * nbs[0] // units, run_step, 0)

    def merge(r, _):
        rows_ = pl.ds(pl.multiple_of(r * DIL_MERGE_ROWS, DIL_MERGE_ROWS), DIL_MERGE_ROWS)
        for half in range(2):
            lses = [lse_ref[g, half, rows_, :] for g in range(ngroups)]
            top = functools.reduce(jnp.maximum, lses)
            ws = [jnp.exp(x - top) for x in lses]
            num = functools.reduce(lambda a, b: a + b, [ws[g] * o_ref[g, half, rows_, :] for g in range(ngroups)])
            den = functools.reduce(lambda a, b: a + b, ws)
            y_ref[0, rows_, half * LANES:(half + 1) * LANES] = (num / den).astype(y_ref.dtype)
        return 0

    lax.fori_loop(0, o_ref.shape[2] // DIL_MERGE_ROWS, merge, 0)


def dilated_attention(qkv, rel_bias_dil, blk=ATT_BLOCK, units=DIL_UNITS):
    bsz, seq, _ = qkv.shape
    hpg = DIL_HEADS_PER_GROUP
    assert hpg * HEAD_DIM == 2 * LANES and seq % DIL_MERGE_ROWS == 0
    ngroups = len(DIL_PATTERNS)
    nslab = DIL_HEADS * HEAD_DIM // LANES
    dils = tuple(d for _, d in DIL_PATTERNS)
    nbs = tuple(seq // d // blk for d in dils)
    nsteps = dils[0] * nbs[0]
    assert all(w // d == blk for w, d in DIL_PATTERNS) and all(d * n == nsteps for d, n in zip(dils, nbs))
    assert nsteps % units == 0
    tiles = [_dilated_bias_tiles(rel_bias_dil[:, g * hpg:(g + 1) * hpg], d, blk) for g, d in enumerate(dils)]
    slab = lambda col: pl.BlockSpec((1, seq, LANES), functools.partial(lambda b, col: (b, 0, col), col=col))
    in_specs = [slab(section * nslab + s) for section in range(3) for s in range(nslab)]
    in_specs += [pl.BlockSpec(t.shape, lambda b: (0, 0, 0), pipeline_mode=pl.Buffered(1)) for t in tiles]
    return pl.pallas_call(
        functools.partial(_dil_kernel, blk=blk, dils=dils, nbs=nbs, units=units, scale=HEAD_DIM ** -0.5),
        grid=(bsz,),
        in_specs=in_specs,
        out_specs=pl.BlockSpec((1, seq, 2 * LANES), lambda b: (b, 0, 0)),
        out_shape=jax.ShapeDtypeStruct((bsz, seq, 2 * LANES), BF16),
        scratch_shapes=[pltpu.VMEM((ngroups, 2, seq, LANES), F32),
                        pltpu.VMEM((ngroups, 2, seq, LANES), F32)],
        compiler_params=_params("parallel"),
        name="dilated_attention",
    )(*([qkv] * (3 * nslab)), *tiles)


def _ssm_kernel(u_ref, bb_ref, a_ref, cc_ref, d_ref, wg_ref, o_ref, x_ref, carry_ref, *, steps, strip):
    @pl.when(pl.program_id(1) == 0)
    def _():
        carry_ref[...] = jnp.zeros_like(carry_ref)

    u = u_ref[0]
    ub = u.astype(BF16)
    width = u.shape[1]
    half = width // 2
    nstates = x_ref.shape[1] // 2
    hs = nstates // 2
    for part in range(2):
        bu = _dot(ub[:, part * half:(part + 1) * half], bb_ref[part])
        x_ref[:, part * hs:(part + 1) * hs] = bu[:, :hs]
        x_ref[:, nstates + part * hs:nstates + (part + 1) * hs] = bu[:, hs:]
    for s in range(nstates // strip):
        re = slice(s * strip, (s + 1) * strip)
        im = slice(nstates + s * strip, nstates + (s + 1) * strip)
        ar = jnp.broadcast_to(a_ref[0:1, re], (SUBLANES, strip))
        ai = jnp.broadcast_to(a_ref[1:2, re], (SUBLANES, strip))
        xr = carry_ref[:, re]
        xi = carry_ref[:, im]
        for t in range(steps):
            rows = slice(t * SUBLANES, (t + 1) * SUBLANES)
            xr, xi = ar * xr - ai * xi + x_ref[rows, re], ar * xi + ai * xr + x_ref[rows, im]
            x_ref[rows, re] = xr
            x_ref[rows, im] = xi
        carry_ref[:, re] = xr
        carry_ref[:, im] = xi
    y_parts = []
    for part in range(2):
        xr = x_ref[:, part * hs:(part + 1) * hs].astype(BF16)
        xi = x_ref[:, nstates + part * hs:nstates + (part + 1) * hs].astype(BF16)
        y_parts.append(_dot(xr, cc_ref[0, part]) + _dot(xi, cc_ref[1, part]))
    y = jnp.concatenate(y_parts, axis=1) + d_ref[...] * u
    y = 0.5 * y * (1.0 + jnp.tanh(math.sqrt(2.0 / math.pi) * (y + 0.044715 * (y * y * y))))
    z = _dot(y.astype(BF16), wg_ref[...])
    o_ref[0] = (z[:, :width] * _sigmoid(z[:, width:])).astype(o_ref.dtype)


def _block_diag(blocks):
    g, r, c = blocks.shape
    eye = jnp.eye(g, dtype=blocks.dtype)
    return (eye[:, None, :, None] * blocks[:, :, None, :]).reshape(g * r, g * c)


def _ssm_tables(lam_re, lam_im, log_dt, b_re, b_im, c_re, c_im):
    lr, li = lam_re.astype(F32), lam_im.astype(F32)
    dt = jnp.exp(log_dt.astype(F32))[:, None]
    mag = jnp.exp(lr * dt)
    a_re, a_im = mag * jnp.cos(li * dt), mag * jnp.sin(li * dt)
    den = lr * lr + li * li
    f_re = ((a_re - 1.0) * lr + a_im * li) / den
    f_im = (a_im * lr - (a_re - 1.0) * li) / den
    br, bi = b_re.astype(F32), b_im.astype(F32)
    bb_re = _block_diag(jnp.transpose(f_re[..., None] * br - f_im[..., None] * bi, (0, 2, 1)))
    bb_im = _block_diag(jnp.transpose(f_re[..., None] * bi + f_im[..., None] * br, (0, 2, 1)))
    cc_re = _block_diag(jnp.transpose(c_re.astype(F32), (0, 2, 1)))
    cc_im = -_block_diag(jnp.transpose(c_im.astype(F32), (0, 2, 1)))
    nstates = SSM_GROUPS * SSM_STATE
    hw, hs = SSM_WIDTH // 2, nstates // 2
    bb = jnp.stack([jnp.concatenate([bb_re[p * hw:(p + 1) * hw, p * hs:(p + 1) * hs],
                                     bb_im[p * hw:(p + 1) * hw, p * hs:(p + 1) * hs]], axis=1) for p in range(2)])
    cc = jnp.stack([jnp.stack([m[p * hs:(p + 1) * hs, p * hw:(p + 1) * hw] for p in range(2)])
                    for m in (cc_re, cc_im)])
    a = jnp.stack([a_re.reshape(nstates), a_im.reshape(nstates)])
    return bb.astype(BF16), a, cc.astype(BF16)


def s5_glu(u, bb, a, cc, d_skip, w_glu, steps=SSM_STEPS, strip=SSM_STRIP):
    bsz, seq, width = u.shape
    nstates = a.shape[1]
    assert bsz % SUBLANES == 0 and seq % steps == 0 and nstates % strip == 0
    ngrp = bsz // SUBLANES
    rows = steps * SUBLANES
    ut = u.reshape(ngrp, SUBLANES, seq, width).transpose(0, 2, 1, 3).reshape(ngrp, seq * SUBLANES, width)
    const = lambda *shape: pl.BlockSpec(shape, lambda g, c: (0,) * len(shape))
    out = pl.pallas_call(
        functools.partial(_ssm_kernel, steps=steps, strip=strip),
        grid=(ngrp, seq // steps),
        in_specs=[pl.BlockSpec((1, rows, width), lambda g, c: (g, c, 0)),
                  const(*bb.shape), const(*a.shape), const(*cc.shape),
                  const(1, width), const(*w_glu.shape)],
        out_specs=pl.BlockSpec((1, rows, width), lambda g, c: (g, c, 0)),
        out_shape=jax.ShapeDtypeStruct((ngrp, seq * SUBLANES, width), BF16),
        scratch_shapes=[pltpu.VMEM((rows, 2 * nstates), F32),
                        pltpu.VMEM((SUBLANES, 2 * nstates), F32)],
        compiler_params=_params("parallel", "arbitrary"),
        name="s5_glu",
    )(ut, bb, a, cc, d_skip.reshape(1, width).astype(F32), w_glu)
    return out.reshape(ngrp, seq, SUBLANES, width).transpose(0, 2, 1, 3).reshape(bsz * seq, width)


def _dsa_kernel(q_ref, k_ref, v_ref, iq_ref, ik_ref, ikq_ref, bias_ref, o_ref,
                vT_ref, key_ref, khi_ref, acc_ref, *, blk, topk, scale):
    i = pl.program_id(1)
    nblk = i + 1
    key_io = lax.broadcasted_iota(jnp.int32, (blk, blk), 0)
    qry_io = lax.broadcasted_iota(jnp.int32, (blk, blk), 1)

    @pl.when(i == 0)
    def _():
        for j in range(vT_ref.shape[0]):
            vT_ref[j] = v_ref[0, j * blk:(j + 1) * blk, :].T

    iqT = iq_ref[0].T
    idx_rhs = jnp.concatenate([iqT[h * IDX_DIM:(h + 1) * IDX_DIM, :] for h in range(IDX_HEADS)], axis=1)
    iw = ikq_ref[0].T[IDX_DIM:IDX_DIM + IDX_HEADS, :]

    def score_block(j, _):
        start = pl.multiple_of(j * blk, blk)
        kk = ik_ref[0, pl.ds(start, blk), :][:, :IDX_DIM].astype(BF16)
        d = _dot(kk, idx_rhs)
        sc = jnp.zeros((blk, blk), F32)
        for h in range(IDX_HEADS):
            sc = sc + iw[h:h + 1, :] * jnp.maximum(d[:, h * blk:(h + 1) * blk], 0.0)
        sc = jnp.where((key_io + j * blk) <= (qry_io + i * blk), sc, NEG_INF)
        bits = pltpu.bitcast(sc, jnp.int32)
        key = bits ^ ((bits >> 31) & 0x7FFFFFFF)
        key_ref[j] = key
        khi_ref[j] = (key >> HALF_BITS).astype(jnp.int16)
        return 0

    lax.fori_loop(0, nblk, score_block, 0)

    @pl.when(nblk % 2 == 1)
    def _():
        key_ref[nblk] = jnp.full((blk, blk), INT_MIN, jnp.int32)
        khi_ref[nblk] = jnp.full((blk, blk), INT16_MIN, jnp.int16)

    npairs = (nblk + 1) // 2

    def count(pred):
        def body(p, acc):
            for j in (2 * p, 2 * p + 1):
                hit = jnp.where(pred(key_ref[j]), 1.0, 0.0)
                acc = acc + jnp.sum(hit.reshape(blk // 8, 8, blk), axis=0)
            return acc
        return jnp.sum(lax.fori_loop(0, npairs, body, jnp.zeros((8, blk), F32)), axis=0, keepdims=True)

    def count_half(cand):
        one, zero = jnp.ones((), BF16), jnp.zeros((), BF16)
        rows = 2 * SUBLANES

        def body(p, acc):
            for j in (2 * p, 2 * p + 1):
                hit = jnp.where(khi_ref[j] >= cand, one, zero)
                for r in range(blk // rows):
                    acc = acc + hit[r * rows:(r + 1) * rows, :]
            return acc
        acc = lax.fori_loop(0, npairs, body, jnp.zeros((rows, blk), BF16))
        return jnp.sum(acc.astype(F32), axis=0, keepdims=True)

    def half_step(b, thr):
        cand = thr + lax.shift_left(jnp.int32(1), HALF_BITS - 1 - b)
        return jnp.where(count_half(cand.astype(jnp.int16)) >= topk, cand, thr)

    start16 = jnp.full((1, blk), INT16_MIN, jnp.int32)
    thr_hi = lax.fori_loop(0, HALF_BITS, half_step, start16)
    thr_hi16 = thr_hi.astype(jnp.int16)

    def lower_halves(j, _):
        hi = khi_ref[j]
        lo = ((key_ref[j] & 0xFFFF) + INT16_MIN).astype(jnp.int16)
        khi_ref[j] = jnp.where(hi > thr_hi16, jnp.int16(-INT16_MIN - 1), jnp.where(hi == thr_hi16, lo, jnp.int16(INT16_MIN)))
        return 0

    lax.fori_loop(0, 2 * npairs, lower_halves, 0)
    thr_lo = lax.fori_loop(0, HALF_BITS, half_step, start16)
    thr = lax.shift_left(thr_hi, HALF_BITS) + (thr_lo - INT16_MIN)

    need = topk - count(lambda key: key > thr)
    upto = jnp.where(qry_io <= key_io, 1.0, 0.0).astype(BF16)

    def select_block(j, run):
        key = key_ref[j]
        tie = jnp.where(key == thr, 1.0, 0.0)
        rank = _dot(upto, tie.astype(BF16)) + run
        keep_tie = jnp.where(rank <= need, 0.0, NEG_INF)
        madd = jnp.where(key > thr, 0.0, jnp.where(key == thr, keep_tie, NEG_INF))
        return madd, run + jnp.sum(tie, axis=0, keepdims=True)

    fold_scale = _is_power_of_two(scale)
    qT = q_ref[0].T
    if fold_scale:
        qT = qT * jnp.asarray(scale, qT.dtype)
    row = lax.broadcasted_iota(jnp.int32, (LANES, 1), 0)
    q_rhs = []
    for g in range(DSA_HEADS // 2):
        pair = qT[g * LANES:(g + 1) * LANES, :]
        zero = jnp.zeros_like(pair)
        q_rhs.append(jnp.concatenate([jnp.where(row < HEAD_DIM, pair, zero),
                                      jnp.where(row >= HEAD_DIM, pair, zero)], axis=1))
    acc_ref[...] = jnp.zeros_like(acc_ref)

    def attend_block(j, carry, near):
        m_all, l_all, run = carry
        start = pl.multiple_of(j * blk, blk)
        kblk = k_ref[0, pl.ds(start, blk), :]
        vT = vT_ref[j]
        madd, run = select_block(j, run)
        s2 = [_dot(kblk[:, g * LANES:(g + 1) * LANES], q_rhs[g]) for g in range(DSA_HEADS // 2)]
        m_out, l_out, probs, alphas = [], [], [], []
        for h in range(DSA_HEADS):
            s = s2[h // 2][:, (h % 2) * blk:(h % 2 + 1) * blk]
            if not fold_scale:
                s = s * scale
            s = s + madd
            if near:
                s = s + bias_ref[h, i - j]
            m_new = jnp.maximum(m_all[h], jnp.max(s, axis=0, keepdims=True))
            p = jnp.exp(s - m_new)
            alpha = jnp.exp(m_all[h] - m_new)
            l_out.append(alpha * l_all[h] + jnp.sum(p, axis=0, keepdims=True))
            m_out.append(m_new)
            probs.append(p.astype(BF16))
            alphas.append(alpha)
        for h in range(DSA_HEADS):
            rows = slice(h * HEAD_DIM, (h + 1) * HEAD_DIM)
            acc_ref[rows, :] = alphas[h] * acc_ref[rows, :] + _dot(vT[rows, :], probs[h])
        return tuple(m_out), tuple(l_out), run

    init = (tuple(jnp.full((1, blk), NEG_INF, F32) for _ in range(DSA_HEADS)),
            tuple(jnp.zeros((1, blk), F32) for _ in range(DSA_HEADS)), jnp.zeros((1, blk), F32))
    nfar = jnp.maximum(i - 1, 0)
    carry = lax.fori_loop(0, nfar, functools.partial(attend_block, near=False), init)
    _, l_all, _ = lax.fori_loop(nfar, nblk, functools.partial(attend_block, near=True), carry)
    for h in range(DSA_HEADS):
        rows = slice(h * HEAD_DIM, (h + 1) * HEAD_DIM)
        acc_ref[rows, :] = acc_ref[rows, :] / l_all[h]
    o_ref[0] = acc_ref[...].T.astype(o_ref.dtype)


def dsa_attention(proj, qkv_col, iq_col, proj_f32, kw_col, bias_tiles, blk=DSA_BLOCK):
    bsz, seq, _ = proj.shape
    width = DSA_HEADS * HEAD_DIM
    nblk = seq // blk
    topk = min(DSA_TOPK, seq // 4)
    iq_width = IDX_HEADS * IDX_DIM
    assert qkv_col % width == 0 and iq_col % iq_width == 0 and kw_col % LANES == 0
    assert nblk % 2 == 0 and topk <= blk
    assert nblk * blk // (2 * SUBLANES) <= 256
    qcol = qkv_col // width
    return pl.pallas_call(
        functools.partial(_dsa_kernel, blk=blk, topk=topk, scale=HEAD_DIM ** -0.5),
        grid=(bsz, nblk),
        in_specs=[pl.BlockSpec((1, blk, width), lambda b, i: (b, i, qcol)),
                  pl.BlockSpec((1, seq, width), lambda b, i: (b, 0, qcol + 1)),
                  pl.BlockSpec((1, seq, width), lambda b, i: (b, 0, qcol + 2)),
                  pl.BlockSpec((1, blk, iq_width), lambda b, i: (b, i, iq_col // iq_width)),
                  pl.BlockSpec((1, seq, LANES), lambda b, i: (b, 0, kw_col // LANES)),
                  pl.BlockSpec((1, blk, LANES), lambda b, i: (b, i, kw_col // LANES)),
                  pl.BlockSpec((DSA_HEADS, 2, blk, blk), lambda b, i: (0, 0, 0, 0))],
        out_specs=pl.BlockSpec((1, blk, width), lambda b, i: (b, i, 0)),
        out_shape=jax.ShapeDtypeStruct((bsz, seq, width), BF16),
        scratch_shapes=[pltpu.VMEM((nblk, width, blk), BF16),
                        pltpu.VMEM((nblk, blk, blk), jnp.int32),
                        pltpu.VMEM((nblk, blk, blk), jnp.int16),
                        pltpu.VMEM((width, blk), F32)],
        compiler_params=_params("parallel", "arbitrary"),
        name="dsa_attention",
    )(proj, proj, proj, proj, proj_f32, proj_f32, bias_tiles)


def hybrid_mixer(xbf, bsz, seq, w_in, rel_bias, ssm_params, d_skip, w_glu, w_branches):
    offs = [0]
    for width in IN_SPLITS:
        offs.append(offs[-1] + width)
    seg = lambda a: w_in[:, offs[a]:offs[a + 1]]
    w_bf = jnp.concatenate([seg(0), seg(3), seg(4)], axis=1).astype(BF16)
    pad = jnp.zeros((w_in.shape[0], LANES - IDX_DIM - IDX_HEADS), w_in.dtype)
    w_f32 = jnp.concatenate([seg(1), seg(2), seg(5), seg(6), pad], axis=1).astype(BF16)
    dsa_col = IN_SPLITS[0]
    iq_col = dsa_col + IN_SPLITS[3]
    ssm_col = IN_SPLITS[1]
    kw_col = ssm_col + IN_SPLITS[2]
    proj = matmul(xbf, w_bf, BF16).reshape(bsz, seq, -1)
    proj_f32 = matmul(xbf, w_f32, F32).reshape(bsz, seq, -1)

    y_sb = stick_breaking_attention(proj).reshape(bsz * seq, -1)

    y_dil = dilated_attention(proj_f32, rel_bias[:, :DIL_HEADS]).reshape(bsz * seq, -1)

    bb, a_bar, cc = _ssm_tables(*ssm_params)
    y_ssm = s5_glu(proj_f32[..., ssm_col:ssm_col + SSM_WIDTH], bb, a_bar, cc, d_skip, w_glu.astype(BF16))

    y_dsa = dsa_attention(proj, dsa_col, iq_col, proj_f32, kw_col, _dsa_bias_tiles(rel_bias[:, DIL_HEADS:], DSA_BLOCK))
    y_dsa = y_dsa.reshape(bsz * seq, -1)

    return gated_branch_merge(xbf, [y_sb, y_dil, y_ssm, y_dsa], [w.astype(BF16) for w in w_branches],
                              seg(7).astype(BF16))


def kernel(x, ln_g, ln_b, ffn1_w_up, ffn1_w_down, w_in, rel_bias, ssm_lam_re, ssm_lam_im, ssm_log_dt,
           ssm_b_re, ssm_b_im, ssm_c_re, ssm_c_im, ssm_d, ssm_w_glu, w_br_sb, w_br_dil, w_br_ssm, w_br_dsa,
           w_out, ffn2_w_up, ffn2_w_down):
    bsz, seq, d = x.shape
    xf = x.reshape(bsz * seq, d)
    xbf = xf
    up1, down1 = ffn1_w_up.astype(BF16), ffn1_w_down.astype(BF16)
    up2, down2 = ffn2_w_up.astype(BF16), ffn2_w_down.astype(BF16)
    out_w = w_out.astype(BF16)
    for l in range(DEPTH):
        h = ffn_up(xbf, up1, l)
        xf, xbf = matmul_residual_layernorm(h, down1, l, xf, ln_g[l, 0], ln_b[l, 0], MACARON)
        merged = hybrid_mixer(xbf, bsz, seq, w_in[l], rel_bias,
                              (ssm_lam_re[l], ssm_lam_im[l], ssm_log_dt[l], ssm_b_re[l], ssm_b_im[l],
                               ssm_c_re[l], ssm_c_im[l]), ssm_d[l], ssm_w_glu[l],
                              (w_br_sb[l], w_br_dil[l], w_br_ssm[l], w_br_dsa[l]))
        xf, xbf = matmul_residual_layernorm(merged, out_w, l, xf, ln_g[l, 1], ln_b[l, 1], 1.0)
        h = ffn_up(xbf, up2, l)
        xf, xbf = matmul_residual_layernorm(h, down2, l, xf, ln_g[l, 2], ln_b[l, 2], MACARON)
    return xf.reshape(bsz, seq, d)
```

```python
import functools
import math

import jax
import jax.numpy as jnp
from jax import lax
from jax.experimental import pallas as pl
from jax.experimental.pallas import tpu as pltpu

F32 = jnp.float32
BF16 = jnp.bfloat16

D_MODEL = 2048
DEPTH = 2
HEAD_DIM = 64
SB_HEADS = 8
DIL_PATTERNS = ((128, 1), (512, 4), (2048, 16))
DIL_HEADS_PER_GROUP = 4
DIL_HEADS = DIL_HEADS_PER_GROUP * len(DIL_PATTERNS)
SSM_WIDTH = 512
SSM_GROUP = 16
SSM_GROUPS = SSM_WIDTH // SSM_GROUP
SSM_STATE = 64
DSA_HEADS = 8
IDX_HEADS = 8
IDX_DIM = 64
DSA_TOPK = 256
N_BRANCH = 4
D_FF = 5632
REL_BUCKETS = 32
REL_MAX_DIST = 128
DN_ALPHA = (2.0 * DEPTH) ** 0.25
LN_EPS = 1e-5
NEG_INF = -1e30
MACARON = 0.5

IN_SPLITS = (3 * SB_HEADS * HEAD_DIM, 3 * DIL_HEADS * HEAD_DIM, SSM_WIDTH, 3 * DSA_HEADS * HEAD_DIM,
             IDX_HEADS * IDX_DIM, IDX_DIM, IDX_HEADS, N_BRANCH * D_MODEL)

LANES = 128
ATT_BLOCK = 128
DIL_UNITS = 1
DIL_MERGE_ROWS = 512
DSA_BLOCK = 256
SB_QUERY_BLOCK = 2048
SB_KEY_BLOCK = 256
SUBLANES = 8
SSM_STEPS = 32
SSM_STRIP = 512
VMEM_LIMIT = 56 * 1024 * 1024
INT_MIN = -2 ** 31
HALF_BITS = 16
INT16_MIN = -2 ** 15
SOFTPLUS_CLAMP = 80.0

_NT = (((1,), (1,)), ((), ()))


def _dot(a, b):
    return jnp.dot(a, b, preferred_element_type=F32)


def _dot_nt(a, b):
    return lax.dot_general(a, b, _NT, preferred_element_type=F32)


def _params(*sem):
    return pltpu.CompilerParams(dimension_semantics=sem, vmem_limit_bytes=VMEM_LIMIT)


def _sigmoid(x):
    return 1.0 / (1.0 + jnp.exp(-x))


def _mm_kernel(x_ref, w_ref, o_ref):
    o_ref[...] = _dot(x_ref[...], w_ref[...]).astype(o_ref.dtype)


def matmul(x, w, out_dtype, tm=512):
    m, k = x.shape
    n = w.shape[1]
    assert m % tm == 0 and n % LANES == 0
    return pl.pallas_call(
        _mm_kernel,
        grid=(m // tm,),
        in_specs=[pl.BlockSpec((tm, k), lambda i: (i, 0)),
                  pl.BlockSpec((k, n), lambda i: (0, 0), pipeline_mode=pl.Buffered(1))],
        out_specs=pl.BlockSpec((tm, n), lambda i: (i, 0)),
        out_shape=jax.ShapeDtypeStruct((m, n), out_dtype),
        compiler_params=_params("parallel"),
        name="matmul",
    )(x, w)


def _ffn_up_kernel(x_ref, wa_ref, wb_ref, o_ref):
    x = x_ref[...].astype(BF16)
    a = _dot(x, wa_ref[...])
    b = _dot(x, wb_ref[...])
    o_ref[...] = (a * _sigmoid(a) * b).astype(o_ref.dtype)


def ffn_up(x, w_up, layer, tm=1024, tn=512):
    m, k = x.shape
    f = w_up.shape[2] // 2
    assert m % tm == 0 and f % tn == 0
    nb = f // tn
    return pl.pallas_call(
        _ffn_up_kernel,
        grid=(m // tm, nb),
        in_specs=[pl.BlockSpec((tm, k), lambda i, j: (i, 0)),
                  pl.BlockSpec((None, k, tn), lambda i, j: (layer, 0, j)),
                  pl.BlockSpec((None, k, tn), lambda i, j: (layer, 0, j + nb))],
        out_specs=pl.BlockSpec((tm, tn), lambda i, j: (i, j)),
        out_shape=jax.ShapeDtypeStruct((m, f), BF16),
        compiler_params=_params("parallel", "arbitrary"),
        name="ffn_up",
    )(x, w_up, w_up)


def _mm_res_ln_kernel(h_ref, w_ref, x_ref, g_ref, b_ref, o_ref, obf_ref, *, scale):
    y = DN_ALPHA * x_ref[...] + scale * _dot(h_ref[...], w_ref[...])
    mu = jnp.mean(y, axis=-1, keepdims=True)
    yc = y - mu
    var = jnp.mean(yc * yc, axis=-1, keepdims=True)
    out = yc * lax.rsqrt(var + LN_EPS) * g_ref[...] + b_ref[...]
    o_ref[...] = out
    obf_ref[...] = out.astype(BF16)


def matmul_residual_layernorm(h, w, layer, x, g, b, scale, tm=256):
    m, k = h.shape
    n = w.shape[2]
    assert m % tm == 0
    return pl.pallas_call(
        functools.partial(_mm_res_ln_kernel, scale=scale),
        grid=(m // tm,),
        in_specs=[pl.BlockSpec((tm, k), lambda i: (i, 0)),
                  pl.BlockSpec((None, k, n), lambda i: (layer, 0, 0), pipeline_mode=pl.Buffered(1)),
                  pl.BlockSpec((tm, n), lambda i: (i, 0)),
                  pl.BlockSpec((1, n), lambda i: (0, 0)),
                  pl.BlockSpec((1, n), lambda i: (0, 0))],
        out_specs=[pl.BlockSpec((tm, n), lambda i: (i, 0)),
                   pl.BlockSpec((tm, n), lambda i: (i, 0))],
        out_shape=[jax.ShapeDtypeStruct((m, n), F32), jax.ShapeDtypeStruct((m, n), BF16)],
        compiler_params=_params("parallel"),
        name="matmul_residual_layernorm",
    )(h, w, x, g.reshape(1, n), b.reshape(1, n))


def _gated_merge_kernel(x_ref, ysb_ref, ydil_ref, yssm_ref, ydsa_ref, wsb_ref, wdil_ref, wssm_ref, wdsa_ref,
                        g0_ref, g1_ref, g2_ref, g3_ref, o_ref):
    x = x_ref[...]
    acc = _sigmoid(_dot(x, g0_ref[...])) * _dot(ysb_ref[...], wsb_ref[...])
    acc += _sigmoid(_dot(x, g1_ref[...])) * _dot(ydil_ref[...], wdil_ref[...])
    acc += _sigmoid(_dot(x, g2_ref[...])) * _dot(yssm_ref[...], wssm_ref[...])
    acc += _sigmoid(_dot(x, g3_ref[...])) * _dot(ydsa_ref[...], wdsa_ref[...])
    o_ref[...] = acc.astype(o_ref.dtype)


def gated_branch_merge(x, ys, ws, w_gate, tm=1024, tn=512):
    m, k = x.shape
    n = ws[0].shape[1]
    assert m % tm == 0 and n % tn == 0
    nb = n // tn
    y_specs = [pl.BlockSpec((tm, y.shape[1]), lambda i, j: (i, 0)) for y in ys]
    w_specs = [pl.BlockSpec((w.shape[0], tn), lambda i, j: (0, j)) for w in ws]
    g_specs = [pl.BlockSpec((k, tn), functools.partial(lambda i, j, br: (0, j + br * nb), br=br))
               for br in range(N_BRANCH)]
    return pl.pallas_call(
        _gated_merge_kernel,
        grid=(m // tm, nb),
        in_specs=[pl.BlockSpec((tm, k), lambda i, j: (i, 0))] + y_specs + w_specs + g_specs,
        out_specs=pl.BlockSpec((tm, tn), lambda i, j: (i, j)),
        out_shape=jax.ShapeDtypeStruct((m, n), BF16),
        compiler_params=_params("parallel", "arbitrary"),
        name="gated_branch_merge",
    )(x, *ys, *ws, w_gate, w_gate, w_gate, w_gate)


def _is_power_of_two(x):
    return math.frexp(x)[0] == 0.5


def _sb_kernel(q_ref, k_ref, v_ref, o_ref, vT_ref, acc_ref, *, qblk, kblk, scale):
    i = pl.program_id(2)
    ratio = qblk // kblk
    fold_scale = _is_power_of_two(scale)

    @pl.when(i == 0)
    def _():
        for j in range(vT_ref.shape[0]):
            vT_ref[j] = v_ref[0, j * kblk:(j + 1) * kblk, :].T

    qT = q_ref[0].T
    if fold_scale:
        qT = qT * jnp.asarray(scale, qT.dtype)
    row = lax.broadcasted_iota(jnp.int32, (LANES, 1), 0)
    zero = jnp.zeros_like(qT)
    q_rhs = jnp.concatenate([jnp.where(row < HEAD_DIM, qT, zero), jnp.where(row >= HEAD_DIM, qT, zero)], axis=1)
    key_io = lax.broadcasted_iota(jnp.int32, (kblk, kblk), 0)
    qry_io = lax.broadcasted_iota(jnp.int32, (kblk, kblk), 1)
    later = jnp.where(qry_io > key_io, 1.0, 0.0).astype(BF16)
    acc_ref[...] = jnp.zeros_like(acc_ref)
    nsub = 2 * ratio

    def block(j, carry, diag_sub):
        start = pl.multiple_of(j * kblk, kblk)
        z_all = _dot(k_ref[0, pl.ds(start, kblk), :], q_rhs)
        vT = vT_ref[j]
        strict = key_io < qry_io
        stage = []
        for c in range(nsub):
            h, sub = divmod(c, ratio)
            if diag_sub is not None and sub < diag_sub:
                stage.append(None)
                continue
            masked = diag_sub is not None and sub == diag_sub
            z = z_all[:, c * kblk:(c + 1) * kblk]
            if not fold_scale:
                z = z * scale
            sp = jnp.maximum(jnp.log(1.0 + jnp.exp(jnp.minimum(z, SOFTPLUS_CLAMP))), z)
            log_1mb = jnp.where(strict, -sp, 0.0) if masked else -sp
            suf = _dot(later, log_1mb.astype(BF16))
            stage.append((z - sp, suf, jnp.sum(log_1mb, axis=0, keepdims=True), masked))
        new_carry = []
        for c in range(nsub):
            h, sub = divmod(c, ratio)
            if stage[c] is None:
                new_carry.append(carry[c])
                continue
            log_beta, suf, colsum, masked = stage[c]
            att = jnp.exp(log_beta + suf + carry[c])
            if masked:
                att = jnp.where(strict, att, 0.0)
            rows = slice(h * HEAD_DIM, (h + 1) * HEAD_DIM)
            cols = slice(sub * kblk, (sub + 1) * kblk)
            acc_ref[rows, cols] += _dot(vT[rows, :], att.astype(BF16))
            new_carry.append(carry[c] + colsum)
        return tuple(new_carry)

    carry = tuple(jnp.zeros((1, kblk), F32) for _ in range(nsub))
    for sub in reversed(range(ratio)):
        carry = block(i * ratio + sub, carry, sub)
    lax.fori_loop(0, i * ratio, lambda jj, c: block(i * ratio - 1 - jj, c, None), carry)
    o_ref[0] = acc_ref[...].T.astype(o_ref.dtype)


def stick_breaking_attention(qkv, col0=0, qblk=SB_QUERY_BLOCK, kblk=SB_KEY_BLOCK):
    bsz, seq, _ = qkv.shape
    width = SB_HEADS * HEAD_DIM
    npair = width // LANES
    qblk = min(qblk, seq)
    assert seq % qblk == 0 and qblk % kblk == 0 and col0 % LANES == 0
    qcol = col0 // LANES
    return pl.pallas_call(
        functools.partial(_sb_kernel, qblk=qblk, kblk=kblk, scale=HEAD_DIM ** -0.5),
        grid=(bsz, npair, seq // qblk),
        in_specs=[pl.BlockSpec((1, qblk, LANES), lambda b, p, i: (b, i, qcol + p)),
                  pl.BlockSpec((1, seq, LANES), lambda b, p, i: (b, 0, qcol + npair + p)),
                  pl.BlockSpec((1, seq, LANES), lambda b, p, i: (b, 0, qcol + 2 * npair + p))],
        out_specs=pl.BlockSpec((1, qblk, LANES), lambda b, p, i: (b, i, p)),
        out_shape=jax.ShapeDtypeStruct((bsz, seq, width), BF16),
        scratch_shapes=[pltpu.VMEM((seq // kblk, LANES, kblk), BF16),
                        pltpu.VMEM((LANES, qblk), F32)],
        compiler_params=_params("parallel", "parallel", "arbitrary"),
        name="stick_breaking_attention",
    )(qkv, qkv, qkv)


def _t5_bucket(dist):
    max_exact = REL_BUCKETS // 2
    d = jnp.maximum(dist, 1).astype(F32)
    large = max_exact + (jnp.log(d / max_exact) / math.log(REL_MAX_DIST / max_exact)
                         * (REL_BUCKETS - max_exact)).astype(jnp.int32)
    large = jnp.minimum(large, REL_BUCKETS - 1)
    return jnp.where(dist < max_exact, dist, large)


def _bias_of_distance(rel_bias, dist):
    one_hot = jax.nn.one_hot(_t5_bucket(dist), REL_BUCKETS, dtype=F32)
    return jnp.einsum("...b,bh->...h", one_hot, rel_bias.astype(F32), precision=lax.Precision.HIGHEST)


def _dilated_bias_tiles(rel_bias_group, dil, blk):
    a = jnp.arange(blk)[:, None]
    cc = jnp.arange(2 * blk)[None, :]
    step = blk + a - cc
    valid = (step >= 0) & (step <= blk)
    bias = _bias_of_distance(rel_bias_group, dil * jnp.clip(step, 0, blk))
    tile = jnp.where(valid[..., None], bias, NEG_INF)
    return jnp.transpose(tile, (2, 0, 1))


def _dsa_bias_tiles(rel_bias_dsa, blk):
    c = jnp.arange(blk)[:, None]
    a = jnp.arange(blk)[None, :]
    own = jnp.where((a - c >= 0)[..., None], _bias_of_distance(rel_bias_dsa, jnp.maximum(a - c, 0)), NEG_INF)
    prev = _bias_of_distance(rel_bias_dsa, blk + a - c)
    half = REL_BUCKETS // 2
    assert half + int(math.log((blk + 1) / half) / math.log(REL_MAX_DIST / half) * (REL_BUCKETS - half)) >= REL_BUCKETS - 1
    far = jnp.broadcast_to(rel_bias_dsa[REL_BUCKETS - 1].astype(F32), prev.shape)
    return jnp.transpose(jnp.stack([own, prev, far]), (3, 0, 1, 2))


def _dil_kernel(*refs, blk, dils, nbs, units, scale):
    ngroups = len(dils)
    nslab = 2 * ngroups
    q_refs, k_refs, v_refs = refs[:nslab], refs[nslab:2 * nslab], refs[2 * nslab:3 * nslab]
    bias_refs = refs[3 * nslab:3 * nslab + ngroups]
    y_ref, o_ref, lse_ref = refs[3 * nslab + ngroups:]
    lane = lax.broadcasted_iota(jnp.int32, (1, LANES), 1)
    fold_scale = _is_power_of_two(scale)
    ones = jnp.ones((blk, LANES), BF16)

    def run_step(step, _):
        work = []
        for g, u in [(g, u) for g in range(ngroups) for u in range(units)]:
            dil, nb = dils[g], nbs[g]
            unit = step * units + u
            c, i = unit // nb, unit % nb

            def rows(block, dil=dil, c=c):
                start = c + dil * blk * block
                return pl.ds(start, blk, stride=dil) if dil > 1 else pl.ds(pl.multiple_of(start, blk), blk)

            cur, prev = rows(i), rows(jnp.maximum(i - 1, 0))
            for half in range(2):
                slab = 2 * g + half
                q = q_refs[slab][0, cur, :]
                if fold_scale:
                    q = q * scale
                q = q.astype(BF16)
                kc, kp = k_refs[slab][0, cur, :].astype(BF16), k_refs[slab][0, prev, :].astype(BF16)
                zero = jnp.zeros_like(q)
                scores = []
                for e in range(2):
                    head = (lane < HEAD_DIM) if e == 0 else (lane >= HEAD_DIM)
                    qm = jnp.where(head, q, zero)
                    scores.append((_dot_nt(qm, kp), _dot_nt(qm, kc)))
                work.append((g, i, cur, prev, half, scores))
        soft = []
        for g, i, cur, prev, half, scores in work:
            has_prev = i > 0
            parts = []
            for e in range(2):
                h = 2 * half + e
                s_prev, s_cur = scores[e]
                if not fold_scale:
                    s_prev, s_cur = s_prev * scale, s_cur * scale
                s_prev = jnp.where(has_prev, s_prev + bias_refs[g][h, :, :blk], NEG_INF)
                s_cur = s_cur + bias_refs[g][h, :, blk:]
                m = jnp.maximum(jnp.max(s_prev, axis=-1, keepdims=True), jnp.max(s_cur, axis=-1, keepdims=True))
                parts.append((jnp.exp(s_prev - m).astype(BF16), jnp.exp(s_cur - m).astype(BF16), m))
            soft.append(parts)
        for (g, i, cur, prev, half, scores), parts in zip(work, soft):
            v_ref = v_refs[2 * g + half]
            vc = jnp.concatenate([v_ref[0, cur, :].astype(BF16), ones], axis=1)
            vp = jnp.concatenate([v_ref[0, prev, :].astype(BF16), ones], axis=1)
            out = jnp.zeros((blk, LANES), F32)
            lse_b = jnp.zeros((blk, LANES), F32)
            for e in range(2):
                head = (lane < HEAD_DIM) if e == 0 else (lane >= HEAD_DIM)
                p_prev, p_cur, m = parts[e]
                o2 = _dot(p_prev, vp) + _dot(p_cur, vc)
                denom = o2[:, LANES:]
                out = jnp.where(head, o2[:, :LANES] / denom, out)
                lse_b = jnp.where(head, m + jnp.log(denom), lse_b)
            o_ref[g, half, cur, :] = out
            lse_ref[g, half, cur, :] = lse_b
        return 0

    lax.fori_loop(0, dils[0] * nbs[0] // units, run_step, 0)

    def merge(r, _):
        rows_ = pl.ds(pl.multiple_of(r * DIL_MERGE_ROWS, DIL_MERGE_ROWS), DIL_MERGE_ROWS)
        for half in range(2):
            lses = [lse_ref[g, half, rows_, :] for g in range(ngroups)]
            top = functools.reduce(jnp.maximum, lses)
            ws = [jnp.exp(x - top) for x in lses]
            num = functools.reduce(lambda a, b: a + b, [ws[g] * o_ref[g, half, rows_, :] for g in range(ngroups)])
            den = functools.reduce(lambda a, b: a + b, ws)
            y_ref[0, rows_, half * LANES:(half + 1) * LANES] = (num / den).astype(y_ref.dtype)
        return 0

    lax.fori_loop(0, o_ref.shape[2] // DIL_MERGE_ROWS, merge, 0)


def dilated_attention(qkv, rel_bias_dil, blk=ATT_BLOCK, units=DIL_UNITS):
    bsz, seq, _ = qkv.shape
    hpg = DIL_HEADS_PER_GROUP
    assert hpg * HEAD_DIM == 2 * LANES and seq % DIL_MERGE_ROWS == 0
    ngroups = len(DIL_PATTERNS)
    nslab = DIL_HEADS * HEAD_DIM // LANES
    dils = tuple(d for _, d in DIL_PATTERNS)
    nbs = tuple(seq // d // blk for d in dils)
    nsteps = dils[0] * nbs[0]
    assert all(w // d == blk for w, d in DIL_PATTERNS) and all(d * n == nsteps for d, n in zip(dils, nbs))
    assert nsteps % units == 0
    tiles = [_dilated_bias_tiles(rel_bias_dil[:, g * hpg:(g + 1) * hpg], d, blk) for g, d in enumerate(dils)]
    slab = lambda col: pl.BlockSpec((1, seq, LANES), functools.partial(lambda b, col: (b, 0, col), col=col))
    in_specs = [slab(section * nslab + s) for section in range(3) for s in range(nslab)]
    in_specs += [pl.BlockSpec(t.shape, lambda b: (0, 0, 0), pipeline_mode=pl.Buffered(1)) for t in tiles]
    return pl.pallas_call(
        functools.partial(_dil_kernel, blk=blk, dils=dils, nbs=nbs, units=units, scale=HEAD_DIM ** -0.5),
        grid=(bsz,),
        in_specs=in_specs,
        out_specs=pl.BlockSpec((1, seq, 2 * LANES), lambda b: (b, 0, 0)),
        out_shape=jax.ShapeDtypeStruct((bsz, seq, 2 * LANES), BF16),
        scratch_shapes=[pltpu.VMEM((ngroups, 2, seq, LANES), F32),
                        pltpu.VMEM((ngroups, 2, seq, LANES), F32)],
        compiler_params=_params("parallel"),
        name="dilated_attention",
    )(*([qkv] * (3 * nslab)), *tiles)


def _ssm_kernel(u_ref, bb_ref, a_ref, cc_ref, d_ref, wg_ref, o_ref, x_ref, carry_ref, *, steps, strip):
    @pl.when(pl.program_id(1) == 0)
    def _():
        carry_ref[...] = jnp.zeros_like(carry_ref)

    u = u_ref[0]
    ub = u.astype(BF16)
    width = u.shape[1]
    half = width // 2
    nstates = x_ref.shape[1] // 2
    hs = nstates // 2
    for part in range(2):
        bu = _dot(ub[:, part * half:(part + 1) * half], bb_ref[part])
        x_ref[:, part * hs:(part + 1) * hs] = bu[:, :hs]
        x_ref[:, nstates + part * hs:nstates + (part + 1) * hs] = bu[:, hs:]
    for s in range(nstates // strip):
        re = slice(s * strip, (s + 1) * strip)
        im = slice(nstates + s * strip, nstates + (s + 1) * strip)
        ar = jnp.broadcast_to(a_ref[0:1, re], (SUBLANES, strip))
        ai = jnp.broadcast_to(a_ref[1:2, re], (SUBLANES, strip))
        xr = carry_ref[:, re]
        xi = carry_ref[:, im]
        for t in range(steps):
            rows = slice(t * SUBLANES, (t + 1) * SUBLANES)
            xr, xi = ar * xr - ai * xi + x_ref[rows, re], ar * xi + ai * xr + x_ref[rows, im]
            x_ref[rows, re] = xr
            x_ref[rows, im] = xi
        carry_ref[:, re] = xr
        carry_ref[:, im] = xi
    y_parts = []
    for part in range(2):
        xr = x_ref[:, part * hs:(part + 1) * hs].astype(BF16)
        xi = x_ref[:, nstates + part * hs:nstates + (part + 1) * hs].astype(BF16)
        y_parts.append(_dot(xr, cc_ref[0, part]) + _dot(xi, cc_ref[1, part]))
    y = jnp.concatenate(y_parts, axis=1) + d_ref[...] * u
    y = 0.5 * y * (1.0 + jnp.tanh(math.sqrt(2.0 / math.pi) * (y + 0.044715 * (y * y * y))))
    z = _dot(y.astype(BF16), wg_ref[...])
    o_ref[0] = (z[:, :width] * _sigmoid(z[:, width:])).astype(o_ref.dtype)


def _block_diag(blocks):
    g, r, c = blocks.shape
    eye = jnp.eye(g, dtype=blocks.dtype)
    return (eye[:, None, :, None] * blocks[:, :, None, :]).reshape(g * r, g * c)


def _ssm_tables(lam_re, lam_im, log_dt, b_re, b_im, c_re, c_im):
    lr, li = lam_re.astype(F32), lam_im.astype(F32)
    dt = jnp.exp(log_dt.astype(F32))[:, None]
    mag = jnp.exp(lr * dt)
    a_re, a_im = mag * jnp.cos(li * dt), mag * jnp.sin(li * dt)
    den = lr * lr + li * li
    f_re = ((a_re - 1.0) * lr + a_im * li) / den
    f_im = (a_im * lr - (a_re - 1.0) * li) / den
    br, bi = b_re.astype(F32), b_im.astype(F32)
    bb_re = _block_diag(jnp.transpose(f_re[..., None] * br - f_im[..., None] * bi, (0, 2, 1)))
    bb_im = _block_diag(jnp.transpose(f_re[..., None] * bi + f_im[..., None] * br, (0, 2, 1)))
    cc_re = _block_diag(jnp.transpose(c_re.astype(F32), (0, 2, 1)))
    cc_im = -_block_diag(jnp.transpose(c_im.astype(F32), (0, 2, 1)))
    nstates = SSM_GROUPS * SSM_STATE
    hw, hs = SSM_WIDTH // 2, nstates // 2
    bb = jnp.stack([jnp.concatenate([bb_re[p * hw:(p + 1) * hw, p * hs:(p + 1) * hs],
                                     bb_im[p * hw:(p + 1) * hw, p * hs:(p + 1) * hs]], axis=1) for p in range(2)])
    cc = jnp.stack([jnp.stack([m[p * hs:(p + 1) * hs, p * hw:(p + 1) * hw] for p in range(2)])
                    for m in (cc_re, cc_im)])
    a = jnp.stack([a_re.reshape(nstates), a_im.reshape(nstates)])
    return bb.astype(BF16), a, cc.astype(BF16)


def s5_glu(u, bb, a, cc, d_skip, w_glu, steps=SSM_STEPS, strip=SSM_STRIP):
    bsz, seq, width = u.shape
    nstates = a.shape[1]
    assert bsz % SUBLANES == 0 and seq % steps == 0 and nstates % strip == 0
    ngrp = bsz // SUBLANES
    rows = steps * SUBLANES
    ut = u.reshape(ngrp, SUBLANES, seq, width).transpose(0, 2, 1, 3).reshape(ngrp, seq * SUBLANES, width)
    const = lambda *shape: pl.BlockSpec(shape, lambda g, c: (0,) * len(shape))
    out = pl.pallas_call(
        functools.partial(_ssm_kernel, steps=steps, strip=strip),
        grid=(ngrp, seq // steps),
        in_specs=[pl.BlockSpec((1, rows, width), lambda g, c: (g, c, 0)),
                  const(*bb.shape), const(*a.shape), const(*cc.shape),
                  const(1, width), const(*w_glu.shape)],
        out_specs=pl.BlockSpec((1, rows, width), lambda g, c: (g, c, 0)),
        out_shape=jax.ShapeDtypeStruct((ngrp, seq * SUBLANES, width), BF16),
        scratch_shapes=[pltpu.VMEM((rows, 2 * nstates), F32),
                        pltpu.VMEM((SUBLANES, 2 * nstates), F32)],
        compiler_params=_params("parallel", "arbitrary"),
        name="s5_glu",
    )(ut, bb, a, cc, d_skip.reshape(1, width).astype(F32), w_glu)
    return out.reshape(ngrp, seq, SUBLANES, width).transpose(0, 2, 1, 3).reshape(bsz * seq, width)


def _dsa_kernel(q_ref, k_ref, v_ref, iq_ref, ik_ref, ikq_ref, bias_ref, o_ref,
                vT_ref, key_ref, khi_ref, acc_ref, *, blk, topk, scale):
    i = pl.program_id(1)
    nblk = i + 1
    key_io = lax.broadcasted_iota(jnp.int32, (blk, blk), 0)
    qry_io = lax.broadcasted_iota(jnp.int32, (blk, blk), 1)

    @pl.when(i == 0)
    def _():
        for j in range(vT_ref.shape[0]):
            vT_ref[j] = v_ref[0, j * blk:(j + 1) * blk, :].T

    iqT = iq_ref[0].T
    idx_rhs = jnp.concatenate([iqT[h * IDX_DIM:(h + 1) * IDX_DIM, :] for h in range(IDX_HEADS)], axis=1)
    iw = ikq_ref[0].T[IDX_DIM:IDX_DIM + IDX_HEADS, :]

    def score_block(j, _):
        start = pl.multiple_of(j * blk, blk)
        kk = ik_ref[0, pl.ds(start, blk), :][:, :IDX_DIM].astype(BF16)
        d = _dot(kk, idx_rhs)
        sc = jnp.zeros((blk, blk), F32)
        for h in range(IDX_HEADS):
            sc = sc + iw[h:h + 1, :] * jnp.maximum(d[:, h * blk:(h + 1) * blk], 0.0)
        sc = jnp.where((key_io + j * blk) <= (qry_io + i * blk), sc, NEG_INF)
        bits = pltpu.bitcast(sc, jnp.int32)
        key = bits ^ ((bits >> 31) & 0x7FFFFFFF)
        key_ref[j] = key
        khi_ref[j] = (key >> HALF_BITS).astype(jnp.int16)
        return 0

    lax.fori_loop(0, nblk, score_block, 0)

    @pl.when(nblk % 2 == 1)
    def _():
        key_ref[nblk] = jnp.full((blk, blk), INT_MIN, jnp.int32)
        khi_ref[nblk] = jnp.full((blk, blk), INT16_MIN, jnp.int16)

    npairs = (nblk + 1) // 2

    def count(pred):
        def body(p, acc):
            for j in (2 * p, 2 * p + 1):
                hit = jnp.where(pred(key_ref[j]), 1.0, 0.0)
                acc = acc + jnp.sum(hit.reshape(blk // 8, 8, blk), axis=0)
            return acc
        return jnp.sum(lax.fori_loop(0, npairs, body, jnp.zeros((8, blk), F32)), axis=0, keepdims=True)

    def count_half(cand):
        one, zero = jnp.ones((), BF16), jnp.zeros((), BF16)
        rows = 2 * SUBLANES

        def body(p, acc):
            for j in (2 * p, 2 * p + 1):
                hit = jnp.where(khi_ref[j] >= cand, one, zero)
                for r in range(blk // rows):
                    acc = acc + hit[r * rows:(r + 1) * rows, :]
            return acc
        acc = lax.fori_loop(0, npairs, body, jnp.zeros((rows, blk), BF16))
        return jnp.sum(acc.astype(F32), axis=0, keepdims=True)

    def half_step(b, thr):
        cand = thr + lax.shift_left(jnp.int32(1), HALF_BITS - 1 - b)
        return jnp.where(count_half(cand.astype(jnp.int16)) >= topk, cand, thr)

    start16 = jnp.full((1, blk), INT16_MIN, jnp.int32)
    thr_hi = lax.fori_loop(0, HALF_BITS, half_step, start16)
    thr_hi16 = thr_hi.astype(jnp.int16)

    def lower_halves(j, _):
        hi = khi_ref[j]
        lo = ((key_ref[j] & 0xFFFF) + INT16_MIN).astype(jnp.int16)
        khi_ref[j] = jnp.where(hi > thr_hi16, jnp.int16(-INT16_MIN - 1), jnp.where(hi == thr_hi16, lo, jnp.int16(INT16_MIN)))
        return 0

    lax.fori_loop(0, 2 * npairs, lower_halves, 0)
    thr_lo = lax.fori_loop(0, HALF_BITS, half_step, start16)
    thr = lax.shift_left(thr_hi, HALF_BITS) + (thr_lo - INT16_MIN)

    need = topk - count(lambda key: key > thr)
    upto = jnp.where(qry_io <= key_io, 1.0, 0.0).astype(BF16)

    def select_block(j, run):
        key = key_ref[j]
        tie = jnp.where(key == thr, 1.0, 0.0)
        rank = _dot(upto, tie.astype(BF16)) + run
        keep_tie = jnp.where(rank <= need, 0.0, NEG_INF)
        madd = jnp.where(key > thr, 0.0, jnp.where(key == thr, keep_tie, NEG_INF))
        return madd, run + jnp.sum(tie, axis=0, keepdims=True)

    fold_scale = _is_power_of_two(scale)
    qT = q_ref[0].T
    if fold_scale:
        qT = qT * jnp.asarray(scale, qT.dtype)
    row = lax.broadcasted_iota(jnp.int32, (LANES, 1), 0)
    q_rhs = []
    for g in range(DSA_HEADS // 2):
        pair = qT[g * LANES:(g + 1) * LANES, :]
        zero = jnp.zeros_like(pair)
        q_rhs.append(jnp.concatenate([jnp.where(row < HEAD_DIM, pair, zero),
                                      jnp.where(row >= HEAD_DIM, pair, zero)], axis=1))
    acc_ref[...] = jnp.zeros_like(acc_ref)

    def attend_block(j, carry):
        m_all, l_all, run = carry
        start = pl.multiple_of(j * blk, blk)
        kblk = k_ref[0, pl.ds(start, blk), :]
        vT = vT_ref[j]
        madd, run = select_block(j, run)
        which = jnp.minimum(i - j, 2)
        s2 = [_dot(kblk[:, g * LANES:(g + 1) * LANES], q_rhs[g]) for g in range(DSA_HEADS // 2)]
        m_out, l_out, probs, alphas = [], [], [], []
        for h in range(DSA_HEADS):
            s = s2[h // 2][:, (h % 2) * blk:(h % 2 + 1) * blk]
            if not fold_scale:
                s = s * scale
            s = s + bias_ref[h, which] + madd
            m_new = jnp.maximum(m_all[h], jnp.max(s, axis=0, keepdims=True))
            p = jnp.exp(s - m_new)
            alpha = jnp.exp(m_all[h] - m_new)
            l_out.append(alpha * l_all[h] + jnp.sum(p, axis=0, keepdims=True))
            m_out.append(m_new)
            probs.append(p.astype(BF16))
            alphas.append(alpha)
        for h in range(DSA_HEADS):
            rows = slice(h * HEAD_DIM, (h + 1) * HEAD_DIM)
            acc_ref[rows, :] = alphas[h] * acc_ref[rows, :] + _dot(vT[rows, :], probs[h])
        return tuple(m_out), tuple(l_out), run

    init = (tuple(jnp.full((1, blk), NEG_INF, F32) for _ in range(DSA_HEADS)),
            tuple(jnp.zeros((1, blk), F32) for _ in range(DSA_HEADS)), jnp.zeros((1, blk), F32))
    _, l_all, _ = lax.fori_loop(0, nblk, attend_block, init)
    for h in range(DSA_HEADS):
        rows = slice(h * HEAD_DIM, (h + 1) * HEAD_DIM)
        acc_ref[rows, :] = acc_ref[rows, :] / l_all[h]
    o_ref[0] = acc_ref[...].T.astype(o_ref.dtype)


def dsa_attention(proj, qkv_col, iq_col, proj_f32, kw_col, bias_tiles, blk=DSA_BLOCK):
    bsz, seq, _ = proj.shape
    width = DSA_HEADS * HEAD_DIM
    nblk = seq // blk
    topk = min(DSA_TOPK, seq // 4)
    iq_width = IDX_HEADS * IDX_DIM
    assert qkv_col % width == 0 and iq_col % iq_width == 0 and kw_col % LANES == 0
    assert nblk % 2 == 0 and topk <= blk
    assert nblk * blk // (2 * SUBLANES) <= 256
    qcol = qkv_col // width
    return pl.pallas_call(
        functools.partial(_dsa_kernel, blk=blk, topk=topk, scale=HEAD_DIM ** -0.5),
        grid=(bsz, nblk),
        in_specs=[pl.BlockSpec((1, blk, width), lambda b, i: (b, i, qcol)),
                  pl.BlockSpec((1, seq, width), lambda b, i: (b, 0, qcol + 1)),
                  pl.BlockSpec((1, seq, width), lambda b, i: (b, 0, qcol + 2)),
                  pl.BlockSpec((1, blk, iq_width), lambda b, i: (b, i, iq_col // iq_width)),
                  pl.BlockSpec((1, seq, LANES), lambda b, i: (b, 0, kw_col // LANES)),
                  pl.BlockSpec((1, blk, LANES), lambda b, i: (b, i, kw_col // LANES)),
                  pl.BlockSpec((DSA_HEADS, 3, blk, blk), lambda b, i: (0, 0, 0, 0))],
        out_specs=pl.BlockSpec((1, blk, width), lambda b, i: (b, i, 0)),
        out_shape=jax.ShapeDtypeStruct((bsz, seq, width), BF16),
        scratch_shapes=[pltpu.VMEM((nblk, width, blk), BF16),
                        pltpu.VMEM((nblk, blk, blk), jnp.int32),
                        pltpu.VMEM((nblk, blk, blk), jnp.int16),
                        pltpu.VMEM((width, blk), F32)],
        compiler_params=_params("parallel", "arbitrary"),
        name="dsa_attention",
    )(proj, proj, proj, proj, proj_f32, proj_f32, bias_tiles)


def hybrid_mixer(xbf, bsz, seq, w_in, rel_bias, ssm_params, d_skip, w_glu, w_branches):
    offs = [0]
    for width in IN_SPLITS:
        offs.append(offs[-1] + width)
    seg = lambda a: w_in[:, offs[a]:offs[a + 1]]
    w_bf = jnp.concatenate([seg(0), seg(3), seg(4)], axis=1).astype(BF16)
    pad = jnp.zeros((w_in.shape[0], LANES - IDX_DIM - IDX_HEADS), w_in.dtype)
    w_f32 = jnp.concatenate([seg(1), seg(2), seg(5), seg(6), pad], axis=1).astype(BF16)
    dsa_col = IN_SPLITS[0]
    iq_col = dsa_col + IN_SPLITS[3]
    ssm_col = IN_SPLITS[1]
    kw_col = ssm_col + IN_SPLITS[2]
    proj = matmul(xbf, w_bf, BF16).reshape(bsz, seq, -1)
    proj_f32 = matmul(xbf, w_f32, F32).reshape(bsz, seq, -1)

    y_sb = stick_breaking_attention(proj).reshape(bsz * seq, -1)

    y_dil = dilated_attention(proj_f32, rel_bias[:, :DIL_HEADS]).reshape(bsz * seq, -1)

    bb, a_bar, cc = _ssm_tables(*ssm_params)
    y_ssm = s5_glu(proj_f32[..., ssm_col:ssm_col + SSM_WIDTH], bb, a_bar, cc, d_skip, w_glu.astype(BF16))

    y_dsa = dsa_attention(proj, dsa_col, iq_col, proj_f32, kw_col, _dsa_bias_tiles(rel_bias[:, DIL_HEADS:], DSA_BLOCK))
    y_dsa = y_dsa.reshape(bsz * seq, -1)

    return gated_branch_merge(xbf, [y_sb, y_dil, y_ssm, y_dsa], [w.astype(BF16) for w in w_branches],
                              seg(7).astype(BF16))


def kernel(x, ln_g, ln_b, ffn1_w_up, ffn1_w_down, w_in, rel_bias, ssm_lam_re, ssm_lam_im, ssm_log_dt,
           ssm_b_re, ssm_b_im, ssm_c_re, ssm_c_im, ssm_d, ssm_w_glu, w_br_sb, w_br_dil, w_br_ssm, w_br_dsa,
           w_out, ffn2_w_up, ffn2_w_down):
    bsz, seq, d = x.shape
    xf = x.reshape(bsz * seq, d)
    xbf = xf
    up1, down1 = ffn1_w_up.astype(BF16), ffn1_w_down.astype(BF16)
    up2, down2 = ffn2_w_up.astype(BF16), ffn2_w_down.astype(BF16)
    out_w = w_out.astype(BF16)
    for l in range(DEPTH):
        h = ffn_up(xbf, up1, l)
        xf, xbf = matmul_residual_layernorm(h, down1, l, xf, ln_g[l, 0], ln_b[l, 0], MACARON)
        merged = hybrid_mixer(xbf, bsz, seq, w_in[l], rel_bias,
                              (ssm_lam_re[l], ssm_lam_im[l], ssm_log_dt[l], ssm_b_re[l], ssm_b_im[l],
                               ssm_c_re[l], ssm_c_im[l]), ssm_d[l], ssm_w_glu[l],
                              (w_br_sb[l], w_br_dil[l], w_br_ssm[l], w_br_dsa[l]))
        xf, xbf = matmul_residual_layernorm(merged, out_w, l, xf, ln_g[l, 1], ln_b[l, 1], 1.0)
        h = ffn_up(xbf, up2, l)
        xf, xbf = matmul_residual_layernorm(h, down2, l, xf, ln_g[l, 2], ln_b[l, 2], MACARON)
    return xf.reshape(bsz, seq, d)
```

```python
import functools
import math

import jax
import jax.numpy as jnp
from jax import lax
from jax.experimental import pallas as pl
from jax.experimental.pallas import tpu as pltpu

F32 = jnp.float32
BF16 = jnp.bfloat16

D_MODEL = 2048
DEPTH = 2
HEAD_DIM = 64
SB_HEADS = 8
DIL_PATTERNS = ((128, 1), (512, 4), (2048, 16))
DIL_HEADS_PER_GROUP = 4
DIL_HEADS = DIL_HEADS_PER_GROUP * len(DIL_PATTERNS)
SSM_WIDTH = 512
SSM_GROUP = 16
SSM_GROUPS = SSM_WIDTH // SSM_GROUP
SSM_STATE = 64
DSA_HEADS = 8
IDX_HEADS = 8
IDX_DIM = 64
DSA_TOPK = 256
N_BRANCH = 4
D_FF = 5632
REL_BUCKETS = 32
REL_MAX_DIST = 128
DN_ALPHA = (2.0 * DEPTH) ** 0.25
LN_EPS = 1e-5
NEG_INF = -1e30
MACARON = 0.5

IN_SPLITS = (3 * SB_HEADS * HEAD_DIM, 3 * DIL_HEADS * HEAD_DIM, SSM_WIDTH, 3 * DSA_HEADS * HEAD_DIM,
             IDX_HEADS * IDX_DIM, IDX_DIM, IDX_HEADS, N_BRANCH * D_MODEL)

LANES = 128
ATT_BLOCK = 128
DIL_UNITS = 2
DIL_MERGE_ROWS = 512
DSA_BLOCK = 256
SB_QUERY_BLOCK = 2048
SB_KEY_BLOCK = 256
SUBLANES = 8
SSM_STEPS = 64
SSM_STRIP = 512
VMEM_LIMIT = 56 * 1024 * 1024
INT_MIN = -2 ** 31
HALF_BITS = 16
INT16_MIN = -2 ** 15
SOFTPLUS_CLAMP = 80.0

_NT = (((1,), (1,)), ((), ()))


def _dot(a, b):
    return jnp.dot(a, b, preferred_element_type=F32)


def _dot_nt(a, b):
    return lax.dot_general(a, b, _NT, preferred_element_type=F32)


def _params(*sem):
    return pltpu.CompilerParams(dimension_semantics=sem, vmem_limit_bytes=VMEM_LIMIT)


def _sigmoid(x):
    return 1.0 / (1.0 + jnp.exp(-x))


def _mm_kernel(x_ref, w_ref, o_ref):
    o_ref[...] = _dot(x_ref[...], w_ref[...]).astype(o_ref.dtype)


def matmul(x, w, out_dtype, tm=512):
    m, k = x.shape
    n = w.shape[1]
    assert m % tm == 0 and n % LANES == 0
    return pl.pallas_call(
        _mm_kernel,
        grid=(m // tm,),
        in_specs=[pl.BlockSpec((tm, k), lambda i: (i, 0)),
                  pl.BlockSpec((k, n), lambda i: (0, 0), pipeline_mode=pl.Buffered(1))],
        out_specs=pl.BlockSpec((tm, n), lambda i: (i, 0)),
        out_shape=jax.ShapeDtypeStruct((m, n), out_dtype),
        compiler_params=_params("parallel"),
        name="matmul",
    )(x, w)


def _ffn_up_kernel(x_ref, wa_ref, wb_ref, o_ref):
    x = x_ref[...].astype(BF16)
    a = _dot(x, wa_ref[...])
    b = _dot(x, wb_ref[...])
    o_ref[...] = (a * _sigmoid(a) * b).astype(o_ref.dtype)


def ffn_up(x, w_up, layer, tm=1024, tn=512):
    m, k = x.shape
    f = w_up.shape[2] // 2
    assert m % tm == 0 and f % tn == 0
    nb = f // tn
    return pl.pallas_call(
        _ffn_up_kernel,
        grid=(m // tm, nb),
        in_specs=[pl.BlockSpec((tm, k), lambda i, j: (i, 0)),
                  pl.BlockSpec((None, k, tn), lambda i, j: (layer, 0, j)),
                  pl.BlockSpec((None, k, tn), lambda i, j: (layer, 0, j + nb))],
        out_specs=pl.BlockSpec((tm, tn), lambda i, j: (i, j)),
        out_shape=jax.ShapeDtypeStruct((m, f), BF16),
        compiler_params=_params("parallel", "arbitrary"),
        name="ffn_up",
    )(x, w_up, w_up)


def _mm_res_ln_kernel(h_ref, w_ref, x_ref, g_ref, b_ref, o_ref, obf_ref, *, scale):
    y = DN_ALPHA * x_ref[...] + scale * _dot(h_ref[...], w_ref[...])
    mu = jnp.mean(y, axis=-1, keepdims=True)
    yc = y - mu
    var = jnp.mean(yc * yc, axis=-1, keepdims=True)
    out = yc * lax.rsqrt(var + LN_EPS) * g_ref[...] + b_ref[...]
    o_ref[...] = out
    obf_ref[...] = out.astype(BF16)


def matmul_residual_layernorm(h, w, layer, x, g, b, scale, tm=256):
    m, k = h.shape
    n = w.shape[2]
    assert m % tm == 0
    return pl.pallas_call(
        functools.partial(_mm_res_ln_kernel, scale=scale),
        grid=(m // tm,),
        in_specs=[pl.BlockSpec((tm, k), lambda i: (i, 0)),
                  pl.BlockSpec((None, k, n), lambda i: (layer, 0, 0), pipeline_mode=pl.Buffered(1)),
                  pl.BlockSpec((tm, n), lambda i: (i, 0)),
                  pl.BlockSpec((1, n), lambda i: (0, 0)),
                  pl.BlockSpec((1, n), lambda i: (0, 0))],
        out_specs=[pl.BlockSpec((tm, n), lambda i: (i, 0)),
                   pl.BlockSpec((tm, n), lambda i: (i, 0))],
        out_shape=[jax.ShapeDtypeStruct((m, n), F32), jax.ShapeDtypeStruct((m, n), BF16)],
        compiler_params=_params("parallel"),
        name="matmul_residual_layernorm",
    )(h, w, x, g.reshape(1, n), b.reshape(1, n))


def _gated_merge_kernel(x_ref, ysb_ref, ydil_ref, yssm_ref, ydsa_ref, wsb_ref, wdil_ref, wssm_ref, wdsa_ref,
                        g0_ref, g1_ref, g2_ref, g3_ref, o_ref):
    x = x_ref[...]
    acc = _sigmoid(_dot(x, g0_ref[...])) * _dot(ysb_ref[...], wsb_ref[...])
    acc += _sigmoid(_dot(x, g1_ref[...])) * _dot(ydil_ref[...], wdil_ref[...])
    acc += _sigmoid(_dot(x, g2_ref[...])) * _dot(yssm_ref[...], wssm_ref[...])
    acc += _sigmoid(_dot(x, g3_ref[...])) * _dot(ydsa_ref[...], wdsa_ref[...])
    o_ref[...] = acc.astype(o_ref.dtype)


def gated_branch_merge(x, ys, ws, w_gate, tm=1024, tn=512):
    m, k = x.shape
    n = ws[0].shape[1]
    assert m % tm == 0 and n % tn == 0
    nb = n // tn
    y_specs = [pl.BlockSpec((tm, y.shape[1]), lambda i, j: (i, 0)) for y in ys]
    w_specs = [pl.BlockSpec((w.shape[0], tn), lambda i, j: (0, j)) for w in ws]
    g_specs = [pl.BlockSpec((k, tn), functools.partial(lambda i, j, br: (0, j + br * nb), br=br))
               for br in range(N_BRANCH)]
    return pl.pallas_call(
        _gated_merge_kernel,
        grid=(m // tm, nb),
        in_specs=[pl.BlockSpec((tm, k), lambda i, j: (i, 0))] + y_specs + w_specs + g_specs,
        out_specs=pl.BlockSpec((tm, tn), lambda i, j: (i, j)),
        out_shape=jax.ShapeDtypeStruct((m, n), BF16),
        compiler_params=_params("parallel", "arbitrary"),
        name="gated_branch_merge",
    )(x, *ys, *ws, w_gate, w_gate, w_gate, w_gate)


def _is_power_of_two(x):
    return math.frexp(x)[0] == 0.5


def _sb_kernel(q_ref, k_ref, v_ref, o_ref, vT_ref, acc_ref, *, qblk, kblk, scale):
    i = pl.program_id(2)
    ratio = qblk // kblk
    fold_scale = _is_power_of_two(scale)

    @pl.when(i == 0)
    def _():
        for j in range(vT_ref.shape[0]):
            vT_ref[j] = v_ref[0, j * kblk:(j + 1) * kblk, :].T

    qT = q_ref[0].T
    if fold_scale:
        qT = qT * jnp.asarray(scale, qT.dtype)
    row = lax.broadcasted_iota(jnp.int32, (LANES, 1), 0)
    zero = jnp.zeros_like(qT)
    q_rhs = jnp.concatenate([jnp.where(row < HEAD_DIM, qT, zero), jnp.where(row >= HEAD_DIM, qT, zero)], axis=1)
    key_io = lax.broadcasted_iota(jnp.int32, (kblk, kblk), 0)
    qry_io = lax.broadcasted_iota(jnp.int32, (kblk, kblk), 1)
    later = jnp.where(qry_io > key_io, 1.0, 0.0).astype(BF16)
    acc_ref[...] = jnp.zeros_like(acc_ref)
    nsub = 2 * ratio

    def block(j, carry, diag_sub):
        start = pl.multiple_of(j * kblk, kblk)
        z_all = _dot(k_ref[0, pl.ds(start, kblk), :], q_rhs)
        vT = vT_ref[j]
        strict = key_io < qry_io
        stage = []
        for c in range(nsub):
            h, sub = divmod(c, ratio)
            if diag_sub is not None and sub < diag_sub:
                stage.append(None)
                continue
            masked = diag_sub is not None and sub == diag_sub
            z = z_all[:, c * kblk:(c + 1) * kblk]
            if not fold_scale:
                z = z * scale
            sp = jnp.maximum(jnp.log(1.0 + jnp.exp(jnp.minimum(z, SOFTPLUS_CLAMP))), z)
            log_1mb = jnp.where(strict, -sp, 0.0) if masked else -sp
            suf = _dot(later, log_1mb.astype(BF16))
            stage.append((z - sp, suf, jnp.sum(log_1mb, axis=0, keepdims=True), masked))
        new_carry = []
        for c in range(nsub):
            h, sub = divmod(c, ratio)
            if stage[c] is None:
                new_carry.append(carry[c])
                continue
            log_beta, suf, colsum, masked = stage[c]
            att = jnp.exp(log_beta + suf + carry[c])
            if masked:
                att = jnp.where(strict, att, 0.0)
            rows = slice(h * HEAD_DIM, (h + 1) * HEAD_DIM)
            cols = slice(sub * kblk, (sub + 1) * kblk)
            acc_ref[rows, cols] += _dot(vT[rows, :], att.astype(BF16))
            new_carry.append(carry[c] + colsum)
        return tuple(new_carry)

    carry = tuple(jnp.zeros((1, kblk), F32) for _ in range(nsub))
    for sub in reversed(range(ratio)):
        carry = block(i * ratio + sub, carry, sub)
    lax.fori_loop(0, i * ratio, lambda jj, c: block(i * ratio - 1 - jj, c, None), carry)
    o_ref[0] = acc_ref[...].T.astype(o_ref.dtype)


def stick_breaking_attention(qkv, col0=0, qblk=SB_QUERY_BLOCK, kblk=SB_KEY_BLOCK):
    bsz, seq, _ = qkv.shape
    width = SB_HEADS * HEAD_DIM
    npair = width // LANES
    qblk = min(qblk, seq)
    assert seq % qblk == 0 and qblk % kblk == 0 and col0 % LANES == 0
    qcol = col0 // LANES
    return pl.pallas_call(
        functools.partial(_sb_kernel, qblk=qblk, kblk=kblk, scale=HEAD_DIM ** -0.5),
        grid=(bsz, npair, seq // qblk),
        in_specs=[pl.BlockSpec((1, qblk, LANES), lambda b, p, i: (b, i, qcol + p)),
                  pl.BlockSpec((1, seq, LANES), lambda b, p, i: (b, 0, qcol + npair + p)),
                  pl.BlockSpec((1, seq, LANES), lambda b, p, i: (b, 0, qcol + 2 * npair + p))],
        out_specs=pl.BlockSpec((1, qblk, LANES), lambda b, p, i: (b, i, p)),
        out_shape=jax.ShapeDtypeStruct((bsz, seq, width), BF16),
        scratch_shapes=[pltpu.VMEM((seq // kblk, LANES, kblk), BF16),
                        pltpu.VMEM((LANES, qblk), F32)],
        compiler_params=_params("parallel", "parallel", "arbitrary"),
        name="stick_breaking_attention",
    )(qkv, qkv, qkv)


def _t5_bucket(dist):
    max_exact = REL_BUCKETS // 2
    d = jnp.maximum(dist, 1).astype(F32)
    large = max_exact + (jnp.log(d / max_exact) / math.log(REL_MAX_DIST / max_exact)
                         * (REL_BUCKETS - max_exact)).astype(jnp.int32)
    large = jnp.minimum(large, REL_BUCKETS - 1)
    return jnp.where(dist < max_exact, dist, large)


def _bias_of_distance(rel_bias, dist):
    one_hot = jax.nn.one_hot(_t5_bucket(dist), REL_BUCKETS, dtype=F32)
    return jnp.einsum("...b,bh->...h", one_hot, rel_bias.astype(F32), precision=lax.Precision.HIGHEST)


def _dilated_bias_tiles(rel_bias_group, dil, blk):
    a = jnp.arange(blk)[:, None]
    cc = jnp.arange(2 * blk)[None, :]
    step = blk + a - cc
    valid = (step >= 0) & (step <= blk)
    bias = _bias_of_distance(rel_bias_group, dil * jnp.clip(step, 0, blk))
    tile = jnp.where(valid[..., None], bias, NEG_INF)
    return jnp.transpose(tile, (2, 0, 1))


def _dsa_bias_tiles(rel_bias_dsa, blk):
    c = jnp.arange(blk)[:, None]
    a = jnp.arange(blk)[None, :]
    own = jnp.where((a - c >= 0)[..., None], _bias_of_distance(rel_bias_dsa, jnp.maximum(a - c, 0)), NEG_INF)
    prev = _bias_of_distance(rel_bias_dsa, blk + a - c)
    half = REL_BUCKETS // 2
    assert half + int(math.log((blk + 1) / half) / math.log(REL_MAX_DIST / half) * (REL_BUCKETS - half)) >= REL_BUCKETS - 1
    far = jnp.broadcast_to(rel_bias_dsa[REL_BUCKETS - 1].astype(F32), prev.shape)
    return jnp.transpose(jnp.stack([own, prev, far]), (3, 0, 1, 2))


def _dil_kernel(*refs, blk, dils, nbs, units, scale):
    ngroups = len(dils)
    nslab = 2 * ngroups
    q_refs, k_refs, v_refs = refs[:nslab], refs[nslab:2 * nslab], refs[2 * nslab:3 * nslab]
    bias_refs = refs[3 * nslab:3 * nslab + ngroups]
    y_ref, o_ref, lse_ref = refs[3 * nslab + ngroups:]
    lane = lax.broadcasted_iota(jnp.int32, (1, LANES), 1)
    fold_scale = _is_power_of_two(scale)
    ones = jnp.ones((blk, LANES), BF16)

    def run_step(step, _):
        work = []
        for g, u in [(g, u) for g in range(ngroups) for u in range(units)]:
            dil, nb = dils[g], nbs[g]
            unit = step * units + u
            c, i = unit // nb, unit % nb

            def rows(block, dil=dil, c=c):
                start = c + dil * blk * block
                return pl.ds(start, blk, stride=dil) if dil > 1 else pl.ds(pl.multiple_of(start, blk), blk)

            cur, prev = rows(i), rows(jnp.maximum(i - 1, 0))
            for half in range(2):
                slab = 2 * g + half
                q = q_refs[slab][0, cur, :]
                if fold_scale:
                    q = q * scale
                q = q.astype(BF16)
                kc, kp = k_refs[slab][0, cur, :].astype(BF16), k_refs[slab][0, prev, :].astype(BF16)
                zero = jnp.zeros_like(q)
                scores = []
                for e in range(2):
                    head = (lane < HEAD_DIM) if e == 0 else (lane >= HEAD_DIM)
                    qm = jnp.where(head, q, zero)
                    scores.append((_dot_nt(qm, kp), _dot_nt(qm, kc)))
                work.append((g, i, cur, prev, half, scores))
        soft = []
        for g, i, cur, prev, half, scores in work:
            has_prev = i > 0
            parts = []
            for e in range(2):
                h = 2 * half + e
                s_prev, s_cur = scores[e]
                if not fold_scale:
                    s_prev, s_cur = s_prev * scale, s_cur * scale
                s_prev = jnp.where(has_prev, s_prev + bias_refs[g][h, :, :blk], NEG_INF)
                s_cur = s_cur + bias_refs[g][h, :, blk:]
                m = jnp.maximum(jnp.max(s_prev, axis=-1, keepdims=True), jnp.max(s_cur, axis=-1, keepdims=True))
                parts.append((jnp.exp(s_prev - m).astype(BF16), jnp.exp(s_cur - m).astype(BF16), m))
            soft.append(parts)
        for (g, i, cur, prev, half, scores), parts in zip(work, soft):
            v_ref = v_refs[2 * g + half]
            vc = jnp.concatenate([v_ref[0, cur, :].astype(BF16), ones], axis=1)
            vp = jnp.concatenate([v_ref[0, prev, :].astype(BF16), ones], axis=1)
            out = jnp.zeros((blk, LANES), F32)
            lse_b = jnp.zeros((blk, LANES), F32)
            for e in range(2):
                head = (lane < HEAD_DIM) if e == 0 else (lane >= HEAD_DIM)
                p_prev, p_cur, m = parts[e]
                o2 = _dot(p_prev, vp) + _dot(p_cur, vc)
                denom = o2[:, LANES:]
                out = jnp.where(head, o2[:, :LANES] / denom, out)
                lse_b = jnp.where(head, m + jnp.log(denom), lse_b)
            o_ref[g, half, cur, :] = out
            lse_ref[g, half, cur, :] = lse_b
        return 0

    lax.fori_loop(0, dils[0] * nbs[0] // units, run_step, 0)

    def merge(r, _):
        rows_ = pl.ds(pl.multiple_of(r * DIL_MERGE_ROWS, DIL_MERGE_ROWS), DIL_MERGE_ROWS)
        for half in range(2):
            lses = [lse_ref[g, half, rows_, :] for g in range(ngroups)]
            top = functools.reduce(jnp.maximum, lses)
            ws = [jnp.exp(x - top) for x in lses]
            num = functools.reduce(lambda a, b: a + b, [ws[g] * o_ref[g, half, rows_, :] for g in range(ngroups)])
            den = functools.reduce(lambda a, b: a + b, ws)
            y_ref[0, rows_, half * LANES:(half + 1) * LANES] = (num / den).astype(y_ref.dtype)
        return 0

    lax.fori_loop(0, o_ref.shape[2] // DIL_MERGE_ROWS, merge, 0)


def dilated_attention(qkv, rel_bias_dil, blk=ATT_BLOCK, units=DIL_UNITS):
    bsz, seq, _ = qkv.shape
    hpg = DIL_HEADS_PER_GROUP
    assert hpg * HEAD_DIM == 2 * LANES and seq % DIL_MERGE_ROWS == 0
    ngroups = len(DIL_PATTERNS)
    nslab = DIL_HEADS * HEAD_DIM // LANES
    dils = tuple(d for _, d in DIL_PATTERNS)
    nbs = tuple(seq // d // blk for d in dils)
    nsteps = dils[0] * nbs[0]
    assert all(w // d == blk for w, d in DIL_PATTERNS) and all(d * n == nsteps for d, n in zip(dils, nbs))
    assert nsteps % units == 0
    tiles = [_dilated_bias_tiles(rel_bias_dil[:, g * hpg:(g + 1) * hpg], d, blk) for g, d in enumerate(dils)]
    slab = lambda col: pl.BlockSpec((1, seq, LANES), functools.partial(lambda b, col: (b, 0, col), col=col))
    in_specs = [slab(section * nslab + s) for section in range(3) for s in range(nslab)]
    in_specs += [pl.BlockSpec(t.shape, lambda b: (0, 0, 0), pipeline_mode=pl.Buffered(1)) for t in tiles]
    return pl.pallas_call(
        functools.partial(_dil_kernel, blk=blk, dils=dils, nbs=nbs, units=units, scale=HEAD_DIM ** -0.5),
        grid=(bsz,),
        in_specs=in_specs,
        out_specs=pl.BlockSpec((1, seq, 2 * LANES), lambda b: (b, 0, 0)),
        out_shape=jax.ShapeDtypeStruct((bsz, seq, 2 * LANES), BF16),
        scratch_shapes=[pltpu.VMEM((ngroups, 2, seq, LANES), F32),
                        pltpu.VMEM((ngroups, 2, seq, LANES), F32)],
        compiler_params=_params("parallel"),
        name="dilated_attention",
    )(*([qkv] * (3 * nslab)), *tiles)


def _ssm_kernel(u_ref, bb_ref, a_ref, cc_ref, d_ref, wg_ref, o_ref, x_ref, carry_ref, *, steps, strip):
    @pl.when(pl.program_id(1) == 0)
    def _():
        carry_ref[...] = jnp.zeros_like(carry_ref)

    u = u_ref[0]
    ub = u.astype(BF16)
    width = u.shape[1]
    half = width // 2
    nstates = x_ref.shape[1] // 2
    hs = nstates // 2
    for part in range(2):
        bu = _dot(ub[:, part * half:(part + 1) * half], bb_ref[part])
        x_ref[:, part * hs:(part + 1) * hs] = bu[:, :hs]
        x_ref[:, nstates + part * hs:nstates + (part + 1) * hs] = bu[:, hs:]
    for s in range(nstates // strip):
        re = slice(s * strip, (s + 1) * strip)
        im = slice(nstates + s * strip, nstates + (s + 1) * strip)
        ar = jnp.broadcast_to(a_ref[0:1, re], (SUBLANES, strip))
        ai = jnp.broadcast_to(a_ref[1:2, re], (SUBLANES, strip))
        xr = carry_ref[:, re]
        xi = carry_ref[:, im]
        for t in range(steps):
            rows = slice(t * SUBLANES, (t + 1) * SUBLANES)
            xr, xi = ar * xr - ai * xi + x_ref[rows, re], ar * xi + ai * xr + x_ref[rows, im]
            x_ref[rows, re] = xr
            x_ref[rows, im] = xi
        carry_ref[:, re] = xr
        carry_ref[:, im] = xi
    y_parts = []
    for part in range(2):
        xr = x_ref[:, part * hs:(part + 1) * hs].astype(BF16)
        xi = x_ref[:, nstates + part * hs:nstates + (part + 1) * hs].astype(BF16)
        y_parts.append(_dot(xr, cc_ref[0, part]) + _dot(xi, cc_ref[1, part]))
    y = jnp.concatenate(y_parts, axis=1) + d_ref[...] * u
    y = 0.5 * y * (1.0 + jnp.tanh(math.sqrt(2.0 / math.pi) * (y + 0.044715 * (y * y * y))))
    z = _dot(y.astype(BF16), wg_ref[...])
    o_ref[0] = (z[:, :width] * _sigmoid(z[:, width:])).astype(o_ref.dtype)


def _block_diag(blocks):
    g, r, c = blocks.shape
    eye = jnp.eye(g, dtype=blocks.dtype)
    return (eye[:, None, :, None] * blocks[:, :, None, :]).reshape(g * r, g * c)


def _ssm_tables(lam_re, lam_im, log_dt, b_re, b_im, c_re, c_im):
    lr, li = lam_re.astype(F32), lam_im.astype(F32)
    dt = jnp.exp(log_dt.astype(F32))[:, None]
    mag = jnp.exp(lr * dt)
    a_re, a_im = mag * jnp.cos(li * dt), mag * jnp.sin(li * dt)
    den = lr * lr + li * li
    f_re = ((a_re - 1.0) * lr + a_im * li) / den
    f_im = (a_im * lr - (a_re - 1.0) * li) / den
    br, bi = b_re.astype(F32), b_im.astype(F32)
    bb_re = _block_diag(jnp.transpose(f_re[..., None] * br - f_im[..., None] * bi, (0, 2, 1)))
    bb_im = _block_diag(jnp.transpose(f_re[..., None] * bi + f_im[..., None] * br, (0, 2, 1)))
    cc_re = _block_diag(jnp.transpose(c_re.astype(F32), (0, 2, 1)))
    cc_im = -_block_diag(jnp.transpose(c_im.astype(F32), (0, 2, 1)))
    nstates = SSM_GROUPS * SSM_STATE
    hw, hs = SSM_WIDTH // 2, nstates // 2
    bb = jnp.stack([jnp.concatenate([bb_re[p * hw:(p + 1) * hw, p * hs:(p + 1) * hs],
                                     bb_im[p * hw:(p + 1) * hw, p * hs:(p + 1) * hs]], axis=1) for p in range(2)])
    cc = jnp.stack([jnp.stack([m[p * hs:(p + 1) * hs, p * hw:(p + 1) * hw] for p in range(2)])
                    for m in (cc_re, cc_im)])
    a = jnp.stack([a_re.reshape(nstates), a_im.reshape(nstates)])
    return bb.astype(BF16), a, cc.astype(BF16)


def s5_glu(u, bb, a, cc, d_skip, w_glu, steps=SSM_STEPS, strip=SSM_STRIP):
    bsz, seq, width = u.shape
    nstates = a.shape[1]
    assert bsz % SUBLANES == 0 and seq % steps == 0 and nstates % strip == 0
    ngrp = bsz // SUBLANES
    rows = steps * SUBLANES
    ut = u.reshape(ngrp, SUBLANES, seq, width).transpose(0, 2, 1, 3).reshape(ngrp, seq * SUBLANES, width)
    const = lambda *shape: pl.BlockSpec(shape, lambda g, c: (0,) * len(shape))
    out = pl.pallas_call(
        functools.partial(_ssm_kernel, steps=steps, strip=strip),
        grid=(ngrp, seq // steps),
        in_specs=[pl.BlockSpec((1, rows, width), lambda g, c: (g, c, 0)),
                  const(*bb.shape), const(*a.shape), const(*cc.shape),
                  const(1, width), const(*w_glu.shape)],
        out_specs=pl.BlockSpec((1, rows, width), lambda g, c: (g, c, 0)),
        out_shape=jax.ShapeDtypeStruct((ngrp, seq * SUBLANES, width), BF16),
        scratch_shapes=[pltpu.VMEM((rows, 2 * nstates), F32),
                        pltpu.VMEM((SUBLANES, 2 * nstates), F32)],
        compiler_params=_params("parallel", "arbitrary"),
        name="s5_glu",
    )(ut, bb, a, cc, d_skip.reshape(1, width).astype(F32), w_glu)
    return out.reshape(ngrp, seq, SUBLANES, width).transpose(0, 2, 1, 3).reshape(bsz * seq, width)


def _dsa_kernel(q_ref, k_ref, v_ref, iq_ref, ik_ref, ikq_ref, bias_ref, o_ref,
                vT_ref, key_ref, khi_ref, acc_ref, *, blk, topk, scale):
    i = pl.program_id(1)
    nblk = i + 1
    key_io = lax.broadcasted_iota(jnp.int32, (blk, blk), 0)
    qry_io = lax.broadcasted_iota(jnp.int32, (blk, blk), 1)

    @pl.when(i == 0)
    def _():
        for j in range(vT_ref.shape[0]):
            vT_ref[j] = v_ref[0, j * blk:(j + 1) * blk, :].T

    iqT = iq_ref[0].T
    idx_rhs = jnp.concatenate([iqT[h * IDX_DIM:(h + 1) * IDX_DIM, :] for h in range(IDX_HEADS)], axis=1)
    iw = ikq_ref[0].T[IDX_DIM:IDX_DIM + IDX_HEADS, :]

    def score_block(j, _):
        start = pl.multiple_of(j * blk, blk)
        kk = ik_ref[0, pl.ds(start, blk), :][:, :IDX_DIM].astype(BF16)
        d = _dot(kk, idx_rhs)
        sc = jnp.zeros((blk, blk), F32)
        for h in range(IDX_HEADS):
            sc = sc + iw[h:h + 1, :] * jnp.maximum(d[:, h * blk:(h + 1) * blk], 0.0)
        sc = jnp.where((key_io + j * blk) <= (qry_io + i * blk), sc, NEG_INF)
        bits = pltpu.bitcast(sc, jnp.int32)
        key = bits ^ ((bits >> 31) & 0x7FFFFFFF)
        key_ref[j] = key
        khi_ref[j] = (key >> HALF_BITS).astype(jnp.int16)
        return 0

    lax.fori_loop(0, nblk, score_block, 0)

    @pl.when(nblk % 2 == 1)
    def _():
        key_ref[nblk] = jnp.full((blk, blk), INT_MIN, jnp.int32)
        khi_ref[nblk] = jnp.full((blk, blk), INT16_MIN, jnp.int16)

    npairs = (nblk + 1) // 2

    def count(pred):
        def body(p, acc):
            for j in (2 * p, 2 * p + 1):
                hit = jnp.where(pred(key_ref[j]), 1.0, 0.0)
                acc = acc + jnp.sum(hit.reshape(blk // 8, 8, blk), axis=0)
            return acc
        return jnp.sum(lax.fori_loop(0, npairs, body, jnp.zeros((8, blk), F32)), axis=0, keepdims=True)

    def count_half(cand):
        one, zero = jnp.ones((), BF16), jnp.zeros((), BF16)
        rows = 2 * SUBLANES

        def body(p, acc):
            for j in (2 * p, 2 * p + 1):
                hit = jnp.where(khi_ref[j] >= cand, one, zero)
                for r in range(blk // rows):
                    acc = acc + hit[r * rows:(r + 1) * rows, :]
            return acc
        acc = lax.fori_loop(0, npairs, body, jnp.zeros((rows, blk), BF16))
        return jnp.sum(acc.astype(F32), axis=0, keepdims=True)

    def half_step(b, thr):
        cand = thr + lax.shift_left(jnp.int32(1), HALF_BITS - 1 - b)
        return jnp.where(count_half(cand.astype(jnp.int16)) >= topk, cand, thr)

    start16 = jnp.full((1, blk), INT16_MIN, jnp.int32)
    thr_hi = lax.fori_loop(0, HALF_BITS, half_step, start16)
    thr_hi16 = thr_hi.astype(jnp.int16)

    def lower_halves(j, _):
        hi = khi_ref[j]
        lo = ((key_ref[j] & 0xFFFF) + INT16_MIN).astype(jnp.int16)
        khi_ref[j] = jnp.where(hi > thr_hi16, jnp.int16(-INT16_MIN - 1), jnp.where(hi == thr_hi16, lo, jnp.int16(INT16_MIN)))
        return 0

    lax.fori_loop(0, 2 * npairs, lower_halves, 0)
    thr_lo = lax.fori_loop(0, HALF_BITS, half_step, start16)
    thr = lax.shift_left(thr_hi, HALF_BITS) + (thr_lo - INT16_MIN)

    need = topk - count(lambda key: key > thr)
    upto = jnp.where(qry_io <= key_io, 1.0, 0.0).astype(BF16)

    def select_block(j, run):
        key = key_ref[j]
        tie = jnp.where(key == thr, 1.0, 0.0)
        rank = _dot(upto, tie.astype(BF16)) + run
        keep_tie = jnp.where(rank <= need, 0.0, NEG_INF)
        madd = jnp.where(key > thr, 0.0, jnp.where(key == thr, keep_tie, NEG_INF))
        return madd, run + jnp.sum(tie, axis=0, keepdims=True)

    fold_scale = _is_power_of_two(scale)
    qT = q_ref[0].T
    if fold_scale:
        qT = qT * jnp.asarray(scale, qT.dtype)
    row = lax.broadcasted_iota(jnp.int32, (LANES, 1), 0)
    q_rhs = []
    for g in range(DSA_HEADS // 2):
        pair = qT[g * LANES:(g + 1) * LANES, :]
        zero = jnp.zeros_like(pair)
        q_rhs.append(jnp.concatenate([jnp.where(row < HEAD_DIM, pair, zero),
                                      jnp.where(row >= HEAD_DIM, pair, zero)], axis=1))
    acc_ref[...] = jnp.zeros_like(acc_ref)

    def attend_block(j, carry):
        m_all, l_all, run = carry
        start = pl.multiple_of(j * blk, blk)
        kblk = k_ref[0, pl.ds(start, blk), :]
        vT = vT_ref[j]
        madd, run = select_block(j, run)
        which = jnp.minimum(i - j, 2)
        s2 = [_dot(kblk[:, g * LANES:(g + 1) * LANES], q_rhs[g]) for g in range(DSA_HEADS // 2)]
        m_out, l_out, probs, alphas = [], [], [], []
        for h in range(DSA_HEADS):
            s = s2[h // 2][:, (h % 2) * blk:(h % 2 + 1) * blk]
            if not fold_scale:
                s = s * scale
            s = s + bias_ref[h, which] + madd
            m_new = jnp.maximum(m_all[h], jnp.max(s, axis=0, keepdims=True))
            p = jnp.exp(s - m_new)
            alpha = jnp.exp(m_all[h] - m_new)
            l_out.append(alpha * l_all[h] + jnp.sum(p, axis=0, keepdims=True))
            m_out.append(m_new)
            probs.append(p.astype(BF16))
            alphas.append(alpha)
        for h in range(DSA_HEADS):
            rows = slice(h * HEAD_DIM, (h + 1) * HEAD_DIM)
            acc_ref[rows, :] = alphas[h] * acc_ref[rows, :] + _dot(vT[rows, :], probs[h])
        return tuple(m_out), tuple(l_out), run

    init = (tuple(jnp.full((1, blk), NEG_INF, F32) for _ in range(DSA_HEADS)),
            tuple(jnp.zeros((1, blk), F32) for _ in range(DSA_HEADS)), jnp.zeros((1, blk), F32))
    _, l_all, _ = lax.fori_loop(0, nblk, attend_block, init)
    for h in range(DSA_HEADS):
        rows = slice(h * HEAD_DIM, (h + 1) * HEAD_DIM)
        acc_ref[rows, :] = acc_ref[rows, :] / l_all[h]
    o_ref[0] = acc_ref[...].T.astype(o_ref.dtype)


def dsa_attention(proj, qkv_col, iq_col, proj_f32, kw_col, bias_tiles, blk=DSA_BLOCK):
    bsz, seq, _ = proj.shape
    width = DSA_HEADS * HEAD_DIM
    nblk = seq // blk
    topk = min(DSA_TOPK, seq // 4)
    iq_width = IDX_HEADS * IDX_DIM
    assert qkv_col % width == 0 and iq_col % iq_width == 0 and kw_col % LANES == 0
    assert nblk % 2 == 0 and topk <= blk
    assert nblk * blk // (2 * SUBLANES) <= 256
    qcol = qkv_col // width
    return pl.pallas_call(
        functools.partial(_dsa_kernel, blk=blk, topk=topk, scale=HEAD_DIM ** -0.5),
        grid=(bsz, nblk),
        in_specs=[pl.BlockSpec((1, blk, width), lambda b, i: (b, i, qcol)),
                  pl.BlockSpec((1, seq, width), lambda b, i: (b, 0, qcol + 1)),
                  pl.BlockSpec((1, seq, width), lambda b, i: (b, 0, qcol + 2)),
                  pl.BlockSpec((1, blk, iq_width), lambda b, i: (b, i, iq_col // iq_width)),
                  pl.BlockSpec((1, seq, LANES), lambda b, i: (b, 0, kw_col // LANES)),
                  pl.BlockSpec((1, blk, LANES), lambda b, i: (b, i, kw_col // LANES)),
                  pl.BlockSpec((DSA_HEADS, 3, blk, blk), lambda b, i: (0, 0, 0, 0))],
        out_specs=pl.BlockSpec((1, blk, width), lambda b, i: (b, i, 0)),
        out_shape=jax.ShapeDtypeStruct((bsz, seq, width), BF16),
        scratch_shapes=[pltpu.VMEM((nblk, width, blk), BF16),
                        pltpu.VMEM((nblk, blk, blk), jnp.int32),
                        pltpu.VMEM((nblk, blk, blk), jnp.int16),
                        pltpu.VMEM((width, blk), F32)],
        compiler_params=_params("parallel", "arbitrary"),
        name="dsa_attention",
    )(proj, proj, proj, proj, proj_f32, proj_f32, bias_tiles)


def hybrid_mixer(xbf, bsz, seq, w_in, rel_bias, ssm_params, d_skip, w_glu, w_branches):
    offs = [0]
    for width in IN_SPLITS:
        offs.append(offs[-1] + width)
    seg = lambda a: w_in[:, offs[a]:offs[a + 1]]
    w_bf = jnp.concatenate([seg(0), seg(3), seg(4)], axis=1).astype(BF16)
    pad = jnp.zeros((w_in.shape[0], LANES - IDX_DIM - IDX_HEADS), w_in.dtype)
    w_f32 = jnp.concatenate([seg(1), seg(2), seg(5), seg(6), pad], axis=1).astype(BF16)
    dsa_col = IN_SPLITS[0]
    iq_col = dsa_col + IN_SPLITS[3]
    ssm_col = IN_SPLITS[1]
    kw_col = ssm_col + IN_SPLITS[2]
    proj = matmul(xbf, w_bf, BF16).reshape(bsz, seq, -1)
    proj_f32 = matmul(xbf, w_f32, F32).reshape(bsz, seq, -1)

    y_sb = stick_breaking_attention(proj).reshape(bsz * seq, -1)

    y_dil = dilated_attention(proj_f32, rel_bias[:, :DIL_HEADS]).reshape(bsz * seq, -1)

    bb, a_bar, cc = _ssm_tables(*ssm_params)
    y_ssm = s5_glu(proj_f32[..., ssm_col:ssm_col + SSM_WIDTH], bb, a_bar, cc, d_skip, w_glu.astype(BF16))

    y_dsa = dsa_attention(proj, dsa_col, iq_col, proj_f32, kw_col, _dsa_bias_tiles(rel_bias[:, DIL_HEADS:], DSA_BLOCK))
    y_dsa = y_dsa.reshape(bsz * seq, -1)

    return gated_branch_merge(xbf, [y_sb, y_dil, y_ssm, y_dsa], [w.astype(BF16) for w in w_branches],
                              seg(7).astype(BF16))


def kernel(x, ln_g, ln_b, ffn1_w_up, ffn1_w_down, w_in, rel_bias, ssm_lam_re, ssm_lam_im, ssm_log_dt,
           ssm_b_re, ssm_b_im, ssm_c_re, ssm_c_im, ssm_d, ssm_w_glu, w_br_sb, w_br_dil, w_br_ssm, w_br_dsa,
           w_out, ffn2_w_up, ffn2_w_down):
    bsz, seq, d = x.shape
    xf = x.reshape(bsz * seq, d)
    xbf = xf
    up1, down1 = ffn1_w_up.astype(BF16), ffn1_w_down.astype(BF16)
    up2, down2 = ffn2_w_up.astype(BF16), ffn2_w_down.astype(BF16)
    out_w = w_out.astype(BF16)
    for l in range(DEPTH):
        h = ffn_up(xbf, up1, l)
        xf, xbf = matmul_residual_layernorm(h, down1, l, xf, ln_g[l, 0], ln_b[l, 0], MACARON)
        merged = hybrid_mixer(xbf, bsz, seq, w_in[l], rel_bias,
                              (ssm_lam_re[l], ssm_lam_im[l], ssm_log_dt[l], ssm_b_re[l], ssm_b_im[l],
                               ssm_c_re[l], ssm_c_im[l]), ssm_d[l], ssm_w_glu[l],
                              (w_br_sb[l], w_br_dil[l], w_br_ssm[l], w_br_dsa[l]))
        xf, xbf = matmul_residual_layernorm(merged, out_w, l, xf, ln_g[l, 1], ln_b[l, 1], 1.0)
        h = ffn_up(xbf, up2, l)
        xf, xbf = matmul_residual_layernorm(h, down2, l, xf, ln_g[l, 2], ln_b[l, 2], MACARON)
    return xf.reshape(bsz, seq, d)
```

```python
import functools
import math

import jax
import jax.numpy as jnp
from jax import lax
from jax.experimental import pallas as pl
from jax.experimental.pallas import tpu as pltpu

F32 = jnp.float32
BF16 = jnp.bfloat16

D_MODEL = 2048
DEPTH = 2
HEAD_DIM = 64
SB_HEADS = 8
DIL_PATTERNS = ((128, 1), (512, 4), (2048, 16))
DIL_HEADS_PER_GROUP = 4
DIL_HEADS = DIL_HEADS_PER_GROUP * len(DIL_PATTERNS)
SSM_WIDTH = 512
SSM_GROUP = 16
SSM_GROUPS = SSM_WIDTH // SSM_GROUP
SSM_STATE = 64
DSA_HEADS = 8
IDX_HEADS = 8
IDX_DIM = 64
DSA_TOPK = 256
N_BRANCH = 4
D_FF = 5632
REL_BUCKETS = 32
REL_MAX_DIST = 128
DN_ALPHA = (2.0 * DEPTH) ** 0.25
LN_EPS = 1e-5
NEG_INF = -1e30
MACARON = 0.5

IN_SPLITS = (3 * SB_HEADS * HEAD_DIM, 3 * DIL_HEADS * HEAD_DIM, SSM_WIDTH, 3 * DSA_HEADS * HEAD_DIM,
             IDX_HEADS * IDX_DIM, IDX_DIM, IDX_HEADS, N_BRANCH * D_MODEL)

LANES = 128
ATT_BLOCK = 128
DIL_UNITS = 2
DIL_MERGE_ROWS = 512
DSA_BLOCK = 256
SB_QUERY_BLOCK = 2048
SB_KEY_BLOCK = 256
SUBLANES = 8
SSM_STEPS = 128
SSM_STRIP = 512
VMEM_LIMIT = 56 * 1024 * 1024
INT_MIN = -2 ** 31
HALF_BITS = 16
INT16_MIN = -2 ** 15
SOFTPLUS_CLAMP = 80.0

_NT = (((1,), (1,)), ((), ()))


def _dot(a, b):
    return jnp.dot(a, b, preferred_element_type=F32)


def _dot_nt(a, b):
    return lax.dot_general(a, b, _NT, preferred_element_type=F32)


def _params(*sem):
    return pltpu.CompilerParams(dimension_semantics=sem, vmem_limit_bytes=VMEM_LIMIT)


def _sigmoid(x):
    return 1.0 / (1.0 + jnp.exp(-x))


def _mm_kernel(x_ref, w_ref, o_ref):
    o_ref[...] = _dot(x_ref[...], w_ref[...]).astype(o_ref.dtype)


def matmul(x, w, out_dtype, tm=512):
    m, k = x.shape
    n = w.shape[1]
    assert m % tm == 0 and n % LANES == 0
    return pl.pallas_call(
        _mm_kernel,
        grid=(m // tm,),
        in_specs=[pl.BlockSpec((tm, k), lambda i: (i, 0)),
                  pl.BlockSpec((k, n), lambda i: (0, 0), pipeline_mode=pl.Buffered(1))],
        out_specs=pl.BlockSpec((tm, n), lambda i: (i, 0)),
        out_shape=jax.ShapeDtypeStruct((m, n), out_dtype),
        compiler_params=_params("parallel"),
        name="matmul",
    )(x, w)


def _ffn_up_kernel(x_ref, wa_ref, wb_ref, o_ref):
    x = x_ref[...].astype(BF16)
    a = _dot(x, wa_ref[...])
    b = _dot(x, wb_ref[...])
    o_ref[...] = (a * _sigmoid(a) * b).astype(o_ref.dtype)


def ffn_up(x, w_up, layer, tm=1024, tn=512):
    m, k = x.shape
    f = w_up.shape[2] // 2
    assert m % tm == 0 and f % tn == 0
    nb = f // tn
    return pl.pallas_call(
        _ffn_up_kernel,
        grid=(m // tm, nb),
        in_specs=[pl.BlockSpec((tm, k), lambda i, j: (i, 0)),
                  pl.BlockSpec((None, k, tn), lambda i, j: (layer, 0, j)),
                  pl.BlockSpec((None, k, tn), lambda i, j: (layer, 0, j + nb))],
        out_specs=pl.BlockSpec((tm, tn), lambda i, j: (i, j)),
        out_shape=jax.ShapeDtypeStruct((m, f), BF16),
        compiler_params=_params("parallel", "arbitrary"),
        name="ffn_up",
    )(x, w_up, w_up)


def _mm_res_ln_kernel(h_ref, w_ref, x_ref, g_ref, b_ref, o_ref, obf_ref, *, scale):
    y = DN_ALPHA * x_ref[...] + scale * _dot(h_ref[...], w_ref[...])
    mu = jnp.mean(y, axis=-1, keepdims=True)
    yc = y - mu
    var = jnp.mean(yc * yc, axis=-1, keepdims=True)
    out = yc * lax.rsqrt(var + LN_EPS) * g_ref[...] + b_ref[...]
    o_ref[...] = out
    obf_ref[...] = out.astype(BF16)


def matmul_residual_layernorm(h, w, layer, x, g, b, scale, tm=256):
    m, k = h.shape
    n = w.shape[2]
    assert m % tm == 0
    return pl.pallas_call(
        functools.partial(_mm_res_ln_kernel, scale=scale),
        grid=(m // tm,),
        in_specs=[pl.BlockSpec((tm, k), lambda i: (i, 0)),
                  pl.BlockSpec((None, k, n), lambda i: (layer, 0, 0), pipeline_mode=pl.Buffered(1)),
                  pl.BlockSpec((tm, n), lambda i: (i, 0)),
                  pl.BlockSpec((1, n), lambda i: (0, 0)),
                  pl.BlockSpec((1, n), lambda i: (0, 0))],
        out_specs=[pl.BlockSpec((tm, n), lambda i: (i, 0)),
                   pl.BlockSpec((tm, n), lambda i: (i, 0))],
        out_shape=[jax.ShapeDtypeStruct((m, n), F32), jax.ShapeDtypeStruct((m, n), BF16)],
        compiler_params=_params("parallel"),
        name="matmul_residual_layernorm",
    )(h, w, x, g.reshape(1, n), b.reshape(1, n))


def _gated_merge_kernel(x_ref, ysb_ref, ydil_ref, yssm_ref, ydsa_ref, wsb_ref, wdil_ref, wssm_ref, wdsa_ref,
                        g0_ref, g1_ref, g2_ref, g3_ref, o_ref):
    x = x_ref[...]
    acc = _sigmoid(_dot(x, g0_ref[...])) * _dot(ysb_ref[...], wsb_ref[...])
    acc += _sigmoid(_dot(x, g1_ref[...])) * _dot(ydil_ref[...], wdil_ref[...])
    acc += _sigmoid(_dot(x, g2_ref[...])) * _dot(yssm_ref[...], wssm_ref[...])
    acc += _sigmoid(_dot(x, g3_ref[...])) * _dot(ydsa_ref[...], wdsa_ref[...])
    o_ref[...] = acc.astype(o_ref.dtype)


def gated_branch_merge(x, ys, ws, w_gate, tm=1024, tn=512):
    m, k = x.shape
    n = ws[0].shape[1]
    assert m % tm == 0 and n % tn == 0
    nb = n // tn
    y_specs = [pl.BlockSpec((tm, y.shape[1]), lambda i, j: (i, 0)) for y in ys]
    w_specs = [pl.BlockSpec((w.shape[0], tn), lambda i, j: (0, j)) for w in ws]
    g_specs = [pl.BlockSpec((k, tn), functools.partial(lambda i, j, br: (0, j + br * nb), br=br))
               for br in range(N_BRANCH)]
    return pl.pallas_call(
        _gated_merge_kernel,
        grid=(m // tm, nb),
        in_specs=[pl.BlockSpec((tm, k), lambda i, j: (i, 0))] + y_specs + w_specs + g_specs,
        out_specs=pl.BlockSpec((tm, tn), lambda i, j: (i, j)),
        out_shape=jax.ShapeDtypeStruct((m, n), BF16),
        compiler_params=_params("parallel", "arbitrary"),
        name="gated_branch_merge",
    )(x, *ys, *ws, w_gate, w_gate, w_gate, w_gate)


def _is_power_of_two(x):
    return math.frexp(x)[0] == 0.5


def _sb_kernel(q_ref, k_ref, v_ref, o_ref, vT_ref, acc_ref, *, qblk, kblk, scale):
    i = pl.program_id(2)
    ratio = qblk // kblk
    fold_scale = _is_power_of_two(scale)

    @pl.when(i == 0)
    def _():
        for j in range(vT_ref.shape[0]):
            vT_ref[j] = v_ref[0, j * kblk:(j + 1) * kblk, :].T

    qT = q_ref[0].T
    if fold_scale:
        qT = qT * jnp.asarray(scale, qT.dtype)
    row = lax.broadcasted_iota(jnp.int32, (LANES, 1), 0)
    zero = jnp.zeros_like(qT)
    q_rhs = jnp.concatenate([jnp.where(row < HEAD_DIM, qT, zero), jnp.where(row >= HEAD_DIM, qT, zero)], axis=1)
    key_io = lax.broadcasted_iota(jnp.int32, (kblk, kblk), 0)
    qry_io = lax.broadcasted_iota(jnp.int32, (kblk, kblk), 1)
    later = jnp.where(qry_io > key_io, 1.0, 0.0).astype(BF16)
    acc_ref[...] = jnp.zeros_like(acc_ref)
    nsub = 2 * ratio

    def block(j, carry, diag_sub):
        start = pl.multiple_of(j * kblk, kblk)
        z_all = _dot(k_ref[0, pl.ds(start, kblk), :], q_rhs)
        vT = vT_ref[j]
        strict = key_io < qry_io
        stage = []
        for c in range(nsub):
            h, sub = divmod(c, ratio)
            if diag_sub is not None and sub < diag_sub:
                stage.append(None)
                continue
            masked = diag_sub is not None and sub == diag_sub
            z = z_all[:, c * kblk:(c + 1) * kblk]
            if not fold_scale:
                z = z * scale
            sp = jnp.maximum(jnp.log(1.0 + jnp.exp(jnp.minimum(z, SOFTPLUS_CLAMP))), z)
            log_1mb = jnp.where(strict, -sp, 0.0) if masked else -sp
            suf = _dot(later, log_1mb.astype(BF16))
            stage.append((z - sp, suf, jnp.sum(log_1mb, axis=0, keepdims=True), masked))
        new_carry = []
        for c in range(nsub):
            h, sub = divmod(c, ratio)
            if stage[c] is None:
                new_carry.append(carry[c])
                continue
            log_beta, suf, colsum, masked = stage[c]
            att = jnp.exp(log_beta + suf + carry[c])
            if masked:
                att = jnp.where(strict, att, 0.0)
            rows = slice(h * HEAD_DIM, (h + 1) * HEAD_DIM)
            cols = slice(sub * kblk, (sub + 1) * kblk)
            acc_ref[rows, cols] += _dot(vT[rows, :], att.astype(BF16))
            new_carry.append(carry[c] + colsum)
        return tuple(new_carry)

    carry = tuple(jnp.zeros((1, kblk), F32) for _ in range(nsub))
    for sub in reversed(range(ratio)):
        carry = block(i * ratio + sub, carry, sub)
    lax.fori_loop(0, i * ratio, lambda jj, c: block(i * ratio - 1 - jj, c, None), carry)
    o_ref[0] = acc_ref[...].T.astype(o_ref.dtype)


def stick_breaking_attention(qkv, col0=0, qblk=SB_QUERY_BLOCK, kblk=SB_KEY_BLOCK):
    bsz, seq, _ = qkv.shape
    width = SB_HEADS * HEAD_DIM
    npair = width // LANES
    qblk = min(qblk, seq)
    assert seq % qblk == 0 and qblk % kblk == 0 and col0 % LANES == 0
    qcol = col0 // LANES
    return pl.pallas_call(
        functools.partial(_sb_kernel, qblk=qblk, kblk=kblk, scale=HEAD_DIM ** -0.5),
        grid=(bsz, npair, seq // qblk),
        in_specs=[pl.BlockSpec((1, qblk, LANES), lambda b, p, i: (b, i, qcol + p)),
                  pl.BlockSpec((1, seq, LANES), lambda b, p, i: (b, 0, qcol + npair + p)),
                  pl.BlockSpec((1, seq, LANES), lambda b, p, i: (b, 0, qcol + 2 * npair + p))],
        out_specs=pl.BlockSpec((1, qblk, LANES), lambda b, p, i: (b, i, p)),
        out_shape=jax.ShapeDtypeStruct((bsz, seq, width), BF16),
        scratch_shapes=[pltpu.VMEM((seq // kblk, LANES, kblk), BF16),
                        pltpu.VMEM((LANES, qblk), F32)],
        compiler_params=_params("parallel", "parallel", "arbitrary"),
        name="stick_breaking_attention",
    )(qkv, qkv, qkv)


def _t5_bucket(dist):
    max_exact = REL_BUCKETS // 2
    d = jnp.maximum(dist, 1).astype(F32)
    large = max_exact + (jnp.log(d / max_exact) / math.log(REL_MAX_DIST / max_exact)
                         * (REL_BUCKETS - max_exact)).astype(jnp.int32)
    large = jnp.minimum(large, REL_BUCKETS - 1)
    return jnp.where(dist < max_exact, dist, large)


def _bias_of_distance(rel_bias, dist):
    one_hot = jax.nn.one_hot(_t5_bucket(dist), REL_BUCKETS, dtype=F32)
    return jnp.einsum("...b,bh->...h", one_hot, rel_bias.astype(F32), precision=lax.Precision.HIGHEST)


def _dilated_bias_tiles(rel_bias_group, dil, blk):
    a = jnp.arange(blk)[:, None]
    cc = jnp.arange(2 * blk)[None, :]
    step = blk + a - cc
    valid = (step >= 0) & (step <= blk)
    bias = _bias_of_distance(rel_bias_group, dil * jnp.clip(step, 0, blk))
    tile = jnp.where(valid[..., None], bias, NEG_INF)
    return jnp.transpose(tile, (2, 0, 1))


def _dsa_bias_tiles(rel_bias_dsa, blk):
    c = jnp.arange(blk)[:, None]
    a = jnp.arange(blk)[None, :]
    own = jnp.where((a - c >= 0)[..., None], _bias_of_distance(rel_bias_dsa, jnp.maximum(a - c, 0)), NEG_INF)
    prev = _bias_of_distance(rel_bias_dsa, blk + a - c)
    half = REL_BUCKETS // 2
    assert half + int(math.log((blk + 1) / half) / math.log(REL_MAX_DIST / half) * (REL_BUCKETS - half)) >= REL_BUCKETS - 1
    far = jnp.broadcast_to(rel_bias_dsa[REL_BUCKETS - 1].astype(F32), prev.shape)
    return jnp.transpose(jnp.stack([own, prev, far]), (3, 0, 1, 2))


def _dil_kernel(*refs, blk, dils, nbs, units, scale):
    ngroups = len(dils)
    nslab = 2 * ngroups
    q_refs, k_refs, v_refs = refs[:nslab], refs[nslab:2 * nslab], refs[2 * nslab:3 * nslab]
    bias_refs = refs[3 * nslab:3 * nslab + ngroups]
    y_ref, o_ref, lse_ref = refs[3 * nslab + ngroups:]
    lane = lax.broadcasted_iota(jnp.int32, (1, LANES), 1)
    fold_scale = _is_power_of_two(scale)
    ones = jnp.ones((blk, LANES), BF16)

    def run_step(step, _):
        work = []
        for g, u in [(g, u) for g in range(ngroups) for u in range(units)]:
            dil, nb = dils[g], nbs[g]
            unit = step * units + u
            c, i = unit // nb, unit % nb

            def rows(block, dil=dil, c=c):
                start = c + dil * blk * block
                return pl.ds(start, blk, stride=dil) if dil > 1 else pl.ds(pl.multiple_of(start, blk), blk)

            cur, prev = rows(i), rows(jnp.maximum(i - 1, 0))
            for half in range(2):
                slab = 2 * g + half
                q = q_refs[slab][0, cur, :]
                if fold_scale:
                    q = q * scale
                q = q.astype(BF16)
                kc, kp = k_refs[slab][0, cur, :].astype(BF16), k_refs[slab][0, prev, :].astype(BF16)
                zero = jnp.zeros_like(q)
                scores = []
                for e in range(2):
                    head = (lane < HEAD_DIM) if e == 0 else (lane >= HEAD_DIM)
                    qm = jnp.where(head, q, zero)
                    scores.append((_dot_nt(qm, kp), _dot_nt(qm, kc)))
                work.append((g, i, cur, prev, half, scores))
        soft = []
        for g, i, cur, prev, half, scores in work:
            has_prev = i > 0
            parts = []
            for e in range(2):
                h = 2 * half + e
                s_prev, s_cur = scores[e]
                if not fold_scale:
                    s_prev, s_cur = s_prev * scale, s_cur * scale
                s_prev = jnp.where(has_prev, s_prev + bias_refs[g][h, :, :blk], NEG_INF)
                s_cur = s_cur + bias_refs[g][h, :, blk:]
                m = jnp.maximum(jnp.max(s_prev, axis=-1, keepdims=True), jnp.max(s_cur, axis=-1, keepdims=True))
                parts.append((jnp.exp(s_prev - m).astype(BF16), jnp.exp(s_cur - m).astype(BF16), m))
            soft.append(parts)
        for (g, i, cur, prev, half, scores), parts in zip(work, soft):
            v_ref = v_refs[2 * g + half]
            vc = jnp.concatenate([v_ref[0, cur, :].astype(BF16), ones], axis=1)
            vp = jnp.concatenate([v_ref[0, prev, :].astype(BF16), ones], axis=1)
            out = jnp.zeros((blk, LANES), F32)
            lse_b = jnp.zeros((blk, LANES), F32)
            for e in range(2):
                head = (lane < HEAD_DIM) if e == 0 else (lane >= HEAD_DIM)
                p_prev, p_cur, m = parts[e]
                o2 = _dot(p_prev, vp) + _dot(p_cur, vc)
                denom = o2[:, LANES:]
                out = jnp.where(head, o2[:, :LANES] / denom, out)
                lse_b = jnp.where(head, m + jnp.log(denom), lse_b)
            o_ref[g, half, cur, :] = out
            lse_ref[g, half, cur, :] = lse_b
        return 0

    lax.fori_loop(0, dils[0] * nbs[0] // units, run_step, 0)

    def merge(r, _):
        rows_ = pl.ds(pl.multiple_of(r * DIL_MERGE_ROWS, DIL_MERGE_ROWS), DIL_MERGE_ROWS)
        for half in range(2):
            lses = [lse_ref[g, half, rows_, :] for g in range(ngroups)]
            top = functools.reduce(jnp.maximum, lses)
            ws = [jnp.exp(x - top) for x in lses]
            num = functools.reduce(lambda a, b: a + b, [ws[g] * o_ref[g, half, rows_, :] for g in range(ngroups)])
            den = functools.reduce(lambda a, b: a + b, ws)
            y_ref[0, rows_, half * LANES:(half + 1) * LANES] = (num / den).astype(y_ref.dtype)
        return 0

    lax.fori_loop(0, o_ref.shape[2] // DIL_MERGE_ROWS, merge, 0)


def dilated_attention(qkv, rel_bias_dil, blk=ATT_BLOCK, units=DIL_UNITS):
    bsz, seq, _ = qkv.shape
    hpg = DIL_HEADS_PER_GROUP
    assert hpg * HEAD_DIM == 2 * LANES and seq % DIL_MERGE_ROWS == 0
    ngroups = len(DIL_PATTERNS)
    nslab = DIL_HEADS * HEAD_DIM // LANES
    dils = tuple(d for _, d in DIL_PATTERNS)
    nbs = tuple(seq // d // blk for d in dils)
    nsteps = dils[0] * nbs[0]
    assert all(w // d == blk for w, d in DIL_PATTERNS) and all(d * n == nsteps for d, n in zip(dils, nbs))
    assert nsteps % units == 0
    tiles = [_dilated_bias_tiles(rel_bias_dil[:, g * hpg:(g + 1) * hpg], d, blk) for g, d in enumerate(dils)]
    slab = lambda col: pl.BlockSpec((1, seq, LANES), functools.partial(lambda b, col: (b, 0, col), col=col))
    in_specs = [slab(section * nslab + s) for section in range(3) for s in range(nslab)]
    in_specs += [pl.BlockSpec(t.shape, lambda b: (0, 0, 0), pipeline_mode=pl.Buffered(1)) for t in tiles]
    return pl.pallas_call(
        functools.partial(_dil_kernel, blk=blk, dils=dils, nbs=nbs, units=units, scale=HEAD_DIM ** -0.5),
        grid=(bsz,),
        in_specs=in_specs,
        out_specs=pl.BlockSpec((1, seq, 2 * LANES), lambda b: (b, 0, 0)),
        out_shape=jax.ShapeDtypeStruct((bsz, seq, 2 * LANES), BF16),
        scratch_shapes=[pltpu.VMEM((ngroups, 2, seq, LANES), F32),
                        pltpu.VMEM((ngroups, 2, seq, LANES), F32)],
        compiler_params=_params("parallel"),
        name="dilated_attention",
    )(*([qkv] * (3 * nslab)), *tiles)


def _ssm_kernel(u_ref, bb_ref, a_ref, cc_ref, d_ref, wg_ref, o_ref, x_ref, carry_ref, *, steps, strip):
    @pl.when(pl.program_id(1) == 0)
    def _():
        carry_ref[...] = jnp.zeros_like(carry_ref)

    u = u_ref[0]
    ub = u.astype(BF16)
    width = u.shape[1]
    half = width // 2
    nstates = x_ref.shape[1] // 2
    hs = nstates // 2
    for part in range(2):
        bu = _dot(ub[:, part * half:(part + 1) * half], bb_ref[part])
        x_ref[:, part * hs:(part + 1) * hs] = bu[:, :hs]
        x_ref[:, nstates + part * hs:nstates + (part + 1) * hs] = bu[:, hs:]
    for s in range(nstates // strip):
        re = slice(s * strip, (s + 1) * strip)
        im = slice(nstates + s * strip, nstates + (s + 1) * strip)
        ar = jnp.broadcast_to(a_ref[0:1, re], (SUBLANES, strip))
        ai = jnp.broadcast_to(a_ref[1:2, re], (SUBLANES, strip))
        xr = carry_ref[:, re]
        xi = carry_ref[:, im]
        for t in range(steps):
            rows = slice(t * SUBLANES, (t + 1) * SUBLANES)
            xr, xi = ar * xr - ai * xi + x_ref[rows, re], ar * xi + ai * xr + x_ref[rows, im]
            x_ref[rows, re] = xr
            x_ref[rows, im] = xi
        carry_ref[:, re] = xr
        carry_ref[:, im] = xi
    y_parts = []
    for part in range(2):
        xr = x_ref[:, part * hs:(part + 1) * hs].astype(BF16)
        xi = x_ref[:, nstates + part * hs:nstates + (part + 1) * hs].astype(BF16)
        y_parts.append(_dot(xr, cc_ref[0, part]) + _dot(xi, cc_ref[1, part]))
    y = jnp.concatenate(y_parts, axis=1) + d_ref[...] * u
    y = 0.5 * y * (1.0 + jnp.tanh(math.sqrt(2.0 / math.pi) * (y + 0.044715 * (y * y * y))))
    z = _dot(y.astype(BF16), wg_ref[...])
    o_ref[0] = (z[:, :width] * _sigmoid(z[:, width:])).astype(o_ref.dtype)


def _block_diag(blocks):
    g, r, c = blocks.shape
    eye = jnp.eye(g, dtype=blocks.dtype)
    return (eye[:, None, :, None] * blocks[:, :, None, :]).reshape(g * r, g * c)


def _ssm_tables(lam_re, lam_im, log_dt, b_re, b_im, c_re, c_im):
    lr, li = lam_re.astype(F32), lam_im.astype(F32)
    dt = jnp.exp(log_dt.astype(F32))[:, None]
    mag = jnp.exp(lr * dt)
    a_re, a_im = mag * jnp.cos(li * dt), mag * jnp.sin(li * dt)
    den = lr * lr + li * li
    f_re = ((a_re - 1.0) * lr + a_im * li) / den
    f_im = (a_im * lr - (a_re - 1.0) * li) / den
    br, bi = b_re.astype(F32), b_im.astype(F32)
    bb_re = _block_diag(jnp.transpose(f_re[..., None] * br - f_im[..., None] * bi, (0, 2, 1)))
    bb_im = _block_diag(jnp.transpose(f_re[..., None] * bi + f_im[..., None] * br, (0, 2, 1)))
    cc_re = _block_diag(jnp.transpose(c_re.astype(F32), (0, 2, 1)))
    cc_im = -_block_diag(jnp.transpose(c_im.astype(F32), (0, 2, 1)))
    nstates = SSM_GROUPS * SSM_STATE
    hw, hs = SSM_WIDTH // 2, nstates // 2
    bb = jnp.stack([jnp.concatenate([bb_re[p * hw:(p + 1) * hw, p * hs:(p + 1) * hs],
                                     bb_im[p * hw:(p + 1) * hw, p * hs:(p + 1) * hs]], axis=1) for p in range(2)])
    cc = jnp.stack([jnp.stack([m[p * hs:(p + 1) * hs, p * hw:(p + 1) * hw] for p in range(2)])
                    for m in (cc_re, cc_im)])
    a = jnp.stack([a_re.reshape(nstates), a_im.reshape(nstates)])
    return bb.astype(BF16), a, cc.astype(BF16)


def s5_glu(u, bb, a, cc, d_skip, w_glu, steps=SSM_STEPS, strip=SSM_STRIP):
    bsz, seq, width = u.shape
    nstates = a.shape[1]
    assert bsz % SUBLANES == 0 and seq % steps == 0 and nstates % strip == 0
    ngrp = bsz // SUBLANES
    rows = steps * SUBLANES
    ut = u.reshape(ngrp, SUBLANES, seq, width).transpose(0, 2, 1, 3).reshape(ngrp, seq * SUBLANES, width)
    const = lambda *shape: pl.BlockSpec(shape, lambda g, c: (0,) * len(shape))
    out = pl.pallas_call(
        functools.partial(_ssm_kernel, steps=steps, strip=strip),
        grid=(ngrp, seq // steps),
        in_specs=[pl.BlockSpec((1, rows, width), lambda g, c: (g, c, 0)),
                  const(*bb.shape), const(*a.shape), const(*cc.shape),
                  const(1, width), const(*w_glu.shape)],
        out_specs=pl.BlockSpec((1, rows, width), lambda g, c: (g, c, 0)),
        out_shape=jax.ShapeDtypeStruct((ngrp, seq * SUBLANES, width), BF16),
        scratch_shapes=[pltpu.VMEM((rows, 2 * nstates), F32),
                        pltpu.VMEM((SUBLANES, 2 * nstates), F32)],
        compiler_params=_params("parallel", "arbitrary"),
        name="s5_glu",
    )(ut, bb, a, cc, d_skip.reshape(1, width).astype(F32), w_glu)
    return out.reshape(ngrp, seq, SUBLANES, width).transpose(0, 2, 1, 3).reshape(bsz * seq, width)


def _dsa_kernel(q_ref, k_ref, v_ref, iq_ref, ik_ref, ikq_ref, bias_ref, o_ref,
                vT_ref, key_ref, khi_ref, acc_ref, *, blk, topk, scale):
    i = pl.program_id(1)
    nblk = i + 1
    key_io = lax.broadcasted_iota(jnp.int32, (blk, blk), 0)
    qry_io = lax.broadcasted_iota(jnp.int32, (blk, blk), 1)

    @pl.when(i == 0)
    def _():
        for j in range(vT_ref.shape[0]):
            vT_ref[j] = v_ref[0, j * blk:(j + 1) * blk, :].T

    iqT = iq_ref[0].T
    idx_rhs = jnp.concatenate([iqT[h * IDX_DIM:(h + 1) * IDX_DIM, :] for h in range(IDX_HEADS)], axis=1)
    iw = ikq_ref[0].T[IDX_DIM:IDX_DIM + IDX_HEADS, :]

    def score_block(j, _):
        start = pl.multiple_of(j * blk, blk)
        kk = ik_ref[0, pl.ds(start, blk), :][:, :IDX_DIM].astype(BF16)
        d = _dot(kk, idx_rhs)
        sc = jnp.zeros((blk, blk), F32)
        for h in range(IDX_HEADS):
            sc = sc + iw[h:h + 1, :] * jnp.maximum(d[:, h * blk:(h + 1) * blk], 0.0)
        sc = jnp.where((key_io + j * blk) <= (qry_io + i * blk), sc, NEG_INF)
        bits = pltpu.bitcast(sc, jnp.int32)
        key = bits ^ ((bits >> 31) & 0x7FFFFFFF)
        key_ref[j] = key
        khi_ref[j] = (key >> HALF_BITS).astype(jnp.int16)
        return 0

    lax.fori_loop(0, nblk, score_block, 0)

    @pl.when(nblk % 2 == 1)
    def _():
        key_ref[nblk] = jnp.full((blk, blk), INT_MIN, jnp.int32)
        khi_ref[nblk] = jnp.full((blk, blk), INT16_MIN, jnp.int16)

    npairs = (nblk + 1) // 2

    def count(pred):
        def body(p, acc):
            for j in (2 * p, 2 * p + 1):
                hit = jnp.where(pred(key_ref[j]), 1.0, 0.0)
                acc = acc + jnp.sum(hit.reshape(blk // 8, 8, blk), axis=0)
            return acc
        return jnp.sum(lax.fori_loop(0, npairs, body, jnp.zeros((8, blk), F32)), axis=0, keepdims=True)

    def count_half(cand):
        one, zero = jnp.ones((), BF16), jnp.zeros((), BF16)
        rows = 2 * SUBLANES

        def body(p, acc):
            for j in (2 * p, 2 * p + 1):
                hit = jnp.where(khi_ref[j] >= cand, one, zero)
                for r in range(blk // rows):
                    acc = acc + hit[r * rows:(r + 1) * rows, :]
            return acc
        acc = lax.fori_loop(0, npairs, body, jnp.zeros((rows, blk), BF16))
        return jnp.sum(acc.astype(F32), axis=0, keepdims=True)

    def half_step(b, thr):
        cand = thr + lax.shift_left(jnp.int32(1), HALF_BITS - 1 - b)
        return jnp.where(count_half(cand.astype(jnp.int16)) >= topk, cand, thr)

    start16 = jnp.full((1, blk), INT16_MIN, jnp.int32)
    thr_hi = lax.fori_loop(0, HALF_BITS, half_step, start16)
    thr_hi16 = thr_hi.astype(jnp.int16)

    def lower_halves(j, _):
        hi = khi_ref[j]
        lo = ((key_ref[j] & 0xFFFF) + INT16_MIN).astype(jnp.int16)
        khi_ref[j] = jnp.where(hi > thr_hi16, jnp.int16(-INT16_MIN - 1), jnp.where(hi == thr_hi16, lo, jnp.int16(INT16_MIN)))
        return 0

    lax.fori_loop(0, 2 * npairs, lower_halves, 0)
    thr_lo = lax.fori_loop(0, HALF_BITS, half_step, start16)
    thr = lax.shift_left(thr_hi, HALF_BITS) + (thr_lo - INT16_MIN)

    need = topk - count(lambda key: key > thr)
    upto = jnp.where(qry_io <= key_io, 1.0, 0.0).astype(BF16)

    def select_block(j, run):
        key = key_ref[j]
        tie = jnp.where(key == thr, 1.0, 0.0)
        rank = _dot(upto, tie.astype(BF16)) + run
        keep_tie = jnp.where(rank <= need, 0.0, NEG_INF)
        madd = jnp.where(key > thr, 0.0, jnp.where(key == thr, keep_tie, NEG_INF))
        return madd, run + jnp.sum(tie, axis=0, keepdims=True)

    fold_scale = _is_power_of_two(scale)
    qT = q_ref[0].T
    if fold_scale:
        qT = qT * jnp.asarray(scale, qT.dtype)
    row = lax.broadcasted_iota(jnp.int32, (LANES, 1), 0)
    q_rhs = []
    for g in range(DSA_HEADS // 2):
        pair = qT[g * LANES:(g + 1) * LANES, :]
        zero = jnp.zeros_like(pair)
        q_rhs.append(jnp.concatenate([jnp.where(row < HEAD_DIM, pair, zero),
                                      jnp.where(row >= HEAD_DIM, pair, zero)], axis=1))
    acc_ref[...] = jnp.zeros_like(acc_ref)

    def attend_block(j, carry):
        m_all, l_all, run = carry
        start = pl.multiple_of(j * blk, blk)
        kblk = k_ref[0, pl.ds(start, blk), :]
        vT = vT_ref[j]
        madd, run = select_block(j, run)
        which = jnp.minimum(i - j, 2)
        s2 = [_dot(kblk[:, g * LANES:(g + 1) * LANES], q_rhs[g]) for g in range(DSA_HEADS // 2)]
        m_out, l_out, probs, alphas = [], [], [], []
        for h in range(DSA_HEADS):
            s = s2[h // 2][:, (h % 2) * blk:(h % 2 + 1) * blk]
            if not fold_scale:
                s = s * scale
            s = s + bias_ref[h, which] + madd
            m_new = jnp.maximum(m_all[h], jnp.max(s, axis=0, keepdims=True))
            p = jnp.exp(s - m_new)
            alpha = jnp.exp(m_all[h] - m_new)
            l_out.append(alpha * l_all[h] + jnp.sum(p, axis=0, keepdims=True))
            m_out.append(m_new)
            probs.append(p.astype(BF16))
            alphas.append(alpha)
        for h in range(DSA_HEADS):
            rows = slice(h * HEAD_DIM, (h + 1) * HEAD_DIM)
            acc_ref[rows, :] = alphas[h] * acc_ref[rows, :] + _dot(vT[rows, :], probs[h])
        return tuple(m_out), tuple(l_out), run

    init = (tuple(jnp.full((1, blk), NEG_INF, F32) for _ in range(DSA_HEADS)),
            tuple(jnp.zeros((1, blk), F32) for _ in range(DSA_HEADS)), jnp.zeros((1, blk), F32))
    _, l_all, _ = lax.fori_loop(0, nblk, attend_block, init)
    for h in range(DSA_HEADS):
        rows = slice(h * HEAD_DIM, (h + 1) * HEAD_DIM)
        acc_ref[rows, :] = acc_ref[rows, :] / l_all[h]
    o_ref[0] = acc_ref[...].T.astype(o_ref.dtype)


def dsa_attention(proj, qkv_col, iq_col, proj_f32, kw_col, bias_tiles, blk=DSA_BLOCK):
    bsz, seq, _ = proj.shape
    width = DSA_HEADS * HEAD_DIM
    nblk = seq // blk
    topk = min(DSA_TOPK, seq // 4)
    iq_width = IDX_HEADS * IDX_DIM
    assert qkv_col % width == 0 and iq_col % iq_width == 0 and kw_col % LANES == 0
    assert nblk % 2 == 0 and topk <= blk
    assert nblk * blk // (2 * SUBLANES) <= 256
    qcol = qkv_col // width
    return pl.pallas_call(
        functools.partial(_dsa_kernel, blk=blk, topk=topk, scale=HEAD_DIM ** -0.5),
        grid=(bsz, nblk),
        in_specs=[pl.BlockSpec((1, blk, width), lambda b, i: (b, i, qcol)),
                  pl.BlockSpec((1, seq, width), lambda b, i: (b, 0, qcol + 1)),
                  pl.BlockSpec((1, seq, width), lambda b, i: (b, 0, qcol + 2)),
                  pl.BlockSpec((1, blk, iq_width), lambda b, i: (b, i, iq_col // iq_width)),
                  pl.BlockSpec((1, seq, LANES), lambda b, i: (b, 0, kw_col // LANES)),
                  pl.BlockSpec((1, blk, LANES), lambda b, i: (b, i, kw_col // LANES)),
                  pl.BlockSpec((DSA_HEADS, 3, blk, blk), lambda b, i: (0, 0, 0, 0))],
        out_specs=pl.BlockSpec((1, blk, width), lambda b, i: (b, i, 0)),
        out_shape=jax.ShapeDtypeStruct((bsz, seq, width), BF16),
        scratch_shapes=[pltpu.VMEM((nblk, width, blk), BF16),
                        pltpu.VMEM((nblk, blk, blk), jnp.int32),
                        pltpu.VMEM((nblk, blk, blk), jnp.int16),
                        pltpu.VMEM((width, blk), F32)],
        compiler_params=_params("parallel", "arbitrary"),
        name="dsa_attention",
    )(proj, proj, proj, proj, proj_f32, proj_f32, bias_tiles)


def hybrid_mixer(xbf, bsz, seq, w_in, rel_bias, ssm_params, d_skip, w_glu, w_branches):
    offs = [0]
    for width in IN_SPLITS:
        offs.append(offs[-1] + width)
    seg = lambda a: w_in[:, offs[a]:offs[a + 1]]
    w_bf = jnp.concatenate([seg(0), seg(3), seg(4)], axis=1).astype(BF16)
    pad = jnp.zeros((w_in.shape[0], LANES - IDX_DIM - IDX_HEADS), w_in.dtype)
    w_f32 = jnp.concatenate([seg(1), seg(2), seg(5), seg(6), pad], axis=1).astype(BF16)
    dsa_col = IN_SPLITS[0]
    iq_col = dsa_col + IN_SPLITS[3]
    ssm_col = IN_SPLITS[1]
    kw_col = ssm_col + IN_SPLITS[2]
    proj = matmul(xbf, w_bf, BF16).reshape(bsz, seq, -1)
    proj_f32 = matmul(xbf, w_f32, F32).reshape(bsz, seq, -1)

    y_sb = stick_breaking_attention(proj).reshape(bsz * seq, -1)

    y_dil = dilated_attention(proj_f32, rel_bias[:, :DIL_HEADS]).reshape(bsz * seq, -1)

    bb, a_bar, cc = _ssm_tables(*ssm_params)
    y_ssm = s5_glu(proj_f32[..., ssm_col:ssm_col + SSM_WIDTH], bb, a_bar, cc, d_skip, w_glu.astype(BF16))

    y_dsa = dsa_attention(proj, dsa_col, iq_col, proj_f32, kw_col, _dsa_bias_tiles(rel_bias[:, DIL_HEADS:], DSA_BLOCK))
    y_dsa = y_dsa.reshape(bsz * seq, -1)

    return gated_branch_merge(xbf, [y_sb, y_dil, y_ssm, y_dsa], [w.astype(BF16) for w in w_branches],
                              seg(7).astype(BF16))


def kernel(x, ln_g, ln_b, ffn1_w_up, ffn1_w_down, w_in, rel_bias, ssm_lam_re, ssm_lam_im, ssm_log_dt,
           ssm_b_re, ssm_b_im, ssm_c_re, ssm_c_im, ssm_d, ssm_w_glu, w_br_sb, w_br_dil, w_br_ssm, w_br_dsa,
           w_out, ffn2_w_up, ffn2_w_down):
    bsz, seq, d = x.shape
    xf = x.reshape(bsz * seq, d)
    xbf = xf
    up1, down1 = ffn1_w_up.astype(BF16), ffn1_w_down.astype(BF16)
    up2, down2 = ffn2_w_up.astype(BF16), ffn2_w_down.astype(BF16)
    out_w = w_out.astype(BF16)
    for l in range(DEPTH):
        h = ffn_up(xbf, up1, l)
        xf, xbf = matmul_residual_layernorm(h, down1, l, xf, ln_g[l, 0], ln_b[l, 0], MACARON)
        merged = hybrid_mixer(xbf, bsz, seq, w_in[l], rel_bias,
                              (ssm_lam_re[l], ssm_lam_im[l], ssm_log_dt[l], ssm_b_re[l], ssm_b_im[l],
                               ssm_c_re[l], ssm_c_im[l]), ssm_d[l], ssm_w_glu[l],
                              (w_br_sb[l], w_br_dil[l], w_br_ssm[l], w_br_dsa[l]))
        xf, xbf = matmul_residual_layernorm(merged, out_w, l, xf, ln_g[l, 1], ln_b[l, 1], 1.0)
        h = ffn_up(xbf, up2, l)
        xf, xbf = matmul_residual_layernorm(h, down2, l, xf, ln_g[l, 2], ln_b[l, 2], MACARON)
    return xf.reshape(bsz, seq, d)
```
